```python
import math
import jax, jax.numpy as jnp
from jax import lax
import numpy as np

D_MODEL = 2048
BATCH = 4
SEQ = 2048
DEPTH = 1

CHUNK = 64
N_META = 16
Q_BLOCK = 128
ATT_HEADS = 8
ATT_QK_DIM = 64
ATT_V_DIM = 128
DN_HEADS = 8
DN_K_DIM = 128
DN_V_DIM = 128
DN_CONV = 4
D_FF = 5632
FFN_CONV = 3
ROPE_THETA = 10000.0
LN_EPS = 1e-5
RMS_EPS = 1e-6

ATT_QK_W = ATT_HEADS * 2 * ATT_QK_DIM
ATT_V_W = ATT_HEADS * ATT_V_DIM
DN_QK_W = DN_HEADS * DN_K_DIM
DN_V_W = DN_HEADS * DN_V_DIM
MIX_WIDTH = ATT_V_W + DN_V_W
SPLITS = (ATT_QK_W, ATT_QK_W, ATT_V_W, DN_QK_W, DN_QK_W, DN_V_W, DN_V_W, DN_HEADS, DN_HEADS)
IN_COLS = 3 * 1024 + 4 * 1024 + 2 * DN_HEADS

kernel_name = "hybrid_diffattn_gdn_convffn_deepnorm"


def layer_norm(x, g, b):
    xf = x.astype(jnp.float32)
    mu = jnp.mean(xf, axis=-1, keepdims=True)
    var = jnp.mean(jnp.square(xf - mu), axis=-1, keepdims=True)
    return ((xf - mu) * lax.rsqrt(var + LN_EPS) * g.astype(jnp.float32) + b.astype(jnp.float32)).astype(x.dtype)


def rms_norm(x, w):
    xf = x.astype(jnp.float32)
    return (xf * lax.rsqrt(jnp.mean(jnp.square(xf), axis=-1, keepdims=True) + RMS_EPS) * w.astype(jnp.float32)).astype(x.dtype)


def l2_normalize(x):
    return x * lax.rsqrt(jnp.sum(jnp.square(x), axis=-1, keepdims=True) + RMS_EPS)


def causal_dwconv(x, w):
    k_w = w.shape[0]
    n = x.shape[1]
    xp = jnp.pad(x, ((0, 0), (k_w - 1, 0), (0, 0)))
    y = xp[:, 0:n] * w[0]
    for j in range(1, k_w):
        y = y + xp[:, j:j + n] * w[j]
    return y


def rope(x, pos):
    d = x.shape[-1]
    inv_freq = ROPE_THETA ** (-jnp.arange(0, d, 2, dtype=jnp.float32) / d)
    ang = pos[:, None] * inv_freq[None, :]
    cos = jnp.cos(ang)[None, :, None, None, :]
    sin = jnp.sin(ang)[None, :, None, None, :]
    xf = x.astype(jnp.float32)
    x1, x2 = xf[..., : d // 2], xf[..., d // 2:]
    return jnp.concatenate([x1 * cos - x2 * sin, x2 * cos + x1 * sin], axis=-1)


def diff_attention(q, k, v, lam, lam_init, norm_w, cid):
    b_, n, h, _, dq = q.shape
    scale = dq ** -0.5
    n_real = n - N_META
    bounds = [(0, N_META)] + [(N_META + i * Q_BLOCK, N_META + (i + 1) * Q_BLOCK) for i in range(n_real // Q_BLOCK)]
    vf = v.astype(jnp.float32)
    outs = []
    for s, e in bounds:
        scores = jnp.einsum('bqhcd,bkhcd->bhcqk', q[:, s:e], k[:, :e]) * scale
        mask = cid[None, :e] <= cid[s:e, None]
        p = jax.nn.softmax(jnp.where(mask, scores, -jnp.inf), axis=-1)
        a = p[:, :, 0] - lam * p[:, :, 1]
        outs.append(jnp.einsum('bhqk,bkhe->bqhe', a, vf[:, :e]))
    o = jnp.concatenate(outs, axis=1)
    o = rms_norm(o, norm_w) * (1.0 - lam_init)
    return o.reshape(b_, n, h * o.shape[-1])


def gated_delta_rule(q, k, v, beta, g):
    b_, n, h, dk = q.shape
    dv = v.shape[-1]
    pad = (-n) % CHUNK
    n_p = n + pad
    nc = n_p // CHUNK

    def front_pad(t):
        return jnp.pad(t, ((0, 0), (pad, 0)) + ((0, 0),) * (t.ndim - 2))

    def to_chunks(t):
        t = jnp.swapaxes(front_pad(t), 1, 2)
        return t.reshape(t.shape[:2] + (nc, CHUNK) + t.shape[3:])

    q, k, v, beta, g = (to_chunks(t) for t in (q, k, v, beta, g))
    gc = jnp.cumsum(g, axis=-1)
    idx = jnp.arange(CHUNK)
    tril = idx[:, None] >= idx[None, :]
    strict = idx[:, None] > idx[None, :]
    decay = jnp.exp(jnp.where(tril, gc[..., :, None] - gc[..., None, :], -jnp.inf))
    kb = k * beta[..., None]
    vb = v * beta[..., None]
    m = jnp.where(strict, jnp.einsum('bhncd,bhnsd->bhncs', kb, k) * decay, 0.0)
    eye = jnp.eye(CHUNK, dtype=jnp.float32)
    rhs = jnp.concatenate([vb, kb * jnp.exp(gc)[..., None]], axis=-1)
    sol = lax.linalg.triangular_solve(eye + m, rhs, left_side=True, lower=True)
    u, w = sol[..., :dv], sol[..., dv:]
    qk = jnp.einsum('bhncd,bhnsd->bhncs', q, k) * decay
    q_dec = q * jnp.exp(gc)[..., None]
    k_dec = k * jnp.exp(gc[..., -1:] - gc)[..., None]
    g_last = jnp.exp(gc[..., -1])
    xs = tuple(jnp.moveaxis(t, 2, 0) for t in (u, w, qk, q_dec, k_dec, g_last))

    def step(state, inp):
        u_c, w_c, qk_c, qd_c, kd_c, gl_c = inp
        v_new = u_c - jnp.einsum('bhck,bhkv->bhcv', w_c, state)
        o_c = jnp.einsum('bhck,bhkv->bhcv', qd_c, state) + jnp.einsum('bhcs,bhsv->bhcv', qk_c, v_new)
        state = state * gl_c[..., None, None] + jnp.einsum('bhck,bhcv->bhkv', kd_c, v_new)
        return state, o_c

    s0 = jnp.zeros((b_, h, dk, dv), jnp.float32)
    _, o = lax.scan(step, s0, xs)
    o = o.transpose(1, 0, 3, 2, 4).reshape(b_, n_p, h, dv)
    return o[:, pad:]


def hybrid_mixer(x, w_in, conv_w, a_log, dt_bias, lq1, lk1, lq2, lk2, diff_norm_w, delta_norm_w, w_out, layer_idx, cid, pos):
    b_, n, _ = x.shape
    points, acc = [], 0
    for s in SPLITS[:-1]:
        acc += s
        points.append(acc)
    aq, ak, av, dq, dk, dv, dz, db, da = jnp.split(x @ w_in, points, axis=-1)

    aq = rope(aq.reshape(b_, n, ATT_HEADS, 2, ATT_QK_DIM), pos)
    ak = rope(ak.reshape(b_, n, ATT_HEADS, 2, ATT_QK_DIM), pos)
    av = av.reshape(b_, n, ATT_HEADS, ATT_V_DIM)
    lam_init = 0.8 - 0.6 * math.exp(-0.3 * layer_idx)
    f32 = jnp.float32
    lam = (jnp.exp(jnp.sum(lq1.astype(f32) * lk1.astype(f32))) - jnp.exp(jnp.sum(lq2.astype(f32) * lk2.astype(f32))) + lam_init)
    o_att = diff_attention(aq, ak, av, lam, lam_init, diff_norm_w, cid)

    qkv = jax.nn.silu(causal_dwconv(jnp.concatenate([dq, dk, dv], axis=-1), conv_w)).astype(f32)
    dq, dk, dv = jnp.split(qkv, [DN_QK_W, 2 * DN_QK_W], axis=-1)
    dq = l2_normalize(dq.reshape(b_, n, DN_HEADS, DN_K_DIM)) * (DN_K_DIM ** -0.5)
    dk = l2_normalize(dk.reshape(b_, n, DN_HEADS, DN_K_DIM))
    dv = dv.reshape(b_, n, DN_HEADS, DN_V_DIM)
    beta = jax.nn.sigmoid(db.astype(f32))
    g = -jnp.exp(a_log.astype(f32)) * jax.nn.softplus(da.astype(f32) + dt_bias.astype(f32))
    o_dn = gated_delta_rule(dq, dk, dv, beta, g)
    o_dn = rms_norm(o_dn, delta_norm_w) * jax.nn.silu(dz.reshape(b_, n, DN_HEADS, DN_V_DIM).astype(f32))
    o_dn = o_dn.reshape(b_, n, DN_V_W)

    o = jnp.concatenate([o_att, o_dn], axis=-1).astype(x.dtype)
    return o @ w_out


def conv_ffn(x, w_gate, w_up, conv_w, conv_b, w_down):
    h = causal_dwconv(x @ w_gate, conv_w) + conv_b
    return (jax.nn.silu(h) * (x @ w_up)) @ w_down


def setup_inputs(seed: int = 0) -> dict:
    key = jax.random.key(seed)
    ks = jax.random.split(key, 24)
    beta_dn = (8.0 * DEPTH) ** -0.25
    nrm = jax.random.normal
    dt = jnp.exp(jax.random.uniform(ks[4], (DEPTH, DN_HEADS)) * (math.log(0.1) - math.log(0.001)) + math.log(0.001))
    return {
        "x": nrm(ks[0], (BATCH, SEQ, D_MODEL), jnp.float32),
        "meta_tokens": nrm(ks[1], (N_META, D_MODEL), jnp.float32),
        "w_in": nrm(ks[2], (DEPTH, D_MODEL, IN_COLS), jnp.float32) * D_MODEL ** -0.5,
        "conv_qkv_w": nrm(ks[3], (DEPTH, DN_CONV, 2 * DN_QK_W + DN_V_W), jnp.float32) * DN_CONV ** -0.5,
        "a_log": jnp.log(jax.random.uniform(ks[5], (DEPTH, DN_HEADS), minval=1.0, maxval=16.0)),
        "dt_bias": dt + jnp.log(-jnp.expm1(-dt)),
        "lambda_q1": nrm(ks[6], (DEPTH, ATT_QK_DIM), jnp.float32) * 0.1,
        "lambda_k1": nrm(ks[7], (DEPTH, ATT_QK_DIM), jnp.float32) * 0.1,
        "lambda_q2": nrm(ks[8], (DEPTH, ATT_QK_DIM), jnp.float32) * 0.1,
        "lambda_k2": nrm(ks[9], (DEPTH, ATT_QK_DIM), jnp.float32) * 0.1,
        "diff_norm_w": 1.0 + 0.01 * nrm(ks[10], (DEPTH, ATT_V_DIM), jnp.float32),
        "delta_norm_w": 1.0 + 0.01 * nrm(ks[11], (DEPTH, DN_V_DIM), jnp.float32),
        "w_out": nrm(ks[12], (DEPTH, MIX_WIDTH, D_MODEL), jnp.float32) * MIX_WIDTH ** -0.5 * beta_dn,
        "ln1_g": 1.0 + 0.01 * nrm(ks[13], (DEPTH, D_MODEL), jnp.float32),
        "ln1_b": 0.01 * nrm(ks[14], (DEPTH, D_MODEL), jnp.float32),
        "ffn_w_gate": nrm(ks[15], (DEPTH, D_MODEL, D_FF), jnp.float32) * D_MODEL ** -0.5,
        "ffn_w_up": nrm(ks[16], (DEPTH, D_MODEL, D_FF), jnp.float32) * D_MODEL ** -0.5,
        "ffn_conv_w": nrm(ks[17], (DEPTH, FFN_CONV, D_FF), jnp.float32) * FFN_CONV ** -0.5,
        "ffn_conv_b": 0.01 * nrm(ks[18], (DEPTH, D_FF), jnp.float32),
        "ffn_w_down": nrm(ks[19], (DEPTH, D_FF, D_MODEL), jnp.float32) * D_FF ** -0.5 * beta_dn,
        "ln2_g": 1.0 + 0.01 * nrm(ks[20], (DEPTH, D_MODEL), jnp.float32),
        "ln2_b": 0.01 * nrm(ks[21], (DEPTH, D_MODEL), jnp.float32),
    }


def reference(x, meta_tokens, w_in, conv_qkv_w, a_log, dt_bias, lambda_q1, lambda_k1, lambda_q2, lambda_k2,
              diff_norm_w, delta_norm_w, w_out, ln1_g, ln1_b, ffn_w_gate, ffn_w_up, ffn_conv_w, ffn_conv_b,
              ffn_w_down, ln2_g, ln2_b):
    b_, n, d = x.shape
    meta = jnp.broadcast_to(meta_tokens[None].astype(x.dtype), (b_, N_META, d))
    h = jnp.concatenate([meta, x], axis=1)
    pos = jnp.arange(n + N_META, dtype=jnp.float32)
    cid = jnp.concatenate([jnp.full((N_META,), -1, jnp.int32), jnp.arange(n, dtype=jnp.int32) // CHUNK])
    alpha = (2.0 * DEPTH) ** 0.25
    for l in range(DEPTH):
        mix = hybrid_mixer(h, w_in[l], conv_qkv_w[l], a_log[l], dt_bias[l], lambda_q1[l], lambda_k1[l],
                           lambda_q2[l], lambda_k2[l], diff_norm_w[l], delta_norm_w[l], w_out[l], l, cid, pos)
        h = layer_norm(alpha * h + mix, ln1_g[l], ln1_b[l])
        ffn = conv_ffn(h, ffn_w_gate[l], ffn_w_up[l], ffn_conv_w[l], ffn_conv_b[l], ffn_w_down[l])
        h = layer_norm(alpha * h + ffn, ln2_g[l], ln2_b[l])
    return h[:, N_META:]
```

```python
import functools
import math

import jax
import jax.numpy as jnp
from jax import lax
from jax.experimental import pallas as pl
from jax.experimental.pallas import tpu as pltpu

F32 = jnp.float32
BF16 = jnp.bfloat16

CHUNK = 64
ATT_HEADS = 8
ATT_QK_DIM = 64
HEAD_W = 128
DN_HEADS = 8
DN_CONV = 4
FFN_CONV = 3
ROPE_THETA = 10000.0
LN_EPS = 1e-5
RMS_EPS = 1e-6
GROUP_W = ATT_HEADS * HEAD_W
N_GROUPS = 7
TAIL_COLS = 2 * DN_HEADS

LANES = 128
HALO = 8
VMEM_LIMIT = 56 * 1024 * 1024

_NT = (((1,), (1,)), ((), ()))
_TN = (((0,), (0,)), ((), ()))


def _params(*sem):
    return pltpu.CompilerParams(dimension_semantics=sem, vmem_limit_bytes=VMEM_LIMIT)


def _sigmoid(x):
    return 1.0 / (1.0 + jnp.exp(-x))


def _silu(x):
    return x * _sigmoid(x)


def _softplus(x):
    return jnp.maximum(x, 0.0) + jnp.log(1.0 + jnp.exp(-jnp.abs(x)))


def _layer_norm(y, g, b):
    mu = jnp.mean(y, axis=1, keepdims=True)
    d = y - mu
    var = jnp.mean(d * d, axis=1, keepdims=True)
    return d * lax.rsqrt(var + LN_EPS) * g + b


def _proj_kernel(x_ref, w_ref, cos_ref, sin_ref, o_ref, xb_ref, *, heads):
    j = pl.program_id(1)

    @pl.when(j == 0)
    def _cast():
        xb_ref[...] = x_ref[...].astype(BF16)

    acc = jnp.dot(xb_ref[...], w_ref[...], preferred_element_type=F32)

    @pl.when(j < 2)
    def _rope():
        cos = cos_ref[...]
        sin = sin_ref[...]
        lane = lax.broadcasted_iota(jnp.int32, (1, HEAD_W), 1)
        first_half = (lane % ATT_QK_DIM) < (ATT_QK_DIM // 2)
        scale = jnp.where(j == 0, ATT_QK_DIM ** -0.5, 1.0).astype(F32)
        for h in range(heads):
            blk = acc[:, h * HEAD_W:(h + 1) * HEAD_W]
            partner = jnp.where(first_half,
                                pltpu.roll(blk, HEAD_W - ATT_QK_DIM // 2, 1),
                                pltpu.roll(blk, ATT_QK_DIM // 2, 1))
            o_ref[h] = ((blk * cos + partner * sin) * scale).astype(BF16)

    @pl.when(j >= 2)
    def _plain():
        for h in range(heads):
            o_ref[h] = acc[:, h * HEAD_W:(h + 1) * HEAD_W].astype(BF16)


def _proj(x2d, w_main, cos, sin, bm):
    m, d = x2d.shape
    n = w_main.shape[1]
    heads = GROUP_W // HEAD_W
    return pl.pallas_call(
        functools.partial(_proj_kernel, heads=heads),
        out_shape=jax.ShapeDtypeStruct((n // HEAD_W, m, HEAD_W), BF16),
        grid=(m // bm, n // GROUP_W),
        in_specs=[
            pl.BlockSpec((bm, d), lambda i, j: (i, 0)),
            pl.BlockSpec((d, GROUP_W), lambda i, j: (0, j)),
            pl.BlockSpec((bm, HEAD_W), lambda i, j: (i, 0)),
            pl.BlockSpec((bm, HEAD_W), lambda i, j: (i, 0)),
        ],
        out_specs=pl.BlockSpec((heads, bm, HEAD_W), lambda i, j: (j, i, 0)),
        scratch_shapes=[pltpu.VMEM((bm, d), BF16)],
        compiler_params=_params("parallel", "arbitrary"),
        name="proj",
    )(x2d, w_main, cos, sin)


def _tail_kernel(x_ref, w_ref, o_ref):
    x = x_ref[...]
    w = w_ref[...]
    xh = x.astype(BF16)
    xl = (x - xh.astype(F32)).astype(BF16)
    wh = w.astype(BF16)
    wl = (w - wh.astype(F32)).astype(BF16)
    o_ref[...] = (jnp.dot(xh, wh, preferred_element_type=F32)
                  + jnp.dot(xh, wl, preferred_element_type=F32)
                  + jnp.dot(xl, wh, preferred_element_type=F32))


def _tail(x2d, w_tail, bm):
    m, d = x2d.shape
    return pl.pallas_call(
        _tail_kernel,
        out_shape=jax.ShapeDtypeStruct((m, LANES), F32),
        grid=(m // bm,),
        in_specs=[pl.BlockSpec((bm, d), lambda i: (i, 0)),
                  pl.BlockSpec((d, LANES), lambda i: (0, 0))],
        out_specs=pl.BlockSpec((bm, LANES), lambda i: (i, 0)),
        compiler_params=_params("parallel"),
        name="tail",
    )(x2d, w_tail)


def _attn_kernel(*refs, tq, lam_init, meta_only):
    if meta_only:
        q_ref, km_ref, vm_ref, lq1, lk1, lq2, lk2, nw_ref, o_ref, m_sc, l_sc, acc_sc = refs
    else:
        (q_ref, k_ref, v_ref, km_ref, vm_ref, lq1, lk1, lq2, lk2, nw_ref, o_ref,
         m_sc, l_sc, acc_sc) = refs
    q = q_ref[...]
    lane = lax.broadcasted_iota(jnp.int32, (1, HEAD_W), 1)
    zero = jnp.zeros_like(q)
    qc = (jnp.where(lane < ATT_QK_DIM, q, zero), jnp.where(lane >= ATT_QK_DIM, q, zero))

    km = km_ref[...]
    vm = vm_ref[...]
    for c in range(2):
        s = lax.dot_general(qc[c], km, _NT, preferred_element_type=F32)
        m0 = jnp.max(s, axis=1, keepdims=True)
        p = jnp.exp(s - m0)
        m_sc[c] = m0
        l_sc[c] = jnp.sum(p, axis=1, keepdims=True)
        acc_sc[c] = jnp.dot(p.astype(BF16), vm, preferred_element_type=F32)

    def update(kt, vt, mask):
        for c in range(2):
            s = lax.dot_general(qc[c], kt, _NT, preferred_element_type=F32)
            if mask is not None:
                s = jnp.where(mask, s, -jnp.inf)
            m_old = m_sc[c]
            m_new = jnp.maximum(m_old, jnp.max(s, axis=1, keepdims=True))
            a = jnp.exp(m_old - m_new)
            p = jnp.exp(s - m_new)
            l_sc[c] = a * l_sc[c] + jnp.sum(p, axis=1, keepdims=True)
            acc_sc[c] = a * acc_sc[c] + jnp.dot(p.astype(BF16), vt, preferred_element_type=F32)
            m_sc[c] = m_new

    if not meta_only:
        qi = pl.program_id(2)

        def body(j, carry):
            off = pl.multiple_of(j * tq, tq)
            update(k_ref[pl.ds(off, tq), :], v_ref[pl.ds(off, tq), :], None)
            return carry

        lax.fori_loop(0, qi, body, 0)
        off = pl.multiple_of(qi * tq, tq)
        rows = lax.broadcasted_iota(jnp.int32, (tq, tq), 0)
        cols = lax.broadcasted_iota(jnp.int32, (tq, tq), 1)
        update(k_ref[pl.ds(off, tq), :], v_ref[pl.ds(off, tq), :], (rows // CHUNK) >= (cols // CHUNK))

    lam = (jnp.exp(jnp.sum(lq1[...] * lk1[...], axis=1, keepdims=True))
           - jnp.exp(jnp.sum(lq2[...] * lk2[...], axis=1, keepdims=True)) + lam_init)
    o = acc_sc[0] / l_sc[0] - lam * (acc_sc[1] / l_sc[1])
    ms = jnp.mean(o * o, axis=1, keepdims=True)
    o = o * lax.rsqrt(ms + RMS_EPS) * nw_ref[...] * (1.0 - lam_init)
    o_ref[...] = o.astype(BF16)


def _attention(p_q, p_kv, p_meta, lams, norm_w, *, batch, seq, tq, lam_init, meta_only):
    n_meta = p_meta.shape[1]
    rows = p_q.shape[1]
    nq = 1 if meta_only else seq // tq
    h_k, h_v = ATT_HEADS, 2 * ATT_HEADS
    small = [pl.BlockSpec((1, ATT_QK_DIM), lambda b, h, i: (0, 0))] * 4 + [
        pl.BlockSpec((1, HEAD_W), lambda b, h, i: (0, 0))]
    meta_specs = [pl.BlockSpec((None, n_meta, HEAD_W), lambda b, h, i: (h_k + h, 0, 0)),
                  pl.BlockSpec((None, n_meta, HEAD_W), lambda b, h, i: (h_v + h, 0, 0))]
    q_spec = pl.BlockSpec((None, tq, HEAD_W), lambda b, h, i: (h, b * nq + i, 0))
    if meta_only:
        in_specs = [q_spec] + meta_specs + small
        args = (p_q, p_meta, p_meta) + lams + (norm_w,)
    else:
        kv_specs = [pl.BlockSpec((None, seq, HEAD_W), lambda b, h, i: (h_k + h, b, 0)),
                    pl.BlockSpec((None, seq, HEAD_W), lambda b, h, i: (h_v + h, b, 0))]
        in_specs = [q_spec] + kv_specs + meta_specs + small
        args = (p_q, p_kv, p_kv, p_meta, p_meta) + lams + (norm_w,)
    return pl.pallas_call(
        functools.partial(_attn_kernel, tq=tq, lam_init=lam_init, meta_only=meta_only),
        out_shape=jax.ShapeDtypeStruct((rows, GROUP_W), BF16),
        grid=(batch, ATT_HEADS, nq),
        in_specs=in_specs,
        out_specs=pl.BlockSpec((tq, HEAD_W), lambda b, h, i: (b * nq + i, h)),
        scratch_shapes=[pltpu.VMEM((2, tq, 1), F32), pltpu.VMEM((2, tq, 1), F32),
                        pltpu.VMEM((2, tq, HEAD_W), F32)],
        compiler_params=_params("parallel", "parallel", "arbitrary"),
        name="attn_meta" if meta_only else "attn",
    )(*args)


def _gdn_prep_kernel(q_ref, k_ref, v_ref, hq_ref, hk_ref, hv_ref, mh_ref, cw_ref, ba_ref, alog_ref,
                     dtb_ref, u_ref, wq_ref, kd_ref, a_ref, gl_ref, xs_sc, *, rows, n_pad):
    first = pl.program_id(1) == 0
    ba = ba_ref[...]
    beta_all = _sigmoid(ba)
    g_all = -jnp.exp(alog_ref[...]) * _softplus(ba + dtb_ref[...])
    if n_pad:
        valid = lax.broadcasted_iota(jnp.int32, (rows, 1), 0) >= n_pad
        beta_all = jnp.where(valid, beta_all, 0.0)
        g_all = jnp.where(valid, g_all, 0.0)

    ri = lax.broadcasted_iota(jnp.int32, (rows, rows), 0)
    ci = lax.broadcasted_iota(jnp.int32, (rows, rows), 1)
    same = (ri // CHUNK) == (ci // CHUNK)
    tril = jnp.logical_and(same, ci <= ri)
    strict = jnp.logical_and(same, ci < ri)
    ones_tril = jnp.where(tril, 1.0, 0.0).astype(F32)
    ones_triu = jnp.where(jnp.logical_and(same, ri <= ci), 1.0, 0.0).astype(F32)
    ones_same = jnp.where(same, 1.0, 0.0).astype(F32)
    hp = lax.Precision.HIGHEST
    gc_all = jnp.dot(ones_tril, g_all, precision=hp, preferred_element_type=F32)
    gl_all = jnp.dot(ones_same, g_all, precision=hp, preferred_element_type=F32)
    gct_all = lax.dot_general(g_all, ones_triu, _TN, precision=hp, preferred_element_type=F32)
    gl_ref[...] = gl_all

    def conv_act(main_ref, halo_ref, stream, h):
        halo = jnp.where(first, mh_ref[stream * DN_HEADS + h], halo_ref[h])
        xs_sc[stream, 0:HALO, :] = halo.astype(F32)
        xs_sc[stream, HALO:, :] = main_ref[h].astype(F32)
        w = cw_ref[stream * DN_HEADS + h]
        y = xs_sc[stream, pl.ds(HALO - DN_CONV + 1, rows), :] * w[0:1]
        for j in range(1, DN_CONV):
            y = y + xs_sc[stream, pl.ds(HALO - DN_CONV + 1 + j, rows), :] * w[j:j + 1]
        return _silu(y)

    for h in range(DN_HEADS):
        q = conv_act(q_ref, hq_ref, 0, h)
        k = conv_act(k_ref, hk_ref, 1, h)
        v = conv_act(v_ref, hv_ref, 2, h)
        q = q * lax.rsqrt(jnp.sum(q * q, axis=1, keepdims=True) + RMS_EPS) * (HEAD_W ** -0.5)
        k = k * lax.rsqrt(jnp.sum(k * k, axis=1, keepdims=True) + RMS_EPS)
        beta = beta_all[:, h:h + 1]
        gc = gc_all[:, DN_HEADS + h:DN_HEADS + h + 1]
        gl = gl_all[:, DN_HEADS + h:DN_HEADS + h + 1]
        gcr = gct_all[DN_HEADS + h:DN_HEADS + h + 1, :]
        egc = jnp.exp(gc)
        kb = k * beta
        k16 = k.astype(BF16)
        kk = lax.dot_general(kb.astype(BF16), k16, _NT, preferred_element_type=F32)
        qk = lax.dot_general(q.astype(BF16), k16, _NT, preferred_element_type=F32)
        dec = jnp.exp(jnp.where(tril, gc - gcr, -jnp.inf))
        a_ref[h] = (qk * dec).astype(BF16)
        pw = jnp.where(strict, kk * dec, 0.0)
        x = jnp.concatenate([v * beta, kb * egc], axis=1)
        x = x - jnp.dot(pw.astype(BF16), x.astype(BF16), preferred_element_type=F32)
        for _ in range(int(math.log2(CHUNK)) - 1):
            p16 = pw.astype(BF16)
            pw = jnp.dot(p16, p16, preferred_element_type=F32)
            x = x + jnp.dot(pw.astype(BF16), x.astype(BF16), preferred_element_type=F32)
        u_ref[h] = x[:, :HEAD_W].astype(BF16)
        w16 = x[:, HEAD_W:].astype(BF16)
        qd16 = (q * egc).astype(BF16)
        for c in range(rows // CHUNK):
            wq_ref[h, 2 * c * CHUNK:(2 * c + 1) * CHUNK, :] = w16[c * CHUNK:(c + 1) * CHUNK]
            wq_ref[h, (2 * c + 1) * CHUNK:(2 * c + 2) * CHUNK, :] = qd16[c * CHUNK:(c + 1) * CHUNK]
        kd_ref[h] = (k * jnp.exp(gl - gc)).astype(BF16)


def _gdn_prep(p, p_halo_src, meta_halo, conv_w, ba, alog, dtb, *, batch, seq, rows, n_pad):
    m = batch * seq
    nt = seq // rows
    g_q, g_k, g_v = 3, 4, 5
    hb = rows // HALO

    def main_spec(g):
        return pl.BlockSpec((DN_HEADS, rows, HEAD_W), lambda b, t: (g, b * nt + t, 0))

    def halo_spec(g):
        return pl.BlockSpec((DN_HEADS, HALO, HEAD_W),
                            lambda b, t: (g, jnp.maximum((b * nt + t) * hb - 1, 0), 0))

    full = lambda shape: pl.BlockSpec(shape, lambda b, t: (0,) * len(shape))
    out_shape = (jax.ShapeDtypeStruct((DN_HEADS, m, HEAD_W), BF16),
                 jax.ShapeDtypeStruct((DN_HEADS, 2 * m, HEAD_W), BF16),
                 jax.ShapeDtypeStruct((DN_HEADS, m, HEAD_W), BF16),
                 jax.ShapeDtypeStruct((DN_HEADS, m, rows), BF16),
                 jax.ShapeDtypeStruct((m, LANES), F32))
    out_specs = (pl.BlockSpec((DN_HEADS, rows, HEAD_W), lambda b, t: (0, b * nt + t, 0)),
                 pl.BlockSpec((DN_HEADS, 2 * rows, HEAD_W), lambda b, t: (0, b * nt + t, 0)),
                 pl.BlockSpec((DN_HEADS, rows, HEAD_W), lambda b, t: (0, b * nt + t, 0)),
                 pl.BlockSpec((DN_HEADS, rows, rows), lambda b, t: (0, b * nt + t, 0)),
                 pl.BlockSpec((rows, LANES), lambda b, t: (b * nt + t, 0)))
    return pl.pallas_call(
        functools.partial(_gdn_prep_kernel, rows=rows, n_pad=n_pad),
        out_shape=out_shape,
        grid=(batch, nt),
        in_specs=[main_spec(g_q), main_spec(g_k), main_spec(g_v),
                  halo_spec(g_q), halo_spec(g_k), halo_spec(g_v),
                  full(meta_halo.shape), full(conv_w.shape),
                  pl.BlockSpec((rows, LANES), lambda b, t: (b * nt + t, 0)),
                  full(alog.shape), full(dtb.shape)],
        out_specs=out_specs,
        scratch_shapes=[pltpu.VMEM((3, rows + HALO, HEAD_W), F32)],
        compiler_params=_params("parallel", "parallel"),
        name="gdn_prep",
    )(p, p, p, p_halo_src, p_halo_src, p_halo_src, meta_halo, conv_w, ba, alog, dtb)


def _gdn_scan_kernel(u_ref, wq_ref, kd_ref, a_ref, gl_ref, z_ref,
                     um_ref, wqm_ref, kdm_ref, am_ref, glm_ref, zm_ref, nw_ref,
                     o_ref, om_ref, s_sc, vbuf_sc, *, rows, hpg):
    hg = pl.program_id(1)
    t = pl.program_id(2)
    lane = lax.broadcasted_iota(jnp.int32, (1, LANES), 1)
    nw = nw_ref[...]

    def decay_total(gl_row, h):
        sel = lane == DN_HEADS + hg * hpg + h
        return jnp.sum(jnp.where(sel, jnp.exp(gl_row), 0.0), axis=1, keepdims=True)

    def gate(o, z):
        ms = jnp.mean(o * o, axis=1, keepdims=True)
        return (o * lax.rsqrt(ms + RMS_EPS) * nw * _silu(z.astype(F32))).astype(BF16)

    def chunk_step(h, u, wq, kd, egl, intra):
        s = s_sc[h]
        sw = jnp.dot(wq, s.astype(BF16), preferred_element_type=F32)
        v_new = u.astype(F32) - sw[:CHUNK]
        v16 = v_new.astype(BF16)
        o = sw[CHUNK:] + intra(v16)
        s_sc[h] = egl * s + lax.dot_general(kd, v16, _TN, preferred_element_type=F32)
        return o

    @pl.when(t == 0)
    def _meta_chunk():
        vbuf_sc[...] = jnp.zeros_like(vbuf_sc)
        glm = glm_ref[0:1, :]
        for h in range(hpg):
            s_sc[h] = jnp.zeros((HEAD_W, HEAD_W), F32)
            o = chunk_step(h, um_ref[h], wqm_ref[h], kdm_ref[h], decay_total(glm, h),
                           lambda v16, h=h: jnp.dot(am_ref[h], v16, preferred_element_type=F32))
            om_ref[:, h * HEAD_W:(h + 1) * HEAD_W] = gate(o, zm_ref[h])

    for c in range(rows // CHUNK):
        r0 = c * CHUNK
        gl_row = gl_ref[r0:r0 + 1, :]
        for h in range(hpg):
            def intra(v16, h=h, r0=r0):
                vbuf_sc[h, r0:r0 + CHUNK, :] = v16
                return jnp.dot(a_ref[h, r0:r0 + CHUNK, :], vbuf_sc[h], preferred_element_type=F32)

            o = chunk_step(h, u_ref[h, r0:r0 + CHUNK, :], wq_ref[h, 2 * r0:2 * r0 + 2 * CHUNK, :],
                           kd_ref[h, r0:r0 + CHUNK, :], decay_total(gl_row, h), intra)
            o_ref[r0:r0 + CHUNK, h * HEAD_W:(h + 1) * HEAD_W] = gate(o, z_ref[h, r0:r0 + CHUNK, :])


def _gdn_scan(real, meta, p, p_meta_pad, norm_w, *, batch, seq, rows, hpg):
    u, wq, kd, a, gl = real
    um, wqm, kdm, am, glm = meta
    m = batch * seq
    nt = seq // rows
    ngrp = DN_HEADS // hpg
    z_blk = (6 * DN_HEADS) // hpg
    mrows = um.shape[1]

    def real_spec(r, w):
        return pl.BlockSpec((hpg, r, w), lambda b, g, t: (g, b * nt + t, 0))

    def meta_spec(r, w):
        return pl.BlockSpec((hpg, r, w), lambda b, g, t: (g, 0, 0))

    return pl.pallas_call(
        functools.partial(_gdn_scan_kernel, rows=rows, hpg=hpg),
        out_shape=(jax.ShapeDtypeStruct((m, GROUP_W), BF16),
                   jax.ShapeDtypeStruct((batch, mrows, GROUP_W), BF16)),
        grid=(batch, ngrp, nt),
        in_specs=[real_spec(rows, HEAD_W), real_spec(2 * rows, HEAD_W), real_spec(rows, HEAD_W),
                  real_spec(rows, rows),
                  pl.BlockSpec((rows, LANES), lambda b, g, t: (b * nt + t, 0)),
                  pl.BlockSpec((hpg, rows, HEAD_W), lambda b, g, t: (z_blk + g, b * nt + t, 0)),
                  meta_spec(mrows, HEAD_W), meta_spec(2 * mrows, HEAD_W), meta_spec(mrows, HEAD_W),
                  meta_spec(mrows, mrows),
                  pl.BlockSpec((mrows, LANES), lambda b, g, t: (0, 0)),
                  pl.BlockSpec((hpg, mrows, HEAD_W), lambda b, g, t: (z_blk + g, 0, 0)),
                  pl.BlockSpec((1, HEAD_W), lambda b, g, t: (0, 0))],
        out_specs=(pl.BlockSpec((rows, hpg * HEAD_W), lambda b, g, t: (b * nt + t, g)),
                   pl.BlockSpec((None, mrows, hpg * HEAD_W), lambda b, g, t: (b, 0, g))),
        scratch_shapes=[pltpu.VMEM((hpg, HEAD_W, HEAD_W), F32),
                        pltpu.VMEM((hpg, rows, HEAD_W), BF16)],
        compiler_params=_params("parallel", "parallel", "arbitrary"),
        name="gdn_scan",
    )(u, wq, kd, a, gl, p, um, wqm, kdm, am, glm, p_meta_pad, norm_w)


def _wout_ln_kernel(oa_ref, od_ref, w_ref, x_ref, g_ref, b_ref, h_ref, *, alpha):
    half = oa_ref.shape[1]
    mix = (jnp.dot(oa_ref[...], w_ref[0:half, :], preferred_element_type=F32)
           + jnp.dot(od_ref[...], w_ref[half:, :], preferred_element_type=F32))
    h_ref[...] = _layer_norm(alpha * x_ref[...] + mix, g_ref[...], b_ref[...])


def _wout_ln(o_att, o_dn, w_out, x2d, g, b, *, bm, alpha):
    m, d = x2d.shape
    half = o_att.shape[1]
    return pl.pallas_call(
        functools.partial(_wout_ln_kernel, alpha=alpha),
        out_shape=jax.ShapeDtypeStruct((m, d), F32),
        grid=(m // bm,),
        in_specs=[pl.BlockSpec((bm, half), lambda i: (i, 0)),
                  pl.BlockSpec((bm, half), lambda i: (i, 0)),
                  pl.BlockSpec((2 * half, d), lambda i: (0, 0)),
                  pl.BlockSpec((bm, d), lambda i: (i, 0)),
                  pl.BlockSpec((1, d), lambda i: (0, 0)),
                  pl.BlockSpec((1, d), lambda i: (0, 0))],
        out_specs=pl.BlockSpec((bm, d), lambda i: (i, 0)),
        compiler_params=_params("parallel"),
        name="wout_ln",
    )(o_att, o_dn, w_out, x2d, g, b)


def _ffn_kernel(h_ref, hprev_ref, hmeta_ref, wg_ref, wu_ref, cw_ref, cb_ref, wd_ref, g_ref, b_ref,
                o_ref, xb_sc, gate_sc, acc_sc, *, bm, tiles_per_batch, alpha):
    i = pl.program_id(0)
    f = pl.program_id(1)

    @pl.when(f == 0)
    def _init():
        halo = jnp.where(i % tiles_per_batch == 0, hmeta_ref[...], hprev_ref[...])
        xb_sc[0:HALO, :] = halo.astype(BF16)
        xb_sc[HALO:, :] = h_ref[...].astype(BF16)
        acc_sc[...] = jnp.zeros_like(acc_sc)

    gate_sc[...] = jnp.dot(xb_sc[...], wg_ref[...], preferred_element_type=F32)
    up = jnp.dot(xb_sc[HALO:, :], wu_ref[...], preferred_element_type=F32)
    cw = cw_ref[...]
    hg = cb_ref[...] + gate_sc[pl.ds(HALO - FFN_CONV + 1, bm), :] * cw[0:1]
    for j in range(1, FFN_CONV):
        hg = hg + gate_sc[pl.ds(HALO - FFN_CONV + 1 + j, bm), :] * cw[j:j + 1]
    act = (_silu(hg) * up).astype(BF16)
    acc_sc[...] += jnp.dot(act, wd_ref[...], preferred_element_type=F32)

    @pl.when(f == pl.num_programs(1) - 1)
    def _finish():
        o_ref[...] = _layer_norm(alpha * h_ref[...] + acc_sc[...], g_ref[...], b_ref[...])


def _ffn(h1, h1_meta, w_gate, w_up, conv_w, conv_b, w_down, g, b, *, seq, bm, tf, alpha):
    m, d = h1.shape
    dff = w_gate.shape[1]
    tiles_per_batch = seq // bm
    hb = bm // HALO
    meta_blk = h1_meta.shape[0] // HALO - 1
    return pl.pallas_call(
        functools.partial(_ffn_kernel, bm=bm, tiles_per_batch=tiles_per_batch, alpha=alpha),
        out_shape=jax.ShapeDtypeStruct((m, d), F32),
        grid=(m // bm, dff // tf),
        in_specs=[pl.BlockSpec((bm, d), lambda i, f: (i, 0)),
                  pl.BlockSpec((HALO, d), lambda i, f: (jnp.maximum(i * hb - 1, 0), 0)),
                  pl.BlockSpec((HALO, d), lambda i, f: (meta_blk, 0)),
                  pl.BlockSpec((d, tf), lambda i, f: (0, f)),
                  pl.BlockSpec((d, tf), lambda i, f: (0, f)),
                  pl.BlockSpec((FFN_CONV, tf), lambda i, f: (0, f)),
                  pl.BlockSpec((1, tf), lambda i, f: (0, f)),
                  pl.BlockSpec((tf, d), lambda i, f: (f, 0)),
                  pl.BlockSpec((1, d), lambda i, f: (0, 0)),
                  pl.BlockSpec((1, d), lambda i, f: (0, 0))],
        out_specs=pl.BlockSpec((bm, d), lambda i, f: (i, 0)),
        scratch_shapes=[pltpu.VMEM((bm + HALO, d), BF16),
                        pltpu.VMEM((bm + HALO, tf), F32),
                        pltpu.VMEM((bm, d), F32)],
        compiler_params=_params("parallel", "arbitrary"),
        name="ffn",
    )(h1, h1, h1_meta, w_gate, w_up, conv_w, conv_b, w_down, g, b)


def _largest_tile(n, cap, mult):
    best = None
    for c in range(mult, min(n, cap) + 1, mult):
        if n % c == 0:
            best = c
    assert best is not None, (n, cap, mult)
    return best


def _rope_tables(n_pos):
    half = ATT_QK_DIM // 2
    inv_freq = ROPE_THETA ** (-jnp.arange(0, ATT_QK_DIM, 2, dtype=F32) / ATT_QK_DIM)
    ang = jnp.arange(n_pos, dtype=F32)[:, None] * inv_freq[None, :]
    cos, sin = jnp.cos(ang), jnp.sin(ang)
    reps = HEAD_W // ATT_QK_DIM
    cos_t = jnp.tile(jnp.concatenate([cos, cos], axis=1), (1, reps))
    sin_t = jnp.tile(jnp.concatenate([-sin, sin], axis=1), (1, reps))
    assert cos_t.shape[1] == HEAD_W and half * 2 == ATT_QK_DIM
    return cos_t, sin_t


def kernel(x, meta_tokens, w_in, conv_qkv_w, a_log, dt_bias, lambda_q1, lambda_k1, lambda_q2, lambda_k2,
           diff_norm_w, delta_norm_w, w_out, ln1_g, ln1_b, ffn_w_gate, ffn_w_up, ffn_conv_w, ffn_conv_b,
           ffn_w_down, ln2_g, ln2_b):
    batch, seq, d = x.shape
    n_meta = meta_tokens.shape[0]
    depth = w_in.shape[0]
    n_main = N_GROUPS * GROUP_W
    assert w_in.shape[2] == n_main + TAIL_COLS and n_meta % HALO == 0 and n_meta <= CHUNK
    m = batch * seq
    alpha = (2.0 * depth) ** 0.25

    bm_proj = _largest_tile(m, 1024, HALO)
    bm_row = _largest_tile(seq, 512, HALO)
    tq = _largest_tile(seq, 256, CHUNK)
    rows = _largest_tile(seq, 256, CHUNK)
    hpg = 4

    cos_t, sin_t = _rope_tables(n_meta + seq)
    cos_meta, sin_meta = cos_t[:n_meta], sin_t[:n_meta]
    cos_real = jnp.tile(cos_t[n_meta:], (batch, 1))
    sin_real = jnp.tile(sin_t[n_meta:], (batch, 1))

    h = x.reshape(m, d)
    h_meta = meta_tokens.astype(x.dtype)
    pad_rows = CHUNK - n_meta
    for l in range(depth):
        w_main = w_in[l, :, :n_main].astype(BF16)
        w_tail = jnp.pad(w_in[l, :, n_main:], ((0, 0), (0, LANES - TAIL_COLS)))
        lane_pad = (DN_HEADS, LANES - TAIL_COLS)
        alog = jnp.pad(a_log[l].astype(F32), lane_pad)[None]
        dtb = jnp.pad(dt_bias[l].astype(F32), lane_pad)[None]
        conv_w = conv_qkv_w[l].reshape(DN_CONV, 3 * DN_HEADS, HEAD_W).transpose(1, 0, 2)
        lams = tuple(v[l][None].astype(F32) for v in (lambda_q1, lambda_k1, lambda_q2, lambda_k2))
        lam_init = 0.8 - 0.6 * math.exp(-0.3 * l)

        p = _proj(h, w_main, cos_real, sin_real, bm_proj)
        p_meta = _proj(h_meta, w_main, cos_meta, sin_meta, n_meta)
        ba = _tail(h, w_tail, bm_proj)
        ba_meta = _tail(h_meta, w_tail, n_meta)

        o_att = _attention(p, p, p_meta, lams, diff_norm_w[l][None], batch=batch, seq=seq, tq=tq,
                           lam_init=lam_init, meta_only=False)
        o_att_meta = _attention(p_meta, None, p_meta, lams, diff_norm_w[l][None], batch=1, seq=n_meta,
                                tq=n_meta, lam_init=lam_init, meta_only=True)

        p_meta_pad = jnp.pad(p_meta, ((0, 0), (pad_rows, 0), (0, 0)))
        ba_meta_pad = jnp.pad(ba_meta, ((pad_rows, 0), (0, 0)))
        meta_halo = p_meta[3 * DN_HEADS:6 * DN_HEADS, n_meta - HALO:, :]
        zero_halo = jnp.zeros_like(meta_halo)
        prep_real = _gdn_prep(p, p, meta_halo, conv_w, ba, alog, dtb, batch=batch, seq=seq, rows=rows,
                              n_pad=0)
        prep_meta = _gdn_prep(p_meta_pad, p_meta_pad, zero_halo, conv_w, ba_meta_pad, alog, dtb, batch=1,
                              seq=CHUNK, rows=CHUNK, n_pad=pad_rows)
        o_dn, o_dn_meta = _gdn_scan(prep_real, prep_meta, p, p_meta_pad, delta_norm_w[l][None],
                                    batch=batch, seq=seq, rows=rows, hpg=hpg)
        o_dn_meta = o_dn_meta[0, pad_rows:]

        w_out_b = w_out[l].astype(BF16)
        g1, b1 = ln1_g[l][None].astype(F32), ln1_b[l][None].astype(F32)
        h1 = _wout_ln(o_att, o_dn, w_out_b, h, g1, b1, bm=bm_row, alpha=alpha)
        h1_meta = _wout_ln(o_att_meta, o_dn_meta, w_out_b, h_meta, g1, b1, bm=n_meta, alpha=alpha)

        dff = ffn_w_gate.shape[2]
        tf = _largest_tile(dff, 512, LANES)
        ffn_args = (ffn_w_gate[l].astype(BF16), ffn_w_up[l].astype(BF16), ffn_conv_w[l].astype(F32),
                    ffn_conv_b[l][None].astype(F32), ffn_w_down[l].astype(BF16),
                    ln2_g[l][None].astype(F32), ln2_b[l][None].astype(F32))
        h_next = _ffn(h1, h1_meta, *ffn_args, seq=seq, bm=bm_row, tf=tf, alpha=alpha)
        if l + 1 < depth:
            zero_meta = jnp.zeros_like(h1_meta)
            h_meta = _ffn(h1_meta, zero_meta, *ffn_args, seq=n_meta, bm=n_meta, tf=tf, alpha=alpha)
        h = h_next
    return h.reshape(batch, seq, d)
```

```python
import functools
import math

import jax
import jax.numpy as jnp
from jax import lax
from jax.experimental import pallas as pl
from jax.experimental.pallas import tpu as pltpu

F32 = jnp.float32
BF16 = jnp.bfloat16

CHUNK = 64
ATT_HEADS = 8
ATT_QK_DIM = 64
HEAD_W = 128
DN_HEADS = 8
DN_CONV = 4
FFN_CONV = 3
ROPE_THETA = 10000.0
LN_EPS = 1e-5
RMS_EPS = 1e-6
GROUP_W = ATT_HEADS * HEAD_W
N_GROUPS = 7
TAIL_COLS = 2 * DN_HEADS

LANES = 128
HALO = 8
VMEM_LIMIT = 56 * 1024 * 1024

_NT = (((1,), (1,)), ((), ()))
_TN = (((0,), (0,)), ((), ()))


def _params(*sem):
    return pltpu.CompilerParams(dimension_semantics=sem, vmem_limit_bytes=VMEM_LIMIT)


def _sigmoid(x):
    return 1.0 / (1.0 + jnp.exp(-x))


def _silu(x):
    return x * _sigmoid(x)


def _softplus(x):
    return jnp.maximum(x, 0.0) + jnp.log(1.0 + jnp.exp(-jnp.abs(x)))


def _layer_norm(y, g, b):
    mu = jnp.mean(y, axis=1, keepdims=True)
    d = y - mu
    var = jnp.mean(d * d, axis=1, keepdims=True)
    return d * lax.rsqrt(var + LN_EPS) * g + b


def _proj_kernel(x_ref, w_ref, cos_ref, sin_ref, o_ref, xb_ref, *, heads):
    j = pl.program_id(1)

    @pl.when(j == 0)
    def _cast():
        xb_ref[...] = x_ref[...].astype(BF16)

    acc = jnp.dot(xb_ref[...], w_ref[...], preferred_element_type=F32)

    @pl.when(j < 2)
    def _rope():
        cos = cos_ref[...]
        sin = sin_ref[...]
        lane = lax.broadcasted_iota(jnp.int32, (1, HEAD_W), 1)
        first_half = (lane % ATT_QK_DIM) < (ATT_QK_DIM // 2)
        scale = jnp.where(j == 0, ATT_QK_DIM ** -0.5, 1.0).astype(F32)
        for h in range(heads):
            blk = acc[:, h * HEAD_W:(h + 1) * HEAD_W]
            partner = jnp.where(first_half,
                                pltpu.roll(blk, HEAD_W - ATT_QK_DIM // 2, 1),
                                pltpu.roll(blk, ATT_QK_DIM // 2, 1))
            o_ref[h] = ((blk * cos + partner * sin) * scale).astype(BF16)

    @pl.when(j >= 2)
    def _plain():
        for h in range(heads):
            o_ref[h] = acc[:, h * HEAD_W:(h + 1) * HEAD_W].astype(BF16)


def _proj(x2d, w_main, cos, sin, bm):
    m, d = x2d.shape
    n = w_main.shape[1]
    heads = GROUP_W // HEAD_W
    return pl.pallas_call(
        functools.partial(_proj_kernel, heads=heads),
        out_shape=jax.ShapeDtypeStruct((n // HEAD_W, m, HEAD_W), BF16),
        grid=(m // bm, n // GROUP_W),
        in_specs=[
            pl.BlockSpec((bm, d), lambda i, j: (i, 0)),
            pl.BlockSpec((d, GROUP_W), lambda i, j: (0, j)),
            pl.BlockSpec((bm, HEAD_W), lambda i, j: (i, 0)),
            pl.BlockSpec((bm, HEAD_W), lambda i, j: (i, 0)),
        ],
        out_specs=pl.BlockSpec((heads, bm, HEAD_W), lambda i, j: (j, i, 0)),
        scratch_shapes=[pltpu.VMEM((bm, d), BF16)],
        compiler_params=_params("parallel", "arbitrary"),
        name="proj",
    )(x2d, w_main, cos, sin)


def _tail_kernel(x_ref, w_ref, o_ref):
    x = x_ref[...]
    w = w_ref[...]
    xh = x.astype(BF16)
    xl = (x - xh.astype(F32)).astype(BF16)
    wh = w.astype(BF16)
    wl = (w - wh.astype(F32)).astype(BF16)
    o_ref[...] = (jnp.dot(xh, wh, preferred_element_type=F32)
                  + jnp.dot(xh, wl, preferred_element_type=F32)
                  + jnp.dot(xl, wh, preferred_element_type=F32))


def _tail(x2d, w_tail, bm):
    m, d = x2d.shape
    return pl.pallas_call(
        _tail_kernel,
        out_shape=jax.ShapeDtypeStruct((m, LANES), F32),
        grid=(m // bm,),
        in_specs=[pl.BlockSpec((bm, d), lambda i: (i, 0)),
                  pl.BlockSpec((d, LANES), lambda i: (0, 0))],
        out_specs=pl.BlockSpec((bm, LANES), lambda i: (i, 0)),
        compiler_params=_params("parallel"),
        name="tail",
    )(x2d, w_tail)


def _attn_kernel(*refs, tq, hps, lam_init, meta_only):
    if meta_only:
        q_ref, km_ref, vm_ref, lq1, lk1, lq2, lk2, nw_ref, o_ref, m_sc, l_sc, acc_sc = refs
    else:
        (q_ref, k_ref, v_ref, km_ref, vm_ref, lq1, lk1, lq2, lk2, nw_ref, o_ref,
         m_sc, l_sc, acc_sc, s_sc) = refs
    lane = lax.broadcasted_iota(jnp.int32, (1, HEAD_W), 1)
    qc = []
    for h in range(hps):
        q = q_ref[h]
        zero = jnp.zeros_like(q)
        qc.append((jnp.where(lane < ATT_QK_DIM, q, zero), jnp.where(lane >= ATT_QK_DIM, q, zero)))

    for h in range(hps):
        for c in range(2):
            s = lax.dot_general(km_ref[h], qc[h][c], _NT, preferred_element_type=F32)
            m0 = jnp.max(s, axis=0, keepdims=True)
            p = jnp.exp(s - m0)
            m_sc[2 * h + c] = m0
            l_sc[2 * h + c] = jnp.sum(p, axis=0, keepdims=True)
            acc_sc[2 * h + c] = lax.dot_general(vm_ref[h], p.astype(BF16), _TN,
                                                preferred_element_type=F32)

    nc = 2 * hps

    def scores(t):
        off = pl.multiple_of(t * tq, tq)
        base = (t % 2) * nc
        for h in range(hps):
            kt = k_ref[h, pl.ds(off, tq), :]
            for c in range(2):
                s_sc[base + 2 * h + c] = lax.dot_general(kt, qc[h][c], _NT,
                                                         preferred_element_type=F32)

    def absorb(t, mask):
        off = pl.multiple_of(t * tq, tq)
        base = (t % 2) * nc
        for h in range(hps):
            vt = v_ref[h, pl.ds(off, tq), :]
            for c in range(2):
                i = 2 * h + c
                s = s_sc[base + i]
                if mask is not None:
                    s = jnp.where(mask, s, -jnp.inf)
                m_old = m_sc[i]
                m_new = jnp.maximum(m_old, jnp.max(s, axis=0, keepdims=True))
                a = jnp.exp(m_old - m_new)
                p = jnp.exp(s - m_new)
                l_sc[i] = a * l_sc[i] + jnp.sum(p, axis=0, keepdims=True)
                acc_sc[i] = a * acc_sc[i] + lax.dot_general(vt, p.astype(BF16), _TN,
                                                            preferred_element_type=F32)
                m_sc[i] = m_new

    if not meta_only:
        qi = pl.program_id(2)
        scores(0)

        def body(t, carry):
            scores(t)
            absorb(t - 1, None)
            return carry

        lax.fori_loop(1, qi + 1, body, 0)
        keys = lax.broadcasted_iota(jnp.int32, (tq, tq), 0)
        queries = lax.broadcasted_iota(jnp.int32, (tq, tq), 1)
        absorb(qi, (keys // CHUNK) <= (queries // CHUNK))

    lam = (jnp.exp(jnp.sum(lq1[...] * lk1[...], axis=1, keepdims=True))
           - jnp.exp(jnp.sum(lq2[...] * lk2[...], axis=1, keepdims=True)) + lam_init)
    for h in range(hps):
        o = acc_sc[2 * h] / l_sc[2 * h] - lam * (acc_sc[2 * h + 1] / l_sc[2 * h + 1])
        ms = jnp.mean(o * o, axis=0, keepdims=True)
        o = o * lax.rsqrt(ms + RMS_EPS) * nw_ref[...] * (1.0 - lam_init)
        o_ref[:, h * HEAD_W:(h + 1) * HEAD_W] = o.T.astype(BF16)


def _attention(p_q, p_kv, p_meta, lams, norm_w, *, batch, seq, tq, hps, lam_init, meta_only):
    n_meta = p_meta.shape[1]
    rows = p_q.shape[1]
    nq = 1 if meta_only else seq // tq
    k_blk, v_blk = ATT_HEADS // hps, 2 * ATT_HEADS // hps
    small = [pl.BlockSpec((1, ATT_QK_DIM), lambda b, g, i: (0, 0))] * 4 + [
        pl.BlockSpec((HEAD_W, 1), lambda b, g, i: (0, 0))]
    meta_specs = [pl.BlockSpec((hps, n_meta, HEAD_W), lambda b, g, i: (k_blk + g, 0, 0)),
                  pl.BlockSpec((hps, n_meta, HEAD_W), lambda b, g, i: (v_blk + g, 0, 0))]
    q_spec = pl.BlockSpec((hps, tq, HEAD_W), lambda b, g, i: (g, b * nq + i, 0))
    if meta_only:
        in_specs = [q_spec] + meta_specs + small
        args = (p_q, p_meta, p_meta) + lams + (norm_w,)
    else:
        kv_specs = [pl.BlockSpec((hps, seq, HEAD_W), lambda b, g, i: (k_blk + g, b, 0)),
                    pl.BlockSpec((hps, seq, HEAD_W), lambda b, g, i: (v_blk + g, b, 0))]
        in_specs = [q_spec] + kv_specs + meta_specs + small
        args = (p_q, p_kv, p_kv, p_meta, p_meta) + lams + (norm_w,)
    scratch = [pltpu.VMEM((2 * hps, 1, tq), F32), pltpu.VMEM((2 * hps, 1, tq), F32),
               pltpu.VMEM((2 * hps, HEAD_W, tq), F32)]
    if not meta_only:
        scratch.append(pltpu.VMEM((2 * 2 * hps, tq, tq), F32))
    return pl.pallas_call(
        functools.partial(_attn_kernel, tq=tq, hps=hps, lam_init=lam_init, meta_only=meta_only),
        out_shape=jax.ShapeDtypeStruct((rows, GROUP_W), BF16),
        grid=(batch, ATT_HEADS // hps, nq),
        in_specs=in_specs,
        out_specs=pl.BlockSpec((tq, hps * HEAD_W), lambda b, g, i: (b * nq + i, g)),
        scratch_shapes=scratch,
        compiler_params=_params("parallel", "parallel", "arbitrary"),
        name="attn_meta" if meta_only else "attn",
    )(*args)


def _gdn_prep_kernel(q_ref, k_ref, v_ref, hq_ref, hk_ref, hv_ref, mh_ref, cw_ref, ba_ref, alog_ref,
                     dtb_ref, u_ref, wq_ref, kd_ref, a_ref, gl_ref, xs_sc, *, rows, n_pad):
    first = pl.program_id(1) == 0
    ba = ba_ref[...]
    beta_all = _sigmoid(ba)
    g_all = -jnp.exp(alog_ref[...]) * _softplus(ba + dtb_ref[...])
    if n_pad:
        valid = lax.broadcasted_iota(jnp.int32, (rows, 1), 0) >= n_pad
        beta_all = jnp.where(valid, beta_all, 0.0)
        g_all = jnp.where(valid, g_all, 0.0)

    ri = lax.broadcasted_iota(jnp.int32, (rows, rows), 0)
    ci = lax.broadcasted_iota(jnp.int32, (rows, rows), 1)
    same = (ri // CHUNK) == (ci // CHUNK)
    tril = jnp.logical_and(same, ci <= ri)
    strict = jnp.logical_and(same, ci < ri)
    ones_tril = jnp.where(tril, 1.0, 0.0).astype(F32)
    ones_triu = jnp.where(jnp.logical_and(same, ri <= ci), 1.0, 0.0).astype(F32)
    ones_same = jnp.where(same, 1.0, 0.0).astype(F32)
    hp = lax.Precision.HIGHEST
    gc_all = jnp.dot(ones_tril, g_all, precision=hp, preferred_element_type=F32)
    gl_all = jnp.dot(ones_same, g_all, precision=hp, preferred_element_type=F32)
    gct_all = lax.dot_general(g_all, ones_triu, _TN, precision=hp, preferred_element_type=F32)
    gl_ref[...] = gl_all

    def conv_act(main_ref, halo_ref, stream, h):
        halo = jnp.where(first, mh_ref[stream * DN_HEADS + h], halo_ref[h])
        xs_sc[stream, 0:HALO, :] = halo.astype(F32)
        xs_sc[stream, HALO:, :] = main_ref[h].astype(F32)
        w = cw_ref[stream * DN_HEADS + h]
        y = xs_sc[stream, pl.ds(HALO - DN_CONV + 1, rows), :] * w[0:1]
        for j in range(1, DN_CONV):
            y = y + xs_sc[stream, pl.ds(HALO - DN_CONV + 1 + j, rows), :] * w[j:j + 1]
        return _silu(y)

    for h in range(DN_HEADS):
        q = conv_act(q_ref, hq_ref, 0, h)
        k = conv_act(k_ref, hk_ref, 1, h)
        v = conv_act(v_ref, hv_ref, 2, h)
        q = q * lax.rsqrt(jnp.sum(q * q, axis=1, keepdims=True) + RMS_EPS) * (HEAD_W ** -0.5)
        k = k * lax.rsqrt(jnp.sum(k * k, axis=1, keepdims=True) + RMS_EPS)
        beta = beta_all[:, h:h + 1]
        gc = gc_all[:, DN_HEADS + h:DN_HEADS + h + 1]
        gl = gl_all[:, DN_HEADS + h:DN_HEADS + h + 1]
        gcr = gct_all[DN_HEADS + h:DN_HEADS + h + 1, :]
        egc = jnp.exp(gc)
        kb = k * beta
        k16 = k.astype(BF16)
        kk = lax.dot_general(kb.astype(BF16), k16, _NT, preferred_element_type=F32)
        qk = lax.dot_general(q.astype(BF16), k16, _NT, preferred_element_type=F32)
        dec = jnp.exp(jnp.where(tril, gc - gcr, -jnp.inf))
        a_ref[h] = (qk * dec).astype(BF16)
        pw = jnp.where(strict, kk * dec, 0.0)
        x = jnp.concatenate([v * beta, kb * egc], axis=1)
        x = x - jnp.dot(pw.astype(BF16), x.astype(BF16), preferred_element_type=F32)
        for _ in range(int(math.log2(CHUNK)) - 1):
            p16 = pw.astype(BF16)
            pw = jnp.dot(p16, p16, preferred_element_type=F32)
            x = x + jnp.dot(pw.astype(BF16), x.astype(BF16), preferred_element_type=F32)
        u_ref[h] = x[:, :HEAD_W].astype(BF16)
        w16 = x[:, HEAD_W:].astype(BF16)
        qd16 = (q * egc).astype(BF16)
        for c in range(rows // CHUNK):
            wq_ref[h, 2 * c * CHUNK:(2 * c + 1) * CHUNK, :] = w16[c * CHUNK:(c + 1) * CHUNK]
            wq_ref[h, (2 * c + 1) * CHUNK:(2 * c + 2) * CHUNK, :] = qd16[c * CHUNK:(c + 1) * CHUNK]
        kd_ref[h] = (k * jnp.exp(gl - gc)).astype(BF16)


def _gdn_prep(p, p_halo_src, meta_halo, conv_w, ba, alog, dtb, *, batch, seq, rows, n_pad):
    m = batch * seq
    nt = seq // rows
    g_q, g_k, g_v = 3, 4, 5
    hb = rows // HALO

    def main_spec(g):
        return pl.BlockSpec((DN_HEADS, rows, HEAD_W), lambda b, t: (g, b * nt + t, 0))

    def halo_spec(g):
        return pl.BlockSpec((DN_HEADS, HALO, HEAD_W),
                            lambda b, t: (g, jnp.maximum((b * nt + t) * hb - 1, 0), 0))

    full = lambda shape: pl.BlockSpec(shape, lambda b, t: (0,) * len(shape))
    out_shape = (jax.ShapeDtypeStruct((DN_HEADS, m, HEAD_W), BF16),
                 jax.ShapeDtypeStruct((DN_HEADS, 2 * m, HEAD_W), BF16),
                 jax.ShapeDtypeStruct((DN_HEADS, m, HEAD_W), BF16),
                 jax.ShapeDtypeStruct((DN_HEADS, m, rows), BF16),
                 jax.ShapeDtypeStruct((m, LANES), F32))
    out_specs = (pl.BlockSpec((DN_HEADS, rows, HEAD_W), lambda b, t: (0, b * nt + t, 0)),
                 pl.BlockSpec((DN_HEADS, 2 * rows, HEAD_W), lambda b, t: (0, b * nt + t, 0)),
                 pl.BlockSpec((DN_HEADS, rows, HEAD_W), lambda b, t: (0, b * nt + t, 0)),
                 pl.BlockSpec((DN_HEADS, rows, rows), lambda b, t: (0, b * nt + t, 0)),
                 pl.BlockSpec((rows, LANES), lambda b, t: (b * nt + t, 0)))
    return pl.pallas_call(
        functools.partial(_gdn_prep_kernel, rows=rows, n_pad=n_pad),
        out_shape=out_shape,
        grid=(batch, nt),
        in_specs=[main_spec(g_q), main_spec(g_k), main_spec(g_v),
                  halo_spec(g_q), halo_spec(g_k), halo_spec(g_v),
                  full(meta_halo.shape), full(conv_w.shape),
                  pl.BlockSpec((rows, LANES), lambda b, t: (b * nt + t, 0)),
                  full(alog.shape), full(dtb.shape)],
        out_specs=out_specs,
        scratch_shapes=[pltpu.VMEM((3, rows + HALO, HEAD_W), F32)],
        compiler_params=_params("parallel", "parallel"),
        name="gdn_prep",
    )(p, p, p, p_halo_src, p_halo_src, p_halo_src, meta_halo, conv_w, ba, alog, dtb)


def _gdn_scan_kernel(u_ref, wq_ref, kd_ref, a_ref, gl_ref, z_ref,
                     um_ref, wqm_ref, kdm_ref, am_ref, glm_ref, zm_ref, nw_ref,
                     o_ref, om_ref, s_sc, vbuf_sc, *, rows, hpg):
    hg = pl.program_id(1)
    t = pl.program_id(2)
    lane = lax.broadcasted_iota(jnp.int32, (1, LANES), 1)
    nw = nw_ref[...]

    def decay_total(gl_row, h):
        sel = lane == DN_HEADS + hg * hpg + h
        return jnp.sum(jnp.where(sel, jnp.exp(gl_row), 0.0), axis=1, keepdims=True)

    def gate(o, z):
        ms = jnp.mean(o * o, axis=1, keepdims=True)
        return (o * lax.rsqrt(ms + RMS_EPS) * nw * _silu(z.astype(F32))).astype(BF16)

    def chunk_step(h, u, wq, kd, egl, intra):
        s = s_sc[h]
        sw = jnp.dot(wq, s.astype(BF16), preferred_element_type=F32)
        v_new = u.astype(F32) - sw[:CHUNK]
        v16 = v_new.astype(BF16)
        o = sw[CHUNK:] + intra(v16)
        s_sc[h] = egl * s + lax.dot_general(kd, v16, _TN, preferred_element_type=F32)
        return o

    @pl.when(t == 0)
    def _meta_chunk():
        vbuf_sc[...] = jnp.zeros_like(vbuf_sc)
        glm = glm_ref[0:1, :]
        for h in range(hpg):
            s_sc[h] = jnp.zeros((HEAD_W, HEAD_W), F32)
            o = chunk_step(h, um_ref[h], wqm_ref[h], kdm_ref[h], decay_total(glm, h),
                           lambda v16, h=h: jnp.dot(am_ref[h], v16, preferred_element_type=F32))
            om_ref[:, h * HEAD_W:(h + 1) * HEAD_W] = gate(o, zm_ref[h])

    for c in range(rows // CHUNK):
        r0 = c * CHUNK
        gl_row = gl_ref[r0:r0 + 1, :]
        for h in range(hpg):
            def intra(v16, h=h, r0=r0):
                vbuf_sc[h, r0:r0 + CHUNK, :] = v16
                return jnp.dot(a_ref[h, r0:r0 + CHUNK, :], vbuf_sc[h], preferred_element_type=F32)

            o = chunk_step(h, u_ref[h, r0:r0 + CHUNK, :], wq_ref[h, 2 * r0:2 * r0 + 2 * CHUNK, :],
                           kd_ref[h, r0:r0 + CHUNK, :], decay_total(gl_row, h), intra)
            o_ref[r0:r0 + CHUNK, h * HEAD_W:(h + 1) * HEAD_W] = gate(o, z_ref[h, r0:r0 + CHUNK, :])


def _gdn_scan(real, meta, p, p_meta_pad, norm_w, *, batch, seq, rows, hpg):
    u, wq, kd, a, gl = real
    um, wqm, kdm, am, glm = meta
    m = batch * seq
    nt = seq // rows
    ngrp = DN_HEADS // hpg
    z_blk = (6 * DN_HEADS) // hpg
    mrows = um.shape[1]

    def real_spec(r, w):
        return pl.BlockSpec((hpg, r, w), lambda b, g, t: (g, b * nt + t, 0))

    def meta_spec(r, w):
        return pl.BlockSpec((hpg, r, w), lambda b, g, t: (g, 0, 0))

    return pl.pallas_call(
        functools.partial(_gdn_scan_kernel, rows=rows, hpg=hpg),
        out_shape=(jax.ShapeDtypeStruct((m, GROUP_W), BF16),
                   jax.ShapeDtypeStruct((batch, mrows, GROUP_W), BF16)),
        grid=(batch, ngrp, nt),
        in_specs=[real_spec(rows, HEAD_W), real_spec(2 * rows, HEAD_W), real_spec(rows, HEAD_W),
                  real_spec(rows, rows),
                  pl.BlockSpec((rows, LANES), lambda b, g, t: (b * nt + t, 0)),
                  pl.BlockSpec((hpg, rows, HEAD_W), lambda b, g, t: (z_blk + g, b * nt + t, 0)),
                  meta_spec(mrows, HEAD_W), meta_spec(2 * mrows, HEAD_W), meta_spec(mrows, HEAD_W),
                  meta_spec(mrows, mrows),
                  pl.BlockSpec((mrows, LANES), lambda b, g, t: (0, 0)),
                  pl.BlockSpec((hpg, mrows, HEAD_W), lambda b, g, t: (z_blk + g, 0, 0)),
                  pl.BlockSpec((1, HEAD_W), lambda b, g, t: (0, 0))],
        out_specs=(pl.BlockSpec((rows, hpg * HEAD_W), lambda b, g, t: (b * nt + t, g)),
                   pl.BlockSpec((None, mrows, hpg * HEAD_W), lambda b, g, t: (b, 0, g))),
        scratch_shapes=[pltpu.VMEM((hpg, HEAD_W, HEAD_W), F32),
                        pltpu.VMEM((hpg, rows, HEAD_W), BF16)],
        compiler_params=_params("parallel", "parallel", "arbitrary"),
        name="gdn_scan",
    )(u, wq, kd, a, gl, p, um, wqm, kdm, am, glm, p_meta_pad, norm_w)


def _wout_ln_kernel(oa_ref, od_ref, w_ref, x_ref, g_ref, b_ref, h_ref, *, alpha):
    half = oa_ref.shape[1]
    mix = (jnp.dot(oa_ref[...], w_ref[0:half, :], preferred_element_type=F32)
           + jnp.dot(od_ref[...], w_ref[half:, :], preferred_element_type=F32))
    h_ref[...] = _layer_norm(alpha * x_ref[...] + mix, g_ref[...], b_ref[...])


def _wout_ln(o_att, o_dn, w_out, x2d, g, b, *, bm, alpha):
    m, d = x2d.shape
    half = o_att.shape[1]
    return pl.pallas_call(
        functools.partial(_wout_ln_kernel, alpha=alpha),
        out_shape=jax.ShapeDtypeStruct((m, d), F32),
        grid=(m // bm,),
        in_specs=[pl.BlockSpec((bm, half), lambda i: (i, 0)),
                  pl.BlockSpec((bm, half), lambda i: (i, 0)),
                  pl.BlockSpec((2 * half, d), lambda i: (0, 0)),
                  pl.BlockSpec((bm, d), lambda i: (i, 0)),
                  pl.BlockSpec((1, d), lambda i: (0, 0)),
                  pl.BlockSpec((1, d), lambda i: (0, 0))],
        out_specs=pl.BlockSpec((bm, d), lambda i: (i, 0)),
        compiler_params=_params("parallel"),
        name="wout_ln",
    )(o_att, o_dn, w_out, x2d, g, b)


def _ffn_kernel(h_ref, hprev_ref, hmeta_ref, wg_ref, wu_ref, cw_ref, cb_ref, wd_ref, g_ref, b_ref,
                o_ref, xb_sc, gate_sc, acc_sc, *, bm, tiles_per_batch, alpha):
    i = pl.program_id(0)
    f = pl.program_id(1)

    @pl.when(f == 0)
    def _init():
        halo = jnp.where(i % tiles_per_batch == 0, hmeta_ref[...], hprev_ref[...])
        xb_sc[0:HALO, :] = halo.astype(BF16)
        xb_sc[HALO:, :] = h_ref[...].astype(BF16)
        acc_sc[...] = jnp.zeros_like(acc_sc)

    gate_sc[...] = jnp.dot(xb_sc[...], wg_ref[...], preferred_element_type=F32)
    up = jnp.dot(xb_sc[HALO:, :], wu_ref[...], preferred_element_type=F32)
    cw = cw_ref[...]
    hg = cb_ref[...] + gate_sc[pl.ds(HALO - FFN_CONV + 1, bm), :] * cw[0:1]
    for j in range(1, FFN_CONV):
        hg = hg + gate_sc[pl.ds(HALO - FFN_CONV + 1 + j, bm), :] * cw[j:j + 1]
    act = (_silu(hg) * up).astype(BF16)
    acc_sc[...] += jnp.dot(act, wd_ref[...], preferred_element_type=F32)

    @pl.when(f == pl.num_programs(1) - 1)
    def _finish():
        o_ref[...] = _layer_norm(alpha * h_ref[...] + acc_sc[...], g_ref[...], b_ref[...])


def _ffn(h1, h1_meta, w_gate, w_up, conv_w, conv_b, w_down, g, b, *, seq, bm, tf, alpha):
    m, d = h1.shape
    dff = w_gate.shape[1]
    tiles_per_batch = seq // bm
    hb = bm // HALO
    meta_blk = h1_meta.shape[0] // HALO - 1
    return pl.pallas_call(
        functools.partial(_ffn_kernel, bm=bm, tiles_per_batch=tiles_per_batch, alpha=alpha),
        out_shape=jax.ShapeDtypeStruct((m, d), F32),
        grid=(m // bm, dff // tf),
        in_specs=[pl.BlockSpec((bm, d), lambda i, f: (i, 0)),
                  pl.BlockSpec((HALO, d), lambda i, f: (jnp.maximum(i * hb - 1, 0), 0)),
                  pl.BlockSpec((HALO, d), lambda i, f: (meta_blk, 0)),
                  pl.BlockSpec((d, tf), lambda i, f: (0, f)),
                  pl.BlockSpec((d, tf), lambda i, f: (0, f)),
                  pl.BlockSpec((FFN_CONV, tf), lambda i, f: (0, f)),
                  pl.BlockSpec((1, tf), lambda i, f: (0, f)),
                  pl.BlockSpec((tf, d), lambda i, f: (f, 0)),
                  pl.BlockSpec((1, d), lambda i, f: (0, 0)),
                  pl.BlockSpec((1, d), lambda i, f: (0, 0))],
        out_specs=pl.BlockSpec((bm, d), lambda i, f: (i, 0)),
        scratch_shapes=[pltpu.VMEM((bm + HALO, d), BF16),
                        pltpu.VMEM((bm + HALO, tf), F32),
                        pltpu.VMEM((bm, d), F32)],
        compiler_params=_params("parallel", "arbitrary"),
        name="ffn",
    )(h1, h1, h1_meta, w_gate, w_up, conv_w, conv_b, w_down, g, b)


def _largest_tile(n, cap, mult):
    best = None
    for c in range(mult, min(n, cap) + 1, mult):
        if n % c == 0:
            best = c
    assert best is not None, (n, cap, mult)
    return best


def _rope_tables(n_pos):
    half = ATT_QK_DIM // 2
    inv_freq = ROPE_THETA ** (-jnp.arange(0, ATT_QK_DIM, 2, dtype=F32) / ATT_QK_DIM)
    ang = jnp.arange(n_pos, dtype=F32)[:, None] * inv_freq[None, :]
    cos, sin = jnp.cos(ang), jnp.sin(ang)
    reps = HEAD_W // ATT_QK_DIM
    cos_t = jnp.tile(jnp.concatenate([cos, cos], axis=1), (1, reps))
    sin_t = jnp.tile(jnp.concatenate([-sin, sin], axis=1), (1, reps))
    assert cos_t.shape[1] == HEAD_W and half * 2 == ATT_QK_DIM
    return cos_t, sin_t


def kernel(x, meta_tokens, w_in, conv_qkv_w, a_log, dt_bias, lambda_q1, lambda_k1, lambda_q2, lambda_k2,
           diff_norm_w, delta_norm_w, w_out, ln1_g, ln1_b, ffn_w_gate, ffn_w_up, ffn_conv_w, ffn_conv_b,
           ffn_w_down, ln2_g, ln2_b):
    batch, seq, d = x.shape
    n_meta = meta_tokens.shape[0]
    depth = w_in.shape[0]
    n_main = N_GROUPS * GROUP_W
    assert w_in.shape[2] == n_main + TAIL_COLS and n_meta % HALO == 0 and n_meta <= CHUNK
    m = batch * seq
    alpha = (2.0 * depth) ** 0.25

    bm_proj = _largest_tile(m, 1024, HALO)
    bm_row = _largest_tile(seq, 512, HALO)
    tq = _largest_tile(seq, 256, CHUNK)
    rows = _largest_tile(seq, 256, CHUNK)
    hpg = 4
    att_hps = 2

    cos_t, sin_t = _rope_tables(n_meta + seq)
    cos_meta, sin_meta = cos_t[:n_meta], sin_t[:n_meta]
    cos_real = jnp.tile(cos_t[n_meta:], (batch, 1))
    sin_real = jnp.tile(sin_t[n_meta:], (batch, 1))

    h = x.reshape(m, d)
    h_meta = meta_tokens.astype(x.dtype)
    pad_rows = CHUNK - n_meta
    for l in range(depth):
        w_main = w_in[l, :, :n_main].astype(BF16)
        w_tail = jnp.pad(w_in[l, :, n_main:], ((0, 0), (0, LANES - TAIL_COLS)))
        lane_pad = (DN_HEADS, LANES - TAIL_COLS)
        alog = jnp.pad(a_log[l].astype(F32), lane_pad)[None]
        dtb = jnp.pad(dt_bias[l].astype(F32), lane_pad)[None]
        conv_w = conv_qkv_w[l].reshape(DN_CONV, 3 * DN_HEADS, HEAD_W).transpose(1, 0, 2)
        lams = tuple(v[l][None].astype(F32) for v in (lambda_q1, lambda_k1, lambda_q2, lambda_k2))
        lam_init = 0.8 - 0.6 * math.exp(-0.3 * l)

        p = _proj(h, w_main, cos_real, sin_real, bm_proj)
        p_meta = _proj(h_meta, w_main, cos_meta, sin_meta, n_meta)
        ba = _tail(h, w_tail, bm_proj)
        ba_meta = _tail(h_meta, w_tail, n_meta)

        att_nw = diff_norm_w[l].astype(F32)[:, None]
        o_att = _attention(p, p, p_meta, lams, att_nw, batch=batch, seq=seq, tq=tq, hps=att_hps,
                           lam_init=lam_init, meta_only=False)
        q_meta = jnp.pad(p_meta[:ATT_HEADS], ((0, 0), (0, LANES - n_meta), (0, 0)))
        o_att_meta = _attention(q_meta, None, p_meta, lams, att_nw, batch=1, seq=LANES, tq=LANES,
                                hps=att_hps, lam_init=lam_init, meta_only=True)[:n_meta]

        p_meta_pad = jnp.pad(p_meta, ((0, 0), (pad_rows, 0), (0, 0)))
        ba_meta_pad = jnp.pad(ba_meta, ((pad_rows, 0), (0, 0)))
        meta_halo = p_meta[3 * DN_HEADS:6 * DN_HEADS, n_meta - HALO:, :]
        zero_halo = jnp.zeros_like(meta_halo)
        prep_real = _gdn_prep(p, p, meta_halo, conv_w, ba, alog, dtb, batch=batch, seq=seq, rows=rows,
                              n_pad=0)
        prep_meta = _gdn_prep(p_meta_pad, p_meta_pad, zero_halo, conv_w, ba_meta_pad, alog, dtb, batch=1,
                              seq=CHUNK, rows=CHUNK, n_pad=pad_rows)
        o_dn, o_dn_meta = _gdn_scan(prep_real, prep_meta, p, p_meta_pad, delta_norm_w[l][None],
                                    batch=batch, seq=seq, rows=rows, hpg=hpg)
        o_dn_meta = o_dn_meta[0, pad_rows:]

        w_out_b = w_out[l].astype(BF16)
        g1, b1 = ln1_g[l][None].astype(F32), ln1_b[l][None].astype(F32)
        h1 = _wout_ln(o_att, o_dn, w_out_b, h, g1, b1, bm=bm_row, alpha=alpha)
        h1_meta = _wout_ln(o_att_meta, o_dn_meta, w_out_b, h_meta, g1, b1, bm=n_meta, alpha=alpha)

        dff = ffn_w_gate.shape[2]
        tf = _largest_tile(dff, 512, LANES)
        ffn_args = (ffn_w_gate[l].astype(BF16), ffn_w_up[l].astype(BF16), ffn_conv_w[l].astype(F32),
                    ffn_conv_b[l][None].astype(F32), ffn_w_down[l].astype(BF16),
                    ln2_g[l][None].astype(F32), ln2_b[l][None].astype(F32))
        h_next = _ffn(h1, h1_meta, *ffn_args, seq=seq, bm=bm_row, tf=tf, alpha=alpha)
        if l + 1 < depth:
            zero_meta = jnp.zeros_like(h1_meta)
            h_meta = _ffn(h1_meta, zero_meta, *ffn_args, seq=n_meta, bm=n_meta, tf=tf, alpha=alpha)
        h = h_next
    return h.reshape(batch, seq, d)
```

```python
import functools
import math

import jax
import jax.numpy as jnp
from jax import lax
from jax.experimental import pallas as pl
from jax.experimental.pallas import tpu as pltpu

F32 = jnp.float32
BF16 = jnp.bfloat16

CHUNK = 64
ATT_HEADS = 8
ATT_QK_DIM = 64
HEAD_W = 128
DN_HEADS = 8
DN_CONV = 4
FFN_CONV = 3
ROPE_THETA = 10000.0
LN_EPS = 1e-5
RMS_EPS = 1e-6
GROUP_W = ATT_HEADS * HEAD_W
N_GROUPS = 7
TAIL_COLS = 2 * DN_HEADS

LANES = 128
HALO = 8
VMEM_LIMIT = 56 * 1024 * 1024

_NT = (((1,), (1,)), ((), ()))
_TN = (((0,), (0,)), ((), ()))


def _params(*sem):
    return pltpu.CompilerParams(dimension_semantics=sem, vmem_limit_bytes=VMEM_LIMIT)


def _sigmoid(x):
    return 1.0 / (1.0 + jnp.exp(-x))


def _silu(x):
    return x * _sigmoid(x)


def _softplus(x):
    return jnp.maximum(x, 0.0) + jnp.log(1.0 + jnp.exp(-jnp.abs(x)))


def _layer_norm(y, g, b):
    mu = jnp.mean(y, axis=1, keepdims=True)
    d = y - mu
    var = jnp.mean(d * d, axis=1, keepdims=True)
    return d * lax.rsqrt(var + LN_EPS) * g + b


def _proj_kernel(x_ref, xm_ref, w_ref, wt_ref, cos_ref, sin_ref, cosm_ref, sinm_ref,
                 p_ref, pm_ref, ba_ref, bam_ref, wb_sc, *, heads):
    j = pl.program_id(0)
    i = pl.program_id(1)

    def tail(x, xh):
        w = wt_ref[...]
        xl = (x - xh.astype(F32)).astype(BF16)
        wh = w.astype(BF16)
        wl = (w - wh.astype(F32)).astype(BF16)
        return (jnp.dot(xh, wh, preferred_element_type=F32)
                + jnp.dot(xh, wl, preferred_element_type=F32)
                + jnp.dot(xl, wh, preferred_element_type=F32))

    def emit(acc, cos_r, sin_r, out_ref):
        @pl.when(j < 2)
        def _rope():
            cos = cos_r[...]
            sin = sin_r[...]
            lane = lax.broadcasted_iota(jnp.int32, (1, HEAD_W), 1)
            first_half = (lane % ATT_QK_DIM) < (ATT_QK_DIM // 2)
            scale = jnp.where(j == 0, ATT_QK_DIM ** -0.5, 1.0).astype(F32)
            for h in range(heads):
                blk = acc[:, h * HEAD_W:(h + 1) * HEAD_W]
                partner = jnp.where(first_half,
                                    pltpu.roll(blk, HEAD_W - ATT_QK_DIM // 2, 1),
                                    pltpu.roll(blk, ATT_QK_DIM // 2, 1))
                out_ref[h] = ((blk * cos + partner * sin) * scale).astype(BF16)

        @pl.when(j >= 2)
        def _plain():
            for h in range(heads):
                out_ref[h] = acc[:, h * HEAD_W:(h + 1) * HEAD_W].astype(BF16)

    @pl.when(i == 0)
    def _new_column_tile():
        wb_sc[...] = w_ref[...].astype(BF16)
        xm = xm_ref[...]
        xmh = xm.astype(BF16)
        emit(jnp.dot(xmh, wb_sc[...], preferred_element_type=F32), cosm_ref, sinm_ref, pm_ref)

        @pl.when(j == 0)
        def _meta_tail():
            bam_ref[...] = tail(xm, xmh)

    x = x_ref[...]
    xh = x.astype(BF16)
    emit(jnp.dot(xh, wb_sc[...], preferred_element_type=F32), cos_ref, sin_ref, p_ref)

    @pl.when(j == 0)
    def _tail():
        ba_ref[...] = tail(x, xh)


def _proj(x2d, x_meta, w_full, w_tail, cos, sin, cos_meta, sin_meta, bm):
    m, d = x2d.shape
    n_meta = x_meta.shape[0]
    heads = GROUP_W // HEAD_W
    n_blocks = N_GROUPS * heads
    nm = m // bm
    const = lambda shape: pl.BlockSpec(shape, lambda j, i: (0,) * len(shape))
    return pl.pallas_call(
        functools.partial(_proj_kernel, heads=heads),
        out_shape=(jax.ShapeDtypeStruct((n_blocks, m, HEAD_W), BF16),
                   jax.ShapeDtypeStruct((n_blocks, n_meta, HEAD_W), BF16),
                   jax.ShapeDtypeStruct((m, LANES), F32),
                   jax.ShapeDtypeStruct((n_meta, LANES), F32)),
        grid=(N_GROUPS, nm),
        in_specs=[
            pl.BlockSpec((bm, d), lambda j, i: (i, 0)),
            const((n_meta, d)),
            pl.BlockSpec((d, GROUP_W), lambda j, i: (0, j)),
            const((d, LANES)),
            pl.BlockSpec((bm, HEAD_W), lambda j, i: (i, 0)),
            pl.BlockSpec((bm, HEAD_W), lambda j, i: (i, 0)),
            const((n_meta, HEAD_W)),
            const((n_meta, HEAD_W)),
        ],
        out_specs=(
            pl.BlockSpec((heads, bm, HEAD_W), lambda j, i: (j, i, 0)),
            pl.BlockSpec((heads, n_meta, HEAD_W), lambda j, i: (j, 0, 0)),
            pl.BlockSpec((bm, LANES), lambda j, i: (jnp.where(j == 0, i, nm - 1), 0)),
            const((n_meta, LANES)),
        ),
        scratch_shapes=[pltpu.VMEM((d, GROUP_W), BF16)],
        compiler_params=_params("arbitrary", "arbitrary"),
        name="proj",
    )(x2d, x_meta, w_full, w_tail, cos, sin, cos_meta, sin_meta)


def _attn_kernel(*refs, tq, hps, lam_init, meta_only):
    if meta_only:
        q_ref, km_ref, vm_ref, lq1, lk1, lq2, lk2, nw_ref, o_ref, m_sc, l_sc, acc_sc = refs
    else:
        (q_ref, k_ref, v_ref, km_ref, vm_ref, lq1, lk1, lq2, lk2, nw_ref, o_ref,
         m_sc, l_sc, acc_sc, s_sc) = refs
    lane = lax.broadcasted_iota(jnp.int32, (1, HEAD_W), 1)
    qc = []
    for h in range(hps):
        q = q_ref[h]
        zero = jnp.zeros_like(q)
        qc.append((jnp.where(lane < ATT_QK_DIM, q, zero), jnp.where(lane >= ATT_QK_DIM, q, zero)))

    for h in range(hps):
        for c in range(2):
            s = lax.dot_general(km_ref[h], qc[h][c], _NT, preferred_element_type=F32)
            m0 = jnp.max(s, axis=0, keepdims=True)
            p = jnp.exp(s - m0)
            m_sc[2 * h + c] = m0
            l_sc[2 * h + c] = jnp.sum(p, axis=0, keepdims=True)
            acc_sc[2 * h + c] = lax.dot_general(vm_ref[h], p.astype(BF16), _TN,
                                                preferred_element_type=F32)

    nc = 2 * hps

    def scores(t):
        off = pl.multiple_of(t * tq, tq)
        base = (t % 2) * nc
        for h in range(hps):
            kt = k_ref[h, pl.ds(off, tq), :]
            for c in range(2):
                s_sc[base + 2 * h + c] = lax.dot_general(kt, qc[h][c], _NT,
                                                         preferred_element_type=F32)

    def absorb(t, mask):
        off = pl.multiple_of(t * tq, tq)
        base = (t % 2) * nc
        for h in range(hps):
            vt = v_ref[h, pl.ds(off, tq), :]
            for c in range(2):
                i = 2 * h + c
                s = s_sc[base + i]
                if mask is not None:
                    s = jnp.where(mask, s, -jnp.inf)
                m_old = m_sc[i]
                m_new = jnp.maximum(m_old, jnp.max(s, axis=0, keepdims=True))
                a = jnp.exp(m_old - m_new)
                p = jnp.exp(s - m_new)
                l_sc[i] = a * l_sc[i] + jnp.sum(p, axis=0, keepdims=True)
                acc_sc[i] = a * acc_sc[i] + lax.dot_general(vt, p.astype(BF16), _TN,
                                                            preferred_element_type=F32)
                m_sc[i] = m_new

    if not meta_only:
        qi = pl.program_id(2)
        scores(0)

        def body(t, carry):
            scores(t)
            absorb(t - 1, None)
            return carry

        lax.fori_loop(1, qi + 1, body, 0)
        keys = lax.broadcasted_iota(jnp.int32, (tq, tq), 0)
        queries = lax.broadcasted_iota(jnp.int32, (tq, tq), 1)
        absorb(qi, (keys // CHUNK) <= (queries // CHUNK))

    lam = (jnp.exp(jnp.sum(lq1[...] * lk1[...], axis=1, keepdims=True))
           - jnp.exp(jnp.sum(lq2[...] * lk2[...], axis=1, keepdims=True)) + lam_init)
    for h in range(hps):
        o = acc_sc[2 * h] / l_sc[2 * h] - lam * (acc_sc[2 * h + 1] / l_sc[2 * h + 1])
        ms = jnp.mean(o * o, axis=0, keepdims=True)
        o = o * lax.rsqrt(ms + RMS_EPS) * nw_ref[...] * (1.0 - lam_init)
        o_ref[:, h * HEAD_W:(h + 1) * HEAD_W] = o.T.astype(BF16)


def _attention(p_q, p_kv, p_meta, lams, norm_w, *, batch, seq, tq, hps, lam_init, meta_only):
    n_meta = p_meta.shape[1]
    rows = p_q.shape[1]
    nq = 1 if meta_only else seq // tq
    k_blk, v_blk = ATT_HEADS // hps, 2 * ATT_HEADS // hps
    small = [pl.BlockSpec((1, ATT_QK_DIM), lambda b, g, i: (0, 0))] * 4 + [
        pl.BlockSpec((HEAD_W, 1), lambda b, g, i: (0, 0))]
    meta_specs = [pl.BlockSpec((hps, n_meta, HEAD_W), lambda b, g, i: (k_blk + g, 0, 0)),
                  pl.BlockSpec((hps, n_meta, HEAD_W), lambda b, g, i: (v_blk + g, 0, 0))]
    q_spec = pl.BlockSpec((hps, tq, HEAD_W), lambda b, g, i: (g, b * nq + i, 0))
    if meta_only:
        in_specs = [q_spec] + meta_specs + small
        args = (p_q, p_meta, p_meta) + lams + (norm_w,)
    else:
        kv_specs = [pl.BlockSpec((hps, seq, HEAD_W), lambda b, g, i: (k_blk + g, b, 0)),
                    pl.BlockSpec((hps, seq, HEAD_W), lambda b, g, i: (v_blk + g, b, 0))]
        in_specs = [q_spec] + kv_specs + meta_specs + small
        args = (p_q, p_kv, p_kv, p_meta, p_meta) + lams + (norm_w,)
    scratch = [pltpu.VMEM((2 * hps, 1, tq), F32), pltpu.VMEM((2 * hps, 1, tq), F32),
               pltpu.VMEM((2 * hps, HEAD_W, tq), F32)]
    if not meta_only:
        scratch.append(pltpu.VMEM((2 * 2 * hps, tq, tq), F32))
    return pl.pallas_call(
        functools.partial(_attn_kernel, tq=tq, hps=hps, lam_init=lam_init, meta_only=meta_only),
        out_shape=jax.ShapeDtypeStruct((rows, GROUP_W), BF16),
        grid=(batch, ATT_HEADS // hps, nq),
        in_specs=in_specs,
        out_specs=pl.BlockSpec((tq, hps * HEAD_W), lambda b, g, i: (b * nq + i, g)),
        scratch_shapes=scratch,
        compiler_params=_params("parallel", "parallel", "arbitrary"),
        name="attn_meta" if meta_only else "attn",
    )(*args)


def _gdn_prep_kernel(q_ref, k_ref, v_ref, hq_ref, hk_ref, hv_ref, mh_ref, cw_ref, ba_ref, alog_ref,
                     dtb_ref, u_ref, wq_ref, kd_ref, a_ref, gl_ref, xs_sc, *, rows, n_pad):
    first = pl.program_id(1) == 0
    ba = ba_ref[...]
    beta_all = _sigmoid(ba)
    g_all = -jnp.exp(alog_ref[...]) * _softplus(ba + dtb_ref[...])
    if n_pad:
        valid = lax.broadcasted_iota(jnp.int32, (rows, 1), 0) >= n_pad
        beta_all = jnp.where(valid, beta_all, 0.0)
        g_all = jnp.where(valid, g_all, 0.0)

    ri = lax.broadcasted_iota(jnp.int32, (rows, rows), 0)
    ci = lax.broadcasted_iota(jnp.int32, (rows, rows), 1)
    same = (ri // CHUNK) == (ci // CHUNK)
    tril = jnp.logical_and(same, ci <= ri)
    strict = jnp.logical_and(same, ci < ri)
    ones_tril = jnp.where(tril, 1.0, 0.0).astype(F32)
    ones_triu = jnp.where(jnp.logical_and(same, ri <= ci), 1.0, 0.0).astype(F32)
    ones_same = jnp.where(same, 1.0, 0.0).astype(F32)
    hp = lax.Precision.HIGHEST
    gc_all = jnp.dot(ones_tril, g_all, precision=hp, preferred_element_type=F32)
    gl_all = jnp.dot(ones_same, g_all, precision=hp, preferred_element_type=F32)
    gct_all = lax.dot_general(g_all, ones_triu, _TN, precision=hp, preferred_element_type=F32)
    gl_ref[...] = gl_all

    def conv_act(main_ref, halo_ref, stream, h):
        halo = jnp.where(first, mh_ref[stream * DN_HEADS + h], halo_ref[h])
        xs_sc[stream, 0:HALO, :] = halo.astype(F32)
        xs_sc[stream, HALO:, :] = main_ref[h].astype(F32)
        w = cw_ref[stream * DN_HEADS + h]
        y = xs_sc[stream, pl.ds(HALO - DN_CONV + 1, rows), :] * w[0:1]
        for j in range(1, DN_CONV):
            y = y + xs_sc[stream, pl.ds(HALO - DN_CONV + 1 + j, rows), :] * w[j:j + 1]
        return _silu(y)

    for h in range(DN_HEADS):
        q = conv_act(q_ref, hq_ref, 0, h)
        k = conv_act(k_ref, hk_ref, 1, h)
        v = conv_act(v_ref, hv_ref, 2, h)
        q = q * lax.rsqrt(jnp.sum(q * q, axis=1, keepdims=True) + RMS_EPS) * (HEAD_W ** -0.5)
        k = k * lax.rsqrt(jnp.sum(k * k, axis=1, keepdims=True) + RMS_EPS)
        beta = beta_all[:, h:h + 1]
        gc = gc_all[:, DN_HEADS + h:DN_HEADS + h + 1]
        gl = gl_all[:, DN_HEADS + h:DN_HEADS + h + 1]
        gcr = gct_all[DN_HEADS + h:DN_HEADS + h + 1, :]
        egc = jnp.exp(gc)
        kb = k * beta
        k16 = k.astype(BF16)
        kk = lax.dot_general(kb.astype(BF16), k16, _NT, preferred_element_type=F32)
        qk = lax.dot_general(q.astype(BF16), k16, _NT, preferred_element_type=F32)
        dec = jnp.exp(jnp.where(tril, gc - gcr, -jnp.inf))
        a_ref[h] = (qk * dec).astype(BF16)
        pw = jnp.where(strict, kk * dec, 0.0)
        x = jnp.concatenate([v * beta, kb * egc], axis=1)
        x = x - jnp.dot(pw.astype(BF16), x.astype(BF16), preferred_element_type=F32)
        for _ in range(int(math.log2(CHUNK)) - 1):
            p16 = pw.astype(BF16)
            pw = jnp.dot(p16, p16, preferred_element_type=F32)
            x = x + jnp.dot(pw.astype(BF16), x.astype(BF16), preferred_element_type=F32)
        u_ref[h] = x[:, :HEAD_W].astype(BF16)
        w16 = x[:, HEAD_W:].astype(BF16)
        qd16 = (q * egc).astype(BF16)
        for c in range(rows // CHUNK):
            wq_ref[h, 2 * c * CHUNK:(2 * c + 1) * CHUNK, :] = w16[c * CHUNK:(c + 1) * CHUNK]
            wq_ref[h, (2 * c + 1) * CHUNK:(2 * c + 2) * CHUNK, :] = qd16[c * CHUNK:(c + 1) * CHUNK]
        kd_ref[h] = (k * jnp.exp(gl - gc)).astype(BF16)


def _gdn_prep(p, p_halo_src, meta_halo, conv_w, ba, alog, dtb, *, batch, seq, rows, n_pad):
    m = batch * seq
    nt = seq // rows
    g_q, g_k, g_v = 3, 4, 5
    hb = rows // HALO

    def main_spec(g):
        return pl.BlockSpec((DN_HEADS, rows, HEAD_W), lambda b, t: (g, b * nt + t, 0))

    def halo_spec(g):
        return pl.BlockSpec((DN_HEADS, HALO, HEAD_W),
                            lambda b, t: (g, jnp.maximum((b * nt + t) * hb - 1, 0), 0))

    full = lambda shape: pl.BlockSpec(shape, lambda b, t: (0,) * len(shape))
    out_shape = (jax.ShapeDtypeStruct((DN_HEADS, m, HEAD_W), BF16),
                 jax.ShapeDtypeStruct((DN_HEADS, 2 * m, HEAD_W), BF16),
                 jax.ShapeDtypeStruct((DN_HEADS, m, HEAD_W), BF16),
                 jax.ShapeDtypeStruct((DN_HEADS, m, rows), BF16),
                 jax.ShapeDtypeStruct((m, LANES), F32))
    out_specs = (pl.BlockSpec((DN_HEADS, rows, HEAD_W), lambda b, t: (0, b * nt + t, 0)),
                 pl.BlockSpec((DN_HEADS, 2 * rows, HEAD_W), lambda b, t: (0, b * nt + t, 0)),
                 pl.BlockSpec((DN_HEADS, rows, HEAD_W), lambda b, t: (0, b * nt + t, 0)),
                 pl.BlockSpec((DN_HEADS, rows, rows), lambda b, t: (0, b * nt + t, 0)),
                 pl.BlockSpec((rows, LANES), lambda b, t: (b * nt + t, 0)))
    return pl.pallas_call(
        functools.partial(_gdn_prep_kernel, rows=rows, n_pad=n_pad),
        out_shape=out_shape,
        grid=(batch, nt),
        in_specs=[main_spec(g_q), main_spec(g_k), main_spec(g_v),
                  halo_spec(g_q), halo_spec(g_k), halo_spec(g_v),
                  full(meta_halo.shape), full(conv_w.shape),
                  pl.BlockSpec((rows, LANES), lambda b, t: (b * nt + t, 0)),
                  full(alog.shape), full(dtb.shape)],
        out_specs=out_specs,
        scratch_shapes=[pltpu.VMEM((3, rows + HALO, HEAD_W), F32)],
        compiler_params=_params("parallel", "parallel"),
        name="gdn_prep",
    )(p, p, p, p_halo_src, p_halo_src, p_halo_src, meta_halo, conv_w, ba, alog, dtb)


def _gdn_scan_kernel(u_ref, wq_ref, kd_ref, a_ref, gl_ref, z_ref,
                     um_ref, wqm_ref, kdm_ref, am_ref, glm_ref, zm_ref, nw_ref,
                     o_ref, om_ref, s_sc, vbuf_sc, *, rows, hpg):
    hg = pl.program_id(1)
    t = pl.program_id(2)
    lane = lax.broadcasted_iota(jnp.int32, (1, LANES), 1)
    nw = nw_ref[...]

    def decay_total(gl_row, h):
        sel = lane == DN_HEADS + hg * hpg + h
        return jnp.sum(jnp.where(sel, jnp.exp(gl_row), 0.0), axis=1, keepdims=True)

    def gate(o, z):
        ms = jnp.mean(o * o, axis=1, keepdims=True)
        return (o * lax.rsqrt(ms + RMS_EPS) * nw * _silu(z.astype(F32))).astype(BF16)

    def chunk_step(h, u, wq, kd, egl, intra):
        s = s_sc[h]
        sw = jnp.dot(wq, s.astype(BF16), preferred_element_type=F32)
        v_new = u.astype(F32) - sw[:CHUNK]
        v16 = v_new.astype(BF16)
        o = sw[CHUNK:] + intra(v16)
        s_sc[h] = egl * s + lax.dot_general(kd, v16, _TN, preferred_element_type=F32)
        return o

    @pl.when(t == 0)
    def _meta_chunk():
        vbuf_sc[...] = jnp.zeros_like(vbuf_sc)
        glm = glm_ref[0:1, :]
        for h in range(hpg):
            s_sc[h] = jnp.zeros((HEAD_W, HEAD_W), F32)
            o = chunk_step(h, um_ref[h], wqm_ref[h], kdm_ref[h], decay_total(glm, h),
                           lambda v16, h=h: jnp.dot(am_ref[h], v16, preferred_element_type=F32))
            om_ref[:, h * HEAD_W:(h + 1) * HEAD_W] = gate(o, zm_ref[h])

    for c in range(rows // CHUNK):
        r0 = c * CHUNK
        gl_row = gl_ref[r0:r0 + 1, :]
        for h in range(hpg):
            def intra(v16, h=h, r0=r0):
                vbuf_sc[h, r0:r0 + CHUNK, :] = v16
                return jnp.dot(a_ref[h, r0:r0 + CHUNK, :], vbuf_sc[h], preferred_element_type=F32)

            o = chunk_step(h, u_ref[h, r0:r0 + CHUNK, :], wq_ref[h, 2 * r0:2 * r0 + 2 * CHUNK, :],
                           kd_ref[h, r0:r0 + CHUNK, :], decay_total(gl_row, h), intra)
            o_ref[r0:r0 + CHUNK, h * HEAD_W:(h + 1) * HEAD_W] = gate(o, z_ref[h, r0:r0 + CHUNK, :])


def _gdn_scan(real, meta, p, p_meta_pad, norm_w, *, batch, seq, rows, hpg):
    u, wq, kd, a, gl = real
    um, wqm, kdm, am, glm = meta
    m = batch * seq
    nt = seq // rows
    ngrp = DN_HEADS // hpg
    z_blk = (6 * DN_HEADS) // hpg
    mrows = um.shape[1]

    def real_spec(r, w):
        return pl.BlockSpec((hpg, r, w), lambda b, g, t: (g, b * nt + t, 0))

    def meta_spec(r, w):
        return pl.BlockSpec((hpg, r, w), lambda b, g, t: (g, 0, 0))

    return pl.pallas_call(
        functools.partial(_gdn_scan_kernel, rows=rows, hpg=hpg),
        out_shape=(jax.ShapeDtypeStruct((m, GROUP_W), BF16),
                   jax.ShapeDtypeStruct((batch, mrows, GROUP_W), BF16)),
        grid=(batch, ngrp, nt),
        in_specs=[real_spec(rows, HEAD_W), real_spec(2 * rows, HEAD_W), real_spec(rows, HEAD_W),
                  real_spec(rows, rows),
                  pl.BlockSpec((rows, LANES), lambda b, g, t: (b * nt + t, 0)),
                  pl.BlockSpec((hpg, rows, HEAD_W), lambda b, g, t: (z_blk + g, b * nt + t, 0)),
                  meta_spec(mrows, HEAD_W), meta_spec(2 * mrows, HEAD_W), meta_spec(mrows, HEAD_W),
                  meta_spec(mrows, mrows),
                  pl.BlockSpec((mrows, LANES), lambda b, g, t: (0, 0)),
                  pl.BlockSpec((hpg, mrows, HEAD_W), lambda b, g, t: (z_blk + g, 0, 0)),
                  pl.BlockSpec((1, HEAD_W), lambda b, g, t: (0, 0))],
        out_specs=(pl.BlockSpec((rows, hpg * HEAD_W), lambda b, g, t: (b * nt + t, g)),
                   pl.BlockSpec((None, mrows, hpg * HEAD_W), lambda b, g, t: (b, 0, g))),
        scratch_shapes=[pltpu.VMEM((hpg, HEAD_W, HEAD_W), F32),
                        pltpu.VMEM((hpg, rows, HEAD_W), BF16)],
        compiler_params=_params("parallel", "parallel", "arbitrary"),
        name="gdn_scan",
    )(u, wq, kd, a, gl, p, um, wqm, kdm, am, glm, p_meta_pad, norm_w)


def _wout_ln_kernel(oa_ref, od_ref, w_ref, x_ref, g_ref, b_ref, h_ref, *, alpha):
    half = oa_ref.shape[1]
    mix = (jnp.dot(oa_ref[...], w_ref[0:half, :], preferred_element_type=F32)
           + jnp.dot(od_ref[...], w_ref[half:, :], preferred_element_type=F32))
    h_ref[...] = _layer_norm(alpha * x_ref[...] + mix, g_ref[...], b_ref[...])


def _wout_ln(o_att, o_dn, w_out, x2d, g, b, *, bm, alpha):
    m, d = x2d.shape
    half = o_att.shape[1]
    return pl.pallas_call(
        functools.partial(_wout_ln_kernel, alpha=alpha),
        out_shape=jax.ShapeDtypeStruct((m, d), F32),
        grid=(m // bm,),
        in_specs=[pl.BlockSpec((bm, half), lambda i: (i, 0)),
                  pl.BlockSpec((bm, half), lambda i: (i, 0)),
                  pl.BlockSpec((2 * half, d), lambda i: (0, 0)),
                  pl.BlockSpec((bm, d), lambda i: (i, 0)),
                  pl.BlockSpec((1, d), lambda i: (0, 0)),
                  pl.BlockSpec((1, d), lambda i: (0, 0))],
        out_specs=pl.BlockSpec((bm, d), lambda i: (i, 0)),
        compiler_params=_params("parallel"),
        name="wout_ln",
    )(o_att, o_dn, w_out, x2d, g, b)


def _ffn_kernel(h_ref, hprev_ref, hmeta_ref, wg_ref, wu_ref, cw_ref, cb_ref, wd_ref, g_ref, b_ref,
                o_ref, xb_sc, gate_sc, acc_sc, *, bm, tiles_per_batch, alpha):
    i = pl.program_id(0)
    f = pl.program_id(1)

    @pl.when(f == 0)
    def _init():
        halo = jnp.where(i % tiles_per_batch == 0, hmeta_ref[...], hprev_ref[...])
        xb_sc[0:HALO, :] = halo.astype(BF16)
        xb_sc[HALO:, :] = h_ref[...].astype(BF16)
        acc_sc[...] = jnp.zeros_like(acc_sc)

    gate_sc[...] = jnp.dot(xb_sc[...], wg_ref[...], preferred_element_type=F32)
    up = jnp.dot(xb_sc[HALO:, :], wu_ref[...], preferred_element_type=F32)
    cw = cw_ref[...]
    hg = cb_ref[...] + gate_sc[pl.ds(HALO - FFN_CONV + 1, bm), :] * cw[0:1]
    for j in range(1, FFN_CONV):
        hg = hg + gate_sc[pl.ds(HALO - FFN_CONV + 1 + j, bm), :] * cw[j:j + 1]
    act = (_silu(hg) * up).astype(BF16)
    acc_sc[...] += jnp.dot(act, wd_ref[...], preferred_element_type=F32)

    @pl.when(f == pl.num_programs(1) - 1)
    def _finish():
        o_ref[...] = _layer_norm(alpha * h_ref[...] + acc_sc[...], g_ref[...], b_ref[...])


def _ffn(h1, h1_meta, w_gate, w_up, conv_w, conv_b, w_down, g, b, *, seq, bm, tf, alpha):
    m, d = h1.shape
    dff = w_gate.shape[1]
    tiles_per_batch = seq // bm
    hb = bm // HALO
    meta_blk = h1_meta.shape[0] // HALO - 1
    return pl.pallas_call(
        functools.partial(_ffn_kernel, bm=bm, tiles_per_batch=tiles_per_batch, alpha=alpha),
        out_shape=jax.ShapeDtypeStruct((m, d), F32),
        grid=(m // bm, dff // tf),
        in_specs=[pl.BlockSpec((bm, d), lambda i, f: (i, 0)),
                  pl.BlockSpec((HALO, d), lambda i, f: (jnp.maximum(i * hb - 1, 0), 0)),
                  pl.BlockSpec((HALO, d), lambda i, f: (meta_blk, 0)),
                  pl.BlockSpec((d, tf), lambda i, f: (0, f)),
                  pl.BlockSpec((d, tf), lambda i, f: (0, f)),
                  pl.BlockSpec((FFN_CONV, tf), lambda i, f: (0, f)),
                  pl.BlockSpec((1, tf), lambda i, f: (0, f)),
                  pl.BlockSpec((tf, d), lambda i, f: (f, 0)),
                  pl.BlockSpec((1, d), lambda i, f: (0, 0)),
                  pl.BlockSpec((1, d), lambda i, f: (0, 0))],
        out_specs=pl.BlockSpec((bm, d), lambda i, f: (i, 0)),
        scratch_shapes=[pltpu.VMEM((bm + HALO, d), BF16),
                        pltpu.VMEM((bm + HALO, tf), F32),
                        pltpu.VMEM((bm, d), F32)],
        compiler_params=_params("parallel", "arbitrary"),
        name="ffn",
    )(h1, h1, h1_meta, w_gate, w_up, conv_w, conv_b, w_down, g, b)


def _largest_tile(n, cap, mult):
    best = None
    for c in range(mult, min(n, cap) + 1, mult):
        if n % c == 0:
            best = c
    assert best is not None, (n, cap, mult)
    return best


def _rope_tables(n_pos):
    half = ATT_QK_DIM // 2
    inv_freq = ROPE_THETA ** (-jnp.arange(0, ATT_QK_DIM, 2, dtype=F32) / ATT_QK_DIM)
    ang = jnp.arange(n_pos, dtype=F32)[:, None] * inv_freq[None, :]
    cos, sin = jnp.cos(ang), jnp.sin(ang)
    reps = HEAD_W // ATT_QK_DIM
    cos_t = jnp.tile(jnp.concatenate([cos, cos], axis=1), (1, reps))
    sin_t = jnp.tile(jnp.concatenate([-sin, sin], axis=1), (1, reps))
    assert cos_t.shape[1] == HEAD_W and half * 2 == ATT_QK_DIM
    return cos_t, sin_t


def kernel(x, meta_tokens, w_in, conv_qkv_w, a_log, dt_bias, lambda_q1, lambda_k1, lambda_q2, lambda_k2,
           diff_norm_w, delta_norm_w, w_out, ln1_g, ln1_b, ffn_w_gate, ffn_w_up, ffn_conv_w, ffn_conv_b,
           ffn_w_down, ln2_g, ln2_b):
    batch, seq, d = x.shape
    n_meta = meta_tokens.shape[0]
    depth = w_in.shape[0]
    n_main = N_GROUPS * GROUP_W
    assert w_in.shape[2] == n_main + TAIL_COLS and n_meta % HALO == 0 and n_meta <= CHUNK
    m = batch * seq
    alpha = (2.0 * depth) ** 0.25

    bm_proj = _largest_tile(m, 512, HALO)
    bm_row = _largest_tile(seq, 512, HALO)
    tq = _largest_tile(seq, 256, CHUNK)
    rows = _largest_tile(seq, 256, CHUNK)
    hpg = 4
    att_hps = 4

    cos_t, sin_t = _rope_tables(n_meta + seq)
    cos_meta, sin_meta = cos_t[:n_meta], sin_t[:n_meta]
    cos_real = jnp.tile(cos_t[n_meta:], (batch, 1))
    sin_real = jnp.tile(sin_t[n_meta:], (batch, 1))

    h = x.reshape(m, d)
    h_meta = meta_tokens.astype(x.dtype)
    pad_rows = CHUNK - n_meta
    for l in range(depth):
        w_tail = jnp.pad(w_in[l, :, n_main:], ((0, 0), (0, LANES - TAIL_COLS)))
        lane_pad = (DN_HEADS, LANES - TAIL_COLS)
        alog = jnp.pad(a_log[l].astype(F32), lane_pad)[None]
        dtb = jnp.pad(dt_bias[l].astype(F32), lane_pad)[None]
        conv_w = conv_qkv_w[l].reshape(DN_CONV, 3 * DN_HEADS, HEAD_W).transpose(1, 0, 2)
        lams = tuple(v[l][None].astype(F32) for v in (lambda_q1, lambda_k1, lambda_q2, lambda_k2))
        lam_init = 0.8 - 0.6 * math.exp(-0.3 * l)

        p, p_meta, ba, ba_meta = _proj(h, h_meta, w_in[l], w_tail, cos_real, sin_real, cos_meta,
                                       sin_meta, bm_proj)

        att_nw = diff_norm_w[l].astype(F32)[:, None]
        o_att = _attention(p, p, p_meta, lams, att_nw, batch=batch, seq=seq, tq=tq, hps=att_hps,
                           lam_init=lam_init, meta_only=False)
        q_meta = jnp.pad(p_meta[:ATT_HEADS], ((0, 0), (0, LANES - n_meta), (0, 0)))
        o_att_meta = _attention(q_meta, None, p_meta, lams, att_nw, batch=1, seq=LANES, tq=LANES,
                                hps=att_hps, lam_init=lam_init, meta_only=True)[:n_meta]

        p_meta_pad = jnp.pad(p_meta, ((0, 0), (pad_rows, 0), (0, 0)))
        ba_meta_pad = jnp.pad(ba_meta, ((pad_rows, 0), (0, 0)))
        meta_halo = p_meta[3 * DN_HEADS:6 * DN_HEADS, n_meta - HALO:, :]
        zero_halo = jnp.zeros_like(meta_halo)
        prep_real = _gdn_prep(p, p, meta_halo, conv_w, ba, alog, dtb, batch=batch, seq=seq, rows=rows,
                              n_pad=0)
        prep_meta = _gdn_prep(p_meta_pad, p_meta_pad, zero_halo, conv_w, ba_meta_pad, alog, dtb, batch=1,
                              seq=CHUNK, rows=CHUNK, n_pad=pad_rows)
        o_dn, o_dn_meta = _gdn_scan(prep_real, prep_meta, p, p_meta_pad, delta_norm_w[l][None],
                                    batch=batch, seq=seq, rows=rows, hpg=hpg)
        o_dn_meta = o_dn_meta[0, pad_rows:]

        w_out_b = w_out[l].astype(BF16)
        g1, b1 = ln1_g[l][None].astype(F32), ln1_b[l][None].astype(F32)
        h1 = _wout_ln(o_att, o_dn, w_out_b, h, g1, b1, bm=bm_row, alpha=alpha)
        h1_meta = _wout_ln(o_att_meta, o_dn_meta, w_out_b, h_meta, g1, b1, bm=n_meta, alpha=alpha)

        dff = ffn_w_gate.shape[2]
        tf = _largest_tile(dff, 512, LANES)
        ffn_args = (ffn_w_gate[l].astype(BF16), ffn_w_up[l].astype(BF16), ffn_conv_w[l].astype(F32),
                    ffn_conv_b[l][None].astype(F32), ffn_w_down[l].astype(BF16),
                    ln2_g[l][None].astype(F32), ln2_b[l][None].astype(F32))
        h_next = _ffn(h1, h1_meta, *ffn_args, seq=seq, bm=bm_row, tf=tf, alpha=alpha)
        if l + 1 < depth:
            zero_meta = jnp.zeros_like(h1_meta)
            h_meta = _ffn(h1_meta, zero_meta, *ffn_args, seq=n_meta, bm=n_meta, tf=tf, alpha=alpha)
        h = h_next
    return h.reshape(batch, seq, d)
```

```python
import functools
import math

import jax
import jax.numpy as jnp
from jax import lax
from jax.experimental import pallas as pl
from jax.experimental.pallas import tpu as pltpu

F32 = jnp.float32
BF16 = jnp.bfloat16

CHUNK = 64
ATT_HEADS = 8
ATT_QK_DIM = 64
HEAD_W = 128
DN_HEADS = 8
DN_CONV = 4
FFN_CONV = 3
ROPE_THETA = 10000.0
LN_EPS = 1e-5
RMS_EPS = 1e-6
GROUP_W = ATT_HEADS * HEAD_W
N_GROUPS = 7
TAIL_COLS = 2 * DN_HEADS

LANES = 128
HALO = 8
VMEM_LIMIT = 56 * 1024 * 1024

_NT = (((1,), (1,)), ((), ()))
_TN = (((0,), (0,)), ((), ()))


def _params(*sem):
    return pltpu.CompilerParams(dimension_semantics=sem, vmem_limit_bytes=VMEM_LIMIT)


def _sigmoid(x):
    return 1.0 / (1.0 + jnp.exp(-x))


def _silu(x):
    return x * _sigmoid(x)


def _softplus(x):
    return jnp.maximum(x, 0.0) + jnp.log(1.0 + jnp.exp(-jnp.abs(x)))


def _layer_norm(y, g, b):
    mu = jnp.mean(y, axis=1, keepdims=True)
    d = y - mu
    var = jnp.mean(d * d, axis=1, keepdims=True)
    return d * lax.rsqrt(var + LN_EPS) * g + b


def _proj_kernel(x_ref, xm_ref, w_ref, wt_ref, cos_ref, sin_ref, cosm_ref, sinm_ref,
                 p_ref, pm_ref, ba_ref, bam_ref, wb_sc, xs_sc, *, heads):
    j = pl.program_id(1)
    i = pl.program_id(2)

    def tail(x, xh):
        w = wt_ref[...]
        xl = (x - xh.astype(F32)).astype(BF16)
        wh = w.astype(BF16)
        wl = (w - wh.astype(F32)).astype(BF16)
        return (jnp.dot(xh, wh, preferred_element_type=F32)
                + jnp.dot(xh, wl, preferred_element_type=F32)
                + jnp.dot(xl, wh, preferred_element_type=F32))

    def emit(acc, cos_r, sin_r, out_ref):
        @pl.when(j < 2)
        def _rope():
            cos = cos_r[...]
            sin = sin_r[...]
            lane = lax.broadcasted_iota(jnp.int32, (1, HEAD_W), 1)
            first_half = (lane % ATT_QK_DIM) < (ATT_QK_DIM // 2)
            scale = jnp.where(j == 0, ATT_QK_DIM ** -0.5, 1.0).astype(F32)
            for h in range(heads):
                blk = acc[:, h * HEAD_W:(h + 1) * HEAD_W]
                partner = jnp.where(first_half,
                                    pltpu.roll(blk, HEAD_W - ATT_QK_DIM // 2, 1),
                                    pltpu.roll(blk, ATT_QK_DIM // 2, 1))
                out_ref[h] = ((blk * cos + partner * sin) * scale).astype(BF16)

        @pl.when(j >= 2)
        def _plain():
            for h in range(heads):
                out_ref[h] = acc[:, h * HEAD_W:(h + 1) * HEAD_W].astype(BF16)

    @pl.when(i == 0)
    def _new_column_tile():
        wb_sc[...] = w_ref[...].astype(BF16)
        xm = xm_ref[...]
        xmh = xm.astype(BF16)
        emit(jnp.dot(xmh, wb_sc[...], preferred_element_type=F32), cosm_ref, sinm_ref, pm_ref)

        @pl.when(j == 0)
        def _meta_tail():
            bam_ref[...] = tail(xm, xmh)

    @pl.when(j == 0)
    def _first_visit():
        x = x_ref[...]
        xh = x.astype(BF16)
        xs_sc[i] = xh
        ba_ref[...] = tail(x, xh)

    emit(jnp.dot(xs_sc[i], wb_sc[...], preferred_element_type=F32), cos_ref, sin_ref, p_ref)


def _proj(x2d, x_meta, w_in, layer, w_tail, cos, sin, cos_meta, sin_meta, bm, n_pass):
    m, d = x2d.shape
    n_meta = x_meta.shape[0]
    heads = GROUP_W // HEAD_W
    n_blocks = N_GROUPS * heads
    t = m // (bm * n_pass)
    const = lambda shape: pl.BlockSpec(shape, lambda p, j, i: (0,) * len(shape))
    row = lambda p, j, i: (p * t + i, 0)
    parked = lambda p, j, i: (p * t + jnp.where(j == 0, i, t - 1), 0)
    p_real, p_meta, ba, ba_meta = pl.pallas_call(
        functools.partial(_proj_kernel, heads=heads),
        out_shape=(jax.ShapeDtypeStruct((n_blocks, m, HEAD_W), BF16),
                   jax.ShapeDtypeStruct((n_pass, n_blocks, n_meta, HEAD_W), BF16),
                   jax.ShapeDtypeStruct((m, LANES), F32),
                   jax.ShapeDtypeStruct((n_pass, n_meta, LANES), F32)),
        grid=(n_pass, N_GROUPS, t),
        in_specs=[
            pl.BlockSpec((bm, d), parked),
            const((n_meta, d)),
            pl.BlockSpec((None, d, GROUP_W), lambda p, j, i: (layer, 0, j)),
            const((d, LANES)),
            pl.BlockSpec((bm, HEAD_W), row),
            pl.BlockSpec((bm, HEAD_W), row),
            const((n_meta, HEAD_W)),
            const((n_meta, HEAD_W)),
        ],
        out_specs=(
            pl.BlockSpec((heads, bm, HEAD_W), lambda p, j, i: (j, p * t + i, 0)),
            pl.BlockSpec((None, heads, n_meta, HEAD_W), lambda p, j, i: (p, j, 0, 0)),
            pl.BlockSpec((bm, LANES), parked),
            pl.BlockSpec((None, n_meta, LANES), lambda p, j, i: (p, 0, 0)),
        ),
        scratch_shapes=[pltpu.VMEM((d, GROUP_W), BF16), pltpu.VMEM((t, bm, d), BF16)],
        compiler_params=_params("arbitrary", "arbitrary", "arbitrary"),
        name="proj",
    )(x2d, x_meta, w_in, w_tail, cos, sin, cos_meta, sin_meta)
    return p_real, p_meta[0], ba, ba_meta[0]


def _attn_kernel(*refs, tq, hps, lam_init, meta_only):
    if meta_only:
        q_ref, km_ref, vm_ref, lq1, lk1, lq2, lk2, nw_ref, o_ref, m_sc, l_sc, acc_sc = refs
    else:
        (q_ref, k_ref, v_ref, km_ref, vm_ref, lq1, lk1, lq2, lk2, nw_ref, o_ref,
         m_sc, l_sc, acc_sc, s_sc) = refs
    lane = lax.broadcasted_iota(jnp.int32, (1, HEAD_W), 1)
    qc = []
    for h in range(hps):
        q = q_ref[h]
        zero = jnp.zeros_like(q)
        qc.append((jnp.where(lane < ATT_QK_DIM, q, zero), jnp.where(lane >= ATT_QK_DIM, q, zero)))

    for h in range(hps):
        for c in range(2):
            s = lax.dot_general(km_ref[h], qc[h][c], _NT, preferred_element_type=F32)
            m0 = jnp.max(s, axis=0, keepdims=True)
            p = jnp.exp(s - m0)
            m_sc[2 * h + c] = m0
            l_sc[2 * h + c] = jnp.sum(p, axis=0, keepdims=True)
            acc_sc[2 * h + c] = lax.dot_general(vm_ref[h], p.astype(BF16), _TN,
                                                preferred_element_type=F32)

    nc = 2 * hps

    def scores(t):
        off = pl.multiple_of(t * tq, tq)
        base = (t % 2) * nc
        for h in range(hps):
            kt = k_ref[h, pl.ds(off, tq), :]
            for c in range(2):
                s_sc[base + 2 * h + c] = lax.dot_general(kt, qc[h][c], _NT,
                                                         preferred_element_type=F32)

    def absorb(t, mask):
        off = pl.multiple_of(t * tq, tq)
        base = (t % 2) * nc
        for h in range(hps):
            vt = v_ref[h, pl.ds(off, tq), :]
            for c in range(2):
                i = 2 * h + c
                s = s_sc[base + i]
                if mask is not None:
                    s = jnp.where(mask, s, -jnp.inf)
                m_old = m_sc[i]
                m_new = jnp.maximum(m_old, jnp.max(s, axis=0, keepdims=True))
                a = jnp.exp(m_old - m_new)
                p = jnp.exp(s - m_new)
                l_sc[i] = a * l_sc[i] + jnp.sum(p, axis=0, keepdims=True)
                acc_sc[i] = a * acc_sc[i] + lax.dot_general(vt, p.astype(BF16), _TN,
                                                            preferred_element_type=F32)
                m_sc[i] = m_new

    if not meta_only:
        qi = pl.program_id(2)
        scores(0)

        def body(t, carry):
            scores(t)
            absorb(t - 1, None)
            return carry

        lax.fori_loop(1, qi + 1, body, 0)
        keys = lax.broadcasted_iota(jnp.int32, (tq, tq), 0)
        queries = lax.broadcasted_iota(jnp.int32, (tq, tq), 1)
        absorb(qi, (keys // CHUNK) <= (queries // CHUNK))

    lam = (jnp.exp(jnp.sum(lq1[...] * lk1[...], axis=1, keepdims=True))
           - jnp.exp(jnp.sum(lq2[...] * lk2[...], axis=1, keepdims=True)) + lam_init)
    for h in range(hps):
        o = acc_sc[2 * h] / l_sc[2 * h] - lam * (acc_sc[2 * h + 1] / l_sc[2 * h + 1])
        ms = jnp.mean(o * o, axis=0, keepdims=True)
        o = o * lax.rsqrt(ms + RMS_EPS) * nw_ref[...] * (1.0 - lam_init)
        o_ref[:, h * HEAD_W:(h + 1) * HEAD_W] = o.T.astype(BF16)


def _attention(p_q, p_kv, p_meta, lams, norm_w, *, batch, seq, tq, hps, lam_init, meta_only):
    n_meta = p_meta.shape[1]
    rows = p_q.shape[1]
    nq = 1 if meta_only else seq // tq
    k_blk, v_blk = ATT_HEADS // hps, 2 * ATT_HEADS // hps
    small = [pl.BlockSpec((1, ATT_QK_DIM), lambda b, g, i: (0, 0))] * 4 + [
        pl.BlockSpec((HEAD_W, 1), lambda b, g, i: (0, 0))]
    meta_specs = [pl.BlockSpec((hps, n_meta, HEAD_W), lambda b, g, i: (k_blk + g, 0, 0)),
                  pl.BlockSpec((hps, n_meta, HEAD_W), lambda b, g, i: (v_blk + g, 0, 0))]
    q_spec = pl.BlockSpec((hps, tq, HEAD_W), lambda b, g, i: (g, b * nq + i, 0))
    if meta_only:
        in_specs = [q_spec] + meta_specs + small
        args = (p_q, p_meta, p_meta) + lams + (norm_w,)
    else:
        kv_specs = [pl.BlockSpec((hps, seq, HEAD_W), lambda b, g, i: (k_blk + g, b, 0)),
                    pl.BlockSpec((hps, seq, HEAD_W), lambda b, g, i: (v_blk + g, b, 0))]
        in_specs = [q_spec] + kv_specs + meta_specs + small
        args = (p_q, p_kv, p_kv, p_meta, p_meta) + lams + (norm_w,)
    scratch = [pltpu.VMEM((2 * hps, 1, tq), F32), pltpu.VMEM((2 * hps, 1, tq), F32),
               pltpu.VMEM((2 * hps, HEAD_W, tq), F32)]
    if not meta_only:
        scratch.append(pltpu.VMEM((2 * 2 * hps, tq, tq), F32))
    return pl.pallas_call(
        functools.partial(_attn_kernel, tq=tq, hps=hps, lam_init=lam_init, meta_only=meta_only),
        out_shape=jax.ShapeDtypeStruct((rows, GROUP_W), BF16),
        grid=(batch, ATT_HEADS // hps, nq),
        in_specs=in_specs,
        out_specs=pl.BlockSpec((tq, hps * HEAD_W), lambda b, g, i: (b * nq + i, g)),
        scratch_shapes=scratch,
        compiler_params=_params("parallel", "parallel", "arbitrary"),
        name="attn_meta" if meta_only else "attn",
    )(*args)


def _gdn_prep_kernel(q_ref, k_ref, v_ref, hq_ref, hk_ref, hv_ref, mh_ref, cw_ref, ba_ref, alog_ref,
                     dtb_ref, u_ref, wq_ref, kd_ref, a_ref, gl_ref, xs_sc, *, rows, n_pad):
    first = pl.program_id(1) == 0
    ba = ba_ref[...]
    beta_all = _sigmoid(ba)
    g_all = -jnp.exp(alog_ref[...]) * _softplus(ba + dtb_ref[...])
    if n_pad:
        valid = lax.broadcasted_iota(jnp.int32, (rows, 1), 0) >= n_pad
        beta_all = jnp.where(valid, beta_all, 0.0)
        g_all = jnp.where(valid, g_all, 0.0)

    ri = lax.broadcasted_iota(jnp.int32, (rows, rows), 0)
    ci = lax.broadcasted_iota(jnp.int32, (rows, rows), 1)
    same = (ri // CHUNK) == (ci // CHUNK)
    tril = jnp.logical_and(same, ci <= ri)
    strict = jnp.logical_and(same, ci < ri)
    ones_tril = jnp.where(tril, 1.0, 0.0).astype(F32)
    ones_triu = jnp.where(jnp.logical_and(same, ri <= ci), 1.0, 0.0).astype(F32)
    ones_same = jnp.where(same, 1.0, 0.0).astype(F32)
    hp = lax.Precision.HIGHEST
    gc_all = jnp.dot(ones_tril, g_all, precision=hp, preferred_element_type=F32)
    gl_all = jnp.dot(ones_same, g_all, precision=hp, preferred_element_type=F32)
    gct_all = lax.dot_general(g_all, ones_triu, _TN, precision=hp, preferred_element_type=F32)
    gl_ref[...] = gl_all

    def conv_act(main_ref, halo_ref, stream, h):
        halo = jnp.where(first, mh_ref[stream * DN_HEADS + h], halo_ref[h])
        xs_sc[stream, 0:HALO, :] = halo.astype(F32)
        xs_sc[stream, HALO:, :] = main_ref[h].astype(F32)
        w = cw_ref[stream * DN_HEADS + h]
        y = xs_sc[stream, pl.ds(HALO - DN_CONV + 1, rows), :] * w[0:1]
        for j in range(1, DN_CONV):
            y = y + xs_sc[stream, pl.ds(HALO - DN_CONV + 1 + j, rows), :] * w[j:j + 1]
        return _silu(y)

    for h in range(DN_HEADS):
        q = conv_act(q_ref, hq_ref, 0, h)
        k = conv_act(k_ref, hk_ref, 1, h)
        v = conv_act(v_ref, hv_ref, 2, h)
        q = q * lax.rsqrt(jnp.sum(q * q, axis=1, keepdims=True) + RMS_EPS) * (HEAD_W ** -0.5)
        k = k * lax.rsqrt(jnp.sum(k * k, axis=1, keepdims=True) + RMS_EPS)
        beta = beta_all[:, h:h + 1]
        gc = gc_all[:, DN_HEADS + h:DN_HEADS + h + 1]
        gl = gl_all[:, DN_HEADS + h:DN_HEADS + h + 1]
        gcr = gct_all[DN_HEADS + h:DN_HEADS + h + 1, :]
        egc = jnp.exp(gc)
        kb = k * beta
        k16 = k.astype(BF16)
        kk = lax.dot_general(kb.astype(BF16), k16, _NT, preferred_element_type=F32)
        qk = lax.dot_general(q.astype(BF16), k16, _NT, preferred_element_type=F32)
        dec = jnp.exp(jnp.where(tril, gc - gcr, -jnp.inf))
        a_ref[h] = (qk * dec).astype(BF16)
        pw = jnp.where(strict, kk * dec, 0.0)
        x = jnp.concatenate([v * beta, kb * egc], axis=1)
        x = x - jnp.dot(pw.astype(BF16), x.astype(BF16), preferred_element_type=F32)
        for _ in range(int(math.log2(CHUNK)) - 1):
            p16 = pw.astype(BF16)
            pw = jnp.dot(p16, p16, preferred_element_type=F32)
            x = x + jnp.dot(pw.astype(BF16), x.astype(BF16), preferred_element_type=F32)
        u_ref[h] = x[:, :HEAD_W].astype(BF16)
        w16 = x[:, HEAD_W:].astype(BF16)
        qd16 = (q * egc).astype(BF16)
        for c in range(rows // CHUNK):
            wq_ref[h, 2 * c * CHUNK:(2 * c + 1) * CHUNK, :] = w16[c * CHUNK:(c + 1) * CHUNK]
            wq_ref[h, (2 * c + 1) * CHUNK:(2 * c + 2) * CHUNK, :] = qd16[c * CHUNK:(c + 1) * CHUNK]
        kd_ref[h] = (k * jnp.exp(gl - gc)).astype(BF16)


def _gdn_prep(p, p_halo_src, meta_halo, conv_w, ba, alog, dtb, *, batch, seq, rows, n_pad):
    m = batch * seq
    nt = seq // rows
    g_q, g_k, g_v = 3, 4, 5
    hb = rows // HALO

    def main_spec(g):
        return pl.BlockSpec((DN_HEADS, rows, HEAD_W), lambda b, t: (g, b * nt + t, 0))

    def halo_spec(g):
        return pl.BlockSpec((DN_HEADS, HALO, HEAD_W),
                            lambda b, t: (g, jnp.maximum((b * nt + t) * hb - 1, 0), 0))

    full = lambda shape: pl.BlockSpec(shape, lambda b, t: (0,) * len(shape))
    out_shape = (jax.ShapeDtypeStruct((DN_HEADS, m, HEAD_W), BF16),
                 jax.ShapeDtypeStruct((DN_HEADS, 2 * m, HEAD_W), BF16),
                 jax.ShapeDtypeStruct((DN_HEADS, m, HEAD_W), BF16),
                 jax.ShapeDtypeStruct((DN_HEADS, m, rows), BF16),
                 jax.ShapeDtypeStruct((m, LANES), F32))
    out_specs = (pl.BlockSpec((DN_HEADS, rows, HEAD_W), lambda b, t: (0, b * nt + t, 0)),
                 pl.BlockSpec((DN_HEADS, 2 * rows, HEAD_W), lambda b, t: (0, b * nt + t, 0)),
                 pl.BlockSpec((DN_HEADS, rows, HEAD_W), lambda b, t: (0, b * nt + t, 0)),
                 pl.BlockSpec((DN_HEADS, rows, rows), lambda b, t: (0, b * nt + t, 0)),
                 pl.BlockSpec((rows, LANES), lambda b, t: (b * nt + t, 0)))
    return pl.pallas_call(
        functools.partial(_gdn_prep_kernel, rows=rows, n_pad=n_pad),
        out_shape=out_shape,
        grid=(batch, nt),
        in_specs=[main_spec(g_q), main_spec(g_k), main_spec(g_v),
                  halo_spec(g_q), halo_spec(g_k), halo_spec(g_v),
                  full(meta_halo.shape), full(conv_w.shape),
                  pl.BlockSpec((rows, LANES), lambda b, t: (b * nt + t, 0)),
                  full(alog.shape), full(dtb.shape)],
        out_specs=out_specs,
        scratch_shapes=[pltpu.VMEM((3, rows + HALO, HEAD_W), F32)],
        compiler_params=_params("parallel", "parallel"),
        name="gdn_prep",
    )(p, p, p, p_halo_src, p_halo_src, p_halo_src, meta_halo, conv_w, ba, alog, dtb)


def _gdn_scan_kernel(u_ref, wq_ref, kd_ref, a_ref, gl_ref, z_ref,
                     um_ref, wqm_ref, kdm_ref, am_ref, glm_ref, zm_ref, nw_ref,
                     o_ref, om_ref, s_sc, vbuf_sc, *, rows, hpg):
    hg = pl.program_id(1)
    t = pl.program_id(2)
    lane = lax.broadcasted_iota(jnp.int32, (1, LANES), 1)
    nw = nw_ref[...]

    def decay_total(gl_row, h):
        sel = lane == DN_HEADS + hg * hpg + h
        return jnp.sum(jnp.where(sel, jnp.exp(gl_row), 0.0), axis=1, keepdims=True)

    def gate(o, z):
        ms = jnp.mean(o * o, axis=1, keepdims=True)
        return (o * lax.rsqrt(ms + RMS_EPS) * nw * _silu(z.astype(F32))).astype(BF16)

    def chunk_step(h, u, wq, kd, egl, intra):
        s = s_sc[h]
        sw = jnp.dot(wq, s.astype(BF16), preferred_element_type=F32)
        v_new = u.astype(F32) - sw[:CHUNK]
        v16 = v_new.astype(BF16)
        o = sw[CHUNK:] + intra(v16)
        s_sc[h] = egl * s + lax.dot_general(kd, v16, _TN, preferred_element_type=F32)
        return o

    @pl.when(t == 0)
    def _meta_chunk():
        vbuf_sc[...] = jnp.zeros_like(vbuf_sc)
        glm = glm_ref[0:1, :]
        for h in range(hpg):
            s_sc[h] = jnp.zeros((HEAD_W, HEAD_W), F32)
            o = chunk_step(h, um_ref[h], wqm_ref[h], kdm_ref[h], decay_total(glm, h),
                           lambda v16, h=h: jnp.dot(am_ref[h], v16, preferred_element_type=F32))
            om_ref[:, h * HEAD_W:(h + 1) * HEAD_W] = gate(o, zm_ref[h])

    for c in range(rows // CHUNK):
        r0 = c * CHUNK
        gl_row = gl_ref[r0:r0 + 1, :]
        for h in range(hpg):
            def intra(v16, h=h, r0=r0):
                vbuf_sc[h, r0:r0 + CHUNK, :] = v16
                return jnp.dot(a_ref[h, r0:r0 + CHUNK, :], vbuf_sc[h], preferred_element_type=F32)

            o = chunk_step(h, u_ref[h, r0:r0 + CHUNK, :], wq_ref[h, 2 * r0:2 * r0 + 2 * CHUNK, :],
                           kd_ref[h, r0:r0 + CHUNK, :], decay_total(gl_row, h), intra)
            o_ref[r0:r0 + CHUNK, h * HEAD_W:(h + 1) * HEAD_W] = gate(o, z_ref[h, r0:r0 + CHUNK, :])


def _gdn_scan(real, meta, p, p_meta_pad, norm_w, *, batch, seq, rows, hpg):
    u, wq, kd, a, gl = real
    um, wqm, kdm, am, glm = meta
    m = batch * seq
    nt = seq // rows
    ngrp = DN_HEADS // hpg
    z_blk = (6 * DN_HEADS) // hpg
    mrows = um.shape[1]

    def real_spec(r, w):
        return pl.BlockSpec((hpg, r, w), lambda b, g, t: (g, b * nt + t, 0))

    def meta_spec(r, w):
        return pl.BlockSpec((hpg, r, w), lambda b, g, t: (g, 0, 0))

    return pl.pallas_call(
        functools.partial(_gdn_scan_kernel, rows=rows, hpg=hpg),
        out_shape=(jax.ShapeDtypeStruct((m, GROUP_W), BF16),
                   jax.ShapeDtypeStruct((batch, mrows, GROUP_W), BF16)),
        grid=(batch, ngrp, nt),
        in_specs=[real_spec(rows, HEAD_W), real_spec(2 * rows, HEAD_W), real_spec(rows, HEAD_W),
                  real_spec(rows, rows),
                  pl.BlockSpec((rows, LANES), lambda b, g, t: (b * nt + t, 0)),
                  pl.BlockSpec((hpg, rows, HEAD_W), lambda b, g, t: (z_blk + g, b * nt + t, 0)),
                  meta_spec(mrows, HEAD_W), meta_spec(2 * mrows, HEAD_W), meta_spec(mrows, HEAD_W),
                  meta_spec(mrows, mrows),
                  pl.BlockSpec((mrows, LANES), lambda b, g, t: (0, 0)),
                  pl.BlockSpec((hpg, mrows, HEAD_W), lambda b, g, t: (z_blk + g, 0, 0)),
                  pl.BlockSpec((1, HEAD_W), lambda b, g, t: (0, 0))],
        out_specs=(pl.BlockSpec((rows, hpg * HEAD_W), lambda b, g, t: (b * nt + t, g)),
                   pl.BlockSpec((None, mrows, hpg * HEAD_W), lambda b, g, t: (b, 0, g))),
        scratch_shapes=[pltpu.VMEM((hpg, HEAD_W, HEAD_W), F32),
                        pltpu.VMEM((hpg, rows, HEAD_W), BF16)],
        compiler_params=_params("parallel", "parallel", "arbitrary"),
        name="gdn_scan",
    )(u, wq, kd, a, gl, p, um, wqm, kdm, am, glm, p_meta_pad, norm_w)


def _wout_ln_kernel(oa_ref, od_ref, w_ref, x_ref, g_ref, b_ref, h_ref, *, alpha):
    half = oa_ref.shape[1]
    mix = (jnp.dot(oa_ref[...], w_ref[0:half, :], preferred_element_type=F32)
           + jnp.dot(od_ref[...], w_ref[half:, :], preferred_element_type=F32))
    h_ref[...] = _layer_norm(alpha * x_ref[...] + mix, g_ref[...], b_ref[...])


def _wout_ln(o_att, o_dn, w_out, x2d, g, b, *, bm, alpha):
    m, d = x2d.shape
    half = o_att.shape[1]
    return pl.pallas_call(
        functools.partial(_wout_ln_kernel, alpha=alpha),
        out_shape=jax.ShapeDtypeStruct((m, d), F32),
        grid=(m // bm,),
        in_specs=[pl.BlockSpec((bm, half), lambda i: (i, 0)),
                  pl.BlockSpec((bm, half), lambda i: (i, 0)),
                  pl.BlockSpec((2 * half, d), lambda i: (0, 0)),
                  pl.BlockSpec((bm, d), lambda i: (i, 0)),
                  pl.BlockSpec((1, d), lambda i: (0, 0)),
                  pl.BlockSpec((1, d), lambda i: (0, 0))],
        out_specs=pl.BlockSpec((bm, d), lambda i: (i, 0)),
        compiler_params=_params("parallel"),
        name="wout_ln",
    )(o_att, o_dn, w_out, x2d, g, b)


def _ffn_kernel(h_ref, hprev_ref, hmeta_ref, wg_ref, wu_ref, cw_ref, cb_ref, wd_ref, g_ref, b_ref,
                o_ref, xb_sc, gate_sc, *, bm, tiles_per_batch, alpha):
    i = pl.program_id(0)
    f = pl.program_id(1)

    @pl.when(f == 0)
    def _init():
        halo = jnp.where(i % tiles_per_batch == 0, hmeta_ref[...], hprev_ref[...])
        xb_sc[0:HALO, :] = halo.astype(BF16)
        xb_sc[HALO:, :] = h_ref[...].astype(BF16)
        o_ref[...] = jnp.zeros_like(o_ref)

    gate_sc[...] = jnp.dot(xb_sc[...], wg_ref[...], preferred_element_type=F32)
    up = jnp.dot(xb_sc[HALO:, :], wu_ref[...], preferred_element_type=F32)
    cw = cw_ref[...]
    hg = cb_ref[...] + gate_sc[pl.ds(HALO - FFN_CONV + 1, bm), :] * cw[0:1]
    for j in range(1, FFN_CONV):
        hg = hg + gate_sc[pl.ds(HALO - FFN_CONV + 1 + j, bm), :] * cw[j:j + 1]
    act = (_silu(hg) * up).astype(BF16)
    o_ref[...] += jnp.dot(act, wd_ref[...], preferred_element_type=F32)

    @pl.when(f == pl.num_programs(1) - 1)
    def _finish():
        o_ref[...] = _layer_norm(alpha * h_ref[...] + o_ref[...], g_ref[...], b_ref[...])


def _ffn(h1, h1_meta, w_gate, w_up, conv_w, conv_b, w_down, g, b, *, seq, bm, tf, alpha):
    m, d = h1.shape
    dff = w_gate.shape[1]
    tiles_per_batch = seq // bm
    hb = bm // HALO
    meta_blk = h1_meta.shape[0] // HALO - 1
    return pl.pallas_call(
        functools.partial(_ffn_kernel, bm=bm, tiles_per_batch=tiles_per_batch, alpha=alpha),
        out_shape=jax.ShapeDtypeStruct((m, d), F32),
        grid=(m // bm, dff // tf),
        in_specs=[pl.BlockSpec((bm, d), lambda i, f: (i, 0), pipeline_mode=pl.Buffered(1)),
                  pl.BlockSpec((HALO, d), lambda i, f: (jnp.maximum(i * hb - 1, 0), 0)),
                  pl.BlockSpec((HALO, d), lambda i, f: (meta_blk, 0)),
                  pl.BlockSpec((d, tf), lambda i, f: (0, f)),
                  pl.BlockSpec((d, tf), lambda i, f: (0, f)),
                  pl.BlockSpec((FFN_CONV, tf), lambda i, f: (0, f)),
                  pl.BlockSpec((1, tf), lambda i, f: (0, f)),
                  pl.BlockSpec((tf, d), lambda i, f: (f, 0)),
                  pl.BlockSpec((1, d), lambda i, f: (0, 0)),
                  pl.BlockSpec((1, d), lambda i, f: (0, 0))],
        out_specs=pl.BlockSpec((bm, d), lambda i, f: (i, 0)),
        scratch_shapes=[pltpu.VMEM((bm + HALO, d), BF16),
                        pltpu.VMEM((bm + HALO, tf), F32)],
        compiler_params=_params("parallel", "arbitrary"),
        name="ffn",
    )(h1, h1, h1_meta, w_gate, w_up, conv_w, conv_b, w_down, g, b)


def _largest_tile(n, cap, mult):
    best = None
    for c in range(mult, min(n, cap) + 1, mult):
        if n % c == 0:
            best = c
    assert best is not None, (n, cap, mult)
    return best


def _rope_tables(n_pos):
    half = ATT_QK_DIM // 2
    inv_freq = ROPE_THETA ** (-jnp.arange(0, ATT_QK_DIM, 2, dtype=F32) / ATT_QK_DIM)
    ang = jnp.arange(n_pos, dtype=F32)[:, None] * inv_freq[None, :]
    cos, sin = jnp.cos(ang), jnp.sin(ang)
    reps = HEAD_W // ATT_QK_DIM
    cos_t = jnp.tile(jnp.concatenate([cos, cos], axis=1), (1, reps))
    sin_t = jnp.tile(jnp.concatenate([-sin, sin], axis=1), (1, reps))
    assert cos_t.shape[1] == HEAD_W and half * 2 == ATT_QK_DIM
    return cos_t, sin_t


def kernel(x, meta_tokens, w_in, conv_qkv_w, a_log, dt_bias, lambda_q1, lambda_k1, lambda_q2, lambda_k2,
           diff_norm_w, delta_norm_w, w_out, ln1_g, ln1_b, ffn_w_gate, ffn_w_up, ffn_conv_w, ffn_conv_b,
           ffn_w_down, ln2_g, ln2_b):
    batch, seq, d = x.shape
    n_meta = meta_tokens.shape[0]
    depth = w_in.shape[0]
    n_main = N_GROUPS * GROUP_W
    assert w_in.shape[2] == n_main + TAIL_COLS and n_meta % HALO == 0 and n_meta <= CHUNK
    m = batch * seq
    alpha = (2.0 * depth) ** 0.25

    bm_proj = _largest_tile(m, 512, HALO)
    n_pass = 1
    while (m // n_pass) * d * 2 > VMEM_LIMIT // 3 and (m // bm_proj) % (2 * n_pass) == 0:
        n_pass *= 2
    bm_row = _largest_tile(seq, 512, HALO)
    bm_ffn = _largest_tile(seq, 1024, HALO)
    tq = _largest_tile(seq, 256, CHUNK)
    rows = _largest_tile(seq, 256, CHUNK)
    hpg = 4
    att_hps = 4

    cos_t, sin_t = _rope_tables(n_meta + seq)
    cos_meta, sin_meta = cos_t[:n_meta], sin_t[:n_meta]
    cos_real = jnp.tile(cos_t[n_meta:], (batch, 1))
    sin_real = jnp.tile(sin_t[n_meta:], (batch, 1))

    h = x.reshape(m, d)
    h_meta = meta_tokens.astype(x.dtype)
    pad_rows = CHUNK - n_meta
    for l in range(depth):
        w_tail = jnp.pad(w_in[l, :, n_main:], ((0, 0), (0, LANES - TAIL_COLS)))
        lane_pad = (DN_HEADS, LANES - TAIL_COLS)
        alog = jnp.pad(a_log[l].astype(F32), lane_pad)[None]
        dtb = jnp.pad(dt_bias[l].astype(F32), lane_pad)[None]
        conv_w = conv_qkv_w[l].reshape(DN_CONV, 3 * DN_HEADS, HEAD_W).transpose(1, 0, 2)
        lams = tuple(v[l][None].astype(F32) for v in (lambda_q1, lambda_k1, lambda_q2, lambda_k2))
        lam_init = 0.8 - 0.6 * math.exp(-0.3 * l)

        p, p_meta, ba, ba_meta = _proj(h, h_meta, w_in, l, w_tail, cos_real, sin_real, cos_meta,
                                       sin_meta, bm_proj, n_pass)

        att_nw = diff_norm_w[l].astype(F32)[:, None]
        o_att = _attention(p, p, p_meta, lams, att_nw, batch=batch, seq=seq, tq=tq, hps=att_hps,
                           lam_init=lam_init, meta_only=False)
        q_meta = jnp.pad(p_meta[:ATT_HEADS], ((0, 0), (0, LANES - n_meta), (0, 0)))
        o_att_meta = _attention(q_meta, None, p_meta, lams, att_nw, batch=1, seq=LANES, tq=LANES,
                                hps=att_hps, lam_init=lam_init, meta_only=True)[:n_meta]

        p_meta_pad = jnp.pad(p_meta, ((0, 0), (pad_rows, 0), (0, 0)))
        ba_meta_pad = jnp.pad(ba_meta, ((pad_rows, 0), (0, 0)))
        meta_halo = p_meta[3 * DN_HEADS:6 * DN_HEADS, n_meta - HALO:, :]
        zero_halo = jnp.zeros_like(meta_halo)
        prep_real = _gdn_prep(p, p, meta_halo, conv_w, ba, alog, dtb, batch=batch, seq=seq, rows=rows,
                              n_pad=0)
        prep_meta = _gdn_prep(p_meta_pad, p_meta_pad, zero_halo, conv_w, ba_meta_pad, alog, dtb, batch=1,
                              seq=CHUNK, rows=CHUNK, n_pad=pad_rows)
        o_dn, o_dn_meta = _gdn_scan(prep_real, prep_meta, p, p_meta_pad, delta_norm_w[l][None],
                                    batch=batch, seq=seq, rows=rows, hpg=hpg)
        o_dn_meta = o_dn_meta[0, pad_rows:]

        w_out_b = w_out[l].astype(BF16)
        g1, b1 = ln1_g[l][None].astype(F32), ln1_b[l][None].astype(F32)
        h1 = _wout_ln(o_att, o_dn, w_out_b, h, g1, b1, bm=bm_row, alpha=alpha)
        h1_meta = _wout_ln(o_att_meta, o_dn_meta, w_out_b, h_meta, g1, b1, bm=n_meta, alpha=alpha)

        dff = ffn_w_gate.shape[2]
        tf = _largest_tile(dff, 256, LANES)
        ffn_args = (ffn_w_gate[l].astype(BF16), ffn_w_up[l].astype(BF16), ffn_conv_w[l].astype(F32),
                    ffn_conv_b[l][None].astype(F32), ffn_w_down[l].astype(BF16),
                    ln2_g[l][None].astype(F32), ln2_b[l][None].astype(F32))
        h_next = _ffn(h1, h1_meta, *ffn_args, seq=seq, bm=bm_ffn, tf=tf, alpha=alpha)
        if l + 1 < depth:
            zero_meta = jnp.zeros_like(h1_meta)
            h_meta = _ffn(h1_meta, zero_meta, *ffn_args, seq=n_meta, bm=n_meta, tf=tf, alpha=alpha)
        h = h_next
    return h.reshape(batch, seq, d)
```

```python
import functools
import math

import jax
import jax.numpy as jnp
from jax import lax
from jax.experimental import pallas as pl
from jax.experimental.pallas import tpu as pltpu

F32 = jnp.float32
BF16 = jnp.bfloat16

CHUNK = 64
ATT_HEADS = 8
ATT_QK_DIM = 64
HEAD_W = 128
DN_HEADS = 8
DN_CONV = 4
FFN_CONV = 3
ROPE_THETA = 10000.0
LN_EPS = 1e-5
RMS_EPS = 1e-6
GROUP_W = ATT_HEADS * HEAD_W
N_GROUPS = 7
TAIL_COLS = 2 * DN_HEADS

LANES = 128
HALO = 8
VMEM_LIMIT = 56 * 1024 * 1024

_NT = (((1,), (1,)), ((), ()))
_TN = (((0,), (0,)), ((), ()))


def _params(*sem):
    return pltpu.CompilerParams(dimension_semantics=sem, vmem_limit_bytes=VMEM_LIMIT)


def _sigmoid(x):
    return 1.0 / (1.0 + jnp.exp(-x))


def _silu(x):
    return x * _sigmoid(x)


def _softplus(x):
    return jnp.maximum(x, 0.0) + jnp.log(1.0 + jnp.exp(-jnp.abs(x)))


def _layer_norm(y, g, b):
    mu = jnp.mean(y, axis=1, keepdims=True)
    d = y - mu
    var = jnp.mean(d * d, axis=1, keepdims=True)
    return d * lax.rsqrt(var + LN_EPS) * g + b


def _proj_kernel(x_ref, xm_ref, w_ref, wt_ref, cos_ref, sin_ref, cosm_ref, sinm_ref,
                 p_ref, pm_ref, ba_ref, bam_ref, *, heads):
    j = pl.program_id(0)
    i = pl.program_id(1)

    def tail(x, xh):
        w = wt_ref[...]
        xl = (x - xh.astype(F32)).astype(BF16)
        wh = w.astype(BF16)
        wl = (w - wh.astype(F32)).astype(BF16)
        return (jnp.dot(xh, wh, preferred_element_type=F32)
                + jnp.dot(xh, wl, preferred_element_type=F32)
                + jnp.dot(xl, wh, preferred_element_type=F32))

    def emit(acc, cos_r, sin_r, out_ref):
        @pl.when(j < 2)
        def _rope():
            cos = cos_r[...]
            sin = sin_r[...]
            lane = lax.broadcasted_iota(jnp.int32, (1, HEAD_W), 1)
            first_half = (lane % ATT_QK_DIM) < (ATT_QK_DIM // 2)
            scale = jnp.where(j == 0, ATT_QK_DIM ** -0.5, 1.0).astype(F32)
            for h in range(heads):
                blk = acc[:, h * HEAD_W:(h + 1) * HEAD_W]
                partner = jnp.where(first_half,
                                    pltpu.roll(blk, HEAD_W - ATT_QK_DIM // 2, 1),
                                    pltpu.roll(blk, ATT_QK_DIM // 2, 1))
                out_ref[h] = ((blk * cos + partner * sin) * scale).astype(BF16)

        @pl.when(j >= 2)
        def _plain():
            for h in range(heads):
                out_ref[h] = acc[:, h * HEAD_W:(h + 1) * HEAD_W].astype(BF16)

    @pl.when(i == 0)
    def _meta_rows():
        xm = xm_ref[...]
        xmh = xm.astype(BF16)
        emit(jnp.dot(xmh, w_ref[...], preferred_element_type=F32), cosm_ref, sinm_ref, pm_ref)

        @pl.when(j == 0)
        def _meta_tail():
            bam_ref[...] = tail(xm, xmh)

    x = x_ref[...]
    xh = x.astype(BF16)
    emit(jnp.dot(xh, w_ref[...], preferred_element_type=F32), cos_ref, sin_ref, p_ref)

    @pl.when(j == 0)
    def _tail():
        ba_ref[...] = tail(x, xh)


def _proj(x2d, x_meta, w_bf16, w_tail, cos, sin, cos_meta, sin_meta, bm):
    m, d = x2d.shape
    n_meta = x_meta.shape[0]
    heads = GROUP_W // HEAD_W
    n_blocks = N_GROUPS * heads
    nm = m // bm
    const = lambda shape: pl.BlockSpec(shape, lambda j, i: (0,) * len(shape))
    row = lambda j, i: (i, 0)
    return pl.pallas_call(
        functools.partial(_proj_kernel, heads=heads),
        out_shape=(jax.ShapeDtypeStruct((n_blocks, m, HEAD_W), BF16),
                   jax.ShapeDtypeStruct((n_blocks, n_meta, HEAD_W), BF16),
                   jax.ShapeDtypeStruct((m, LANES), F32),
                   jax.ShapeDtypeStruct((n_meta, LANES), F32)),
        grid=(N_GROUPS, nm),
        in_specs=[
            pl.BlockSpec((bm, d), row),
            const((n_meta, d)),
            pl.BlockSpec((d, GROUP_W), lambda j, i: (0, j)),
            const((d, LANES)),
            pl.BlockSpec((bm, HEAD_W), row),
            pl.BlockSpec((bm, HEAD_W), row),
            const((n_meta, HEAD_W)),
            const((n_meta, HEAD_W)),
        ],
        out_specs=(
            pl.BlockSpec((heads, bm, HEAD_W), lambda j, i: (j, i, 0)),
            pl.BlockSpec((heads, n_meta, HEAD_W), lambda j, i: (j, 0, 0)),
            pl.BlockSpec((bm, LANES), lambda j, i: (jnp.where(j == 0, i, nm - 1), 0)),
            const((n_meta, LANES)),
        ),
        compiler_params=_params("arbitrary", "arbitrary"),
        name="proj",
    )(x2d, x_meta, w_bf16, w_tail, cos, sin, cos_meta, sin_meta)


def _attn_kernel(*refs, tq, hps, lam_init, meta_only):
    if meta_only:
        q_ref, km_ref, vm_ref, lq1, lk1, lq2, lk2, nw_ref, o_ref, m_sc, l_sc, acc_sc = refs
    else:
        (q_ref, k_ref, v_ref, km_ref, vm_ref, lq1, lk1, lq2, lk2, nw_ref, o_ref,
         m_sc, l_sc, acc_sc, s_sc, a_sc, p_sc) = refs
    lane = lax.broadcasted_iota(jnp.int32, (1, HEAD_W), 1)
    qc = []
    for h in range(hps):
        q = q_ref[h]
        zero = jnp.zeros_like(q)
        qc.append((jnp.where(lane < ATT_QK_DIM, q, zero), jnp.where(lane >= ATT_QK_DIM, q, zero)))

    for h in range(hps):
        for c in range(2):
            s = lax.dot_general(km_ref[h], qc[h][c], _NT, preferred_element_type=F32)
            m0 = jnp.max(s, axis=0, keepdims=True)
            p = jnp.exp(s - m0)
            m_sc[2 * h + c] = m0
            l_sc[2 * h + c] = jnp.sum(p, axis=0, keepdims=True)
            acc_sc[2 * h + c] = lax.dot_general(vm_ref[h], p.astype(BF16), _TN,
                                                preferred_element_type=F32)

    nc = 2 * hps

    def scores(t, slot):
        off = pl.multiple_of(t * tq, tq)
        for h in range(hps):
            kt = k_ref[h, pl.ds(off, tq), :]
            for c in range(2):
                s_sc[slot * nc + 2 * h + c] = lax.dot_general(kt, qc[h][c], _NT,
                                                              preferred_element_type=F32)

    def absorb(t, slot, mask):
        off = pl.multiple_of(t * tq, tq)
        for i in range(nc):
            s = s_sc[slot * nc + i]
            if mask is not None:
                s = jnp.where(mask, s, -jnp.inf)
            m_old = m_sc[i]
            m_new = jnp.maximum(m_old, jnp.max(s, axis=0, keepdims=True))
            a = jnp.exp(m_old - m_new)
            p = jnp.exp(s - m_new)
            l_sc[i] = a * l_sc[i] + jnp.sum(p, axis=0, keepdims=True)
            m_sc[i] = m_new
            a_sc[i] = a
            p_sc[i] = p.astype(BF16)
        for h in range(hps):
            vt = v_ref[h, pl.ds(off, tq), :]
            for c in range(2):
                i = 2 * h + c
                acc_sc[i] = a_sc[i] * acc_sc[i] + lax.dot_general(vt, p_sc[i], _TN,
                                                                  preferred_element_type=F32)

    if not meta_only:
        qi = pl.program_id(2)
        keys = lax.broadcasted_iota(jnp.int32, (tq, tq), 0)
        queries = lax.broadcasted_iota(jnp.int32, (tq, tq), 1)
        diag = (keys // CHUNK) <= (queries // CHUNK)
        scores(0, 0)

        def body(k, carry):
            scores(2 * k + 1, 1)
            absorb(2 * k, 0, None)
            scores(2 * k + 2, 0)
            absorb(2 * k + 1, 1, None)
            return carry

        lax.fori_loop(0, qi // 2, body, 0)

        @pl.when(qi % 2 == 1)
        def _odd_tail():
            scores(qi, 1)
            absorb(qi - 1, 0, None)
            absorb(qi, 1, diag)

        @pl.when(qi % 2 == 0)
        def _even_tail():
            absorb(qi, 0, diag)

    lam = (jnp.exp(jnp.sum(lq1[...] * lk1[...], axis=1, keepdims=True))
           - jnp.exp(jnp.sum(lq2[...] * lk2[...], axis=1, keepdims=True)) + lam_init)
    for h in range(hps):
        o = acc_sc[2 * h] / l_sc[2 * h] - lam * (acc_sc[2 * h + 1] / l_sc[2 * h + 1])
        ms = jnp.mean(o * o, axis=0, keepdims=True)
        o = o * lax.rsqrt(ms + RMS_EPS) * nw_ref[...] * (1.0 - lam_init)
        o_ref[:, h * HEAD_W:(h + 1) * HEAD_W] = o.T.astype(BF16)


def _attention(p_q, p_kv, p_meta, lams, norm_w, *, batch, seq, tq, hps, lam_init, meta_only):
    n_meta = p_meta.shape[1]
    rows = p_q.shape[1]
    nq = 1 if meta_only else seq // tq
    k_blk, v_blk = ATT_HEADS // hps, 2 * ATT_HEADS // hps
    small = [pl.BlockSpec((1, ATT_QK_DIM), lambda b, g, i: (0, 0))] * 4 + [
        pl.BlockSpec((HEAD_W, 1), lambda b, g, i: (0, 0))]
    meta_specs = [pl.BlockSpec((hps, n_meta, HEAD_W), lambda b, g, i: (k_blk + g, 0, 0)),
                  pl.BlockSpec((hps, n_meta, HEAD_W), lambda b, g, i: (v_blk + g, 0, 0))]
    q_spec = pl.BlockSpec((hps, tq, HEAD_W), lambda b, g, i: (g, b * nq + i, 0))
    if meta_only:
        in_specs = [q_spec] + meta_specs + small
        args = (p_q, p_meta, p_meta) + lams + (norm_w,)
    else:
        kv_specs = [pl.BlockSpec((hps, seq, HEAD_W), lambda b, g, i: (k_blk + g, b, 0)),
                    pl.BlockSpec((hps, seq, HEAD_W), lambda b, g, i: (v_blk + g, b, 0))]
        in_specs = [q_spec] + kv_specs + meta_specs + small
        args = (p_q, p_kv, p_kv, p_meta, p_meta) + lams + (norm_w,)
    scratch = [pltpu.VMEM((2 * hps, 1, tq), F32), pltpu.VMEM((2 * hps, 1, tq), F32),
               pltpu.VMEM((2 * hps, HEAD_W, tq), F32)]
    if not meta_only:
        scratch += [pltpu.VMEM((2 * 2 * hps, tq, tq), F32),
                    pltpu.VMEM((2 * hps, 1, tq), F32),
                    pltpu.VMEM((2 * hps, tq, tq), BF16)]
    return pl.pallas_call(
        functools.partial(_attn_kernel, tq=tq, hps=hps, lam_init=lam_init, meta_only=meta_only),
        out_shape=jax.ShapeDtypeStruct((rows, GROUP_W), BF16),
        grid=(batch, ATT_HEADS // hps, nq),
        in_specs=in_specs,
        out_specs=pl.BlockSpec((tq, hps * HEAD_W), lambda b, g, i: (b * nq + i, g)),
        scratch_shapes=scratch,
        compiler_params=_params("parallel", "parallel", "arbitrary"),
        name="attn_meta" if meta_only else "attn",
    )(*args)


def _gdn_prep_kernel(q_ref, k_ref, v_ref, hq_ref, hk_ref, hv_ref, mh_ref, cw_ref, ba_ref, alog_ref,
                     dtb_ref, u_ref, wq_ref, kd_ref, a_ref, gl_ref, xs_sc, gc_sc, pw_sc, x_sc,
                     *, rows, sub, n_pad):
    first = pl.program_id(1) == 0
    n_sub = rows // sub
    ba = ba_ref[...]
    beta_all = _sigmoid(ba)
    g_all = -jnp.exp(alog_ref[...]) * _softplus(ba + dtb_ref[...])
    row = lax.broadcasted_iota(jnp.int32, (rows, 1), 0)
    if n_pad:
        beta_all = jnp.where(row >= n_pad, beta_all, 0.0)
        g_all = jnp.where(row >= n_pad, g_all, 0.0)

    pos = row % CHUNK
    gc_all = g_all
    shift = 1
    while shift < CHUNK:
        gc_all = gc_all + jnp.where(pos >= shift, pltpu.roll(gc_all, shift, 0), 0.0)
        shift *= 2
    gc_sc[...] = gc_all
    gl_all = jnp.concatenate(
        [jnp.broadcast_to(gc_sc[(c + 1) * CHUNK - 1:(c + 1) * CHUNK, :], (CHUNK, LANES))
         for c in range(rows // CHUNK)], axis=0)
    gl_ref[...] = gl_all
    gct_all = gc_all.T

    ri = lax.broadcasted_iota(jnp.int32, (sub, sub), 0)
    ci = lax.broadcasted_iota(jnp.int32, (sub, sub), 1)
    same = (ri // CHUNK) == (ci // CHUNK)
    tril = jnp.logical_and(same, ci <= ri)
    strict = jnp.logical_and(same, ci < ri)

    def conv_act(main_ref, halo_ref, stream, h):
        halo = jnp.where(first, mh_ref[stream * DN_HEADS + h], halo_ref[h])
        xs_sc[stream, 0:HALO, :] = halo.astype(F32)
        xs_sc[stream, HALO:, :] = main_ref[h].astype(F32)
        w = cw_ref[stream * DN_HEADS + h]
        y = xs_sc[stream, pl.ds(HALO - DN_CONV + 1, rows), :] * w[0:1]
        for j in range(1, DN_CONV):
            y = y + xs_sc[stream, pl.ds(HALO - DN_CONV + 1 + j, rows), :] * w[j:j + 1]
        return _silu(y)

    for h in range(DN_HEADS):
        q = conv_act(q_ref, hq_ref, 0, h)
        k = conv_act(k_ref, hk_ref, 1, h)
        v = conv_act(v_ref, hv_ref, 2, h)
        q = q * lax.rsqrt(jnp.sum(q * q, axis=1, keepdims=True) + RMS_EPS) * (HEAD_W ** -0.5)
        k = k * lax.rsqrt(jnp.sum(k * k, axis=1, keepdims=True) + RMS_EPS)
        beta = beta_all[:, h:h + 1]
        gc = gc_all[:, DN_HEADS + h:DN_HEADS + h + 1]
        gl = gl_all[:, DN_HEADS + h:DN_HEADS + h + 1]
        egc = jnp.exp(gc)
        kb = k * beta
        k16 = k.astype(BF16)
        kb16 = kb.astype(BF16)
        q16 = q.astype(BF16)
        x_all = jnp.concatenate([v * beta, kb * egc], axis=1)
        qd16 = (q * egc).astype(BF16)
        kd_ref[h] = (k * jnp.exp(gl - gc)).astype(BF16)
        for c in range(rows // CHUNK):
            wq_ref[h, (2 * c + 1) * CHUNK:(2 * c + 2) * CHUNK, :] = qd16[c * CHUNK:(c + 1) * CHUNK]
        for s in range(n_sub):
            lo, hi = s * sub, (s + 1) * sub
            gcr = gct_all[DN_HEADS + h:DN_HEADS + h + 1, lo:hi]
            kk = lax.dot_general(kb16[lo:hi], k16[lo:hi], _NT, preferred_element_type=F32)
            qk = lax.dot_general(q16[lo:hi], k16[lo:hi], _NT, preferred_element_type=F32)
            dec = jnp.exp(jnp.where(tril, gc[lo:hi] - gcr, -jnp.inf))
            a_ref[h, lo:hi, :] = (qk * dec).astype(BF16)
            pw_sc[h * n_sub + s] = jnp.where(strict, kk * dec, 0.0).astype(BF16)
            x_sc[h * n_sub + s] = x_all[lo:hi]

    items = range(DN_HEADS * n_sub)
    for it in items:
        x = x_sc[it]
        x_sc[it] = x - jnp.dot(pw_sc[it], x.astype(BF16), preferred_element_type=F32)
    for _ in range(int(math.log2(CHUNK)) - 1):
        for it in items:
            p16 = pw_sc[it]
            pw_sc[it] = jnp.dot(p16, p16, preferred_element_type=F32).astype(BF16)
        for it in items:
            x = x_sc[it]
            x_sc[it] = x + jnp.dot(pw_sc[it], x.astype(BF16), preferred_element_type=F32)
    for it in items:
        h, lo = it // n_sub, (it % n_sub) * sub
        x = x_sc[it]
        u_ref[h, lo:lo + sub, :] = x[:, :HEAD_W].astype(BF16)
        w16 = x[:, HEAD_W:].astype(BF16)
        for c in range(sub // CHUNK):
            r0 = lo + c * CHUNK
            wq_ref[h, 2 * r0:2 * r0 + CHUNK, :] = w16[c * CHUNK:(c + 1) * CHUNK]


def _gdn_prep(p, p_halo_src, meta_halo, conv_w, ba, alog, dtb, *, batch, seq, rows, sub, n_pad):
    m = batch * seq
    nt = seq // rows
    g_q, g_k, g_v = 3, 4, 5
    hb = rows // HALO

    def main_spec(g):
        return pl.BlockSpec((DN_HEADS, rows, HEAD_W), lambda b, t: (g, b * nt + t, 0))

    def halo_spec(g):
        return pl.BlockSpec((DN_HEADS, HALO, HEAD_W),
                            lambda b, t: (g, jnp.maximum((b * nt + t) * hb - 1, 0), 0))

    full = lambda shape: pl.BlockSpec(shape, lambda b, t: (0,) * len(shape))
    out_shape = (jax.ShapeDtypeStruct((DN_HEADS, m, HEAD_W), BF16),
                 jax.ShapeDtypeStruct((DN_HEADS, 2 * m, HEAD_W), BF16),
                 jax.ShapeDtypeStruct((DN_HEADS, m, HEAD_W), BF16),
                 jax.ShapeDtypeStruct((DN_HEADS, m, sub), BF16),
                 jax.ShapeDtypeStruct((m, LANES), F32))
    out_specs = (pl.BlockSpec((DN_HEADS, rows, HEAD_W), lambda b, t: (0, b * nt + t, 0)),
                 pl.BlockSpec((DN_HEADS, 2 * rows, HEAD_W), lambda b, t: (0, b * nt + t, 0)),
                 pl.BlockSpec((DN_HEADS, rows, HEAD_W), lambda b, t: (0, b * nt + t, 0)),
                 pl.BlockSpec((DN_HEADS, rows, sub), lambda b, t: (0, b * nt + t, 0)),
                 pl.BlockSpec((rows, LANES), lambda b, t: (b * nt + t, 0)))
    return pl.pallas_call(
        functools.partial(_gdn_prep_kernel, rows=rows, sub=sub, n_pad=n_pad),
        out_shape=out_shape,
        grid=(batch, nt),
        in_specs=[main_spec(g_q), main_spec(g_k), main_spec(g_v),
                  halo_spec(g_q), halo_spec(g_k), halo_spec(g_v),
                  full(meta_halo.shape), full(conv_w.shape),
                  pl.BlockSpec((rows, LANES), lambda b, t: (b * nt + t, 0)),
                  full(alog.shape), full(dtb.shape)],
        out_specs=out_specs,
        scratch_shapes=[pltpu.VMEM((3, rows + HALO, HEAD_W), F32), pltpu.VMEM((rows, LANES), F32),
                        pltpu.VMEM((DN_HEADS * (rows // sub), sub, sub), BF16),
                        pltpu.VMEM((DN_HEADS * (rows // sub), sub, 2 * HEAD_W), F32)],
        compiler_params=_params("parallel", "parallel"),
        name="gdn_prep",
    )(p, p, p, p_halo_src, p_halo_src, p_halo_src, meta_halo, conv_w, ba, alog, dtb)


def _gdn_scan_kernel(u_ref, wq_ref, kd_ref, a_ref, gl_ref, z_ref,
                     um_ref, wqm_ref, kdm_ref, am_ref, glm_ref, zm_ref, nw_ref,
                     o_ref, om_ref, s_sc, vbuf_sc, sw_sc, *, rows, sub, hpg):
    hg = pl.program_id(1)
    t = pl.program_id(2)
    lane = lax.broadcasted_iota(jnp.int32, (1, LANES), 1)
    nw = nw_ref[...]

    def decay_total(gl_row, h):
        sel = lane == DN_HEADS + hg * hpg + h
        return jnp.sum(jnp.where(sel, jnp.exp(gl_row), 0.0), axis=1, keepdims=True)

    def gate(o, z):
        ms = jnp.mean(o * o, axis=1, keepdims=True)
        return (o * lax.rsqrt(ms + RMS_EPS) * nw * _silu(z.astype(F32))).astype(BF16)

    def run_chunk(r0, refs, out_ref, out_r0):
        u_r, wq_r, kd_r, a_r, gl_r, z_r = refs
        slot = r0 % sub
        gl_row = gl_r[r0:r0 + 1, :]
        for h in range(hpg):
            sw_sc[h] = jnp.dot(wq_r[h, 2 * r0:2 * r0 + 2 * CHUNK, :], s_sc[h].astype(BF16),
                               preferred_element_type=F32)
        for h in range(hpg):
            v_new = u_r[h, r0:r0 + CHUNK, :].astype(F32) - sw_sc[h, 0:CHUNK, :]
            vbuf_sc[h, slot:slot + CHUNK, :] = v_new.astype(BF16)
        for h in range(hpg):
            o = sw_sc[h, CHUNK:, :] + jnp.dot(a_r[h, r0:r0 + CHUNK, :], vbuf_sc[h],
                                              preferred_element_type=F32)
            s_sc[h] = decay_total(gl_row, h) * s_sc[h] + lax.dot_general(
                kd_r[h, r0:r0 + CHUNK, :], vbuf_sc[h, slot:slot + CHUNK, :], _TN,
                preferred_element_type=F32)
            out_ref[out_r0:out_r0 + CHUNK, h * HEAD_W:(h + 1) * HEAD_W] = gate(
                o, z_r[h, r0:r0 + CHUNK, :])

    @pl.when(t == 0)
    def _meta_chunk():
        vbuf_sc[...] = jnp.zeros_like(vbuf_sc)
        s_sc[...] = jnp.zeros_like(s_sc)
        run_chunk(sub - CHUNK, (um_ref, wqm_ref, kdm_ref, am_ref, glm_ref, zm_ref), om_ref, 0)

    for c in range(rows // CHUNK):
        run_chunk(c * CHUNK, (u_ref, wq_ref, kd_ref, a_ref, gl_ref, z_ref), o_ref, c * CHUNK)


def _gdn_scan(real, meta, p, p_meta_pad, norm_w, *, batch, seq, rows, sub, hpg):
    u, wq, kd, a, gl = real
    um, wqm, kdm, am, glm = meta
    m = batch * seq
    nt = seq // rows
    ngrp = DN_HEADS // hpg
    z_blk = (6 * DN_HEADS) // hpg
    mrows = um.shape[1]
    assert mrows == sub

    def real_spec(r, w):
        return pl.BlockSpec((hpg, r, w), lambda b, g, t: (g, b * nt + t, 0))

    def meta_spec(r, w):
        return pl.BlockSpec((hpg, r, w), lambda b, g, t: (g, 0, 0))

    return pl.pallas_call(
        functools.partial(_gdn_scan_kernel, rows=rows, sub=sub, hpg=hpg),
        out_shape=(jax.ShapeDtypeStruct((m, GROUP_W), BF16),
                   jax.ShapeDtypeStruct((batch, CHUNK, GROUP_W), BF16)),
        grid=(batch, ngrp, nt),
        in_specs=[real_spec(rows, HEAD_W), real_spec(2 * rows, HEAD_W), real_spec(rows, HEAD_W),
                  real_spec(rows, sub),
                  pl.BlockSpec((rows, LANES), lambda b, g, t: (b * nt + t, 0)),
                  pl.BlockSpec((hpg, rows, HEAD_W), lambda b, g, t: (z_blk + g, b * nt + t, 0)),
                  meta_spec(mrows, HEAD_W), meta_spec(2 * mrows, HEAD_W), meta_spec(mrows, HEAD_W),
                  meta_spec(mrows, mrows),
                  pl.BlockSpec((mrows, LANES), lambda b, g, t: (0, 0)),
                  pl.BlockSpec((hpg, mrows, HEAD_W), lambda b, g, t: (z_blk + g, 0, 0)),
                  pl.BlockSpec((1, HEAD_W), lambda b, g, t: (0, 0))],
        out_specs=(pl.BlockSpec((rows, hpg * HEAD_W), lambda b, g, t: (b * nt + t, g)),
                   pl.BlockSpec((None, CHUNK, hpg * HEAD_W), lambda b, g, t: (b, 0, g))),
        scratch_shapes=[pltpu.VMEM((hpg, HEAD_W, HEAD_W), F32),
                        pltpu.VMEM((hpg, sub, HEAD_W), BF16),
                        pltpu.VMEM((hpg, 2 * CHUNK, HEAD_W), F32)],
        compiler_params=_params("parallel", "parallel", "arbitrary"),
        name="gdn_scan",
    )(u, wq, kd, a, gl, p, um, wqm, kdm, am, glm, p_meta_pad, norm_w)


def _wout_ln_kernel(oa_ref, od_ref, w_ref, x_ref, g_ref, b_ref, h_ref, *, alpha):
    half = oa_ref.shape[1]
    mix = (jnp.dot(oa_ref[...], w_ref[0:half, :], preferred_element_type=F32)
           + jnp.dot(od_ref[...], w_ref[half:, :], preferred_element_type=F32))
    h_ref[...] = _layer_norm(alpha * x_ref[...] + mix, g_ref[...], b_ref[...])


def _wout_ln(o_att, o_dn, w_out, x2d, g, b, *, bm, alpha):
    m, d = x2d.shape
    half = o_att.shape[1]
    return pl.pallas_call(
        functools.partial(_wout_ln_kernel, alpha=alpha),
        out_shape=jax.ShapeDtypeStruct((m, d), F32),
        grid=(m // bm,),
        in_specs=[pl.BlockSpec((bm, half), lambda i: (i, 0)),
                  pl.BlockSpec((bm, half), lambda i: (i, 0)),
                  pl.BlockSpec((2 * half, d), lambda i: (0, 0)),
                  pl.BlockSpec((bm, d), lambda i: (i, 0)),
                  pl.BlockSpec((1, d), lambda i: (0, 0)),
                  pl.BlockSpec((1, d), lambda i: (0, 0))],
        out_specs=pl.BlockSpec((bm, d), lambda i: (i, 0)),
        compiler_params=_params("parallel"),
        name="wout_ln",
    )(o_att, o_dn, w_out, x2d, g, b)


def _ffn_kernel(h_ref, hprev_ref, hmeta_ref, wg_ref, wu_ref, cw_ref, cb_ref, wd_ref, g_ref, b_ref,
                o_ref, xb_sc, gate_sc, *, bm, tiles_per_batch, alpha):
    i = pl.program_id(0)
    f = pl.program_id(1)

    @pl.when(f == 0)
    def _init():
        halo = jnp.where(i % tiles_per_batch == 0, hmeta_ref[...], hprev_ref[...])
        xb_sc[0:HALO, :] = halo.astype(BF16)
        xb_sc[HALO:, :] = h_ref[...].astype(BF16)
        o_ref[...] = jnp.zeros_like(o_ref)

    gate_sc[...] = jnp.dot(xb_sc[...], wg_ref[...], preferred_element_type=F32)
    up = jnp.dot(xb_sc[HALO:, :], wu_ref[...], preferred_element_type=F32)
    cw = cw_ref[...]
    hg = cb_ref[...] + gate_sc[pl.ds(HALO - FFN_CONV + 1, bm), :] * cw[0:1]
    for j in range(1, FFN_CONV):
        hg = hg + gate_sc[pl.ds(HALO - FFN_CONV + 1 + j, bm), :] * cw[j:j + 1]
    act = (_silu(hg) * up).astype(BF16)
    o_ref[...] += jnp.dot(act, wd_ref[...], preferred_element_type=F32)

    @pl.when(f == pl.num_programs(1) - 1)
    def _finish():
        o_ref[...] = _layer_norm(alpha * h_ref[...] + o_ref[...], g_ref[...], b_ref[...])


def _ffn(h1, h1_meta, w_gate, w_up, conv_w, conv_b, w_down, g, b, *, seq, bm, tf, alpha):
    m, d = h1.shape
    dff = w_gate.shape[1]
    tiles_per_batch = seq // bm
    hb = bm // HALO
    meta_blk = h1_meta.shape[0] // HALO - 1
    return pl.pallas_call(
        functools.partial(_ffn_kernel, bm=bm, tiles_per_batch=tiles_per_batch, alpha=alpha),
        out_shape=jax.ShapeDtypeStruct((m, d), F32),
        grid=(m // bm, dff // tf),
        in_specs=[pl.BlockSpec((bm, d), lambda i, f: (i, 0)),
                  pl.BlockSpec((HALO, d), lambda i, f: (jnp.maximum(i * hb - 1, 0), 0)),
                  pl.BlockSpec((HALO, d), lambda i, f: (meta_blk, 0)),
                  pl.BlockSpec((d, tf), lambda i, f: (0, f)),
                  pl.BlockSpec((d, tf), lambda i, f: (0, f)),
                  pl.BlockSpec((FFN_CONV, tf), lambda i, f: (0, f)),
                  pl.BlockSpec((1, tf), lambda i, f: (0, f)),
                  pl.BlockSpec((tf, d), lambda i, f: (f, 0)),
                  pl.BlockSpec((1, d), lambda i, f: (0, 0)),
                  pl.BlockSpec((1, d), lambda i, f: (0, 0))],
        out_specs=pl.BlockSpec((bm, d), lambda i, f: (i, 0)),
        scratch_shapes=[pltpu.VMEM((bm + HALO, d), BF16),
                        pltpu.VMEM((bm + HALO, tf), F32)],
        compiler_params=_params("parallel", "arbitrary"),
        name="ffn",
    )(h1, h1, h1_meta, w_gate, w_up, conv_w, conv_b, w_down, g, b)


def _largest_tile(n, cap, mult):
    best = None
    for c in range(mult, min(n, cap) + 1, mult):
        if n % c == 0:
            best = c
    assert best is not None, (n, cap, mult)
    return best


def _rope_tables(n_pos):
    half = ATT_QK_DIM // 2
    inv_freq = ROPE_THETA ** (-jnp.arange(0, ATT_QK_DIM, 2, dtype=F32) / ATT_QK_DIM)
    ang = jnp.arange(n_pos, dtype=F32)[:, None] * inv_freq[None, :]
    cos, sin = jnp.cos(ang), jnp.sin(ang)
    reps = HEAD_W // ATT_QK_DIM
    cos_t = jnp.tile(jnp.concatenate([cos, cos], axis=1), (1, reps))
    sin_t = jnp.tile(jnp.concatenate([-sin, sin], axis=1), (1, reps))
    assert cos_t.shape[1] == HEAD_W and half * 2 == ATT_QK_DIM
    return cos_t, sin_t


def kernel(x, meta_tokens, w_in, conv_qkv_w, a_log, dt_bias, lambda_q1, lambda_k1, lambda_q2, lambda_k2,
           diff_norm_w, delta_norm_w, w_out, ln1_g, ln1_b, ffn_w_gate, ffn_w_up, ffn_conv_w, ffn_conv_b,
           ffn_w_down, ln2_g, ln2_b):
    batch, seq, d = x.shape
    n_meta = meta_tokens.shape[0]
    depth = w_in.shape[0]
    n_main = N_GROUPS * GROUP_W
    assert w_in.shape[2] == n_main + TAIL_COLS and n_meta % HALO == 0 and n_meta <= CHUNK
    m = batch * seq
    alpha = (2.0 * depth) ** 0.25

    bm_proj = _largest_tile(m, 1024, HALO)
    bm_row = _largest_tile(seq, 512, HALO)
    tq = _largest_tile(seq, 256, CHUNK)
    rows = _largest_tile(seq, 256, LANES)
    sub = LANES
    hpg = DN_HEADS
    att_hps = 4

    cos_t, sin_t = _rope_tables(n_meta + seq)
    cos_meta, sin_meta = cos_t[:n_meta], sin_t[:n_meta]
    cos_real = jnp.tile(cos_t[n_meta:], (batch, 1))
    sin_real = jnp.tile(sin_t[n_meta:], (batch, 1))

    h = x.reshape(m, d)
    h_meta = meta_tokens.astype(x.dtype)
    pad_rows = sub - n_meta
    for l in range(depth):
        w_tail = jnp.pad(w_in[l, :, n_main:], ((0, 0), (0, LANES - TAIL_COLS)))
        lane_pad = (DN_HEADS, LANES - TAIL_COLS)
        alog = jnp.pad(a_log[l].astype(F32), lane_pad)[None]
        dtb = jnp.pad(dt_bias[l].astype(F32), lane_pad)[None]
        conv_w = conv_qkv_w[l].reshape(DN_CONV, 3 * DN_HEADS, HEAD_W).transpose(1, 0, 2)
        lams = tuple(v[l][None].astype(F32) for v in (lambda_q1, lambda_k1, lambda_q2, lambda_k2))
        lam_init = 0.8 - 0.6 * math.exp(-0.3 * l)

        p, p_meta, ba, ba_meta = _proj(h, h_meta, w_in[l].astype(BF16), w_tail, cos_real, sin_real,
                                       cos_meta, sin_meta, bm_proj)

        att_nw = diff_norm_w[l].astype(F32)[:, None]
        o_att = _attention(p, p, p_meta, lams, att_nw, batch=batch, seq=seq, tq=tq, hps=att_hps,
                           lam_init=lam_init, meta_only=False)
        q_meta = jnp.pad(p_meta[:ATT_HEADS], ((0, 0), (0, LANES - n_meta), (0, 0)))
        o_att_meta = _attention(q_meta, None, p_meta, lams, att_nw, batch=1, seq=LANES, tq=LANES,
                                hps=att_hps, lam_init=lam_init, meta_only=True)[:n_meta]

        p_meta_pad = jnp.pad(p_meta, ((0, 0), (pad_rows, 0), (0, 0)))
        ba_meta_pad = jnp.pad(ba_meta, ((pad_rows, 0), (0, 0)))
        meta_halo = p_meta[3 * DN_HEADS:6 * DN_HEADS, n_meta - HALO:, :]
        zero_halo = jnp.zeros_like(meta_halo)
        prep_real = _gdn_prep(p, p, meta_halo, conv_w, ba, alog, dtb, batch=batch, seq=seq, rows=rows,
                              sub=sub, n_pad=0)
        prep_meta = _gdn_prep(p_meta_pad, p_meta_pad, zero_halo, conv_w, ba_meta_pad, alog, dtb, batch=1,
                              seq=sub, rows=sub, sub=sub, n_pad=pad_rows)
        o_dn, o_dn_meta = _gdn_scan(prep_real, prep_meta, p, p_meta_pad, delta_norm_w[l][None],
                                    batch=batch, seq=seq, rows=rows, sub=sub, hpg=hpg)
        o_dn_meta = o_dn_meta[0, CHUNK - n_meta:]

        w_out_b = w_out[l].astype(BF16)
        g1, b1 = ln1_g[l][None].astype(F32), ln1_b[l][None].astype(F32)
        h1 = _wout_ln(o_att, o_dn, w_out_b, h, g1, b1, bm=bm_row, alpha=alpha)
        h1_meta = _wout_ln(o_att_meta, o_dn_meta, w_out_b, h_meta, g1, b1, bm=n_meta, alpha=alpha)

        dff = ffn_w_gate.shape[2]
        tf = _largest_tile(dff, 512, LANES)
        ffn_args = (ffn_w_gate[l].astype(BF16), ffn_w_up[l].astype(BF16), ffn_conv_w[l].astype(F32),
                    ffn_conv_b[l][None].astype(F32), ffn_w_down[l].astype(BF16),
                    ln2_g[l][None].astype(F32), ln2_b[l][None].astype(F32))
        h_next = _ffn(h1, h1_meta, *ffn_args, seq=seq, bm=bm_row, tf=tf, alpha=alpha)
        if l + 1 < depth:
            zero_meta = jnp.zeros_like(h1_meta)
            h_meta = _ffn(h1_meta, zero_meta, *ffn_args, seq=n_meta, bm=n_meta, tf=tf, alpha=alpha)
        h = h_next
    return h.reshape(batch, seq, d)
```

```python
import functools
import math

import jax
import jax.numpy as jnp
from jax import lax
from jax.experimental import pallas as pl
from jax.experimental.pallas import tpu as pltpu

F32 = jnp.float32
BF16 = jnp.bfloat16

CHUNK = 64
ATT_HEADS = 8
ATT_QK_DIM = 64
HEAD_W = 128
DN_HEADS = 8
DN_CONV = 4
FFN_CONV = 3
ROPE_THETA = 10000.0
LN_EPS = 1e-5
RMS_EPS = 1e-6
GROUP_W = ATT_HEADS * HEAD_W
N_GROUPS = 7
TAIL_COLS = 2 * DN_HEADS

LANES = 128
HALO = 8
VMEM_LIMIT = 56 * 1024 * 1024

_NT = (((1,), (1,)), ((), ()))
_TN = (((0,), (0,)), ((), ()))


def _params(*sem):
    return pltpu.CompilerParams(dimension_semantics=sem, vmem_limit_bytes=VMEM_LIMIT)


def _sigmoid(x):
    return 1.0 / (1.0 + jnp.exp(-x))


def _silu(x):
    return x * _sigmoid(x)


def _softplus(x):
    return jnp.maximum(x, 0.0) + jnp.log(1.0 + jnp.exp(-jnp.abs(x)))


def _layer_norm(y, g, b):
    mu = jnp.mean(y, axis=1, keepdims=True)
    d = y - mu
    var = jnp.mean(d * d, axis=1, keepdims=True)
    return d * lax.rsqrt(var + LN_EPS) * g + b


def _proj_kernel(x_ref, xm_ref, w_ref, wt_ref, cos_ref, sin_ref, cosm_ref, sinm_ref,
                 p_ref, pm_ref, ba_ref, bam_ref, *, heads):
    j = pl.program_id(0)
    i = pl.program_id(1)

    def tail(x, xh):
        w = wt_ref[...]
        xl = (x - xh.astype(F32)).astype(BF16)
        wh = w.astype(BF16)
        wl = (w - wh.astype(F32)).astype(BF16)
        return (jnp.dot(xh, wh, preferred_element_type=F32)
                + jnp.dot(xh, wl, preferred_element_type=F32)
                + jnp.dot(xl, wh, preferred_element_type=F32))

    def emit(acc, cos_r, sin_r, out_ref):
        is_rope = j < 2
        scale = jnp.where(j == 0, ATT_QK_DIM ** -0.5, 1.0).astype(F32)
        cos = jnp.where(is_rope, cos_r[...], 1.0) * scale
        sin = jnp.where(is_rope, sin_r[...], 0.0) * scale
        for h in range(heads):
            blk = acc[:, h * HEAD_W:(h + 1) * HEAD_W]
            out_ref[h] = (blk * cos + pltpu.roll(blk, HEAD_W // 2, 1) * sin).astype(BF16)

    @pl.when(i == 0)
    def _meta_rows():
        xm = xm_ref[...]
        xmh = xm.astype(BF16)
        emit(jnp.dot(xmh, w_ref[...], preferred_element_type=F32), cosm_ref, sinm_ref, pm_ref)

        @pl.when(j == 0)
        def _meta_tail():
            bam_ref[...] = tail(xm, xmh)

    x = x_ref[...]
    xh = x.astype(BF16)
    emit(jnp.dot(xh, w_ref[...], preferred_element_type=F32), cos_ref, sin_ref, p_ref)

    @pl.when(j == 0)
    def _tail():
        ba_ref[...] = tail(x, xh)


def _proj(x2d, x_meta, w_bf16, layer, w_tail, cos, sin, cos_meta, sin_meta, bm):
    m, d = x2d.shape
    n_meta = x_meta.shape[0]
    heads = GROUP_W // HEAD_W
    n_blocks = N_GROUPS * heads
    nm = m // bm
    const = lambda shape: pl.BlockSpec(shape, lambda j, i: (0,) * len(shape))
    row = lambda j, i: (i, 0)
    return pl.pallas_call(
        functools.partial(_proj_kernel, heads=heads),
        out_shape=(jax.ShapeDtypeStruct((n_blocks, m, HEAD_W), BF16),
                   jax.ShapeDtypeStruct((n_blocks, n_meta, HEAD_W), BF16),
                   jax.ShapeDtypeStruct((m, LANES), F32),
                   jax.ShapeDtypeStruct((n_meta, LANES), F32)),
        grid=(N_GROUPS, nm),
        in_specs=[
            pl.BlockSpec((bm, d), row),
            const((n_meta, d)),
            pl.BlockSpec((None, d, GROUP_W), lambda j, i: (layer, 0, j)),
            const((d, LANES)),
            pl.BlockSpec((bm, HEAD_W), row),
            pl.BlockSpec((bm, HEAD_W), row),
            const((n_meta, HEAD_W)),
            const((n_meta, HEAD_W)),
        ],
        out_specs=(
            pl.BlockSpec((heads, bm, HEAD_W), lambda j, i: (j, i, 0)),
            pl.BlockSpec((heads, n_meta, HEAD_W), lambda j, i: (j, 0, 0)),
            pl.BlockSpec((bm, LANES), lambda j, i: (jnp.where(j == 0, i, nm - 1), 0)),
            const((n_meta, LANES)),
        ),
        compiler_params=_params("arbitrary", "arbitrary"),
        name="proj",
    )(x2d, x_meta, w_bf16, w_tail, cos, sin, cos_meta, sin_meta)


def _attn_kernel(*refs, tq, hps, lam_init, meta_only):
    if meta_only:
        q_ref, km_ref, vm_ref, lq1, lk1, lq2, lk2, nw_ref, o_ref, m_sc, l_sc, acc_sc = refs
    else:
        (q_ref, k_ref, v_ref, km_ref, vm_ref, lq1, lk1, lq2, lk2, nw_ref, o_ref,
         m_sc, l_sc, acc_sc, s_sc, a_sc, p_sc) = refs
    lane = lax.broadcasted_iota(jnp.int32, (1, HEAD_W), 1)
    comp = (lane // (ATT_QK_DIM // 2)) % 2
    qc = []
    for h in range(hps):
        q = q_ref[h]
        zero = jnp.zeros_like(q)
        qc.append((jnp.where(comp == 0, q, zero), jnp.where(comp == 1, q, zero)))

    nc = 2 * hps

    def init_from_meta():
        s_meta = [lax.dot_general(km_ref[i // 2], qc[i // 2][i % 2], _NT, preferred_element_type=F32)
                  for i in range(nc)]
        p_meta = []
        for i in range(nc):
            m0 = jnp.max(s_meta[i], axis=0, keepdims=True)
            p = jnp.exp(s_meta[i] - m0)
            m_sc[i] = m0
            l_sc[i] = jnp.sum(p, axis=0, keepdims=True)
            p_meta.append(p.astype(BF16))
        for i in range(nc):
            acc_sc[i] = lax.dot_general(vm_ref[i // 2], p_meta[i], _TN, preferred_element_type=F32)

    def scores(t, slot):
        off = pl.multiple_of(t * tq, tq)
        for h in range(hps):
            kt = k_ref[h, pl.ds(off, tq), :]
            for c in range(2):
                s_sc[slot * nc + 2 * h + c] = lax.dot_general(kt, qc[h][c], _NT,
                                                              preferred_element_type=F32)

    def absorb(t, slot, mask):
        off = pl.multiple_of(t * tq, tq)
        for i in range(nc):
            s = s_sc[slot * nc + i]
            if mask is not None:
                s = jnp.where(mask, s, -jnp.inf)
            m_old = m_sc[i]
            m_new = jnp.maximum(m_old, jnp.max(s, axis=0, keepdims=True))
            a = jnp.exp(m_old - m_new)
            p = jnp.exp(s - m_new)
            l_sc[i] = a * l_sc[i] + jnp.sum(p, axis=0, keepdims=True)
            m_sc[i] = m_new
            a_sc[i] = a
            p_sc[i] = p.astype(BF16)
        for h in range(hps):
            vt = v_ref[h, pl.ds(off, tq), :]
            for c in range(2):
                i = 2 * h + c
                acc_sc[i] = a_sc[i] * acc_sc[i] + lax.dot_general(vt, p_sc[i], _TN,
                                                                  preferred_element_type=F32)

    if meta_only:
        init_from_meta()
    else:
        qi = pl.program_id(2)
        keys = lax.broadcasted_iota(jnp.int32, (tq, tq), 0)
        queries = lax.broadcasted_iota(jnp.int32, (tq, tq), 1)
        diag = (keys // CHUNK) <= (queries // CHUNK)
        scores(0, 0)
        init_from_meta()

        def body(k, carry):
            scores(2 * k + 1, 1)
            absorb(2 * k, 0, None)
            scores(2 * k + 2, 0)
            absorb(2 * k + 1, 1, None)
            return carry

        lax.fori_loop(0, qi // 2, body, 0)

        @pl.when(qi % 2 == 1)
        def _odd_tail():
            scores(qi, 1)
            absorb(qi - 1, 0, None)
            absorb(qi, 1, diag)

        @pl.when(qi % 2 == 0)
        def _even_tail():
            absorb(qi, 0, diag)

    lam = (jnp.exp(jnp.sum(lq1[...] * lk1[...], axis=1, keepdims=True))
           - jnp.exp(jnp.sum(lq2[...] * lk2[...], axis=1, keepdims=True)) + lam_init)
    for h in range(hps):
        o = acc_sc[2 * h] / l_sc[2 * h] - lam * (acc_sc[2 * h + 1] / l_sc[2 * h + 1])
        ms = jnp.mean(o * o, axis=0, keepdims=True)
        o = o * lax.rsqrt(ms + RMS_EPS) * nw_ref[...] * (1.0 - lam_init)
        o_ref[:, h * HEAD_W:(h + 1) * HEAD_W] = o.T.astype(BF16)


def _attention(p_q, p_kv, p_meta, lams, norm_w, *, batch, seq, tq, hps, lam_init, meta_only):
    n_meta = p_meta.shape[1]
    rows = p_q.shape[1]
    nq = 1 if meta_only else seq // tq
    k_blk, v_blk = ATT_HEADS // hps, 2 * ATT_HEADS // hps
    small = [pl.BlockSpec((1, ATT_QK_DIM), lambda b, g, i: (0, 0))] * 4 + [
        pl.BlockSpec((HEAD_W, 1), lambda b, g, i: (0, 0))]
    meta_specs = [pl.BlockSpec((hps, n_meta, HEAD_W), lambda b, g, i: (k_blk + g, 0, 0)),
                  pl.BlockSpec((hps, n_meta, HEAD_W), lambda b, g, i: (v_blk + g, 0, 0))]
    q_spec = pl.BlockSpec((hps, tq, HEAD_W), lambda b, g, i: (g, b * nq + i, 0))
    if meta_only:
        in_specs = [q_spec] + meta_specs + small
        args = (p_q, p_meta, p_meta) + lams + (norm_w,)
    else:
        kv_specs = [pl.BlockSpec((hps, seq, HEAD_W), lambda b, g, i: (k_blk + g, b, 0)),
                    pl.BlockSpec((hps, seq, HEAD_W), lambda b, g, i: (v_blk + g, b, 0))]
        in_specs = [q_spec] + kv_specs + meta_specs + small
        args = (p_q, p_kv, p_kv, p_meta, p_meta) + lams + (norm_w,)
    scratch = [pltpu.VMEM((2 * hps, 1, tq), F32), pltpu.VMEM((2 * hps, 1, tq), F32),
               pltpu.VMEM((2 * hps, HEAD_W, tq), F32)]
    if not meta_only:
        scratch += [pltpu.VMEM((2 * 2 * hps, tq, tq), F32),
                    pltpu.VMEM((2 * hps, 1, tq), F32),
                    pltpu.VMEM((2 * hps, tq, tq), BF16)]
    return pl.pallas_call(
        functools.partial(_attn_kernel, tq=tq, hps=hps, lam_init=lam_init, meta_only=meta_only),
        out_shape=jax.ShapeDtypeStruct((rows, GROUP_W), BF16),
        grid=(batch, ATT_HEADS // hps, nq),
        in_specs=in_specs,
        out_specs=pl.BlockSpec((tq, hps * HEAD_W), lambda b, g, i: (b * nq + i, g)),
        scratch_shapes=scratch,
        compiler_params=_params("parallel", "parallel", "arbitrary"),
        name="attn_meta" if meta_only else "attn",
    )(*args)


def _gdn_prep_kernel(q_ref, k_ref, v_ref, hq_ref, hk_ref, hv_ref, mh_ref, cw_ref, ba_ref, alog_ref,
                     dtb_ref, u_ref, wq_ref, kd_ref, a_ref, gl_ref, xs_sc, gc_sc, pw_sc, x_sc,
                     *, rows, sub, n_pad):
    first = pl.program_id(1) == 0
    n_sub = rows // sub
    ba = ba_ref[...]
    beta_all = _sigmoid(ba)
    g_all = -jnp.exp(alog_ref[...]) * _softplus(ba + dtb_ref[...])
    row = lax.broadcasted_iota(jnp.int32, (rows, 1), 0)
    if n_pad:
        beta_all = jnp.where(row >= n_pad, beta_all, 0.0)
        g_all = jnp.where(row >= n_pad, g_all, 0.0)

    pos = row % CHUNK
    gc_all = g_all
    shift = 1
    while shift < CHUNK:
        gc_all = gc_all + jnp.where(pos >= shift, pltpu.roll(gc_all, shift, 0), 0.0)
        shift *= 2
    gc_sc[...] = gc_all
    gl_all = jnp.concatenate(
        [jnp.broadcast_to(gc_sc[(c + 1) * CHUNK - 1:(c + 1) * CHUNK, :], (CHUNK, LANES))
         for c in range(rows // CHUNK)], axis=0)
    gl_ref[...] = gl_all
    gct_all = gc_all.T

    ri = lax.broadcasted_iota(jnp.int32, (sub, sub), 0)
    ci = lax.broadcasted_iota(jnp.int32, (sub, sub), 1)
    same = (ri // CHUNK) == (ci // CHUNK)
    tril = jnp.logical_and(same, ci <= ri)
    strict = jnp.logical_and(same, ci < ri)

    def conv_act(main_ref, halo_ref, stream, h):
        halo = jnp.where(first, mh_ref[stream * DN_HEADS + h], halo_ref[h])
        xs_sc[stream, 0:HALO, :] = halo.astype(F32)
        xs_sc[stream, HALO:, :] = main_ref[h].astype(F32)
        w = cw_ref[stream * DN_HEADS + h]
        y = xs_sc[stream, pl.ds(HALO - DN_CONV + 1, rows), :] * w[0:1]
        for j in range(1, DN_CONV):
            y = y + xs_sc[stream, pl.ds(HALO - DN_CONV + 1 + j, rows), :] * w[j:j + 1]
        return _silu(y)

    for h in range(DN_HEADS):
        q = conv_act(q_ref, hq_ref, 0, h)
        k = conv_act(k_ref, hk_ref, 1, h)
        v = conv_act(v_ref, hv_ref, 2, h)
        q = q * lax.rsqrt(jnp.sum(q * q, axis=1, keepdims=True) + RMS_EPS) * (HEAD_W ** -0.5)
        k = k * lax.rsqrt(jnp.sum(k * k, axis=1, keepdims=True) + RMS_EPS)
        beta = beta_all[:, h:h + 1]
        gc = gc_all[:, DN_HEADS + h:DN_HEADS + h + 1]
        gl = gl_all[:, DN_HEADS + h:DN_HEADS + h + 1]
        egc = jnp.exp(gc)
        kb = k * beta
        k16 = k.astype(BF16)
        kb16 = kb.astype(BF16)
        q16 = q.astype(BF16)
        x_all = jnp.concatenate([v * beta, kb * egc], axis=1)
        qd16 = (q * egc).astype(BF16)
        kd_ref[h] = (k * jnp.exp(gl - gc)).astype(BF16)
        for c in range(rows // CHUNK):
            wq_ref[h, (2 * c + 1) * CHUNK:(2 * c + 2) * CHUNK, :] = qd16[c * CHUNK:(c + 1) * CHUNK]
        for s in range(n_sub):
            lo, hi = s * sub, (s + 1) * sub
            gcr = gct_all[DN_HEADS + h:DN_HEADS + h + 1, lo:hi]
            kk = lax.dot_general(kb16[lo:hi], k16[lo:hi], _NT, preferred_element_type=F32)
            qk = lax.dot_general(q16[lo:hi], k16[lo:hi], _NT, preferred_element_type=F32)
            dec = jnp.exp(jnp.where(tril, gc[lo:hi] - gcr, -jnp.inf))
            a_ref[h, lo:hi, :] = (qk * dec).astype(BF16)
            pw_sc[h * n_sub + s] = jnp.where(strict, kk * dec, 0.0).astype(BF16)
            x_sc[h * n_sub + s] = x_all[lo:hi]

    items = range(DN_HEADS * n_sub)
    for it in items:
        x = x_sc[it]
        x_sc[it] = x - jnp.dot(pw_sc[it], x.astype(BF16), preferred_element_type=F32)
    for _ in range(int(math.log2(CHUNK)) - 1):
        for it in items:
            p16 = pw_sc[it]
            pw_sc[it] = jnp.dot(p16, p16, preferred_element_type=F32).astype(BF16)
        for it in items:
            x = x_sc[it]
            x_sc[it] = x + jnp.dot(pw_sc[it], x.astype(BF16), preferred_element_type=F32)
    for it in items:
        h, lo = it // n_sub, (it % n_sub) * sub
        x = x_sc[it]
        u_ref[h, lo:lo + sub, :] = x[:, :HEAD_W].astype(BF16)
        w16 = x[:, HEAD_W:].astype(BF16)
        for c in range(sub // CHUNK):
            r0 = lo + c * CHUNK
            wq_ref[h, 2 * r0:2 * r0 + CHUNK, :] = w16[c * CHUNK:(c + 1) * CHUNK]


def _gdn_prep(p, p_halo_src, meta_halo, conv_w, ba, alog, dtb, *, batch, seq, rows, sub, n_pad):
    m = batch * seq
    nt = seq // rows
    g_q, g_k, g_v = 3, 4, 5
    hb = rows // HALO

    def main_spec(g):
        return pl.BlockSpec((DN_HEADS, rows, HEAD_W), lambda b, t: (g, b * nt + t, 0))

    def halo_spec(g):
        return pl.BlockSpec((DN_HEADS, HALO, HEAD_W),
                            lambda b, t: (g, jnp.maximum((b * nt + t) * hb - 1, 0), 0))

    full = lambda shape: pl.BlockSpec(shape, lambda b, t: (0,) * len(shape))
    out_shape = (jax.ShapeDtypeStruct((DN_HEADS, m, HEAD_W), BF16),
                 jax.ShapeDtypeStruct((DN_HEADS, 2 * m, HEAD_W), BF16),
                 jax.ShapeDtypeStruct((DN_HEADS, m, HEAD_W), BF16),
                 jax.ShapeDtypeStruct((DN_HEADS, m, sub), BF16),
                 jax.ShapeDtypeStruct((m, LANES), F32))
    out_specs = (pl.BlockSpec((DN_HEADS, rows, HEAD_W), lambda b, t: (0, b * nt + t, 0)),
                 pl.BlockSpec((DN_HEADS, 2 * rows, HEAD_W), lambda b, t: (0, b * nt + t, 0)),
                 pl.BlockSpec((DN_HEADS, rows, HEAD_W), lambda b, t: (0, b * nt + t, 0)),
                 pl.BlockSpec((DN_HEADS, rows, sub), lambda b, t: (0, b * nt + t, 0)),
                 pl.BlockSpec((rows, LANES), lambda b, t: (b * nt + t, 0)))
    return pl.pallas_call(
        functools.partial(_gdn_prep_kernel, rows=rows, sub=sub, n_pad=n_pad),
        out_shape=out_shape,
        grid=(batch, nt),
        in_specs=[main_spec(g_q), main_spec(g_k), main_spec(g_v),
                  halo_spec(g_q), halo_spec(g_k), halo_spec(g_v),
                  full(meta_halo.shape), full(conv_w.shape),
                  pl.BlockSpec((rows, LANES), lambda b, t: (b * nt + t, 0)),
                  full(alog.shape), full(dtb.shape)],
        out_specs=out_specs,
        scratch_shapes=[pltpu.VMEM((3, rows + HALO, HEAD_W), F32), pltpu.VMEM((rows, LANES), F32),
                        pltpu.VMEM((DN_HEADS * (rows // sub), sub, sub), BF16),
                        pltpu.VMEM((DN_HEADS * (rows // sub), sub, 2 * HEAD_W), F32)],
        compiler_params=_params("parallel", "parallel"),
        name="gdn_prep",
    )(p, p, p, p_halo_src, p_halo_src, p_halo_src, meta_halo, conv_w, ba, alog, dtb)


def _gdn_scan_kernel(u_ref, wq_ref, kd_ref, a_ref, gl_ref, z_ref,
                     um_ref, wqm_ref, kdm_ref, am_ref, glm_ref, zm_ref, nw_ref,
                     o_ref, om_ref, s_sc, vbuf_sc, sw_sc, *, rows, sub, hpg):
    hg = pl.program_id(1)
    t = pl.program_id(2)
    lane = lax.broadcasted_iota(jnp.int32, (1, LANES), 1)
    nw = nw_ref[...]

    def decay_total(gl_row, h):
        sel = lane == DN_HEADS + hg * hpg + h
        return jnp.sum(jnp.where(sel, jnp.exp(gl_row), 0.0), axis=1, keepdims=True)

    def gate(o, z):
        ms = jnp.mean(o * o, axis=1, keepdims=True)
        return (o * lax.rsqrt(ms + RMS_EPS) * nw * _silu(z.astype(F32))).astype(BF16)

    def run_chunk(r0, refs, out_ref, out_r0):
        u_r, wq_r, kd_r, a_r, gl_r, z_r = refs
        slot = r0 % sub
        gl_row = gl_r[r0:r0 + 1, :]
        for h in range(hpg):
            sw_sc[h] = jnp.dot(wq_r[h, 2 * r0:2 * r0 + 2 * CHUNK, :], s_sc[h].astype(BF16),
                               preferred_element_type=F32)
        for h in range(hpg):
            v_new = u_r[h, r0:r0 + CHUNK, :].astype(F32) - sw_sc[h, 0:CHUNK, :]
            vbuf_sc[h, slot:slot + CHUNK, :] = v_new.astype(BF16)
        for h in range(hpg):
            o = sw_sc[h, CHUNK:, :] + jnp.dot(a_r[h, r0:r0 + CHUNK, :], vbuf_sc[h],
                                              preferred_element_type=F32)
            s_sc[h] = decay_total(gl_row, h) * s_sc[h] + lax.dot_general(
                kd_r[h, r0:r0 + CHUNK, :], vbuf_sc[h, slot:slot + CHUNK, :], _TN,
                preferred_element_type=F32)
            out_ref[out_r0:out_r0 + CHUNK, h * HEAD_W:(h + 1) * HEAD_W] = gate(
                o, z_r[h, r0:r0 + CHUNK, :])

    @pl.when(t == 0)
    def _meta_chunk():
        vbuf_sc[...] = jnp.zeros_like(vbuf_sc)
        s_sc[...] = jnp.zeros_like(s_sc)
        run_chunk(sub - CHUNK, (um_ref, wqm_ref, kdm_ref, am_ref, glm_ref, zm_ref), om_ref, 0)

    for c in range(rows // CHUNK):
        run_chunk(c * CHUNK, (u_ref, wq_ref, kd_ref, a_ref, gl_ref, z_ref), o_ref, c * CHUNK)


def _gdn_scan(real, meta, p, p_meta_pad, norm_w, *, batch, seq, rows, sub, hpg):
    u, wq, kd, a, gl = real
    um, wqm, kdm, am, glm = meta
    m = batch * seq
    nt = seq // rows
    ngrp = DN_HEADS // hpg
    z_blk = (6 * DN_HEADS) // hpg
    mrows = um.shape[1]
    assert mrows == sub

    def real_spec(r, w):
        return pl.BlockSpec((hpg, r, w), lambda b, g, t: (g, b * nt + t, 0))

    def meta_spec(r, w):
        return pl.BlockSpec((hpg, r, w), lambda b, g, t: (g, 0, 0))

    return pl.pallas_call(
        functools.partial(_gdn_scan_kernel, rows=rows, sub=sub, hpg=hpg),
        out_shape=(jax.ShapeDtypeStruct((m, GROUP_W), BF16),
                   jax.ShapeDtypeStruct((batch, CHUNK, GROUP_W), BF16)),
        grid=(batch, ngrp, nt),
        in_specs=[real_spec(rows, HEAD_W), real_spec(2 * rows, HEAD_W), real_spec(rows, HEAD_W),
                  real_spec(rows, sub),
                  pl.BlockSpec((rows, LANES), lambda b, g, t: (b * nt + t, 0)),
                  pl.BlockSpec((hpg, rows, HEAD_W), lambda b, g, t: (z_blk + g, b * nt + t, 0)),
                  meta_spec(mrows, HEAD_W), meta_spec(2 * mrows, HEAD_W), meta_spec(mrows, HEAD_W),
                  meta_spec(mrows, mrows),
                  pl.BlockSpec((mrows, LANES), lambda b, g, t: (0, 0)),
                  pl.BlockSpec((hpg, mrows, HEAD_W), lambda b, g, t: (z_blk + g, 0, 0)),
                  pl.BlockSpec((1, HEAD_W), lambda b, g, t: (0, 0))],
        out_specs=(pl.BlockSpec((rows, hpg * HEAD_W), lambda b, g, t: (b * nt + t, g)),
                   pl.BlockSpec((None, CHUNK, hpg * HEAD_W), lambda b, g, t: (b, 0, g))),
        scratch_shapes=[pltpu.VMEM((hpg, HEAD_W, HEAD_W), F32),
                        pltpu.VMEM((hpg, sub, HEAD_W), BF16),
                        pltpu.VMEM((hpg, 2 * CHUNK, HEAD_W), F32)],
        compiler_params=_params("parallel", "parallel", "arbitrary"),
        name="gdn_scan",
    )(u, wq, kd, a, gl, p, um, wqm, kdm, am, glm, p_meta_pad, norm_w)


def _wout_ln_kernel(oa_ref, od_ref, w_ref, x_ref, g_ref, b_ref, h_ref, *, alpha):
    half = oa_ref.shape[1]
    mix = (jnp.dot(oa_ref[...], w_ref[0:half, :], preferred_element_type=F32)
           + jnp.dot(od_ref[...], w_ref[half:, :], preferred_element_type=F32))
    h_ref[...] = _layer_norm(alpha * x_ref[...] + mix, g_ref[...], b_ref[...])


def _wout_ln(o_att, o_dn, w_out, x2d, g, b, *, bm, alpha):
    m, d = x2d.shape
    half = o_att.shape[1]
    return pl.pallas_call(
        functools.partial(_wout_ln_kernel, alpha=alpha),
        out_shape=jax.ShapeDtypeStruct((m, d), F32),
        grid=(m // bm,),
        in_specs=[pl.BlockSpec((bm, half), lambda i: (i, 0)),
                  pl.BlockSpec((bm, half), lambda i: (i, 0)),
                  pl.BlockSpec((2 * half, d), lambda i: (0, 0)),
                  pl.BlockSpec((bm, d), lambda i: (i, 0)),
                  pl.BlockSpec((1, d), lambda i: (0, 0)),
                  pl.BlockSpec((1, d), lambda i: (0, 0))],
        out_specs=pl.BlockSpec((bm, d), lambda i: (i, 0)),
        compiler_params=_params("parallel"),
        name="wout_ln",
    )(o_att, o_dn, w_out, x2d, g, b)


def _ffn_kernel(h_ref, hprev_ref, hmeta_ref, wg_ref, wu_ref, cw_ref, cb_ref, wd_ref, g_ref, b_ref,
                o_ref, xb_sc, gate_sc, *, bm, tiles_per_batch, alpha):
    i = pl.program_id(0)
    f = pl.program_id(1)

    @pl.when(f == 0)
    def _init():
        halo = jnp.where(i % tiles_per_batch == 0, hmeta_ref[...], hprev_ref[...])
        xb_sc[0:HALO, :] = halo.astype(BF16)
        xb_sc[HALO:, :] = h_ref[...].astype(BF16)
        o_ref[...] = jnp.zeros_like(o_ref)

    gate_sc[...] = jnp.dot(xb_sc[...], wg_ref[...], preferred_element_type=F32)
    up = jnp.dot(xb_sc[HALO:, :], wu_ref[...], preferred_element_type=F32)
    cw = cw_ref[...]
    hg = cb_ref[...] + gate_sc[pl.ds(HALO - FFN_CONV + 1, bm), :] * cw[0:1]
    for j in range(1, FFN_CONV):
        hg = hg + gate_sc[pl.ds(HALO - FFN_CONV + 1 + j, bm), :] * cw[j:j + 1]
    act = (_silu(hg) * up).astype(BF16)
    o_ref[...] += jnp.dot(act, wd_ref[...], preferred_element_type=F32)

    @pl.when(f == pl.num_programs(1) - 1)
    def _finish():
        o_ref[...] = _layer_norm(alpha * h_ref[...] + o_ref[...], g_ref[...], b_ref[...])


def _ffn(h1, h1_meta, w_gate, w_up, conv_w, conv_b, w_down, g, b, *, seq, bm, tf, alpha):
    m, d = h1.shape
    dff = w_gate.shape[1]
    tiles_per_batch = seq // bm
    hb = bm // HALO
    meta_blk = h1_meta.shape[0] // HALO - 1
    return pl.pallas_call(
        functools.partial(_ffn_kernel, bm=bm, tiles_per_batch=tiles_per_batch, alpha=alpha),
        out_shape=jax.ShapeDtypeStruct((m, d), F32),
        grid=(m // bm, dff // tf),
        in_specs=[pl.BlockSpec((bm, d), lambda i, f: (i, 0)),
                  pl.BlockSpec((HALO, d), lambda i, f: (jnp.maximum(i * hb - 1, 0), 0)),
                  pl.BlockSpec((HALO, d), lambda i, f: (meta_blk, 0)),
                  pl.BlockSpec((d, tf), lambda i, f: (0, f)),
                  pl.BlockSpec((d, tf), lambda i, f: (0, f)),
                  pl.BlockSpec((FFN_CONV, tf), lambda i, f: (0, f)),
                  pl.BlockSpec((1, tf), lambda i, f: (0, f)),
                  pl.BlockSpec((tf, d), lambda i, f: (f, 0)),
                  pl.BlockSpec((1, d), lambda i, f: (0, 0)),
                  pl.BlockSpec((1, d), lambda i, f: (0, 0))],
        out_specs=pl.BlockSpec((bm, d), lambda i, f: (i, 0)),
        scratch_shapes=[pltpu.VMEM((bm + HALO, d), BF16),
                        pltpu.VMEM((bm + HALO, tf), F32)],
        compiler_params=_params("parallel", "arbitrary"),
        name="ffn",
    )(h1, h1, h1_meta, w_gate, w_up, conv_w, conv_b, w_down, g, b)


def _largest_tile(n, cap, mult):
    best = None
    for c in range(mult, min(n, cap) + 1, mult):
        if n % c == 0:
            best = c
    assert best is not None, (n, cap, mult)
    return best


def _split_halves(w_qk):
    n_comp, quarter = HEAD_W // ATT_QK_DIM, ATT_QK_DIM // 2
    lead = w_qk.shape[:-1]
    w = w_qk.reshape(lead + (-1, n_comp, 2, quarter))
    return jnp.swapaxes(w, -3, -2).reshape(w_qk.shape)


def _rope_tables(n_pos):
    n_comp = HEAD_W // ATT_QK_DIM
    inv_freq = ROPE_THETA ** (-jnp.arange(0, ATT_QK_DIM, 2, dtype=F32) / ATT_QK_DIM)
    ang = jnp.arange(n_pos, dtype=F32)[:, None] * inv_freq[None, :]
    cos, sin = jnp.cos(ang), jnp.sin(ang)
    cos_t = jnp.tile(cos, (1, 2 * n_comp))
    sin_t = jnp.concatenate([jnp.tile(-sin, (1, n_comp)), jnp.tile(sin, (1, n_comp))], axis=1)
    assert cos_t.shape[1] == HEAD_W
    return cos_t, sin_t


def kernel(x, meta_tokens, w_in, conv_qkv_w, a_log, dt_bias, lambda_q1, lambda_k1, lambda_q2, lambda_k2,
           diff_norm_w, delta_norm_w, w_out, ln1_g, ln1_b, ffn_w_gate, ffn_w_up, ffn_conv_w, ffn_conv_b,
           ffn_w_down, ln2_g, ln2_b):
    batch, seq, d = x.shape
    n_meta = meta_tokens.shape[0]
    depth = w_in.shape[0]
    n_main = N_GROUPS * GROUP_W
    assert w_in.shape[2] == n_main + TAIL_COLS and n_meta % HALO == 0 and n_meta <= CHUNK
    m = batch * seq
    alpha = (2.0 * depth) ** 0.25

    bm_proj = _largest_tile(m, 1024, HALO)
    bm_row = _largest_tile(seq, 512, HALO)
    tq = _largest_tile(seq, 256, CHUNK)
    rows = _largest_tile(seq, 256, LANES)
    sub = LANES
    hpg = DN_HEADS
    att_hps = 4

    cos_t, sin_t = _rope_tables(n_meta + seq)
    cos_meta, sin_meta = cos_t[:n_meta], sin_t[:n_meta]
    cos_real = jnp.tile(cos_t[n_meta:], (batch, 1))
    sin_real = jnp.tile(sin_t[n_meta:], (batch, 1))

    h = x.reshape(m, d)
    h_meta = meta_tokens.astype(x.dtype)
    n_qk = 2 * GROUP_W
    w_in_b = jnp.concatenate([_split_halves(w_in[..., :n_qk]), w_in[..., n_qk:]], axis=-1).astype(BF16)
    pad_rows = sub - n_meta
    for l in range(depth):
        w_tail = jnp.pad(w_in[l, :, n_main:], ((0, 0), (0, LANES - TAIL_COLS)))
        lane_pad = (DN_HEADS, LANES - TAIL_COLS)
        alog = jnp.pad(a_log[l].astype(F32), lane_pad)[None]
        dtb = jnp.pad(dt_bias[l].astype(F32), lane_pad)[None]
        conv_w = conv_qkv_w[l].reshape(DN_CONV, 3 * DN_HEADS, HEAD_W).transpose(1, 0, 2)
        lams = tuple(v[l][None].astype(F32) for v in (lambda_q1, lambda_k1, lambda_q2, lambda_k2))
        lam_init = 0.8 - 0.6 * math.exp(-0.3 * l)

        p, p_meta, ba, ba_meta = _proj(h, h_meta, w_in_b, l, w_tail, cos_real, sin_real,
                                       cos_meta, sin_meta, bm_proj)

        att_nw = diff_norm_w[l].astype(F32)[:, None]
        o_att = _attention(p, p, p_meta, lams, att_nw, batch=batch, seq=seq, tq=tq, hps=att_hps,
                           lam_init=lam_init, meta_only=False)
        q_meta = jnp.pad(p_meta[:ATT_HEADS], ((0, 0), (0, LANES - n_meta), (0, 0)))
        o_att_meta = _attention(q_meta, None, p_meta, lams, att_nw, batch=1, seq=LANES, tq=LANES,
                                hps=att_hps, lam_init=lam_init, meta_only=True)[:n_meta]

        p_meta_pad = jnp.pad(p_meta, ((0, 0), (pad_rows, 0), (0, 0)))
        ba_meta_pad = jnp.pad(ba_meta, ((pad_rows, 0), (0, 0)))
        meta_halo = p_meta[3 * DN_HEADS:6 * DN_HEADS, n_meta - HALO:, :]
        zero_halo = jnp.zeros_like(meta_halo)
        prep_real = _gdn_prep(p, p, meta_halo, conv_w, ba, alog, dtb, batch=batch, seq=seq, rows=rows,
                              sub=sub, n_pad=0)
        prep_meta = _gdn_prep(p_meta_pad, p_meta_pad, zero_halo, conv_w, ba_meta_pad, alog, dtb, batch=1,
                              seq=sub, rows=sub, sub=sub, n_pad=pad_rows)
        o_dn, o_dn_meta = _gdn_scan(prep_real, prep_meta, p, p_meta_pad, delta_norm_w[l][None],
                                    batch=batch, seq=seq, rows=rows, sub=sub, hpg=hpg)
        o_dn_meta = o_dn_meta[0, CHUNK - n_meta:]

        w_out_b = w_out[l].astype(BF16)
        g1, b1 = ln1_g[l][None].astype(F32), ln1_b[l][None].astype(F32)
        h1 = _wout_ln(o_att, o_dn, w_out_b, h, g1, b1, bm=bm_row, alpha=alpha)
        h1_meta = _wout_ln(o_att_meta, o_dn_meta, w_out_b, h_meta, g1, b1, bm=n_meta, alpha=alpha)

        dff = ffn_w_gate.shape[2]
        tf = _largest_tile(dff, 512, LANES)
        ffn_args = (ffn_w_gate[l].astype(BF16), ffn_w_up[l].astype(BF16), ffn_conv_w[l].astype(F32),
                    ffn_conv_b[l][None].astype(F32), ffn_w_down[l].astype(BF16),
                    ln2_g[l][None].astype(F32), ln2_b[l][None].astype(F32))
        h_next = _ffn(h1, h1_meta, *ffn_args, seq=seq, bm=bm_row, tf=tf, alpha=alpha)
        if l + 1 < depth:
            zero_meta = jnp.zeros_like(h1_meta)
            h_meta = _ffn(h1_meta, zero_meta, *ffn_args, seq=n_meta, bm=n_meta, tf=tf, alpha=alpha)
        h = h_next
    return h.reshape(batch, seq, d)
```

```python
import functools
import math

import jax
import jax.numpy as jnp
from jax import lax
from jax.experimental import pallas as pl
from jax.experimental.pallas import tpu as pltpu

F32 = jnp.float32
BF16 = jnp.bfloat16

CHUNK = 64
ATT_HEADS = 8
ATT_QK_DIM = 64
HEAD_W = 128
DN_HEADS = 8
DN_CONV = 4
FFN_CONV = 3
ROPE_THETA = 10000.0
LN_EPS = 1e-5
RMS_EPS = 1e-6
GROUP_W = ATT_HEADS * HEAD_W
N_GROUPS = 7
TAIL_COLS = 2 * DN_HEADS

LANES = 128
HALO = 8
VMEM_LIMIT = 56 * 1024 * 1024

_NT = (((1,), (1,)), ((), ()))
_TN = (((0,), (0,)), ((), ()))


def _params(*sem):
    return pltpu.CompilerParams(dimension_semantics=sem, vmem_limit_bytes=VMEM_LIMIT)


def _sigmoid(x):
    return 1.0 / (1.0 + jnp.exp(-x))


def _silu(x):
    return x * _sigmoid(x)


def _softplus(x):
    return jnp.maximum(x, 0.0) + jnp.log(1.0 + jnp.exp(-jnp.abs(x)))


def _layer_norm(y, g, b):
    mu = jnp.mean(y, axis=1, keepdims=True)
    d = y - mu
    var = jnp.mean(d * d, axis=1, keepdims=True)
    return d * lax.rsqrt(var + LN_EPS) * g + b


def _proj_kernel(x_ref, xm_ref, wqk_ref, w_ref, wt_ref, cos_ref, sin_ref, cosm_ref, sinm_ref,
                 p_ref, pm_ref, ba_ref, bam_ref, *, heads, n_qk_tiles):
    j = pl.program_id(0)
    i = pl.program_id(1)

    def tail(x, xh):
        w = wt_ref[...]
        xl = (x - xh.astype(F32)).astype(BF16)
        wh = w.astype(BF16)
        wl = (w - wh.astype(F32)).astype(BF16)
        return (lax.dot_general(xh, wh, _NT, preferred_element_type=F32)
                + lax.dot_general(xh, wl, _NT, preferred_element_type=F32)
                + lax.dot_general(xl, wh, _NT, preferred_element_type=F32))

    def project(xh, wsel_ref, rope_tables, out_ref):
        acc = lax.dot_general(xh, wsel_ref[...], _NT, preferred_element_type=F32)
        if rope_tables is not None:
            scale = jnp.where(j == 0, ATT_QK_DIM ** -0.5, 1.0).astype(F32)
            cos = rope_tables[0][...] * scale
            sin = rope_tables[1][...] * scale
        for h in range(heads):
            blk = acc[:, h * HEAD_W:(h + 1) * HEAD_W]
            if rope_tables is not None:
                blk = blk * cos + pltpu.roll(blk, HEAD_W // 2, 1) * sin
            out_ref[h] = blk.astype(BF16)

    def tile(wsel_ref, rope):
        @pl.when(i == 0)
        def _meta_rows():
            project(xm_ref[...].astype(BF16), wsel_ref, (cosm_ref, sinm_ref) if rope else None, pm_ref)

        project(x_ref[...].astype(BF16), wsel_ref, (cos_ref, sin_ref) if rope else None, p_ref)

    @pl.when(j < n_qk_tiles)
    def _qk_tile():
        tile(wqk_ref, True)

    @pl.when(j >= n_qk_tiles)
    def _plain_tile():
        tile(w_ref, False)

    @pl.when(j == 0)
    def _tail():
        x = x_ref[...]
        ba_ref[...] = tail(x, x.astype(BF16))

        @pl.when(i == 0)
        def _meta_tail():
            xm = xm_ref[...]
            bam_ref[...] = tail(xm, xm.astype(BF16))


def _proj(x2d, x_meta, wqk_t, w_t, layer, w_tail_t, cos, sin, cos_meta, sin_meta, bm):
    m, d = x2d.shape
    n_meta = x_meta.shape[0]
    heads = GROUP_W // HEAD_W
    n_blocks = N_GROUPS * heads
    nm = m // bm
    n_qk_tiles = wqk_t.shape[1] // GROUP_W
    const = lambda shape: pl.BlockSpec(shape, lambda j, i: (0,) * len(shape))
    row = lambda j, i: (i, 0)
    return pl.pallas_call(
        functools.partial(_proj_kernel, heads=heads, n_qk_tiles=n_qk_tiles),
        out_shape=(jax.ShapeDtypeStruct((n_blocks, m, HEAD_W), BF16),
                   jax.ShapeDtypeStruct((n_blocks, n_meta, HEAD_W), BF16),
                   jax.ShapeDtypeStruct((m, LANES), F32),
                   jax.ShapeDtypeStruct((n_meta, LANES), F32)),
        grid=(N_GROUPS, nm),
        in_specs=[
            pl.BlockSpec((bm, d), row),
            const((n_meta, d)),
            pl.BlockSpec((None, GROUP_W, d), lambda j, i: (layer, jnp.minimum(j, n_qk_tiles - 1), 0)),
            pl.BlockSpec((None, GROUP_W, d), lambda j, i: (layer, jnp.maximum(j, n_qk_tiles), 0)),
            const((LANES, d)),
            pl.BlockSpec((bm, HEAD_W), row),
            pl.BlockSpec((bm, HEAD_W), row),
            const((n_meta, HEAD_W)),
            const((n_meta, HEAD_W)),
        ],
        out_specs=(
            pl.BlockSpec((heads, bm, HEAD_W), lambda j, i: (j, i, 0)),
            pl.BlockSpec((heads, n_meta, HEAD_W), lambda j, i: (j, 0, 0)),
            pl.BlockSpec((bm, LANES), lambda j, i: (jnp.where(j == 0, i, nm - 1), 0)),
            const((n_meta, LANES)),
        ),
        compiler_params=_params("arbitrary", "arbitrary"),
        name="proj",
    )(x2d, x_meta, wqk_t, w_t, w_tail_t, cos, sin, cos_meta, sin_meta)


def _attn_kernel(*refs, tq, hps, lam_init, meta_only):
    if meta_only:
        q_ref, km_ref, vm_ref, lq1, lk1, lq2, lk2, nw_ref, o_ref, m_sc, l_sc, acc_sc = refs
    else:
        (q_ref, k_ref, v_ref, km_ref, vm_ref, lq1, lk1, lq2, lk2, nw_ref, o_ref,
         m_sc, l_sc, acc_sc, s_sc, a_sc, p_sc) = refs
    lane = lax.broadcasted_iota(jnp.int32, (1, HEAD_W), 1)
    comp = (lane // (ATT_QK_DIM // 2)) % 2
    qc = []
    for h in range(hps):
        q = q_ref[h]
        zero = jnp.zeros_like(q)
        qc.append((jnp.where(comp == 0, q, zero), jnp.where(comp == 1, q, zero)))

    nc = 2 * hps

    def init_from_meta():
        s_meta = [lax.dot_general(km_ref[i // 2], qc[i // 2][i % 2], _NT, preferred_element_type=F32)
                  for i in range(nc)]
        p_meta = []
        for i in range(nc):
            m0 = jnp.max(s_meta[i], axis=0, keepdims=True)
            p = jnp.exp(s_meta[i] - m0)
            m_sc[i] = m0
            l_sc[i] = jnp.sum(p, axis=0, keepdims=True)
            p_meta.append(p.astype(BF16))
        for i in range(nc):
            acc_sc[i] = lax.dot_general(vm_ref[i // 2], p_meta[i], _TN, preferred_element_type=F32)

    def scores(t, slot):
        off = pl.multiple_of(t * tq, tq)
        for h in range(hps):
            kt = k_ref[h, pl.ds(off, tq), :]
            for c in range(2):
                s_sc[slot * nc + 2 * h + c] = lax.dot_general(kt, qc[h][c], _NT,
                                                              preferred_element_type=F32)

    def absorb(t, slot, mask):
        off = pl.multiple_of(t * tq, tq)
        for i in range(nc):
            s = s_sc[slot * nc + i]
            if mask is not None:
                s = jnp.where(mask, s, -jnp.inf)
            m_old = m_sc[i]
            m_new = jnp.maximum(m_old, jnp.max(s, axis=0, keepdims=True))
            a = jnp.exp(m_old - m_new)
            p = jnp.exp(s - m_new)
            l_sc[i] = a * l_sc[i] + jnp.sum(p, axis=0, keepdims=True)
            m_sc[i] = m_new
            a_sc[i] = a
            p_sc[i] = p.astype(BF16)
        for h in range(hps):
            vt = v_ref[h, pl.ds(off, tq), :]
            for c in range(2):
                i = 2 * h + c
                acc_sc[i] = a_sc[i] * acc_sc[i] + lax.dot_general(vt, p_sc[i], _TN,
                                                                  preferred_element_type=F32)

    if meta_only:
        init_from_meta()
    else:
        qi = pl.program_id(2)
        keys = lax.broadcasted_iota(jnp.int32, (tq, tq), 0)
        queries = lax.broadcasted_iota(jnp.int32, (tq, tq), 1)
        diag = (keys // CHUNK) <= (queries // CHUNK)
        scores(0, 0)
        init_from_meta()

        def body(k, carry):
            scores(2 * k + 1, 1)
            absorb(2 * k, 0, None)
            scores(2 * k + 2, 0)
            absorb(2 * k + 1, 1, None)
            return carry

        lax.fori_loop(0, qi // 2, body, 0)

        @pl.when(qi % 2 == 1)
        def _odd_tail():
            scores(qi, 1)
            absorb(qi - 1, 0, None)
            absorb(qi, 1, diag)

        @pl.when(qi % 2 == 0)
        def _even_tail():
            absorb(qi, 0, diag)

    lam = (jnp.exp(jnp.sum(lq1[...] * lk1[...], axis=1, keepdims=True))
           - jnp.exp(jnp.sum(lq2[...] * lk2[...], axis=1, keepdims=True)) + lam_init)
    for h in range(hps):
        o = acc_sc[2 * h] / l_sc[2 * h] - lam * (acc_sc[2 * h + 1] / l_sc[2 * h + 1])
        ms = jnp.mean(o * o, axis=0, keepdims=True)
        o = o * lax.rsqrt(ms + RMS_EPS) * nw_ref[...] * (1.0 - lam_init)
        o_ref[:, h * HEAD_W:(h + 1) * HEAD_W] = o.T.astype(BF16)


def _attention(p_q, p_kv, p_meta, lams, norm_w, *, batch, seq, tq, hps, lam_init, meta_only):
    n_meta = p_meta.shape[1]
    rows = p_q.shape[1]
    nq = 1 if meta_only else seq // tq
    k_blk, v_blk = ATT_HEADS // hps, 2 * ATT_HEADS // hps
    small = [pl.BlockSpec((1, ATT_QK_DIM), lambda b, g, i: (0, 0))] * 4 + [
        pl.BlockSpec((HEAD_W, 1), lambda b, g, i: (0, 0))]
    meta_specs = [pl.BlockSpec((hps, n_meta, HEAD_W), lambda b, g, i: (k_blk + g, 0, 0)),
                  pl.BlockSpec((hps, n_meta, HEAD_W), lambda b, g, i: (v_blk + g, 0, 0))]
    q_spec = pl.BlockSpec((hps, tq, HEAD_W), lambda b, g, i: (g, b * nq + i, 0))
    if meta_only:
        in_specs = [q_spec] + meta_specs + small
        args = (p_q, p_meta, p_meta) + lams + (norm_w,)
    else:
        kv_specs = [pl.BlockSpec((hps, seq, HEAD_W), lambda b, g, i: (k_blk + g, b, 0)),
                    pl.BlockSpec((hps, seq, HEAD_W), lambda b, g, i: (v_blk + g, b, 0))]
        in_specs = [q_spec] + kv_specs + meta_specs + small
        args = (p_q, p_kv, p_kv, p_meta, p_meta) + lams + (norm_w,)
    scratch = [pltpu.VMEM((2 * hps, 1, tq), F32), pltpu.VMEM((2 * hps, 1, tq), F32),
               pltpu.VMEM((2 * hps, HEAD_W, tq), F32)]
    if not meta_only:
        scratch += [pltpu.VMEM((2 * 2 * hps, tq, tq), F32),
                    pltpu.VMEM((2 * hps, 1, tq), F32),
                    pltpu.VMEM((2 * hps, tq, tq), BF16)]
    return pl.pallas_call(
        functools.partial(_attn_kernel, tq=tq, hps=hps, lam_init=lam_init, meta_only=meta_only),
        out_shape=jax.ShapeDtypeStruct((rows, GROUP_W), BF16),
        grid=(batch, ATT_HEADS // hps, nq),
        in_specs=in_specs,
        out_specs=pl.BlockSpec((tq, hps * HEAD_W), lambda b, g, i: (b * nq + i, g)),
        scratch_shapes=scratch,
        compiler_params=_params("parallel", "parallel", "arbitrary"),
        name="attn_meta" if meta_only else "attn",
    )(*args)


def _gdn_prep_kernel(q_ref, k_ref, v_ref, hq_ref, hk_ref, hv_ref, mh_ref, cw_ref, ba_ref, alog_ref,
                     dtb_ref, u_ref, wq_ref, kd_ref, a_ref, gl_ref, xs_sc, gc_sc, pw_sc, x_sc,
                     *, rows, sub, n_pad):
    first = pl.program_id(1) == 0
    n_sub = rows // sub
    ba = ba_ref[...]
    beta_all = _sigmoid(ba)
    g_all = -jnp.exp(alog_ref[...]) * _softplus(ba + dtb_ref[...])
    row = lax.broadcasted_iota(jnp.int32, (rows, 1), 0)
    if n_pad:
        beta_all = jnp.where(row >= n_pad, beta_all, 0.0)
        g_all = jnp.where(row >= n_pad, g_all, 0.0)

    pos = row % CHUNK
    gc_all = g_all
    shift = 1
    while shift < CHUNK:
        gc_all = gc_all + jnp.where(pos >= shift, pltpu.roll(gc_all, shift, 0), 0.0)
        shift *= 2
    gc_sc[...] = gc_all
    gl_all = jnp.concatenate(
        [jnp.broadcast_to(gc_sc[(c + 1) * CHUNK - 1:(c + 1) * CHUNK, :], (CHUNK, LANES))
         for c in range(rows // CHUNK)], axis=0)
    gl_ref[...] = gl_all
    gct_all = gc_all.T

    ri = lax.broadcasted_iota(jnp.int32, (sub, sub), 0)
    ci = lax.broadcasted_iota(jnp.int32, (sub, sub), 1)
    same = (ri // CHUNK) == (ci // CHUNK)
    tril = jnp.logical_and(same, ci <= ri)
    strict = jnp.logical_and(same, ci < ri)

    def conv_act(main_ref, halo_ref, stream, h):
        halo = jnp.where(first, mh_ref[stream * DN_HEADS + h], halo_ref[h])
        xs_sc[stream, 0:HALO, :] = halo.astype(F32)
        xs_sc[stream, HALO:, :] = main_ref[h].astype(F32)
        w = cw_ref[stream * DN_HEADS + h]
        y = xs_sc[stream, pl.ds(HALO - DN_CONV + 1, rows), :] * w[0:1]
        for j in range(1, DN_CONV):
            y = y + xs_sc[stream, pl.ds(HALO - DN_CONV + 1 + j, rows), :] * w[j:j + 1]
        return _silu(y)

    for h in range(DN_HEADS):
        q = conv_act(q_ref, hq_ref, 0, h)
        k = conv_act(k_ref, hk_ref, 1, h)
        v = conv_act(v_ref, hv_ref, 2, h)
        q = q * lax.rsqrt(jnp.sum(q * q, axis=1, keepdims=True) + RMS_EPS) * (HEAD_W ** -0.5)
        k = k * lax.rsqrt(jnp.sum(k * k, axis=1, keepdims=True) + RMS_EPS)
        beta = beta_all[:, h:h + 1]
        gc = gc_all[:, DN_HEADS + h:DN_HEADS + h + 1]
        gl = gl_all[:, DN_HEADS + h:DN_HEADS + h + 1]
        egc = jnp.exp(gc)
        kb = k * beta
        k16 = k.astype(BF16)
        kb16 = kb.astype(BF16)
        q16 = q.astype(BF16)
        x_all = jnp.concatenate([v * beta, kb * egc], axis=1)
        qd16 = (q * egc).astype(BF16)
        kd_ref[h] = (k * jnp.exp(gl - gc)).astype(BF16)
        for c in range(rows // CHUNK):
            wq_ref[h, (2 * c + 1) * CHUNK:(2 * c + 2) * CHUNK, :] = qd16[c * CHUNK:(c + 1) * CHUNK]
        for s in range(n_sub):
            lo, hi = s * sub, (s + 1) * sub
            gcr = gct_all[DN_HEADS + h:DN_HEADS + h + 1, lo:hi]
            kk = lax.dot_general(kb16[lo:hi], k16[lo:hi], _NT, preferred_element_type=F32)
            qk = lax.dot_general(q16[lo:hi], k16[lo:hi], _NT, preferred_element_type=F32)
            dec = jnp.exp(jnp.where(tril, gc[lo:hi] - gcr, -jnp.inf))
            a_ref[h, lo:hi, :] = (qk * dec).astype(BF16)
            pw_sc[h * n_sub + s] = jnp.where(strict, kk * dec, 0.0).astype(BF16)
            x_sc[h * n_sub + s] = x_all[lo:hi]

    items = range(DN_HEADS * n_sub)
    for it in items:
        x = x_sc[it]
        x_sc[it] = x - jnp.dot(pw_sc[it], x.astype(BF16), preferred_element_type=F32)
    for _ in range(int(math.log2(CHUNK)) - 1):
        for it in items:
            p16 = pw_sc[it]
            pw_sc[it] = jnp.dot(p16, p16, preferred_element_type=F32).astype(BF16)
        for it in items:
            x = x_sc[it]
            x_sc[it] = x + jnp.dot(pw_sc[it], x.astype(BF16), preferred_element_type=F32)
    for it in items:
        h, lo = it // n_sub, (it % n_sub) * sub
        x = x_sc[it]
        u_ref[h, lo:lo + sub, :] = x[:, :HEAD_W].astype(BF16)
        w16 = x[:, HEAD_W:].astype(BF16)
        for c in range(sub // CHUNK):
            r0 = lo + c * CHUNK
            wq_ref[h, 2 * r0:2 * r0 + CHUNK, :] = w16[c * CHUNK:(c + 1) * CHUNK]


def _gdn_prep(p, p_halo_src, meta_halo, conv_w, ba, alog, dtb, *, batch, seq, rows, sub, n_pad):
    m = batch * seq
    nt = seq // rows
    g_q, g_k, g_v = 3, 4, 5
    hb = rows // HALO

    def main_spec(g):
        return pl.BlockSpec((DN_HEADS, rows, HEAD_W), lambda b, t: (g, b * nt + t, 0))

    def halo_spec(g):
        return pl.BlockSpec((DN_HEADS, HALO, HEAD_W),
                            lambda b, t: (g, jnp.maximum((b * nt + t) * hb - 1, 0), 0))

    full = lambda shape: pl.BlockSpec(shape, lambda b, t: (0,) * len(shape))
    out_shape = (jax.ShapeDtypeStruct((DN_HEADS, m, HEAD_W), BF16),
                 jax.ShapeDtypeStruct((DN_HEADS, 2 * m, HEAD_W), BF16),
                 jax.ShapeDtypeStruct((DN_HEADS, m, HEAD_W), BF16),
                 jax.ShapeDtypeStruct((DN_HEADS, m, sub), BF16),
                 jax.ShapeDtypeStruct((m, LANES), F32))
    out_specs = (pl.BlockSpec((DN_HEADS, rows, HEAD_W), lambda b, t: (0, b * nt + t, 0)),
                 pl.BlockSpec((DN_HEADS, 2 * rows, HEAD_W), lambda b, t: (0, b * nt + t, 0)),
                 pl.BlockSpec((DN_HEADS, rows, HEAD_W), lambda b, t: (0, b * nt + t, 0)),
                 pl.BlockSpec((DN_HEADS, rows, sub), lambda b, t: (0, b * nt + t, 0)),
                 pl.BlockSpec((rows, LANES), lambda b, t: (b * nt + t, 0)))
    return pl.pallas_call(
        functools.partial(_gdn_prep_kernel, rows=rows, sub=sub, n_pad=n_pad),
        out_shape=out_shape,
        grid=(batch, nt),
        in_specs=[main_spec(g_q), main_spec(g_k), main_spec(g_v),
                  halo_spec(g_q), halo_spec(g_k), halo_spec(g_v),
                  full(meta_halo.shape), full(conv_w.shape),
                  pl.BlockSpec((rows, LANES), lambda b, t: (b * nt + t, 0)),
                  full(alog.shape), full(dtb.shape)],
        out_specs=out_specs,
        scratch_shapes=[pltpu.VMEM((3, rows + HALO, HEAD_W), F32), pltpu.VMEM((rows, LANES), F32),
                        pltpu.VMEM((DN_HEADS * (rows // sub), sub, sub), BF16),
                        pltpu.VMEM((DN_HEADS * (rows // sub), sub, 2 * HEAD_W), F32)],
        compiler_params=_params("parallel", "parallel"),
        name="gdn_prep",
    )(p, p, p, p_halo_src, p_halo_src, p_halo_src, meta_halo, conv_w, ba, alog, dtb)


def _gdn_scan_kernel(u_ref, wq_ref, kd_ref, a_ref, gl_ref, z_ref,
                     um_ref, wqm_ref, kdm_ref, am_ref, glm_ref, zm_ref, nw_ref,
                     o_ref, om_ref, s_sc, vbuf_sc, sw_sc, *, rows, sub, hpg):
    hg = pl.program_id(1)
    t = pl.program_id(2)
    lane = lax.broadcasted_iota(jnp.int32, (1, LANES), 1)
    nw = nw_ref[...]

    def decay_total(gl_row, h):
        sel = lane == DN_HEADS + hg * hpg + h
        return jnp.sum(jnp.where(sel, jnp.exp(gl_row), 0.0), axis=1, keepdims=True)

    def gate(o, z):
        ms = jnp.mean(o * o, axis=1, keepdims=True)
        return (o * lax.rsqrt(ms + RMS_EPS) * nw * _silu(z.astype(F32))).astype(BF16)

    def run_chunk(r0, refs, out_ref, out_r0):
        u_r, wq_r, kd_r, a_r, gl_r, z_r = refs
        slot = r0 % sub
        gl_row = gl_r[r0:r0 + 1, :]
        for h in range(hpg):
            sw_sc[h] = jnp.dot(wq_r[h, 2 * r0:2 * r0 + 2 * CHUNK, :], s_sc[h].astype(BF16),
                               preferred_element_type=F32)
        for h in range(hpg):
            v_new = u_r[h, r0:r0 + CHUNK, :].astype(F32) - sw_sc[h, 0:CHUNK, :]
            vbuf_sc[h, slot:slot + CHUNK, :] = v_new.astype(BF16)
        for h in range(hpg):
            o = sw_sc[h, CHUNK:, :] + jnp.dot(a_r[h, r0:r0 + CHUNK, :], vbuf_sc[h],
                                              preferred_element_type=F32)
            s_sc[h] = decay_total(gl_row, h) * s_sc[h] + lax.dot_general(
                kd_r[h, r0:r0 + CHUNK, :], vbuf_sc[h, slot:slot + CHUNK, :], _TN,
                preferred_element_type=F32)
            out_ref[out_r0:out_r0 + CHUNK, h * HEAD_W:(h + 1) * HEAD_W] = gate(
                o, z_r[h, r0:r0 + CHUNK, :])

    @pl.when(t == 0)
    def _meta_chunk():
        vbuf_sc[...] = jnp.zeros_like(vbuf_sc)
        s_sc[...] = jnp.zeros_like(s_sc)
        run_chunk(sub - CHUNK, (um_ref, wqm_ref, kdm_ref, am_ref, glm_ref, zm_ref), om_ref, 0)

    for c in range(rows // CHUNK):
        run_chunk(c * CHUNK, (u_ref, wq_ref, kd_ref, a_ref, gl_ref, z_ref), o_ref, c * CHUNK)


def _gdn_scan(real, meta, p, p_meta_pad, norm_w, *, batch, seq, rows, sub, hpg):
    u, wq, kd, a, gl = real
    um, wqm, kdm, am, glm = meta
    m = batch * seq
    nt = seq // rows
    ngrp = DN_HEADS // hpg
    z_blk = (6 * DN_HEADS) // hpg
    mrows = um.shape[1]
    assert mrows == sub

    def real_spec(r, w):
        return pl.BlockSpec((hpg, r, w), lambda b, g, t: (g, b * nt + t, 0))

    def meta_spec(r, w):
        return pl.BlockSpec((hpg, r, w), lambda b, g, t: (g, 0, 0))

    return pl.pallas_call(
        functools.partial(_gdn_scan_kernel, rows=rows, sub=sub, hpg=hpg),
        out_shape=(jax.ShapeDtypeStruct((m, GROUP_W), BF16),
                   jax.ShapeDtypeStruct((batch, CHUNK, GROUP_W), BF16)),
        grid=(batch, ngrp, nt),
        in_specs=[real_spec(rows, HEAD_W), real_spec(2 * rows, HEAD_W), real_spec(rows, HEAD_W),
                  real_spec(rows, sub),
                  pl.BlockSpec((rows, LANES), lambda b, g, t: (b * nt + t, 0)),
                  pl.BlockSpec((hpg, rows, HEAD_W), lambda b, g, t: (z_blk + g, b * nt + t, 0)),
                  meta_spec(mrows, HEAD_W), meta_spec(2 * mrows, HEAD_W), meta_spec(mrows, HEAD_W),
                  meta_spec(mrows, mrows),
                  pl.BlockSpec((mrows, LANES), lambda b, g, t: (0, 0)),
                  pl.BlockSpec((hpg, mrows, HEAD_W), lambda b, g, t: (z_blk + g, 0, 0)),
                  pl.BlockSpec((1, HEAD_W), lambda b, g, t: (0, 0))],
        out_specs=(pl.BlockSpec((rows, hpg * HEAD_W), lambda b, g, t: (b * nt + t, g)),
                   pl.BlockSpec((None, CHUNK, hpg * HEAD_W), lambda b, g, t: (b, 0, g))),
        scratch_shapes=[pltpu.VMEM((hpg, HEAD_W, HEAD_W), F32),
                        pltpu.VMEM((hpg, sub, HEAD_W), BF16),
                        pltpu.VMEM((hpg, 2 * CHUNK, HEAD_W), F32)],
        compiler_params=_params("parallel", "parallel", "arbitrary"),
        name="gdn_scan",
    )(u, wq, kd, a, gl, p, um, wqm, kdm, am, glm, p_meta_pad, norm_w)


def _wout_ln_kernel(oa_ref, od_ref, w_ref, x_ref, g_ref, b_ref, h_ref, *, alpha):
    half = oa_ref.shape[1]
    mix = (jnp.dot(oa_ref[...], w_ref[0:half, :], preferred_element_type=F32)
           + jnp.dot(od_ref[...], w_ref[half:, :], preferred_element_type=F32))
    h_ref[...] = _layer_norm(alpha * x_ref[...] + mix, g_ref[...], b_ref[...])


def _wout_ln(o_att, o_dn, w_out, x2d, g, b, *, bm, alpha):
    m, d = x2d.shape
    half = o_att.shape[1]
    return pl.pallas_call(
        functools.partial(_wout_ln_kernel, alpha=alpha),
        out_shape=jax.ShapeDtypeStruct((m, d), F32),
        grid=(m // bm,),
        in_specs=[pl.BlockSpec((bm, half), lambda i: (i, 0)),
                  pl.BlockSpec((bm, half), lambda i: (i, 0)),
                  pl.BlockSpec((2 * half, d), lambda i: (0, 0)),
                  pl.BlockSpec((bm, d), lambda i: (i, 0)),
                  pl.BlockSpec((1, d), lambda i: (0, 0)),
                  pl.BlockSpec((1, d), lambda i: (0, 0))],
        out_specs=pl.BlockSpec((bm, d), lambda i: (i, 0)),
        compiler_params=_params("parallel"),
        name="wout_ln",
    )(o_att, o_dn, w_out, x2d, g, b)


def _ffn_kernel(h_ref, hprev_ref, hmeta_ref, wg_ref, wu_ref, cw_ref, cb_ref, wd_ref, g_ref, b_ref,
                o_ref, xb_sc, gate_sc, *, bm, tiles_per_batch, alpha):
    i = pl.program_id(0)
    f = pl.program_id(1)

    @pl.when(f == 0)
    def _init():
        halo = jnp.where(i % tiles_per_batch == 0, hmeta_ref[...], hprev_ref[...])
        xb_sc[0:HALO, :] = halo.astype(BF16)
        xb_sc[HALO:, :] = h_ref[...].astype(BF16)
        o_ref[...] = jnp.zeros_like(o_ref)

    gate_sc[...] = jnp.dot(xb_sc[...], wg_ref[...], preferred_element_type=F32)
    up = jnp.dot(xb_sc[HALO:, :], wu_ref[...], preferred_element_type=F32)
    cw = cw_ref[...]
    hg = cb_ref[...] + gate_sc[pl.ds(HALO - FFN_CONV + 1, bm), :] * cw[0:1]
    for j in range(1, FFN_CONV):
        hg = hg + gate_sc[pl.ds(HALO - FFN_CONV + 1 + j, bm), :] * cw[j:j + 1]
    act = (_silu(hg) * up).astype(BF16)
    o_ref[...] += jnp.dot(act, wd_ref[...], preferred_element_type=F32)

    @pl.when(f == pl.num_programs(1) - 1)
    def _finish():
        o_ref[...] = _layer_norm(alpha * h_ref[...] + o_ref[...], g_ref[...], b_ref[...])


def _ffn(h1, h1_meta, w_gate, w_up, conv_w, conv_b, w_down, g, b, *, seq, bm, tf, alpha):
    m, d = h1.shape
    dff = w_gate.shape[1]
    tiles_per_batch = seq // bm
    hb = bm // HALO
    meta_blk = h1_meta.shape[0] // HALO - 1
    return pl.pallas_call(
        functools.partial(_ffn_kernel, bm=bm, tiles_per_batch=tiles_per_batch, alpha=alpha),
        out_shape=jax.ShapeDtypeStruct((m, d), F32),
        grid=(m // bm, dff // tf),
        in_specs=[pl.BlockSpec((bm, d), lambda i, f: (i, 0)),
                  pl.BlockSpec((HALO, d), lambda i, f: (jnp.maximum(i * hb - 1, 0), 0)),
                  pl.BlockSpec((HALO, d), lambda i, f: (meta_blk, 0)),
                  pl.BlockSpec((d, tf), lambda i, f: (0, f)),
                  pl.BlockSpec((d, tf), lambda i, f: (0, f)),
                  pl.BlockSpec((FFN_CONV, tf), lambda i, f: (0, f)),
                  pl.BlockSpec((1, tf), lambda i, f: (0, f)),
                  pl.BlockSpec((tf, d), lambda i, f: (f, 0)),
                  pl.BlockSpec((1, d), lambda i, f: (0, 0)),
                  pl.BlockSpec((1, d), lambda i, f: (0, 0))],
        out_specs=pl.BlockSpec((bm, d), lambda i, f: (i, 0)),
        scratch_shapes=[pltpu.VMEM((bm + HALO, d), BF16),
                        pltpu.VMEM((bm + HALO, tf), F32)],
        compiler_params=_params("parallel", "arbitrary"),
        name="ffn",
    )(h1, h1, h1_meta, w_gate, w_up, conv_w, conv_b, w_down, g, b)


def _largest_tile(n, cap, mult):
    best = None
    for c in range(mult, min(n, cap) + 1, mult):
        if n % c == 0:
            best = c
    assert best is not None, (n, cap, mult)
    return best


def _split_halves(w_qk_t):
    n_comp, quarter = HEAD_W // ATT_QK_DIM, ATT_QK_DIM // 2
    depth, n, d = w_qk_t.shape
    w = w_qk_t.reshape(depth, -1, n_comp, 2, quarter, d)
    return jnp.swapaxes(w, 2, 3).reshape(depth, n, d)


def _rope_tables(n_pos):
    n_comp = HEAD_W // ATT_QK_DIM
    inv_freq = ROPE_THETA ** (-jnp.arange(0, ATT_QK_DIM, 2, dtype=F32) / ATT_QK_DIM)
    ang = jnp.arange(n_pos, dtype=F32)[:, None] * inv_freq[None, :]
    cos, sin = jnp.cos(ang), jnp.sin(ang)
    cos_t = jnp.tile(cos, (1, 2 * n_comp))
    sin_t = jnp.concatenate([jnp.tile(-sin, (1, n_comp)), jnp.tile(sin, (1, n_comp))], axis=1)
    assert cos_t.shape[1] == HEAD_W
    return cos_t, sin_t


def kernel(x, meta_tokens, w_in, conv_qkv_w, a_log, dt_bias, lambda_q1, lambda_k1, lambda_q2, lambda_k2,
           diff_norm_w, delta_norm_w, w_out, ln1_g, ln1_b, ffn_w_gate, ffn_w_up, ffn_conv_w, ffn_conv_b,
           ffn_w_down, ln2_g, ln2_b):
    batch, seq, d = x.shape
    n_meta = meta_tokens.shape[0]
    depth = w_in.shape[0]
    n_main = N_GROUPS * GROUP_W
    assert w_in.shape[2] == n_main + TAIL_COLS and n_meta % HALO == 0 and n_meta <= CHUNK
    m = batch * seq
    alpha = (2.0 * depth) ** 0.25

    bm_proj = _largest_tile(m, 1024, HALO)
    bm_row = _largest_tile(seq, 512, HALO)
    tq = _largest_tile(seq, 256, CHUNK)
    rows = _largest_tile(seq, 256, LANES)
    sub = LANES
    hpg = DN_HEADS
    att_hps = 4

    cos_t, sin_t = _rope_tables(n_meta + seq)
    cos_meta, sin_meta = cos_t[:n_meta], sin_t[:n_meta]
    cos_real = jnp.tile(cos_t[n_meta:], (batch, 1))
    sin_real = jnp.tile(sin_t[n_meta:], (batch, 1))

    h = x.reshape(m, d)
    h_meta = meta_tokens.astype(x.dtype)
    n_qk = 2 * GROUP_W
    w_in_t = jnp.swapaxes(w_in, 1, 2)
    w_t = w_in_t.astype(BF16)
    wqk_t = _split_halves(w_t[:, :n_qk])
    pad_rows = sub - n_meta
    for l in range(depth):
        w_tail_t = jnp.pad(w_in_t[l, n_main:], ((0, LANES - TAIL_COLS), (0, 0)))
        lane_pad = (DN_HEADS, LANES - TAIL_COLS)
        alog = jnp.pad(a_log[l].astype(F32), lane_pad)[None]
        dtb = jnp.pad(dt_bias[l].astype(F32), lane_pad)[None]
        conv_w = conv_qkv_w[l].reshape(DN_CONV, 3 * DN_HEADS, HEAD_W).transpose(1, 0, 2)
        lams = tuple(v[l][None].astype(F32) for v in (lambda_q1, lambda_k1, lambda_q2, lambda_k2))
        lam_init = 0.8 - 0.6 * math.exp(-0.3 * l)

        p, p_meta, ba, ba_meta = _proj(h, h_meta, wqk_t, w_t, l, w_tail_t, cos_real, sin_real,
                                       cos_meta, sin_meta, bm_proj)

        att_nw = diff_norm_w[l].astype(F32)[:, None]
        o_att = _attention(p, p, p_meta, lams, att_nw, batch=batch, seq=seq, tq=tq, hps=att_hps,
                           lam_init=lam_init, meta_only=False)
        q_meta = jnp.pad(p_meta[:ATT_HEADS], ((0, 0), (0, LANES - n_meta), (0, 0)))
        o_att_meta = _attention(q_meta, None, p_meta, lams, att_nw, batch=1, seq=LANES, tq=LANES,
                                hps=att_hps, lam_init=lam_init, meta_only=True)[:n_meta]

        p_meta_pad = jnp.pad(p_meta, ((0, 0), (pad_rows, 0), (0, 0)))
        ba_meta_pad = jnp.pad(ba_meta, ((pad_rows, 0), (0, 0)))
        meta_halo = p_meta[3 * DN_HEADS:6 * DN_HEADS, n_meta - HALO:, :]
        zero_halo = jnp.zeros_like(meta_halo)
        prep_real = _gdn_prep(p, p, meta_halo, conv_w, ba, alog, dtb, batch=batch, seq=seq, rows=rows,
                              sub=sub, n_pad=0)
        prep_meta = _gdn_prep(p_meta_pad, p_meta_pad, zero_halo, conv_w, ba_meta_pad, alog, dtb, batch=1,
                              seq=sub, rows=sub, sub=sub, n_pad=pad_rows)
        o_dn, o_dn_meta = _gdn_scan(prep_real, prep_meta, p, p_meta_pad, delta_norm_w[l][None],
                                    batch=batch, seq=seq, rows=rows, sub=sub, hpg=hpg)
        o_dn_meta = o_dn_meta[0, CHUNK - n_meta:]

        w_out_b = w_out[l].astype(BF16)
        g1, b1 = ln1_g[l][None].astype(F32), ln1_b[l][None].astype(F32)
        h1 = _wout_ln(o_att, o_dn, w_out_b, h, g1, b1, bm=bm_row, alpha=alpha)
        h1_meta = _wout_ln(o_att_meta, o_dn_meta, w_out_b, h_meta, g1, b1, bm=n_meta, alpha=alpha)

        dff = ffn_w_gate.shape[2]
        tf = _largest_tile(dff, 512, LANES)
        ffn_args = (ffn_w_gate[l].astype(BF16), ffn_w_up[l].astype(BF16), ffn_conv_w[l].astype(F32),
                    ffn_conv_b[l][None].astype(F32), ffn_w_down[l].astype(BF16),
                    ln2_g[l][None].astype(F32), ln2_b[l][None].astype(F32))
        h_next = _ffn(h1, h1_meta, *ffn_args, seq=seq, bm=bm_row, tf=tf, alpha=alpha)
        if l + 1 < depth:
            zero_meta = jnp.zeros_like(h1_meta)
            h_meta = _ffn(h1_meta, zero_meta, *ffn_args, seq=n_meta, bm=n_meta, tf=tf, alpha=alpha)
        h = h_next
    return h.reshape(batch, seq, d)
```

```python
import functools
import math

import jax
import jax.numpy as jnp
from jax import lax
from jax.experimental import pallas as pl
from jax.experimental.pallas import tpu as pltpu

F32 = jnp.float32
BF16 = jnp.bfloat16

CHUNK = 64
ATT_HEADS = 8
ATT_QK_DIM = 64
HEAD_W = 128
DN_HEADS = 8
DN_CONV = 4
FFN_CONV = 3
ROPE_THETA = 10000.0
LN_EPS = 1e-5
RMS_EPS = 1e-6
GROUP_W = ATT_HEADS * HEAD_W
Q_SCALE = ATT_QK_DIM ** -0.5 * math.log2(math.e)
N_GROUPS = 7
TAIL_COLS = 2 * DN_HEADS

LANES = 128
HALO = 8
VMEM_LIMIT = 56 * 1024 * 1024

_NT = (((1,), (1,)), ((), ()))
_TN = (((0,), (0,)), ((), ()))


def _params(*sem):
    return pltpu.CompilerParams(dimension_semantics=sem, vmem_limit_bytes=VMEM_LIMIT)


def _sigmoid(x):
    return 1.0 / (1.0 + jnp.exp(-x))


def _silu(x):
    return x * _sigmoid(x)


def _softplus(x):
    return jnp.maximum(x, 0.0) + jnp.log(1.0 + jnp.exp(-jnp.abs(x)))


def _layer_norm(y, g, b):
    mu = jnp.mean(y, axis=1, keepdims=True)
    d = y - mu
    var = jnp.mean(d * d, axis=1, keepdims=True)
    return d * lax.rsqrt(var + LN_EPS) * g + b


def _proj_kernel(x_ref, xm_ref, wqk_ref, w_ref, wt_ref, cos_ref, sin_ref, cosm_ref, sinm_ref,
                 p_ref, pm_ref, ba_ref, bam_ref, *, heads, n_qk_tiles):
    j = pl.program_id(0)
    i = pl.program_id(1)

    def tail(x, xh):
        w = wt_ref[...]
        xl = (x - xh.astype(F32)).astype(BF16)
        wh = w.astype(BF16)
        wl = (w - wh.astype(F32)).astype(BF16)
        return (lax.dot_general(xh, wh, _NT, preferred_element_type=F32)
                + lax.dot_general(xh, wl, _NT, preferred_element_type=F32)
                + lax.dot_general(xl, wh, _NT, preferred_element_type=F32))

    def project(xh, wsel_ref, rope_tables, out_ref):
        acc = lax.dot_general(xh, wsel_ref[...], _NT, preferred_element_type=F32)
        if rope_tables is not None:
            scale = jnp.where(j == 0, Q_SCALE, 1.0).astype(F32)
            cos = rope_tables[0][...] * scale
            sin = rope_tables[1][...] * scale
        for h in range(heads):
            blk = acc[:, h * HEAD_W:(h + 1) * HEAD_W]
            if rope_tables is not None:
                blk = blk * cos + pltpu.roll(blk, HEAD_W // 2, 1) * sin
            out_ref[h] = blk.astype(BF16)

    def tile(wsel_ref, rope):
        @pl.when(i == 0)
        def _meta_rows():
            project(xm_ref[...].astype(BF16), wsel_ref, (cosm_ref, sinm_ref) if rope else None, pm_ref)

        project(x_ref[...].astype(BF16), wsel_ref, (cos_ref, sin_ref) if rope else None, p_ref)

    @pl.when(j < n_qk_tiles)
    def _qk_tile():
        tile(wqk_ref, True)

    @pl.when(j >= n_qk_tiles)
    def _plain_tile():
        tile(w_ref, False)

    @pl.when(j == 0)
    def _tail():
        x = x_ref[...]
        ba_ref[...] = tail(x, x.astype(BF16))

        @pl.when(i == 0)
        def _meta_tail():
            xm = xm_ref[...]
            bam_ref[...] = tail(xm, xm.astype(BF16))


def _proj(x2d, x_meta, wqk_t, w_t, layer, w_tail_t, cos, sin, cos_meta, sin_meta, bm):
    m, d = x2d.shape
    n_meta = x_meta.shape[0]
    heads = GROUP_W // HEAD_W
    n_blocks = N_GROUPS * heads
    nm = m // bm
    n_qk_tiles = wqk_t.shape[1] // GROUP_W
    const = lambda shape: pl.BlockSpec(shape, lambda j, i: (0,) * len(shape))
    row = lambda j, i: (i, 0)
    tiles_per_batch = cos.shape[0] // bm
    pos = lambda j, i: (i % tiles_per_batch, 0)
    return pl.pallas_call(
        functools.partial(_proj_kernel, heads=heads, n_qk_tiles=n_qk_tiles),
        out_shape=(jax.ShapeDtypeStruct((n_blocks, m, HEAD_W), BF16),
                   jax.ShapeDtypeStruct((n_blocks, n_meta, HEAD_W), BF16),
                   jax.ShapeDtypeStruct((m, LANES), F32),
                   jax.ShapeDtypeStruct((n_meta, LANES), F32)),
        grid=(N_GROUPS, nm),
        in_specs=[
            pl.BlockSpec((bm, d), row),
            const((n_meta, d)),
            pl.BlockSpec((None, GROUP_W, d), lambda j, i: (layer, jnp.minimum(j, n_qk_tiles - 1), 0)),
            pl.BlockSpec((None, GROUP_W, d), lambda j, i: (layer, jnp.maximum(j, n_qk_tiles), 0)),
            const((LANES, d)),
            pl.BlockSpec((bm, HEAD_W), pos),
            pl.BlockSpec((bm, HEAD_W), pos),
            const((n_meta, HEAD_W)),
            const((n_meta, HEAD_W)),
        ],
        out_specs=(
            pl.BlockSpec((heads, bm, HEAD_W), lambda j, i: (j, i, 0)),
            pl.BlockSpec((heads, n_meta, HEAD_W), lambda j, i: (j, 0, 0)),
            pl.BlockSpec((bm, LANES), lambda j, i: (jnp.where(j == 0, i, nm - 1), 0)),
            const((n_meta, LANES)),
        ),
        compiler_params=_params("arbitrary", "arbitrary"),
        name="proj",
    )(x2d, x_meta, wqk_t, w_t, w_tail_t, cos, sin, cos_meta, sin_meta)


def _attn_kernel(*refs, tq, hps, lam_init, meta_only):
    if meta_only:
        q_ref, km_ref, vm_ref, lq1, lk1, lq2, lk2, nw_ref, o_ref, m_sc, l_sc, acc_sc = refs
    else:
        (q_ref, k_ref, v_ref, km_ref, vm_ref, lq1, lk1, lq2, lk2, nw_ref, o_ref,
         m_sc, l_sc, acc_sc, s_sc, a_sc, p_sc) = refs
    lane = lax.broadcasted_iota(jnp.int32, (1, HEAD_W), 1)
    comp = (lane // (ATT_QK_DIM // 2)) % 2
    qc = []
    for h in range(hps):
        q = q_ref[h]
        zero = jnp.zeros_like(q)
        qc.append((jnp.where(comp == 0, q, zero), jnp.where(comp == 1, q, zero)))

    nc = 2 * hps

    def init_from_meta():
        s_meta = [lax.dot_general(km_ref[i // 2], qc[i // 2][i % 2], _NT, preferred_element_type=F32)
                  for i in range(nc)]
        p_meta = []
        for i in range(nc):
            m0 = jnp.max(s_meta[i], axis=0, keepdims=True)
            p = jnp.exp2(s_meta[i] - m0)
            m_sc[i] = m0
            l_sc[i] = jnp.sum(p, axis=0, keepdims=True)
            p_meta.append(p.astype(BF16))
        for i in range(nc):
            acc_sc[i] = lax.dot_general(vm_ref[i // 2], p_meta[i], _TN, preferred_element_type=F32)

    def scores(t, slot):
        off = pl.multiple_of(t * tq, tq)
        for h in range(hps):
            kt = k_ref[h, pl.ds(off, tq), :]
            for c in range(2):
                s_sc[slot * nc + 2 * h + c] = lax.dot_general(kt, qc[h][c], _NT,
                                                              preferred_element_type=F32)

    def absorb(t, slot, mask):
        off = pl.multiple_of(t * tq, tq)
        for i in range(nc):
            s = s_sc[slot * nc + i]
            if mask is not None:
                s = jnp.where(mask, s, -jnp.inf)
            m_old = m_sc[i]
            m_new = jnp.maximum(m_old, jnp.max(s, axis=0, keepdims=True))
            a = jnp.exp2(m_old - m_new)
            p = jnp.exp2(s - m_new)
            l_sc[i] = a * l_sc[i] + jnp.sum(p, axis=0, keepdims=True)
            m_sc[i] = m_new
            a_sc[i] = a
            p_sc[i] = p.astype(BF16)
        for h in range(hps):
            vt = v_ref[h, pl.ds(off, tq), :]
            for c in range(2):
                i = 2 * h + c
                acc_sc[i] = a_sc[i] * acc_sc[i] + lax.dot_general(vt, p_sc[i], _TN,
                                                                  preferred_element_type=F32)

    if meta_only:
        init_from_meta()
    else:
        qi = pl.program_id(2)
        keys = lax.broadcasted_iota(jnp.int32, (tq, tq), 0)
        queries = lax.broadcasted_iota(jnp.int32, (tq, tq), 1)
        diag = (keys // CHUNK) <= (queries // CHUNK)
        scores(0, 0)
        init_from_meta()

        def body(k, carry):
            scores(2 * k + 1, 1)
            absorb(2 * k, 0, None)
            scores(2 * k + 2, 0)
            absorb(2 * k + 1, 1, None)
            return carry

        lax.fori_loop(0, qi // 2, body, 0)

        @pl.when(qi % 2 == 1)
        def _odd_tail():
            scores(qi, 1)
            absorb(qi - 1, 0, None)
            absorb(qi, 1, diag)

        @pl.when(qi % 2 == 0)
        def _even_tail():
            absorb(qi, 0, diag)

    lam = (jnp.exp(jnp.sum(lq1[...] * lk1[...], axis=1, keepdims=True))
           - jnp.exp(jnp.sum(lq2[...] * lk2[...], axis=1, keepdims=True)) + lam_init)
    for h in range(hps):
        o = acc_sc[2 * h] / l_sc[2 * h] - lam * (acc_sc[2 * h + 1] / l_sc[2 * h + 1])
        ms = jnp.mean(o * o, axis=0, keepdims=True)
        o = o * lax.rsqrt(ms + RMS_EPS) * nw_ref[...] * (1.0 - lam_init)
        o_ref[:, h * HEAD_W:(h + 1) * HEAD_W] = o.T.astype(BF16)


def _attention(p_q, p_kv, p_meta, lams, norm_w, *, batch, seq, tq, hps, lam_init, meta_only):
    n_meta = p_meta.shape[1]
    rows = p_q.shape[1]
    nq = 1 if meta_only else seq // tq
    k_blk, v_blk = ATT_HEADS // hps, 2 * ATT_HEADS // hps
    small = [pl.BlockSpec((1, ATT_QK_DIM), lambda b, g, i: (0, 0))] * 4 + [
        pl.BlockSpec((HEAD_W, 1), lambda b, g, i: (0, 0))]
    meta_specs = [pl.BlockSpec((hps, n_meta, HEAD_W), lambda b, g, i: (k_blk + g, 0, 0)),
                  pl.BlockSpec((hps, n_meta, HEAD_W), lambda b, g, i: (v_blk + g, 0, 0))]
    q_spec = pl.BlockSpec((hps, tq, HEAD_W), lambda b, g, i: (g, b * nq + i, 0))
    if meta_only:
        in_specs = [q_spec] + meta_specs + small
        args = (p_q, p_meta, p_meta) + lams + (norm_w,)
    else:
        kv_specs = [pl.BlockSpec((hps, seq, HEAD_W), lambda b, g, i: (k_blk + g, b, 0)),
                    pl.BlockSpec((hps, seq, HEAD_W), lambda b, g, i: (v_blk + g, b, 0))]
        in_specs = [q_spec] + kv_specs + meta_specs + small
        args = (p_q, p_kv, p_kv, p_meta, p_meta) + lams + (norm_w,)
    scratch = [pltpu.VMEM((2 * hps, 1, tq), F32), pltpu.VMEM((2 * hps, 1, tq), F32),
               pltpu.VMEM((2 * hps, HEAD_W, tq), F32)]
    if not meta_only:
        scratch += [pltpu.VMEM((2 * 2 * hps, tq, tq), F32),
                    pltpu.VMEM((2 * hps, 1, tq), F32),
                    pltpu.VMEM((2 * hps, tq, tq), BF16)]
    return pl.pallas_call(
        functools.partial(_attn_kernel, tq=tq, hps=hps, lam_init=lam_init, meta_only=meta_only),
        out_shape=jax.ShapeDtypeStruct((rows, GROUP_W), BF16),
        grid=(batch, ATT_HEADS // hps, nq),
        in_specs=in_specs,
        out_specs=pl.BlockSpec((tq, hps * HEAD_W), lambda b, g, i: (b * nq + i, g)),
        scratch_shapes=scratch,
        compiler_params=_params("parallel", "parallel", "arbitrary"),
        name="attn_meta" if meta_only else "attn",
    )(*args)


def _gdn_prep_kernel(q_ref, k_ref, v_ref, hq_ref, hk_ref, hv_ref, mh_ref, cw_ref, ba_ref, alog_ref,
                     dtb_ref, u_ref, wq_ref, kd_ref, a_ref, gl_ref, xs_sc, gc_sc, pw_sc, x_sc,
                     *, rows, sub, n_pad):
    first = pl.program_id(1) == 0
    n_sub = rows // sub
    ba = ba_ref[...]
    beta_all = _sigmoid(ba)
    g_all = -jnp.exp(alog_ref[...]) * _softplus(ba + dtb_ref[...])
    row = lax.broadcasted_iota(jnp.int32, (rows, 1), 0)
    if n_pad:
        beta_all = jnp.where(row >= n_pad, beta_all, 0.0)
        g_all = jnp.where(row >= n_pad, g_all, 0.0)

    pos = row % CHUNK
    gc_all = g_all
    shift = 1
    while shift < CHUNK:
        gc_all = gc_all + jnp.where(pos >= shift, pltpu.roll(gc_all, shift, 0), 0.0)
        shift *= 2
    gc_sc[...] = gc_all
    gl_all = jnp.concatenate(
        [jnp.broadcast_to(gc_sc[(c + 1) * CHUNK - 1:(c + 1) * CHUNK, :], (CHUNK, LANES))
         for c in range(rows // CHUNK)], axis=0)
    gl_ref[...] = gl_all
    gct_all = gc_all.T

    ri = lax.broadcasted_iota(jnp.int32, (sub, sub), 0)
    ci = lax.broadcasted_iota(jnp.int32, (sub, sub), 1)
    same = (ri // CHUNK) == (ci // CHUNK)
    tril = jnp.logical_and(same, ci <= ri)
    strict = jnp.logical_and(same, ci < ri)

    def conv_act(main_ref, halo_ref, stream, h):
        halo = jnp.where(first, mh_ref[stream * DN_HEADS + h], halo_ref[h])
        xs_sc[stream, 0:HALO, :] = halo.astype(F32)
        xs_sc[stream, HALO:, :] = main_ref[h].astype(F32)
        w = cw_ref[stream * DN_HEADS + h]
        y = xs_sc[stream, pl.ds(HALO - DN_CONV + 1, rows), :] * w[0:1]
        for j in range(1, DN_CONV):
            y = y + xs_sc[stream, pl.ds(HALO - DN_CONV + 1 + j, rows), :] * w[j:j + 1]
        return _silu(y)

    for h in range(DN_HEADS):
        q = conv_act(q_ref, hq_ref, 0, h)
        k = conv_act(k_ref, hk_ref, 1, h)
        v = conv_act(v_ref, hv_ref, 2, h)
        q = q * lax.rsqrt(jnp.sum(q * q, axis=1, keepdims=True) + RMS_EPS) * (HEAD_W ** -0.5)
        k = k * lax.rsqrt(jnp.sum(k * k, axis=1, keepdims=True) + RMS_EPS)
        beta = beta_all[:, h:h + 1]
        gc = gc_all[:, DN_HEADS + h:DN_HEADS + h + 1]
        gl = gl_all[:, DN_HEADS + h:DN_HEADS + h + 1]
        egc = jnp.exp(gc)
        kb = k * beta
        k16 = k.astype(BF16)
        kb16 = kb.astype(BF16)
        q16 = q.astype(BF16)
        x_all = jnp.concatenate([v * beta, kb * egc], axis=1)
        qd16 = (q * egc).astype(BF16)
        kd_ref[h] = (k * jnp.exp(gl - gc)).astype(BF16)
        for c in range(rows // CHUNK):
            wq_ref[h, (2 * c + 1) * CHUNK:(2 * c + 2) * CHUNK, :] = qd16[c * CHUNK:(c + 1) * CHUNK]
        for s in range(n_sub):
            lo, hi = s * sub, (s + 1) * sub
            gcr = gct_all[DN_HEADS + h:DN_HEADS + h + 1, lo:hi]
            kk = lax.dot_general(kb16[lo:hi], k16[lo:hi], _NT, preferred_element_type=F32)
            qk = lax.dot_general(q16[lo:hi], k16[lo:hi], _NT, preferred_element_type=F32)
            dec = jnp.exp(jnp.where(tril, gc[lo:hi] - gcr, -jnp.inf))
            a_ref[h, lo:hi, :] = (qk * dec).astype(BF16)
            pw_sc[h * n_sub + s] = jnp.where(strict, kk * dec, 0.0).astype(BF16)
            x_sc[h * n_sub + s] = x_all[lo:hi]

    items = range(DN_HEADS * n_sub)
    for it in items:
        x = x_sc[it]
        x_sc[it] = x - jnp.dot(pw_sc[it], x.astype(BF16), preferred_element_type=F32)
    for _ in range(int(math.log2(CHUNK)) - 1):
        for it in items:
            p16 = pw_sc[it]
            pw_sc[it] = jnp.dot(p16, p16, preferred_element_type=F32).astype(BF16)
        for it in items:
            x = x_sc[it]
            x_sc[it] = x + jnp.dot(pw_sc[it], x.astype(BF16), preferred_element_type=F32)
    for it in items:
        h, lo = it // n_sub, (it % n_sub) * sub
        x = x_sc[it]
        u_ref[h, lo:lo + sub, :] = x[:, :HEAD_W].astype(BF16)
        w16 = x[:, HEAD_W:].astype(BF16)
        for c in range(sub // CHUNK):
            r0 = lo + c * CHUNK
            wq_ref[h, 2 * r0:2 * r0 + CHUNK, :] = w16[c * CHUNK:(c + 1) * CHUNK]


def _gdn_prep(p, p_halo_src, meta_halo, conv_w, ba, alog, dtb, *, batch, seq, rows, sub, n_pad):
    m = batch * seq
    nt = seq // rows
    g_q, g_k, g_v = 3, 4, 5
    hb = rows // HALO

    def main_spec(g):
        return pl.BlockSpec((DN_HEADS, rows, HEAD_W), lambda b, t: (g, b * nt + t, 0))

    def halo_spec(g):
        return pl.BlockSpec((DN_HEADS, HALO, HEAD_W),
                            lambda b, t: (g, jnp.maximum((b * nt + t) * hb - 1, 0), 0))

    full = lambda shape: pl.BlockSpec(shape, lambda b, t: (0,) * len(shape))
    out_shape = (jax.ShapeDtypeStruct((DN_HEADS, m, HEAD_W), BF16),
                 jax.ShapeDtypeStruct((DN_HEADS, 2 * m, HEAD_W), BF16),
                 jax.ShapeDtypeStruct((DN_HEADS, m, HEAD_W), BF16),
                 jax.ShapeDtypeStruct((DN_HEADS, m, sub), BF16),
                 jax.ShapeDtypeStruct((m, LANES), F32))
    out_specs = (pl.BlockSpec((DN_HEADS, rows, HEAD_W), lambda b, t: (0, b * nt + t, 0)),
                 pl.BlockSpec((DN_HEADS, 2 * rows, HEAD_W), lambda b, t: (0, b * nt + t, 0)),
                 pl.BlockSpec((DN_HEADS, rows, HEAD_W), lambda b, t: (0, b * nt + t, 0)),
                 pl.BlockSpec((DN_HEADS, rows, sub), lambda b, t: (0, b * nt + t, 0)),
                 pl.BlockSpec((rows, LANES), lambda b, t: (b * nt + t, 0)))
    return pl.pallas_call(
        functools.partial(_gdn_prep_kernel, rows=rows, sub=sub, n_pad=n_pad),
        out_shape=out_shape,
        grid=(batch, nt),
        in_specs=[main_spec(g_q), main_spec(g_k), main_spec(g_v),
                  halo_spec(g_q), halo_spec(g_k), halo_spec(g_v),
                  full(meta_halo.shape), full(conv_w.shape),
                  pl.BlockSpec((rows, LANES), lambda b, t: (b * nt + t, 0)),
                  full(alog.shape), full(dtb.shape)],
        out_specs=out_specs,
        scratch_shapes=[pltpu.VMEM((3, rows + HALO, HEAD_W), F32), pltpu.VMEM((rows, LANES), F32),
                        pltpu.VMEM((DN_HEADS * (rows // sub), sub, sub), BF16),
                        pltpu.VMEM((DN_HEADS * (rows // sub), sub, 2 * HEAD_W), F32)],
        compiler_params=_params("parallel", "parallel"),
        name="gdn_prep",
    )(p, p, p, p_halo_src, p_halo_src, p_halo_src, meta_halo, conv_w, ba, alog, dtb)


def _gdn_scan_kernel(u_ref, wq_ref, kd_ref, a_ref, gl_ref, z_ref,
                     um_ref, wqm_ref, kdm_ref, am_ref, glm_ref, zm_ref, nw_ref,
                     o_ref, om_ref, s_sc, vbuf_sc, sw_sc, *, rows, sub, hpg):
    hg = pl.program_id(1)
    t = pl.program_id(2)
    lane = lax.broadcasted_iota(jnp.int32, (1, LANES), 1)
    nw = nw_ref[...]

    def decay_total(gl_row, h):
        sel = lane == DN_HEADS + hg * hpg + h
        return jnp.sum(jnp.where(sel, jnp.exp(gl_row), 0.0), axis=1, keepdims=True)

    def gate(o, z):
        ms = jnp.mean(o * o, axis=1, keepdims=True)
        return (o * lax.rsqrt(ms + RMS_EPS) * nw * _silu(z.astype(F32))).astype(BF16)

    def run_chunk(r0, refs, out_ref, out_r0):
        u_r, wq_r, kd_r, a_r, gl_r, z_r = refs
        slot = r0 % sub
        gl_row = gl_r[r0:r0 + 1, :]
        for h in range(hpg):
            sw_sc[h] = jnp.dot(wq_r[h, 2 * r0:2 * r0 + 2 * CHUNK, :], s_sc[h].astype(BF16),
                               preferred_element_type=F32)
        for h in range(hpg):
            v_new = u_r[h, r0:r0 + CHUNK, :].astype(F32) - sw_sc[h, 0:CHUNK, :]
            vbuf_sc[h, slot:slot + CHUNK, :] = v_new.astype(BF16)
        for h in range(hpg):
            o = sw_sc[h, CHUNK:, :] + jnp.dot(a_r[h, r0:r0 + CHUNK, :], vbuf_sc[h],
                                              preferred_element_type=F32)
            s_sc[h] = decay_total(gl_row, h) * s_sc[h] + lax.dot_general(
                kd_r[h, r0:r0 + CHUNK, :], vbuf_sc[h, slot:slot + CHUNK, :], _TN,
                preferred_element_type=F32)
            out_ref[out_r0:out_r0 + CHUNK, h * HEAD_W:(h + 1) * HEAD_W] = gate(
                o, z_r[h, r0:r0 + CHUNK, :])

    @pl.when(t == 0)
    def _meta_chunk():
        vbuf_sc[...] = jnp.zeros_like(vbuf_sc)
        s_sc[...] = jnp.zeros_like(s_sc)
        run_chunk(sub - CHUNK, (um_ref, wqm_ref, kdm_ref, am_ref, glm_ref, zm_ref), om_ref, 0)

    for c in range(rows // CHUNK):
        run_chunk(c * CHUNK, (u_ref, wq_ref, kd_ref, a_ref, gl_ref, z_ref), o_ref, c * CHUNK)


def _gdn_scan(real, meta, p, p_meta_pad, norm_w, *, batch, seq, rows, sub, hpg):
    u, wq, kd, a, gl = real
    um, wqm, kdm, am, glm = meta
    m = batch * seq
    nt = seq // rows
    ngrp = DN_HEADS // hpg
    z_blk = (6 * DN_HEADS) // hpg
    mrows = um.shape[1]
    assert mrows == sub

    def real_spec(r, w):
        return pl.BlockSpec((hpg, r, w), lambda b, g, t: (g, b * nt + t, 0))

    def meta_spec(r, w):
        return pl.BlockSpec((hpg, r, w), lambda b, g, t: (g, 0, 0))

    return pl.pallas_call(
        functools.partial(_gdn_scan_kernel, rows=rows, sub=sub, hpg=hpg),
        out_shape=(jax.ShapeDtypeStruct((m, GROUP_W), BF16),
                   jax.ShapeDtypeStruct((batch, CHUNK, GROUP_W), BF16)),
        grid=(batch, ngrp, nt),
        in_specs=[real_spec(rows, HEAD_W), real_spec(2 * rows, HEAD_W), real_spec(rows, HEAD_W),
                  real_spec(rows, sub),
                  pl.BlockSpec((rows, LANES), lambda b, g, t: (b * nt + t, 0)),
                  pl.BlockSpec((hpg, rows, HEAD_W), lambda b, g, t: (z_blk + g, b * nt + t, 0)),
                  meta_spec(mrows, HEAD_W), meta_spec(2 * mrows, HEAD_W), meta_spec(mrows, HEAD_W),
                  meta_spec(mrows, mrows),
                  pl.BlockSpec((mrows, LANES), lambda b, g, t: (0, 0)),
                  pl.BlockSpec((hpg, mrows, HEAD_W), lambda b, g, t: (z_blk + g, 0, 0)),
                  pl.BlockSpec((1, HEAD_W), lambda b, g, t: (0, 0))],
        out_specs=(pl.BlockSpec((rows, hpg * HEAD_W), lambda b, g, t: (b * nt + t, g)),
                   pl.BlockSpec((None, CHUNK, hpg * HEAD_W), lambda b, g, t: (b, 0, g))),
        scratch_shapes=[pltpu.VMEM((hpg, HEAD_W, HEAD_W), F32),
                        pltpu.VMEM((hpg, sub, HEAD_W), BF16),
                        pltpu.VMEM((hpg, 2 * CHUNK, HEAD_W), F32)],
        compiler_params=_params("parallel", "parallel", "arbitrary"),
        name="gdn_scan",
    )(u, wq, kd, a, gl, p, um, wqm, kdm, am, glm, p_meta_pad, norm_w)


def _wout_ln_kernel(oa_ref, od_ref, w_ref, x_ref, g_ref, b_ref, h_ref, *, alpha):
    half = oa_ref.shape[1]
    mix = (jnp.dot(oa_ref[...], w_ref[0:half, :], preferred_element_type=F32)
           + jnp.dot(od_ref[...], w_ref[half:, :], preferred_element_type=F32))
    h_ref[...] = _layer_norm(alpha * x_ref[...] + mix, g_ref[...], b_ref[...])


def _wout_ln(o_att, o_dn, w_out, x2d, g, b, *, bm, alpha):
    m, d = x2d.shape
    half = o_att.shape[1]
    return pl.pallas_call(
        functools.partial(_wout_ln_kernel, alpha=alpha),
        out_shape=jax.ShapeDtypeStruct((m, d), F32),
        grid=(m // bm,),
        in_specs=[pl.BlockSpec((bm, half), lambda i: (i, 0)),
                  pl.BlockSpec((bm, half), lambda i: (i, 0)),
                  pl.BlockSpec((2 * half, d), lambda i: (0, 0)),
                  pl.BlockSpec((bm, d), lambda i: (i, 0)),
                  pl.BlockSpec((1, d), lambda i: (0, 0)),
                  pl.BlockSpec((1, d), lambda i: (0, 0))],
        out_specs=pl.BlockSpec((bm, d), lambda i: (i, 0)),
        compiler_params=_params("parallel"),
        name="wout_ln",
    )(o_att, o_dn, w_out, x2d, g, b)


def _ffn_kernel(h_ref, hprev_ref, hmeta_ref, wg_ref, wu_ref, cw_ref, cb_ref, wd_ref, g_ref, b_ref,
                o_ref, xb_sc, gate_sc, *, bm, tiles_per_batch, alpha):
    i = pl.program_id(0)
    f = pl.program_id(1)

    @pl.when(f == 0)
    def _init():
        halo = jnp.where(i % tiles_per_batch == 0, hmeta_ref[...], hprev_ref[...])
        xb_sc[0:HALO, :] = halo.astype(BF16)
        xb_sc[HALO:, :] = h_ref[...].astype(BF16)
        o_ref[...] = jnp.zeros_like(o_ref)

    gate_sc[...] = jnp.dot(xb_sc[...], wg_ref[...], preferred_element_type=F32)
    up = jnp.dot(xb_sc[HALO:, :], wu_ref[...], preferred_element_type=F32)
    cw = cw_ref[...]
    hg = cb_ref[...] + gate_sc[pl.ds(HALO - FFN_CONV + 1, bm), :] * cw[0:1]
    for j in range(1, FFN_CONV):
        hg = hg + gate_sc[pl.ds(HALO - FFN_CONV + 1 + j, bm), :] * cw[j:j + 1]
    act = (_silu(hg) * up).astype(BF16)
    o_ref[...] += jnp.dot(act, wd_ref[...], preferred_element_type=F32)

    @pl.when(f == pl.num_programs(1) - 1)
    def _finish():
        o_ref[...] = _layer_norm(alpha * h_ref[...] + o_ref[...], g_ref[...], b_ref[...])


def _ffn(h1, h1_meta, w_gate, w_up, conv_w, conv_b, w_down, g, b, *, seq, bm, tf, alpha):
    m, d = h1.shape
    dff = w_gate.shape[1]
    tiles_per_batch = seq // bm
    hb = bm // HALO
    meta_blk = h1_meta.shape[0] // HALO - 1
    return pl.pallas_call(
        functools.partial(_ffn_kernel, bm=bm, tiles_per_batch=tiles_per_batch, alpha=alpha),
        out_shape=jax.ShapeDtypeStruct((m, d), F32),
        grid=(m // bm, dff // tf),
        in_specs=[pl.BlockSpec((bm, d), lambda i, f: (i, 0)),
                  pl.BlockSpec((HALO, d), lambda i, f: (jnp.maximum(i * hb - 1, 0), 0)),
                  pl.BlockSpec((HALO, d), lambda i, f: (meta_blk, 0)),
                  pl.BlockSpec((d, tf), lambda i, f: (0, f)),
                  pl.BlockSpec((d, tf), lambda i, f: (0, f)),
                  pl.BlockSpec((FFN_CONV, tf), lambda i, f: (0, f)),
                  pl.BlockSpec((1, tf), lambda i, f: (0, f)),
                  pl.BlockSpec((tf, d), lambda i, f: (f, 0)),
                  pl.BlockSpec((1, d), lambda i, f: (0, 0)),
                  pl.BlockSpec((1, d), lambda i, f: (0, 0))],
        out_specs=pl.BlockSpec((bm, d), lambda i, f: (i, 0)),
        scratch_shapes=[pltpu.VMEM((bm + HALO, d), BF16),
                        pltpu.VMEM((bm + HALO, tf), F32)],
        compiler_params=_params("parallel", "arbitrary"),
        name="ffn",
    )(h1, h1, h1_meta, w_gate, w_up, conv_w, conv_b, w_down, g, b)


def _largest_tile(n, cap, mult):
    best = None
    for c in range(mult, min(n, cap) + 1, mult):
        if n % c == 0:
            best = c
    assert best is not None, (n, cap, mult)
    return best


def _split_halves(w_qk_t):
    n_comp, quarter = HEAD_W // ATT_QK_DIM, ATT_QK_DIM // 2
    depth, n, d = w_qk_t.shape
    w = w_qk_t.reshape(depth, -1, n_comp, 2, quarter, d)
    return jnp.swapaxes(w, 2, 3).reshape(depth, n, d)


def _rope_tables(n_pos):
    n_comp = HEAD_W // ATT_QK_DIM
    inv_freq = ROPE_THETA ** (-jnp.arange(0, ATT_QK_DIM, 2, dtype=F32) / ATT_QK_DIM)
    ang = jnp.arange(n_pos, dtype=F32)[:, None] * inv_freq[None, :]
    cos, sin = jnp.cos(ang), jnp.sin(ang)
    cos_t = jnp.tile(cos, (1, 2 * n_comp))
    sin_t = jnp.concatenate([jnp.tile(-sin, (1, n_comp)), jnp.tile(sin, (1, n_comp))], axis=1)
    assert cos_t.shape[1] == HEAD_W
    return cos_t, sin_t


def kernel(x, meta_tokens, w_in, conv_qkv_w, a_log, dt_bias, lambda_q1, lambda_k1, lambda_q2, lambda_k2,
           diff_norm_w, delta_norm_w, w_out, ln1_g, ln1_b, ffn_w_gate, ffn_w_up, ffn_conv_w, ffn_conv_b,
           ffn_w_down, ln2_g, ln2_b):
    batch, seq, d = x.shape
    n_meta = meta_tokens.shape[0]
    depth = w_in.shape[0]
    n_main = N_GROUPS * GROUP_W
    assert w_in.shape[2] == n_main + TAIL_COLS and n_meta % HALO == 0 and n_meta <= CHUNK
    m = batch * seq
    alpha = (2.0 * depth) ** 0.25

    bm_proj = _largest_tile(seq, 1024, HALO)
    bm_row = _largest_tile(seq, 512, HALO)
    tq = _largest_tile(seq, 256, CHUNK)
    rows = _largest_tile(seq, 256, LANES)
    sub = LANES
    hpg = DN_HEADS
    att_hps = ATT_HEADS

    cos_t, sin_t = _rope_tables(n_meta + seq)
    cos_meta, sin_meta = cos_t[:n_meta], sin_t[:n_meta]
    cos_real, sin_real = cos_t[n_meta:], sin_t[n_meta:]

    h = x.reshape(m, d)
    h_meta = meta_tokens.astype(x.dtype)
    n_qk = 2 * GROUP_W
    w_in_t = jnp.swapaxes(w_in, 1, 2)
    w_t = w_in_t.astype(BF16)
    wqk_t = _split_halves(w_t[:, :n_qk])
    pad_rows = sub - n_meta
    for l in range(depth):
        w_tail_t = jnp.pad(w_in_t[l, n_main:], ((0, LANES - TAIL_COLS), (0, 0)))
        lane_pad = (DN_HEADS, LANES - TAIL_COLS)
        alog = jnp.pad(a_log[l].astype(F32), lane_pad)[None]
        dtb = jnp.pad(dt_bias[l].astype(F32), lane_pad)[None]
        conv_w = conv_qkv_w[l].reshape(DN_CONV, 3 * DN_HEADS, HEAD_W).transpose(1, 0, 2)
        lams = tuple(v[l][None].astype(F32) for v in (lambda_q1, lambda_k1, lambda_q2, lambda_k2))
        lam_init = 0.8 - 0.6 * math.exp(-0.3 * l)

        p, p_meta, ba, ba_meta = _proj(h, h_meta, wqk_t, w_t, l, w_tail_t, cos_real, sin_real,
                                       cos_meta, sin_meta, bm_proj)

        att_nw = diff_norm_w[l].astype(F32)[:, None]
        o_att = _attention(p, p, p_meta, lams, att_nw, batch=batch, seq=seq, tq=tq, hps=att_hps,
                           lam_init=lam_init, meta_only=False)
        q_meta = jnp.pad(p_meta[:ATT_HEADS], ((0, 0), (0, LANES - n_meta), (0, 0)))
        o_att_meta = _attention(q_meta, None, p_meta, lams, att_nw, batch=1, seq=LANES, tq=LANES,
                                hps=att_hps, lam_init=lam_init, meta_only=True)[:n_meta]

        p_meta_pad = jnp.pad(p_meta, ((0, 0), (pad_rows, 0), (0, 0)))
        ba_meta_pad = jnp.pad(ba_meta, ((pad_rows, 0), (0, 0)))
        meta_halo = p_meta[3 * DN_HEADS:6 * DN_HEADS, n_meta - HALO:, :]
        zero_halo = jnp.zeros_like(meta_halo)
        prep_real = _gdn_prep(p, p, meta_halo, conv_w, ba, alog, dtb, batch=batch, seq=seq, rows=rows,
                              sub=sub, n_pad=0)
        prep_meta = _gdn_prep(p_meta_pad, p_meta_pad, zero_halo, conv_w, ba_meta_pad, alog, dtb, batch=1,
                              seq=sub, rows=sub, sub=sub, n_pad=pad_rows)
        o_dn, o_dn_meta = _gdn_scan(prep_real, prep_meta, p, p_meta_pad, delta_norm_w[l][None],
                                    batch=batch, seq=seq, rows=rows, sub=sub, hpg=hpg)
        o_dn_meta = o_dn_meta[0, CHUNK - n_meta:]

        w_out_b = w_out[l].astype(BF16)
        g1, b1 = ln1_g[l][None].astype(F32), ln1_b[l][None].astype(F32)
        h1 = _wout_ln(o_att, o_dn, w_out_b, h, g1, b1, bm=bm_row, alpha=alpha)
        h1_meta = _wout_ln(o_att_meta, o_dn_meta, w_out_b, h_meta, g1, b1, bm=n_meta, alpha=alpha)

        dff = ffn_w_gate.shape[2]
        tf = _largest_tile(dff, 512, LANES)
        ffn_args = (ffn_w_gate[l].astype(BF16), ffn_w_up[l].astype(BF16), ffn_conv_w[l].astype(F32),
                    ffn_conv_b[l][None].astype(F32), ffn_w_down[l].astype(BF16),
                    ln2_g[l][None].astype(F32), ln2_b[l][None].astype(F32))
        h_next = _ffn(h1, h1_meta, *ffn_args, seq=seq, bm=bm_row, tf=tf, alpha=alpha)
        if l + 1 < depth:
            zero_meta = jnp.zeros_like(h1_meta)
            h_meta = _ffn(h1_meta, zero_meta, *ffn_args, seq=n_meta, bm=n_meta, tf=tf, alpha=alpha)
        h = h_next
    return h.reshape(batch, seq, d)
```

```python
import functools
import math

import jax
import jax.numpy as jnp
from jax import lax
from jax.experimental import pallas as pl
from jax.experimental.pallas import tpu as pltpu

F32 = jnp.float32
BF16 = jnp.bfloat16

CHUNK = 64
ATT_HEADS = 8
ATT_QK_DIM = 64
HEAD_W = 128
DN_HEADS = 8
DN_CONV = 4
FFN_CONV = 3
ROPE_THETA = 10000.0
LN_EPS = 1e-5
RMS_EPS = 1e-6
GROUP_W = ATT_HEADS * HEAD_W
Q_SCALE = ATT_QK_DIM ** -0.5 * math.log2(math.e)
N_GROUPS = 7
TAIL_COLS = 2 * DN_HEADS

LANES = 128
HALO = 8
VMEM_LIMIT = 56 * 1024 * 1024

_NT = (((1,), (1,)), ((), ()))
_TN = (((0,), (0,)), ((), ()))


def _params(*sem):
    return pltpu.CompilerParams(dimension_semantics=sem, vmem_limit_bytes=VMEM_LIMIT)


def _sigmoid(x):
    return 1.0 / (1.0 + jnp.exp(-x))


def _silu(x):
    return x * _sigmoid(x)


def _softplus(x):
    return jnp.maximum(x, 0.0) + jnp.log(1.0 + jnp.exp(-jnp.abs(x)))


def _layer_norm(y, g, b):
    mu = jnp.mean(y, axis=1, keepdims=True)
    d = y - mu
    var = jnp.mean(d * d, axis=1, keepdims=True)
    return d * lax.rsqrt(var + LN_EPS) * g + b


def _proj_kernel(x_ref, xm_ref, wqk_ref, w_ref, wt_ref, cos_ref, sin_ref, cosm_ref, sinm_ref,
                 p_ref, pm_ref, ba_ref, bam_ref, *, heads, n_qk_tiles):
    j = pl.program_id(0)
    i = pl.program_id(1)

    def tail(x_r):
        return lax.dot_general(x_r[...].astype(BF16), wt_ref[...], _NT, preferred_element_type=F32)

    def project(xh, wsel_ref, rope_tables, out_ref):
        acc = lax.dot_general(xh, wsel_ref[...], _NT, preferred_element_type=F32)
        if rope_tables is not None:
            scale = jnp.where(j == 0, Q_SCALE, 1.0).astype(F32)
            cos = rope_tables[0][...] * scale
            sin = rope_tables[1][...] * scale
        for h in range(heads):
            blk = acc[:, h * HEAD_W:(h + 1) * HEAD_W]
            if rope_tables is not None:
                blk = blk * cos + pltpu.roll(blk, HEAD_W // 2, 1) * sin
            out_ref[h] = blk.astype(BF16)

    def tile(wsel_ref, rope):
        @pl.when(i == 0)
        def _meta_rows():
            project(xm_ref[...].astype(BF16), wsel_ref, (cosm_ref, sinm_ref) if rope else None, pm_ref)

        project(x_ref[...].astype(BF16), wsel_ref, (cos_ref, sin_ref) if rope else None, p_ref)

    @pl.when(j < n_qk_tiles)
    def _qk_tile():
        tile(wqk_ref, True)

    @pl.when(j >= n_qk_tiles)
    def _plain_tile():
        tile(w_ref, False)

    @pl.when(j == 0)
    def _tail():
        ba_ref[...] = tail(x_ref)

        @pl.when(i == 0)
        def _meta_tail():
            bam_ref[...] = tail(xm_ref)


def _proj(x2d, x_meta, wqk_t, w_t, layer, w_tail_t, cos, sin, cos_meta, sin_meta, bm):
    m, d = x2d.shape
    n_meta = x_meta.shape[0]
    heads = GROUP_W // HEAD_W
    n_blocks = N_GROUPS * heads
    nm = m // bm
    n_qk_tiles = wqk_t.shape[1] // GROUP_W
    const = lambda shape: pl.BlockSpec(shape, lambda j, i: (0,) * len(shape))
    row = lambda j, i: (i, 0)
    tiles_per_batch = cos.shape[0] // bm
    pos = lambda j, i: (i % tiles_per_batch, 0)
    return pl.pallas_call(
        functools.partial(_proj_kernel, heads=heads, n_qk_tiles=n_qk_tiles),
        out_shape=(jax.ShapeDtypeStruct((n_blocks, m, HEAD_W), BF16),
                   jax.ShapeDtypeStruct((n_blocks, n_meta, HEAD_W), BF16),
                   jax.ShapeDtypeStruct((m, LANES), F32),
                   jax.ShapeDtypeStruct((n_meta, LANES), F32)),
        grid=(N_GROUPS, nm),
        in_specs=[
            pl.BlockSpec((bm, d), row),
            const((n_meta, d)),
            pl.BlockSpec((None, GROUP_W, d), lambda j, i: (layer, jnp.minimum(j, n_qk_tiles - 1), 0)),
            pl.BlockSpec((None, GROUP_W, d), lambda j, i: (layer, jnp.maximum(j, n_qk_tiles), 0)),
            const((LANES, d)),
            pl.BlockSpec((bm, HEAD_W), pos),
            pl.BlockSpec((bm, HEAD_W), pos),
            const((n_meta, HEAD_W)),
            const((n_meta, HEAD_W)),
        ],
        out_specs=(
            pl.BlockSpec((heads, bm, HEAD_W), lambda j, i: (j, i, 0)),
            pl.BlockSpec((heads, n_meta, HEAD_W), lambda j, i: (j, 0, 0)),
            pl.BlockSpec((bm, LANES), lambda j, i: (jnp.where(j == 0, i, nm - 1), 0)),
            const((n_meta, LANES)),
        ),
        compiler_params=_params("arbitrary", "arbitrary"),
        name="proj",
    )(x2d, x_meta, wqk_t, w_t, w_tail_t, cos, sin, cos_meta, sin_meta)


def _attn_kernel(*refs, tq, hps, lam_init, meta_only):
    if meta_only:
        q_ref, km_ref, vm_ref, lq1, lk1, lq2, lk2, nw_ref, o_ref, m_sc, l_sc, acc_sc = refs
    else:
        (q_ref, k_ref, v_ref, km_ref, vm_ref, lq1, lk1, lq2, lk2, nw_ref, o_ref,
         m_sc, l_sc, acc_sc, s_sc, a_sc, p_sc) = refs
    lane = lax.broadcasted_iota(jnp.int32, (1, HEAD_W), 1)
    comp = (lane // (ATT_QK_DIM // 2)) % 2
    qc = []
    for h in range(hps):
        q = q_ref[h]
        zero = jnp.zeros_like(q)
        qc.append((jnp.where(comp == 0, q, zero), jnp.where(comp == 1, q, zero)))

    nc = 2 * hps

    def init_from_meta():
        s_meta = [lax.dot_general(km_ref[i // 2], qc[i // 2][i % 2], _NT, preferred_element_type=F32)
                  for i in range(nc)]
        p_meta = []
        for i in range(nc):
            m0 = jnp.max(s_meta[i], axis=0, keepdims=True)
            p = jnp.exp2(s_meta[i] - m0)
            m_sc[i] = m0
            l_sc[i] = jnp.sum(p, axis=0, keepdims=True)
            p_meta.append(p.astype(BF16))
        for i in range(nc):
            acc_sc[i] = lax.dot_general(vm_ref[i // 2], p_meta[i], _TN, preferred_element_type=F32)

    def scores(t, slot):
        off = pl.multiple_of(t * tq, tq)
        for h in range(hps):
            kt = k_ref[h, pl.ds(off, tq), :]
            for c in range(2):
                s_sc[slot * nc + 2 * h + c] = lax.dot_general(kt, qc[h][c], _NT,
                                                              preferred_element_type=F32)

    def absorb(t, slot, mask):
        off = pl.multiple_of(t * tq, tq)
        for i in range(nc):
            s = s_sc[slot * nc + i]
            if mask is not None:
                s = jnp.where(mask, s, -jnp.inf)
            m_old = m_sc[i]
            m_new = jnp.maximum(m_old, jnp.max(s, axis=0, keepdims=True))
            a = jnp.exp2(m_old - m_new)
            p = jnp.exp2(s - m_new)
            l_sc[i] = a * l_sc[i] + jnp.sum(p, axis=0, keepdims=True)
            m_sc[i] = m_new
            a_sc[i] = a
            p_sc[i] = p.astype(BF16)
        for h in range(hps):
            vt = v_ref[h, pl.ds(off, tq), :]
            for c in range(2):
                i = 2 * h + c
                acc_sc[i] = a_sc[i] * acc_sc[i] + lax.dot_general(vt, p_sc[i], _TN,
                                                                  preferred_element_type=F32)

    if meta_only:
        init_from_meta()
    else:
        qi = pl.program_id(2)
        keys = lax.broadcasted_iota(jnp.int32, (tq, tq), 0)
        queries = lax.broadcasted_iota(jnp.int32, (tq, tq), 1)
        diag = (keys // CHUNK) <= (queries // CHUNK)
        scores(0, 0)
        init_from_meta()

        def body(k, carry):
            scores(2 * k + 1, 1)
            absorb(2 * k, 0, None)
            scores(2 * k + 2, 0)
            absorb(2 * k + 1, 1, None)
            return carry

        lax.fori_loop(0, qi // 2, body, 0)

        @pl.when(qi % 2 == 1)
        def _odd_tail():
            scores(qi, 1)
            absorb(qi - 1, 0, None)
            absorb(qi, 1, diag)

        @pl.when(qi % 2 == 0)
        def _even_tail():
            absorb(qi, 0, diag)

    lam = (jnp.exp(jnp.sum(lq1[...] * lk1[...], axis=1, keepdims=True))
           - jnp.exp(jnp.sum(lq2[...] * lk2[...], axis=1, keepdims=True)) + lam_init)
    for h in range(hps):
        o = acc_sc[2 * h] / l_sc[2 * h] - lam * (acc_sc[2 * h + 1] / l_sc[2 * h + 1])
        ms = jnp.mean(o * o, axis=0, keepdims=True)
        o = o * lax.rsqrt(ms + RMS_EPS) * nw_ref[...] * (1.0 - lam_init)
        o_ref[:, h * HEAD_W:(h + 1) * HEAD_W] = o.T.astype(BF16)


def _attention(p_q, p_kv, p_meta, lams, norm_w, *, batch, seq, tq, hps, lam_init, meta_only):
    n_meta = p_meta.shape[1]
    rows = p_q.shape[1]
    nq = 1 if meta_only else seq // tq
    k_blk, v_blk = ATT_HEADS // hps, 2 * ATT_HEADS // hps
    small = [pl.BlockSpec((1, ATT_QK_DIM), lambda b, g, i: (0, 0))] * 4 + [
        pl.BlockSpec((HEAD_W, 1), lambda b, g, i: (0, 0))]
    meta_specs = [pl.BlockSpec((hps, n_meta, HEAD_W), lambda b, g, i: (k_blk + g, 0, 0)),
                  pl.BlockSpec((hps, n_meta, HEAD_W), lambda b, g, i: (v_blk + g, 0, 0))]
    q_spec = pl.BlockSpec((hps, tq, HEAD_W), lambda b, g, i: (g, b * nq + i, 0))
    if meta_only:
        in_specs = [q_spec] + meta_specs + small
        args = (p_q, p_meta, p_meta) + lams + (norm_w,)
    else:
        kv_specs = [pl.BlockSpec((hps, seq, HEAD_W), lambda b, g, i: (k_blk + g, b, 0)),
                    pl.BlockSpec((hps, seq, HEAD_W), lambda b, g, i: (v_blk + g, b, 0))]
        in_specs = [q_spec] + kv_specs + meta_specs + small
        args = (p_q, p_kv, p_kv, p_meta, p_meta) + lams + (norm_w,)
    scratch = [pltpu.VMEM((2 * hps, 1, tq), F32), pltpu.VMEM((2 * hps, 1, tq), F32),
               pltpu.VMEM((2 * hps, HEAD_W, tq), F32)]
    if not meta_only:
        scratch += [pltpu.VMEM((2 * 2 * hps, tq, tq), F32),
                    pltpu.VMEM((2 * hps, 1, tq), F32),
                    pltpu.VMEM((2 * hps, tq, tq), BF16)]
    return pl.pallas_call(
        functools.partial(_attn_kernel, tq=tq, hps=hps, lam_init=lam_init, meta_only=meta_only),
        out_shape=jax.ShapeDtypeStruct((rows, GROUP_W), BF16),
        grid=(batch, ATT_HEADS // hps, nq),
        in_specs=in_specs,
        out_specs=pl.BlockSpec((tq, hps * HEAD_W), lambda b, g, i: (b * nq + i, g)),
        scratch_shapes=scratch,
        compiler_params=_params("parallel", "parallel", "arbitrary"),
        name="attn_meta" if meta_only else "attn",
    )(*args)


def _gdn_prep_kernel(q_ref, k_ref, v_ref, hq_ref, hk_ref, hv_ref, mh_ref, cw_ref, ba_ref, alog_ref,
                     dtb_ref, u_ref, wq_ref, kd_ref, a_ref, gl_ref, xs_sc, gc_sc, pw_sc, x_sc,
                     *, rows, sub, n_pad):
    first = pl.program_id(1) == 0
    n_sub = rows // sub
    ba = ba_ref[...]
    beta_all = _sigmoid(ba)
    g_all = -jnp.exp(alog_ref[...]) * _softplus(ba + dtb_ref[...])
    row = lax.broadcasted_iota(jnp.int32, (rows, 1), 0)
    if n_pad:
        beta_all = jnp.where(row >= n_pad, beta_all, 0.0)
        g_all = jnp.where(row >= n_pad, g_all, 0.0)

    pos = row % CHUNK
    gc_all = g_all
    shift = 1
    while shift < CHUNK:
        gc_all = gc_all + jnp.where(pos >= shift, pltpu.roll(gc_all, shift, 0), 0.0)
        shift *= 2
    gc_sc[...] = gc_all
    gl_all = jnp.concatenate(
        [jnp.broadcast_to(gc_sc[(c + 1) * CHUNK - 1:(c + 1) * CHUNK, :], (CHUNK, LANES))
         for c in range(rows // CHUNK)], axis=0)
    gl_ref[...] = gl_all
    gct_all = gc_all.T

    ri = lax.broadcasted_iota(jnp.int32, (sub, sub), 0)
    ci = lax.broadcasted_iota(jnp.int32, (sub, sub), 1)
    same = (ri // CHUNK) == (ci // CHUNK)
    tril = jnp.logical_and(same, ci <= ri)
    strict = jnp.logical_and(same, ci < ri)

    def conv_act(main_ref, halo_ref, stream, h):
        halo = jnp.where(first, mh_ref[stream * DN_HEADS + h], halo_ref[h])
        xs_sc[stream, 0:HALO, :] = halo.astype(F32)
        xs_sc[stream, HALO:, :] = main_ref[h].astype(F32)
        w = cw_ref[stream * DN_HEADS + h]
        y = xs_sc[stream, pl.ds(HALO - DN_CONV + 1, rows), :] * w[0:1]
        for j in range(1, DN_CONV):
            y = y + xs_sc[stream, pl.ds(HALO - DN_CONV + 1 + j, rows), :] * w[j:j + 1]
        return _silu(y)

    for h in range(DN_HEADS):
        q = conv_act(q_ref, hq_ref, 0, h)
        k = conv_act(k_ref, hk_ref, 1, h)
        v = conv_act(v_ref, hv_ref, 2, h)
        q = q * lax.rsqrt(jnp.sum(q * q, axis=1, keepdims=True) + RMS_EPS) * (HEAD_W ** -0.5)
        k = k * lax.rsqrt(jnp.sum(k * k, axis=1, keepdims=True) + RMS_EPS)
        beta = beta_all[:, h:h + 1]
        gc = gc_all[:, DN_HEADS + h:DN_HEADS + h + 1]
        gl = gl_all[:, DN_HEADS + h:DN_HEADS + h + 1]
        egc = jnp.exp(gc)
        kb = k * beta
        k16 = k.astype(BF16)
        kb16 = kb.astype(BF16)
        q16 = q.astype(BF16)
        x_all = jnp.concatenate([v * beta, kb * egc], axis=1)
        qd16 = (q * egc).astype(BF16)
        kd_ref[h] = (k * jnp.exp(gl - gc)).astype(BF16)
        for c in range(rows // CHUNK):
            wq_ref[h, (2 * c + 1) * CHUNK:(2 * c + 2) * CHUNK, :] = qd16[c * CHUNK:(c + 1) * CHUNK]
        for s in range(n_sub):
            lo, hi = s * sub, (s + 1) * sub
            gcr = gct_all[DN_HEADS + h:DN_HEADS + h + 1, lo:hi]
            kk = lax.dot_general(kb16[lo:hi], k16[lo:hi], _NT, preferred_element_type=F32)
            qk = lax.dot_general(q16[lo:hi], k16[lo:hi], _NT, preferred_element_type=F32)
            dec = jnp.exp(jnp.where(tril, gc[lo:hi] - gcr, -jnp.inf))
            a_ref[h, lo:hi, :] = (qk * dec).astype(BF16)
            pw_sc[h * n_sub + s] = jnp.where(strict, kk * dec, 0.0).astype(BF16)
            x_sc[h * n_sub + s] = x_all[lo:hi]

    items = range(DN_HEADS * n_sub)
    for it in items:
        x = x_sc[it]
        x_sc[it] = x - jnp.dot(pw_sc[it], x.astype(BF16), preferred_element_type=F32)
    for _ in range(int(math.log2(CHUNK)) - 1):
        for it in items:
            p16 = pw_sc[it]
            pw_sc[it] = jnp.dot(p16, p16, preferred_element_type=F32).astype(BF16)
        for it in items:
            x = x_sc[it]
            x_sc[it] = x + jnp.dot(pw_sc[it], x.astype(BF16), preferred_element_type=F32)
    for it in items:
        h, lo = it // n_sub, (it % n_sub) * sub
        x = x_sc[it]
        u_ref[h, lo:lo + sub, :] = x[:, :HEAD_W].astype(BF16)
        w16 = x[:, HEAD_W:].astype(BF16)
        for c in range(sub // CHUNK):
            r0 = lo + c * CHUNK
            wq_ref[h, 2 * r0:2 * r0 + CHUNK, :] = w16[c * CHUNK:(c + 1) * CHUNK]


def _gdn_prep(p, p_halo_src, meta_halo, conv_w, ba, alog, dtb, *, batch, seq, rows, sub, n_pad):
    m = batch * seq
    nt = seq // rows
    g_q, g_k, g_v = 3, 4, 5
    hb = rows // HALO

    def main_spec(g):
        return pl.BlockSpec((DN_HEADS, rows, HEAD_W), lambda b, t: (g, b * nt + t, 0))

    def halo_spec(g):
        return pl.BlockSpec((DN_HEADS, HALO, HEAD_W),
                            lambda b, t: (g, jnp.maximum((b * nt + t) * hb - 1, 0), 0))

    full = lambda shape: pl.BlockSpec(shape, lambda b, t: (0,) * len(shape))
    out_shape = (jax.ShapeDtypeStruct((DN_HEADS, m, HEAD_W), BF16),
                 jax.ShapeDtypeStruct((DN_HEADS, 2 * m, HEAD_W), BF16),
                 jax.ShapeDtypeStruct((DN_HEADS, m, HEAD_W), BF16),
                 jax.ShapeDtypeStruct((DN_HEADS, m, sub), BF16),
                 jax.ShapeDtypeStruct((m, LANES), F32))
    out_specs = (pl.BlockSpec((DN_HEADS, rows, HEAD_W), lambda b, t: (0, b * nt + t, 0)),
                 pl.BlockSpec((DN_HEADS, 2 * rows, HEAD_W), lambda b, t: (0, b * nt + t, 0)),
                 pl.BlockSpec((DN_HEADS, rows, HEAD_W), lambda b, t: (0, b * nt + t, 0)),
                 pl.BlockSpec((DN_HEADS, rows, sub), lambda b, t: (0, b * nt + t, 0)),
                 pl.BlockSpec((rows, LANES), lambda b, t: (b * nt + t, 0)))
    return pl.pallas_call(
        functools.partial(_gdn_prep_kernel, rows=rows, sub=sub, n_pad=n_pad),
        out_shape=out_shape,
        grid=(batch, nt),
        in_specs=[main_spec(g_q), main_spec(g_k), main_spec(g_v),
                  halo_spec(g_q), halo_spec(g_k), halo_spec(g_v),
                  full(meta_halo.shape), full(conv_w.shape),
                  pl.BlockSpec((rows, LANES), lambda b, t: (b * nt + t, 0)),
                  full(alog.shape), full(dtb.shape)],
        out_specs=out_specs,
        scratch_shapes=[pltpu.VMEM((3, rows + HALO, HEAD_W), F32), pltpu.VMEM((rows, LANES), F32),
                        pltpu.VMEM((DN_HEADS * (rows // sub), sub, sub), BF16),
                        pltpu.VMEM((DN_HEADS * (rows // sub), sub, 2 * HEAD_W), F32)],
        compiler_params=_params("parallel", "parallel"),
        name="gdn_prep",
    )(p, p, p, p_halo_src, p_halo_src, p_halo_src, meta_halo, conv_w, ba, alog, dtb)


def _gdn_scan_kernel(u_ref, wq_ref, kd_ref, a_ref, gl_ref, z_ref,
                     um_ref, wqm_ref, kdm_ref, am_ref, glm_ref, zm_ref, nw_ref,
                     o_ref, om_ref, s_sc, vbuf_sc, sw_sc, *, rows, sub, hpg):
    hg = pl.program_id(1)
    t = pl.program_id(2)
    lane = lax.broadcasted_iota(jnp.int32, (1, LANES), 1)
    nw = nw_ref[...]

    def decay_total(gl_row, h):
        sel = lane == DN_HEADS + hg * hpg + h
        return jnp.sum(jnp.where(sel, jnp.exp(gl_row), 0.0), axis=1, keepdims=True)

    def gate(o, z):
        ms = jnp.mean(o * o, axis=1, keepdims=True)
        return (o * lax.rsqrt(ms + RMS_EPS) * nw * _silu(z.astype(F32))).astype(BF16)

    def run_chunk(r0, refs, out_ref, out_r0):
        u_r, wq_r, kd_r, a_r, gl_r, z_r = refs
        slot = r0 % sub
        gl_row = gl_r[r0:r0 + 1, :]
        for h in range(hpg):
            sw_sc[h] = jnp.dot(wq_r[h, 2 * r0:2 * r0 + 2 * CHUNK, :], s_sc[h].astype(BF16),
                               preferred_element_type=F32)
        for h in range(hpg):
            v_new = u_r[h, r0:r0 + CHUNK, :].astype(F32) - sw_sc[h, 0:CHUNK, :]
            vbuf_sc[h, slot:slot + CHUNK, :] = v_new.astype(BF16)
        for h in range(hpg):
            o = sw_sc[h, CHUNK:, :] + jnp.dot(a_r[h, r0:r0 + CHUNK, :], vbuf_sc[h],
                                              preferred_element_type=F32)
            s_sc[h] = decay_total(gl_row, h) * s_sc[h] + lax.dot_general(
                kd_r[h, r0:r0 + CHUNK, :], vbuf_sc[h, slot:slot + CHUNK, :], _TN,
                preferred_element_type=F32)
            out_ref[out_r0:out_r0 + CHUNK, h * HEAD_W:(h + 1) * HEAD_W] = gate(
                o, z_r[h, r0:r0 + CHUNK, :])

    @pl.when(t == 0)
    def _meta_chunk():
        vbuf_sc[...] = jnp.zeros_like(vbuf_sc)
        s_sc[...] = jnp.zeros_like(s_sc)
        run_chunk(sub - CHUNK, (um_ref, wqm_ref, kdm_ref, am_ref, glm_ref, zm_ref), om_ref, 0)

    for c in range(rows // CHUNK):
        run_chunk(c * CHUNK, (u_ref, wq_ref, kd_ref, a_ref, gl_ref, z_ref), o_ref, c * CHUNK)


def _gdn_scan(real, meta, p, p_meta_pad, norm_w, *, batch, seq, rows, sub, hpg):
    u, wq, kd, a, gl = real
    um, wqm, kdm, am, glm = meta
    m = batch * seq
    nt = seq // rows
    ngrp = DN_HEADS // hpg
    z_blk = (6 * DN_HEADS) // hpg
    mrows = um.shape[1]
    assert mrows == sub

    def real_spec(r, w):
        return pl.BlockSpec((hpg, r, w), lambda b, g, t: (g, b * nt + t, 0))

    def meta_spec(r, w):
        return pl.BlockSpec((hpg, r, w), lambda b, g, t: (g, 0, 0))

    return pl.pallas_call(
        functools.partial(_gdn_scan_kernel, rows=rows, sub=sub, hpg=hpg),
        out_shape=(jax.ShapeDtypeStruct((m, GROUP_W), BF16),
                   jax.ShapeDtypeStruct((batch, CHUNK, GROUP_W), BF16)),
        grid=(batch, ngrp, nt),
        in_specs=[real_spec(rows, HEAD_W), real_spec(2 * rows, HEAD_W), real_spec(rows, HEAD_W),
                  real_spec(rows, sub),
                  pl.BlockSpec((rows, LANES), lambda b, g, t: (b * nt + t, 0)),
                  pl.BlockSpec((hpg, rows, HEAD_W), lambda b, g, t: (z_blk + g, b * nt + t, 0)),
                  meta_spec(mrows, HEAD_W), meta_spec(2 * mrows, HEAD_W), meta_spec(mrows, HEAD_W),
                  meta_spec(mrows, mrows),
                  pl.BlockSpec((mrows, LANES), lambda b, g, t: (0, 0)),
                  pl.BlockSpec((hpg, mrows, HEAD_W), lambda b, g, t: (z_blk + g, 0, 0)),
                  pl.BlockSpec((1, HEAD_W), lambda b, g, t: (0, 0))],
        out_specs=(pl.BlockSpec((rows, hpg * HEAD_W), lambda b, g, t: (b * nt + t, g)),
                   pl.BlockSpec((None, CHUNK, hpg * HEAD_W), lambda b, g, t: (b, 0, g))),
        scratch_shapes=[pltpu.VMEM((hpg, HEAD_W, HEAD_W), F32),
                        pltpu.VMEM((hpg, sub, HEAD_W), BF16),
                        pltpu.VMEM((hpg, 2 * CHUNK, HEAD_W), F32)],
        compiler_params=_params("parallel", "parallel", "arbitrary"),
        name="gdn_scan",
    )(u, wq, kd, a, gl, p, um, wqm, kdm, am, glm, p_meta_pad, norm_w)


def _wout_ln_kernel(oa_ref, od_ref, w_ref, x_ref, g_ref, b_ref, h_ref, *, alpha):
    half = oa_ref.shape[1]
    mix = (jnp.dot(oa_ref[...], w_ref[0:half, :], preferred_element_type=F32)
           + jnp.dot(od_ref[...], w_ref[half:, :], preferred_element_type=F32))
    h_ref[...] = _layer_norm(alpha * x_ref[...] + mix, g_ref[...], b_ref[...])


def _wout_ln(o_att, o_dn, w_out, x2d, g, b, *, bm, alpha):
    m, d = x2d.shape
    half = o_att.shape[1]
    return pl.pallas_call(
        functools.partial(_wout_ln_kernel, alpha=alpha),
        out_shape=jax.ShapeDtypeStruct((m, d), F32),
        grid=(m // bm,),
        in_specs=[pl.BlockSpec((bm, half), lambda i: (i, 0)),
                  pl.BlockSpec((bm, half), lambda i: (i, 0)),
                  pl.BlockSpec((2 * half, d), lambda i: (0, 0)),
                  pl.BlockSpec((bm, d), lambda i: (i, 0)),
                  pl.BlockSpec((1, d), lambda i: (0, 0)),
                  pl.BlockSpec((1, d), lambda i: (0, 0))],
        out_specs=pl.BlockSpec((bm, d), lambda i: (i, 0)),
        compiler_params=_params("parallel"),
        name="wout_ln",
    )(o_att, o_dn, w_out, x2d, g, b)


def _ffn_kernel(h_ref, hprev_ref, hmeta_ref, wg_ref, wu_ref, cw_ref, cb_ref, wd_ref, g_ref, b_ref,
                o_ref, xb_sc, gate_sc, up_sc, *, bm, n_parts, tiles_per_batch, alpha):
    i = pl.program_id(0)
    f = pl.program_id(1)

    @pl.when(f == 0)
    def _init():
        halo = jnp.where(i % tiles_per_batch == 0, hmeta_ref[...], hprev_ref[...])
        xb_sc[0:HALO, :] = halo.astype(BF16)
        xb_sc[HALO:, :] = h_ref[...].astype(BF16)
        o_ref[...] = jnp.zeros_like(o_ref)

    cw = cw_ref[...]
    part = bm // n_parts

    def gate_up(r):
        lo = r * part
        g_lo = 0 if r == 0 else lo + HALO
        gate_sc[g_lo:lo + part + HALO, :] = jnp.dot(xb_sc[g_lo:lo + part + HALO, :], wg_ref[...],
                                                    preferred_element_type=F32)
        up_sc[lo:lo + part, :] = jnp.dot(xb_sc[lo + HALO:lo + part + HALO, :], wu_ref[...],
                                         preferred_element_type=F32)

    def gated_down(r):
        lo = r * part
        hg = cb_ref[...] + gate_sc[pl.ds(lo + HALO - FFN_CONV + 1, part), :] * cw[0:1]
        for j in range(1, FFN_CONV):
            hg = hg + gate_sc[pl.ds(lo + HALO - FFN_CONV + 1 + j, part), :] * cw[j:j + 1]
        act = (_silu(hg) * up_sc[lo:lo + part, :]).astype(BF16)
        o_ref[lo:lo + part, :] += jnp.dot(act, wd_ref[...], preferred_element_type=F32)

    gate_up(0)
    for r in range(n_parts):
        if r + 1 < n_parts:
            gate_up(r + 1)
        gated_down(r)

    @pl.when(f == pl.num_programs(1) - 1)
    def _finish():
        o_ref[...] = _layer_norm(alpha * h_ref[...] + o_ref[...], g_ref[...], b_ref[...])


def _ffn(h1, h1_meta, w_gate, w_up, conv_w, conv_b, w_down, g, b, *, seq, bm, tf, alpha):
    m, d = h1.shape
    dff = w_gate.shape[1]
    tiles_per_batch = seq // bm
    hb = bm // HALO
    meta_blk = h1_meta.shape[0] // HALO - 1
    return pl.pallas_call(
        functools.partial(_ffn_kernel, bm=bm, n_parts=2 if bm % (2 * HALO) == 0 else 1,
                          tiles_per_batch=tiles_per_batch, alpha=alpha),
        out_shape=jax.ShapeDtypeStruct((m, d), F32),
        grid=(m // bm, dff // tf),
        in_specs=[pl.BlockSpec((bm, d), lambda i, f: (i, 0)),
                  pl.BlockSpec((HALO, d), lambda i, f: (jnp.maximum(i * hb - 1, 0), 0)),
                  pl.BlockSpec((HALO, d), lambda i, f: (meta_blk, 0)),
                  pl.BlockSpec((d, tf), lambda i, f: (0, f)),
                  pl.BlockSpec((d, tf), lambda i, f: (0, f)),
                  pl.BlockSpec((FFN_CONV, tf), lambda i, f: (0, f)),
                  pl.BlockSpec((1, tf), lambda i, f: (0, f)),
                  pl.BlockSpec((tf, d), lambda i, f: (f, 0)),
                  pl.BlockSpec((1, d), lambda i, f: (0, 0)),
                  pl.BlockSpec((1, d), lambda i, f: (0, 0))],
        out_specs=pl.BlockSpec((bm, d), lambda i, f: (i, 0)),
        scratch_shapes=[pltpu.VMEM((bm + HALO, d), BF16),
                        pltpu.VMEM((bm + HALO, tf), F32),
                        pltpu.VMEM((bm, tf), F32)],
        compiler_params=_params("parallel", "arbitrary"),
        name="ffn",
    )(h1, h1, h1_meta, w_gate, w_up, conv_w, conv_b, w_down, g, b)


def _largest_tile(n, cap, mult):
    best = None
    for c in range(mult, min(n, cap) + 1, mult):
        if n % c == 0:
            best = c
    assert best is not None, (n, cap, mult)
    return best


def _split_halves(w_qk_t):
    n_comp, quarter = HEAD_W // ATT_QK_DIM, ATT_QK_DIM // 2
    depth, n, d = w_qk_t.shape
    w = w_qk_t.reshape(depth, -1, n_comp, 2, quarter, d)
    return jnp.swapaxes(w, 2, 3).reshape(depth, n, d)


def _rope_tables(n_pos):
    n_comp = HEAD_W // ATT_QK_DIM
    inv_freq = ROPE_THETA ** (-jnp.arange(0, ATT_QK_DIM, 2, dtype=F32) / ATT_QK_DIM)
    ang = jnp.arange(n_pos, dtype=F32)[:, None] * inv_freq[None, :]
    cos, sin = jnp.cos(ang), jnp.sin(ang)
    cos_t = jnp.tile(cos, (1, 2 * n_comp))
    sin_t = jnp.concatenate([jnp.tile(-sin, (1, n_comp)), jnp.tile(sin, (1, n_comp))], axis=1)
    assert cos_t.shape[1] == HEAD_W
    return cos_t, sin_t


def kernel(x, meta_tokens, w_in, conv_qkv_w, a_log, dt_bias, lambda_q1, lambda_k1, lambda_q2, lambda_k2,
           diff_norm_w, delta_norm_w, w_out, ln1_g, ln1_b, ffn_w_gate, ffn_w_up, ffn_conv_w, ffn_conv_b,
           ffn_w_down, ln2_g, ln2_b):
    batch, seq, d = x.shape
    n_meta = meta_tokens.shape[0]
    depth = w_in.shape[0]
    n_main = N_GROUPS * GROUP_W
    assert w_in.shape[2] == n_main + TAIL_COLS and n_meta % HALO == 0 and n_meta <= CHUNK
    m = batch * seq
    alpha = (2.0 * depth) ** 0.25

    bm_proj = _largest_tile(seq, 1024, HALO)
    bm_row = _largest_tile(seq, 512, HALO)
    tq = _largest_tile(seq, 256, CHUNK)
    rows = _largest_tile(seq, 256, LANES)
    sub = LANES
    hpg = DN_HEADS
    att_hps = ATT_HEADS

    cos_t, sin_t = _rope_tables(n_meta + seq)
    cos_meta, sin_meta = cos_t[:n_meta], sin_t[:n_meta]
    cos_real, sin_real = cos_t[n_meta:], sin_t[n_meta:]

    h = x.reshape(m, d)
    h_meta = meta_tokens.astype(x.dtype)
    n_qk = 2 * GROUP_W
    w_in_t = jnp.swapaxes(w_in, 1, 2)
    w_t = w_in_t.astype(BF16)
    wqk_t = _split_halves(w_t[:, :n_qk])
    pad_rows = sub - n_meta
    for l in range(depth):
        w_tail_t = jnp.pad(w_t[l, n_main:], ((0, LANES - TAIL_COLS), (0, 0)))
        lane_pad = (DN_HEADS, LANES - TAIL_COLS)
        alog = jnp.pad(a_log[l].astype(F32), lane_pad)[None]
        dtb = jnp.pad(dt_bias[l].astype(F32), lane_pad)[None]
        conv_w = conv_qkv_w[l].reshape(DN_CONV, 3 * DN_HEADS, HEAD_W).transpose(1, 0, 2)
        lams = tuple(v[l][None].astype(F32) for v in (lambda_q1, lambda_k1, lambda_q2, lambda_k2))
        lam_init = 0.8 - 0.6 * math.exp(-0.3 * l)

        p, p_meta, ba, ba_meta = _proj(h, h_meta, wqk_t, w_t, l, w_tail_t, cos_real, sin_real,
                                       cos_meta, sin_meta, bm_proj)

        att_nw = diff_norm_w[l].astype(F32)[:, None]
        o_att = _attention(p, p, p_meta, lams, att_nw, batch=batch, seq=seq, tq=tq, hps=att_hps,
                           lam_init=lam_init, meta_only=False)
        q_meta = jnp.pad(p_meta[:ATT_HEADS], ((0, 0), (0, LANES - n_meta), (0, 0)))
        o_att_meta = _attention(q_meta, None, p_meta, lams, att_nw, batch=1, seq=LANES, tq=LANES,
                                hps=att_hps, lam_init=lam_init, meta_only=True)[:n_meta]

        p_meta_pad = jnp.pad(p_meta, ((0, 0), (pad_rows, 0), (0, 0)))
        ba_meta_pad = jnp.pad(ba_meta, ((pad_rows, 0), (0, 0)))
        meta_halo = p_meta[3 * DN_HEADS:6 * DN_HEADS, n_meta - HALO:, :]
        zero_halo = jnp.zeros_like(meta_halo)
        prep_real = _gdn_prep(p, p, meta_halo, conv_w, ba, alog, dtb, batch=batch, seq=seq, rows=rows,
                              sub=sub, n_pad=0)
        prep_meta = _gdn_prep(p_meta_pad, p_meta_pad, zero_halo, conv_w, ba_meta_pad, alog, dtb, batch=1,
                              seq=sub, rows=sub, sub=sub, n_pad=pad_rows)
        o_dn, o_dn_meta = _gdn_scan(prep_real, prep_meta, p, p_meta_pad, delta_norm_w[l][None],
                                    batch=batch, seq=seq, rows=rows, sub=sub, hpg=hpg)
        o_dn_meta = o_dn_meta[0, CHUNK - n_meta:]

        w_out_b = w_out[l].astype(BF16)
        g1, b1 = ln1_g[l][None].astype(F32), ln1_b[l][None].astype(F32)
        h1 = _wout_ln(o_att, o_dn, w_out_b, h, g1, b1, bm=bm_row, alpha=alpha)
        h1_meta = _wout_ln(o_att_meta, o_dn_meta, w_out_b, h_meta, g1, b1, bm=n_meta, alpha=alpha)

        dff = ffn_w_gate.shape[2]
        tf = _largest_tile(dff, 512, LANES)
        ffn_args = (ffn_w_gate[l].astype(BF16), ffn_w_up[l].astype(BF16), ffn_conv_w[l].astype(F32),
                    ffn_conv_b[l][None].astype(F32), ffn_w_down[l].astype(BF16),
                    ln2_g[l][None].astype(F32), ln2_b[l][None].astype(F32))
        h_next = _ffn(h1, h1_meta, *ffn_args, seq=seq, bm=bm_row, tf=tf, alpha=alpha)
        if l + 1 < depth:
            zero_meta = jnp.zeros_like(h1_meta)
            h_meta = _ffn(h1_meta, zero_meta, *ffn_args, seq=n_meta, bm=n_meta, tf=tf, alpha=alpha)
        h = h_next
    return h.reshape(batch, seq, d)
```

```python
import functools
import math

import jax
import jax.numpy as jnp
from jax import lax
from jax.experimental import pallas as pl
from jax.experimental.pallas import tpu as pltpu

F32 = jnp.float32
BF16 = jnp.bfloat16

CHUNK = 64
ATT_HEADS = 8
ATT_QK_DIM = 64
HEAD_W = 128
DN_HEADS = 8
DN_CONV = 4
FFN_CONV = 3
ROPE_THETA = 10000.0
LN_EPS = 1e-5
RMS_EPS = 1e-6
GROUP_W = ATT_HEADS * HEAD_W
Q_SCALE = ATT_QK_DIM ** -0.5 * math.log2(math.e)
N_GROUPS = 7
TAIL_COLS = 2 * DN_HEADS

LANES = 128
HALO = 8
VMEM_LIMIT = 56 * 1024 * 1024

_NT = (((1,), (1,)), ((), ()))
_TN = (((0,), (0,)), ((), ()))


def _params(*sem):
    return pltpu.CompilerParams(dimension_semantics=sem, vmem_limit_bytes=VMEM_LIMIT)


def _sigmoid(x):
    return 1.0 / (1.0 + jnp.exp(-x))


def _silu(x):
    return x * _sigmoid(x)


def _softplus(x):
    return jnp.maximum(x, 0.0) + jnp.log(1.0 + jnp.exp(-jnp.abs(x)))


def _layer_norm(y, g, b):
    mu = jnp.mean(y, axis=1, keepdims=True)
    d = y - mu
    var = jnp.mean(d * d, axis=1, keepdims=True)
    return d * lax.rsqrt(var + LN_EPS) * g + b


def _proj_kernel(x_ref, xm_ref, wqk_ref, w_ref, wt_ref, cos_ref, sin_ref, cosm_ref, sinm_ref,
                 p_ref, pm_ref, ba_ref, bam_ref, *, heads, n_qk_tiles):
    j = pl.program_id(0)
    i = pl.program_id(1)

    def tail(x_r):
        return lax.dot_general(x_r[...].astype(BF16), wt_ref[...], _NT, preferred_element_type=F32)

    def project(xh, wsel_ref, rope_tables, out_ref):
        acc = lax.dot_general(xh, wsel_ref[...], _NT, preferred_element_type=F32)
        if rope_tables is not None:
            scale = jnp.where(j == 0, Q_SCALE, 1.0).astype(F32)
            cos = rope_tables[0][...] * scale
            sin = rope_tables[1][...] * scale
        for h in range(heads):
            blk = acc[:, h * HEAD_W:(h + 1) * HEAD_W]
            if rope_tables is not None:
                blk = blk * cos + pltpu.roll(blk, HEAD_W // 2, 1) * sin
            out_ref[h] = blk.astype(BF16)

    def tile(wsel_ref, rope):
        @pl.when(i == 0)
        def _meta_rows():
            project(xm_ref[...].astype(BF16), wsel_ref, (cosm_ref, sinm_ref) if rope else None, pm_ref)

        project(x_ref[...].astype(BF16), wsel_ref, (cos_ref, sin_ref) if rope else None, p_ref)

    @pl.when(j < n_qk_tiles)
    def _qk_tile():
        tile(wqk_ref, True)

    @pl.when(j >= n_qk_tiles)
    def _plain_tile():
        tile(w_ref, False)

    @pl.when(j == 0)
    def _tail():
        ba_ref[...] = tail(x_ref)

        @pl.when(i == 0)
        def _meta_tail():
            bam_ref[...] = tail(xm_ref)


def _proj(x2d, x_meta, wqk_t, w_t, layer, w_tail_t, cos, sin, cos_meta, sin_meta, bm):
    m, d = x2d.shape
    n_meta = x_meta.shape[0]
    heads = GROUP_W // HEAD_W
    n_blocks = N_GROUPS * heads
    nm = m // bm
    n_qk_tiles = wqk_t.shape[1] // GROUP_W
    const = lambda shape: pl.BlockSpec(shape, lambda j, i: (0,) * len(shape))
    row = lambda j, i: (i, 0)
    tiles_per_batch = cos.shape[0] // bm
    pos = lambda j, i: (i % tiles_per_batch, 0)
    return pl.pallas_call(
        functools.partial(_proj_kernel, heads=heads, n_qk_tiles=n_qk_tiles),
        out_shape=(jax.ShapeDtypeStruct((n_blocks, m, HEAD_W), BF16),
                   jax.ShapeDtypeStruct((n_blocks, n_meta, HEAD_W), BF16),
                   jax.ShapeDtypeStruct((m, LANES), F32),
                   jax.ShapeDtypeStruct((n_meta, LANES), F32)),
        grid=(N_GROUPS, nm),
        in_specs=[
            pl.BlockSpec((bm, d), row),
            const((n_meta, d)),
            pl.BlockSpec((None, GROUP_W, d), lambda j, i: (layer, jnp.minimum(j, n_qk_tiles - 1), 0)),
            pl.BlockSpec((None, GROUP_W, d), lambda j, i: (layer, jnp.maximum(j, n_qk_tiles), 0)),
            const((LANES, d)),
            pl.BlockSpec((bm, HEAD_W), pos),
            pl.BlockSpec((bm, HEAD_W), pos),
            const((n_meta, HEAD_W)),
            const((n_meta, HEAD_W)),
        ],
        out_specs=(
            pl.BlockSpec((heads, bm, HEAD_W), lambda j, i: (j, i, 0)),
            pl.BlockSpec((heads, n_meta, HEAD_W), lambda j, i: (j, 0, 0)),
            pl.BlockSpec((bm, LANES), lambda j, i: (jnp.where(j == 0, i, nm - 1), 0)),
            const((n_meta, LANES)),
        ),
        compiler_params=_params("arbitrary", "arbitrary"),
        name="proj",
    )(x2d, x_meta, wqk_t, w_t, w_tail_t, cos, sin, cos_meta, sin_meta)


def _attn_kernel(*refs, tq, hps, lam_init, meta_only):
    if meta_only:
        q_ref, km_ref, vm_ref, lq1, lk1, lq2, lk2, nw_ref, o_ref, m_sc, l_sc, acc_sc = refs
    else:
        (q_ref, k_ref, v_ref, km_ref, vm_ref, lq1, lk1, lq2, lk2, nw_ref, o_ref,
         m_sc, l_sc, acc_sc, s_sc, a_sc, p_sc) = refs
    lane = lax.broadcasted_iota(jnp.int32, (1, HEAD_W), 1)
    comp = (lane // (ATT_QK_DIM // 2)) % 2
    qc = []
    for h in range(hps):
        q = q_ref[h]
        zero = jnp.zeros_like(q)
        qc.append((jnp.where(comp == 0, q, zero), jnp.where(comp == 1, q, zero)))

    nc = 2 * hps

    def init_from_meta():
        s_meta = [lax.dot_general(km_ref[i // 2], qc[i // 2][i % 2], _NT, preferred_element_type=F32)
                  for i in range(nc)]
        p_meta = []
        for i in range(nc):
            m0 = jnp.max(s_meta[i], axis=0, keepdims=True)
            p = jnp.exp2(s_meta[i] - m0)
            m_sc[i] = m0
            l_sc[i] = jnp.sum(p, axis=0, keepdims=True)
            p_meta.append(p.astype(BF16))
        for i in range(nc):
            acc_sc[i] = lax.dot_general(vm_ref[i // 2], p_meta[i], _TN, preferred_element_type=F32)

    def scores(t, slot):
        off = pl.multiple_of(t * tq, tq)
        for h in range(hps):
            kt = k_ref[h, pl.ds(off, tq), :]
            for c in range(2):
                s_sc[slot * nc + 2 * h + c] = lax.dot_general(kt, qc[h][c], _NT,
                                                              preferred_element_type=F32)

    def absorb(t, slot, mask):
        off = pl.multiple_of(t * tq, tq)
        for i in range(nc):
            s = s_sc[slot * nc + i]
            if mask is not None:
                s = jnp.where(mask, s, -jnp.inf)
            m_old = m_sc[i]
            m_new = jnp.maximum(m_old, jnp.max(s, axis=0, keepdims=True))
            a = jnp.exp2(m_old - m_new)
            p = jnp.exp2(s - m_new)
            l_sc[i] = a * l_sc[i] + jnp.sum(p, axis=0, keepdims=True)
            m_sc[i] = m_new
            a_sc[i] = a
            p_sc[i] = p.astype(BF16)
        for h in range(hps):
            vt = v_ref[h, pl.ds(off, tq), :]
            for c in range(2):
                i = 2 * h + c
                acc_sc[i] = a_sc[i] * acc_sc[i] + lax.dot_general(vt, p_sc[i], _TN,
                                                                  preferred_element_type=F32)

    if meta_only:
        init_from_meta()
    else:
        qi = pl.program_id(2)
        keys = lax.broadcasted_iota(jnp.int32, (tq, tq), 0)
        queries = lax.broadcasted_iota(jnp.int32, (tq, tq), 1)
        diag = (keys // CHUNK) <= (queries // CHUNK)
        scores(0, 0)
        init_from_meta()

        def body(k, carry):
            scores(2 * k + 1, 1)
            absorb(2 * k, 0, None)
            scores(2 * k + 2, 0)
            absorb(2 * k + 1, 1, None)
            return carry

        lax.fori_loop(0, qi // 2, body, 0)

        @pl.when(qi % 2 == 1)
        def _odd_tail():
            scores(qi, 1)
            absorb(qi - 1, 0, None)
            absorb(qi, 1, diag)

        @pl.when(qi % 2 == 0)
        def _even_tail():
            absorb(qi, 0, diag)

    lam = (jnp.exp(jnp.sum(lq1[...] * lk1[...], axis=1, keepdims=True))
           - jnp.exp(jnp.sum(lq2[...] * lk2[...], axis=1, keepdims=True)) + lam_init)
    for h in range(hps):
        o = acc_sc[2 * h] / l_sc[2 * h] - lam * (acc_sc[2 * h + 1] / l_sc[2 * h + 1])
        ms = jnp.mean(o * o, axis=0, keepdims=True)
        o = o * lax.rsqrt(ms + RMS_EPS) * nw_ref[...] * (1.0 - lam_init)
        o_ref[:, h * HEAD_W:(h + 1) * HEAD_W] = o.T.astype(BF16)


def _attention(p_q, p_kv, p_meta, lams, norm_w, *, batch, seq, tq, hps, lam_init, meta_only):
    n_meta = p_meta.shape[1]
    rows = p_q.shape[1]
    nq = 1 if meta_only else seq // tq
    k_blk, v_blk = ATT_HEADS // hps, 2 * ATT_HEADS // hps
    small = [pl.BlockSpec((1, ATT_QK_DIM), lambda b, g, i: (0, 0))] * 4 + [
        pl.BlockSpec((HEAD_W, 1), lambda b, g, i: (0, 0))]
    meta_specs = [pl.BlockSpec((hps, n_meta, HEAD_W), lambda b, g, i: (k_blk + g, 0, 0)),
                  pl.BlockSpec((hps, n_meta, HEAD_W), lambda b, g, i: (v_blk + g, 0, 0))]
    q_spec = pl.BlockSpec((hps, tq, HEAD_W), lambda b, g, i: (g, b * nq + i, 0))
    if meta_only:
        in_specs = [q_spec] + meta_specs + small
        args = (p_q, p_meta, p_meta) + lams + (norm_w,)
    else:
        kv_specs = [pl.BlockSpec((hps, seq, HEAD_W), lambda b, g, i: (k_blk + g, b, 0)),
                    pl.BlockSpec((hps, seq, HEAD_W), lambda b, g, i: (v_blk + g, b, 0))]
        in_specs = [q_spec] + kv_specs + meta_specs + small
        args = (p_q, p_kv, p_kv, p_meta, p_meta) + lams + (norm_w,)
    scratch = [pltpu.VMEM((2 * hps, 1, tq), F32), pltpu.VMEM((2 * hps, 1, tq), F32),
               pltpu.VMEM((2 * hps, HEAD_W, tq), F32)]
    if not meta_only:
        scratch += [pltpu.VMEM((2 * 2 * hps, tq, tq), F32),
                    pltpu.VMEM((2 * hps, 1, tq), F32),
                    pltpu.VMEM((2 * hps, tq, tq), BF16)]
    return pl.pallas_call(
        functools.partial(_attn_kernel, tq=tq, hps=hps, lam_init=lam_init, meta_only=meta_only),
        out_shape=jax.ShapeDtypeStruct((rows, GROUP_W), BF16),
        grid=(batch, ATT_HEADS // hps, nq),
        in_specs=in_specs,
        out_specs=pl.BlockSpec((tq, hps * HEAD_W), lambda b, g, i: (b * nq + i, g)),
        scratch_shapes=scratch,
        compiler_params=_params("parallel", "parallel", "arbitrary"),
        name="attn_meta" if meta_only else "attn",
    )(*args)


def _gdn_prep_kernel(q_ref, k_ref, v_ref, hq_ref, hk_ref, hv_ref, mh_ref, cw_ref, ba_ref, alog_ref,
                     dtb_ref, u_ref, wq_ref, kd_ref, a_ref, gl_ref, xs_sc, gc_sc, pw_sc, x_sc,
                     *, rows, sub, n_pad):
    first = pl.program_id(1) == 0
    n_sub = rows // sub
    ba = ba_ref[...]
    beta_all = _sigmoid(ba)
    g_all = -jnp.exp(alog_ref[...]) * _softplus(ba + dtb_ref[...])
    row = lax.broadcasted_iota(jnp.int32, (rows, 1), 0)
    if n_pad:
        beta_all = jnp.where(row >= n_pad, beta_all, 0.0)
        g_all = jnp.where(row >= n_pad, g_all, 0.0)

    pos = row % CHUNK
    gc_all = g_all
    shift = 1
    while shift < CHUNK:
        gc_all = gc_all + jnp.where(pos >= shift, pltpu.roll(gc_all, shift, 0), 0.0)
        shift *= 2
    gc_sc[...] = gc_all
    gl_all = jnp.concatenate(
        [jnp.broadcast_to(gc_sc[(c + 1) * CHUNK - 1:(c + 1) * CHUNK, :], (CHUNK, LANES))
         for c in range(rows // CHUNK)], axis=0)
    gl_ref[...] = gl_all
    gct_all = gc_all.T

    ri = lax.broadcasted_iota(jnp.int32, (sub, sub), 0)
    ci = lax.broadcasted_iota(jnp.int32, (sub, sub), 1)
    same = (ri // CHUNK) == (ci // CHUNK)
    tril = jnp.logical_and(same, ci <= ri)
    strict = jnp.logical_and(same, ci < ri)

    def conv_act(main_ref, halo_ref, stream, h):
        halo = jnp.where(first, mh_ref[stream * DN_HEADS + h], halo_ref[h])
        xs_sc[stream, 0:HALO, :] = halo.astype(F32)
        xs_sc[stream, HALO:, :] = main_ref[h].astype(F32)
        w = cw_ref[stream * DN_HEADS + h]
        y = xs_sc[stream, pl.ds(HALO - DN_CONV + 1, rows), :] * w[0:1]
        for j in range(1, DN_CONV):
            y = y + xs_sc[stream, pl.ds(HALO - DN_CONV + 1 + j, rows), :] * w[j:j + 1]
        return _silu(y)

    for h in range(DN_HEADS):
        q = conv_act(q_ref, hq_ref, 0, h)
        k = conv_act(k_ref, hk_ref, 1, h)
        v = conv_act(v_ref, hv_ref, 2, h)
        q = q * lax.rsqrt(jnp.sum(q * q, axis=1, keepdims=True) + RMS_EPS) * (HEAD_W ** -0.5)
        k = k * lax.rsqrt(jnp.sum(k * k, axis=1, keepdims=True) + RMS_EPS)
        beta = beta_all[:, h:h + 1]
        gc = gc_all[:, DN_HEADS + h:DN_HEADS + h + 1]
        gl = gl_all[:, DN_HEADS + h:DN_HEADS + h + 1]
        egc = jnp.exp(gc)
        kb = k * beta
        k16 = k.astype(BF16)
        kb16 = kb.astype(BF16)
        q16 = q.astype(BF16)
        x_all = jnp.concatenate([v * beta, kb * egc], axis=1)
        qd16 = (q * egc).astype(BF16)
        kd_ref[h] = (k * jnp.exp(gl - gc)).astype(BF16)
        for c in range(rows // CHUNK):
            wq_ref[h, (2 * c + 1) * CHUNK:(2 * c + 2) * CHUNK, :] = qd16[c * CHUNK:(c + 1) * CHUNK]
        for s in range(n_sub):
            lo, hi = s * sub, (s + 1) * sub
            gcr = gct_all[DN_HEADS + h:DN_HEADS + h + 1, lo:hi]
            kk = lax.dot_general(kb16[lo:hi], k16[lo:hi], _NT, preferred_element_type=F32)
            qk = lax.dot_general(q16[lo:hi], k16[lo:hi], _NT, preferred_element_type=F32)
            dec = jnp.exp(jnp.where(tril, gc[lo:hi] - gcr, -jnp.inf))
            a_ref[h, lo:hi, :] = (qk * dec).astype(BF16)
            pw_sc[h * n_sub + s] = jnp.where(strict, kk * dec, 0.0).astype(BF16)
            x_sc[h * n_sub + s] = x_all[lo:hi]

    items = range(DN_HEADS * n_sub)
    for it in items:
        x = x_sc[it]
        x_sc[it] = x - jnp.dot(pw_sc[it], x.astype(BF16), preferred_element_type=F32)
    for _ in range(int(math.log2(CHUNK)) - 1):
        for it in items:
            p16 = pw_sc[it]
            pw_sc[it] = jnp.dot(p16, p16, preferred_element_type=F32).astype(BF16)
        for it in items:
            x = x_sc[it]
            x_sc[it] = x + jnp.dot(pw_sc[it], x.astype(BF16), preferred_element_type=F32)
    for it in items:
        h, lo = it // n_sub, (it % n_sub) * sub
        x = x_sc[it]
        u_ref[h, lo:lo + sub, :] = x[:, :HEAD_W].astype(BF16)
        w16 = x[:, HEAD_W:].astype(BF16)
        for c in range(sub // CHUNK):
            r0 = lo + c * CHUNK
            wq_ref[h, 2 * r0:2 * r0 + CHUNK, :] = w16[c * CHUNK:(c + 1) * CHUNK]


def _gdn_prep(p, p_halo_src, meta_halo, conv_w, ba, alog, dtb, *, batch, seq, rows, sub, n_pad):
    m = batch * seq
    nt = seq // rows
    g_q, g_k, g_v = 3, 4, 5
    hb = rows // HALO

    def main_spec(g):
        return pl.BlockSpec((DN_HEADS, rows, HEAD_W), lambda b, t: (g, b * nt + t, 0))

    def halo_spec(g):
        return pl.BlockSpec((DN_HEADS, HALO, HEAD_W),
                            lambda b, t: (g, jnp.maximum((b * nt + t) * hb - 1, 0), 0))

    full = lambda shape: pl.BlockSpec(shape, lambda b, t: (0,) * len(shape))
    out_shape = (jax.ShapeDtypeStruct((DN_HEADS, m, HEAD_W), BF16),
                 jax.ShapeDtypeStruct((DN_HEADS, 2 * m, HEAD_W), BF16),
                 jax.ShapeDtypeStruct((DN_HEADS, m, HEAD_W), BF16),
                 jax.ShapeDtypeStruct((DN_HEADS, m, sub), BF16),
                 jax.ShapeDtypeStruct((m, LANES), F32))
    out_specs = (pl.BlockSpec((DN_HEADS, rows, HEAD_W), lambda b, t: (0, b * nt + t, 0)),
                 pl.BlockSpec((DN_HEADS, 2 * rows, HEAD_W), lambda b, t: (0, b * nt + t, 0)),
                 pl.BlockSpec((DN_HEADS, rows, HEAD_W), lambda b, t: (0, b * nt + t, 0)),
                 pl.BlockSpec((DN_HEADS, rows, sub), lambda b, t: (0, b * nt + t, 0)),
                 pl.BlockSpec((rows, LANES), lambda b, t: (b * nt + t, 0)))
    return pl.pallas_call(
        functools.partial(_gdn_prep_kernel, rows=rows, sub=sub, n_pad=n_pad),
        out_shape=out_shape,
        grid=(batch, nt),
        in_specs=[main_spec(g_q), main_spec(g_k), main_spec(g_v),
                  halo_spec(g_q), halo_spec(g_k), halo_spec(g_v),
                  full(meta_halo.shape), full(conv_w.shape),
                  pl.BlockSpec((rows, LANES), lambda b, t: (b * nt + t, 0)),
                  full(alog.shape), full(dtb.shape)],
        out_specs=out_specs,
        scratch_shapes=[pltpu.VMEM((3, rows + HALO, HEAD_W), F32), pltpu.VMEM((rows, LANES), F32),
                        pltpu.VMEM((DN_HEADS * (rows // sub), sub, sub), BF16),
                        pltpu.VMEM((DN_HEADS * (rows // sub), sub, 2 * HEAD_W), F32)],
        compiler_params=_params("parallel", "parallel"),
        name="gdn_prep",
    )(p, p, p, p_halo_src, p_halo_src, p_halo_src, meta_halo, conv_w, ba, alog, dtb)


def _gdn_scan_kernel(u_ref, wq_ref, kd_ref, a_ref, gl_ref, z_ref,
                     um_ref, wqm_ref, kdm_ref, am_ref, glm_ref, zm_ref, nw_ref,
                     o_ref, om_ref, s_sc, vbuf_sc, sw_sc, *, rows, sub, hpg):
    hg = pl.program_id(1)
    t = pl.program_id(2)
    lane = lax.broadcasted_iota(jnp.int32, (1, LANES), 1)
    nw = nw_ref[...]

    def decay_total(gl_row, h):
        sel = lane == DN_HEADS + hg * hpg + h
        return jnp.sum(jnp.where(sel, jnp.exp(gl_row), 0.0), axis=1, keepdims=True)

    def gate(o, z):
        ms = jnp.mean(o * o, axis=1, keepdims=True)
        return (o * lax.rsqrt(ms + RMS_EPS) * nw * _silu(z.astype(F32))).astype(BF16)

    def run_chunk(r0, refs, out_ref, out_r0):
        u_r, wq_r, kd_r, a_r, gl_r, z_r = refs
        slot = r0 % sub
        gl_row = gl_r[r0:r0 + 1, :]
        for h in range(hpg):
            sw_sc[h] = jnp.dot(wq_r[h, 2 * r0:2 * r0 + 2 * CHUNK, :], s_sc[h].astype(BF16),
                               preferred_element_type=F32)
        for h in range(hpg):
            v_new = u_r[h, r0:r0 + CHUNK, :].astype(F32) - sw_sc[h, 0:CHUNK, :]
            vbuf_sc[h, slot:slot + CHUNK, :] = v_new.astype(BF16)
        for h in range(hpg):
            o = sw_sc[h, CHUNK:, :] + jnp.dot(a_r[h, r0:r0 + CHUNK, :], vbuf_sc[h],
                                              preferred_element_type=F32)
            s_sc[h] = decay_total(gl_row, h) * s_sc[h] + lax.dot_general(
                kd_r[h, r0:r0 + CHUNK, :], vbuf_sc[h, slot:slot + CHUNK, :], _TN,
                preferred_element_type=F32)
            out_ref[out_r0:out_r0 + CHUNK, h * HEAD_W:(h + 1) * HEAD_W] = gate(
                o, z_r[h, r0:r0 + CHUNK, :])

    @pl.when(t == 0)
    def _meta_chunk():
        vbuf_sc[...] = jnp.zeros_like(vbuf_sc)
        s_sc[...] = jnp.zeros_like(s_sc)
        run_chunk(sub - CHUNK, (um_ref, wqm_ref, kdm_ref, am_ref, glm_ref, zm_ref), om_ref, 0)

    for c in range(rows // CHUNK):
        run_chunk(c * CHUNK, (u_ref, wq_ref, kd_ref, a_ref, gl_ref, z_ref), o_ref, c * CHUNK)


def _gdn_scan(real, meta, p, p_meta_pad, norm_w, *, batch, seq, rows, sub, hpg):
    u, wq, kd, a, gl = real
    um, wqm, kdm, am, glm = meta
    m = batch * seq
    nt = seq // rows
    ngrp = DN_HEADS // hpg
    z_blk = (6 * DN_HEADS) // hpg
    mrows = um.shape[1]
    assert mrows == sub

    def real_spec(r, w):
        return pl.BlockSpec((hpg, r, w), lambda b, g, t: (g, b * nt + t, 0))

    def meta_spec(r, w):
        return pl.BlockSpec((hpg, r, w), lambda b, g, t: (g, 0, 0))

    return pl.pallas_call(
        functools.partial(_gdn_scan_kernel, rows=rows, sub=sub, hpg=hpg),
        out_shape=(jax.ShapeDtypeStruct((m, GROUP_W), BF16),
                   jax.ShapeDtypeStruct((batch, CHUNK, GROUP_W), BF16)),
        grid=(batch, ngrp, nt),
        in_specs=[real_spec(rows, HEAD_W), real_spec(2 * rows, HEAD_W), real_spec(rows, HEAD_W),
                  real_spec(rows, sub),
                  pl.BlockSpec((rows, LANES), lambda b, g, t: (b * nt + t, 0)),
                  pl.BlockSpec((hpg, rows, HEAD_W), lambda b, g, t: (z_blk + g, b * nt + t, 0)),
                  meta_spec(mrows, HEAD_W), meta_spec(2 * mrows, HEAD_W), meta_spec(mrows, HEAD_W),
                  meta_spec(mrows, mrows),
                  pl.BlockSpec((mrows, LANES), lambda b, g, t: (0, 0)),
                  pl.BlockSpec((hpg, mrows, HEAD_W), lambda b, g, t: (z_blk + g, 0, 0)),
                  pl.BlockSpec((1, HEAD_W), lambda b, g, t: (0, 0))],
        out_specs=(pl.BlockSpec((rows, hpg * HEAD_W), lambda b, g, t: (b * nt + t, g)),
                   pl.BlockSpec((None, CHUNK, hpg * HEAD_W), lambda b, g, t: (b, 0, g))),
        scratch_shapes=[pltpu.VMEM((hpg, HEAD_W, HEAD_W), F32),
                        pltpu.VMEM((hpg, sub, HEAD_W), BF16),
                        pltpu.VMEM((hpg, 2 * CHUNK, HEAD_W), F32)],
        compiler_params=_params("parallel", "parallel", "arbitrary"),
        name="gdn_scan",
    )(u, wq, kd, a, gl, p, um, wqm, kdm, am, glm, p_meta_pad, norm_w)


def _wout_ln_kernel(oa_ref, od_ref, w_ref, x_ref, g_ref, b_ref, h_ref, *, alpha):
    half = oa_ref.shape[1]
    mix = (jnp.dot(oa_ref[...], w_ref[0:half, :], preferred_element_type=F32)
           + jnp.dot(od_ref[...], w_ref[half:, :], preferred_element_type=F32))
    h_ref[...] = _layer_norm(alpha * x_ref[...] + mix, g_ref[...], b_ref[...])


def _wout_ln(o_att, o_dn, w_out, x2d, g, b, *, bm, alpha):
    m, d = x2d.shape
    half = o_att.shape[1]
    return pl.pallas_call(
        functools.partial(_wout_ln_kernel, alpha=alpha),
        out_shape=jax.ShapeDtypeStruct((m, d), F32),
        grid=(m // bm,),
        in_specs=[pl.BlockSpec((bm, half), lambda i: (i, 0)),
                  pl.BlockSpec((bm, half), lambda i: (i, 0)),
                  pl.BlockSpec((2 * half, d), lambda i: (0, 0)),
                  pl.BlockSpec((bm, d), lambda i: (i, 0)),
                  pl.BlockSpec((1, d), lambda i: (0, 0)),
                  pl.BlockSpec((1, d), lambda i: (0, 0))],
        out_specs=pl.BlockSpec((bm, d), lambda i: (i, 0)),
        compiler_params=_params("parallel"),
        name="wout_ln",
    )(o_att, o_dn, w_out, x2d, g, b)


def _ffn_kernel(h_ref, hprev_ref, hmeta_ref, wg_ref, wu_ref, cw_ref, cb_ref, wd_ref, g_ref, b_ref,
                o_ref, xb_sc, gate_sc, up_sc, *, bm, n_parts, tiles_per_batch, alpha):
    i = pl.program_id(0)
    f = pl.program_id(1)

    @pl.when(f == 0)
    def _init():
        halo = jnp.where(i % tiles_per_batch == 0, hmeta_ref[...], hprev_ref[...])
        xb_sc[0:HALO, :] = halo.astype(BF16)
        xb_sc[HALO:, :] = h_ref[...].astype(BF16)
        o_ref[...] = jnp.zeros_like(o_ref)

    cw = cw_ref[...]
    part = bm // n_parts

    def gate_up(r):
        lo = r * part
        g_lo = 0 if r == 0 else lo + HALO
        gate_sc[g_lo:lo + part + HALO, :] = jnp.dot(xb_sc[g_lo:lo + part + HALO, :], wg_ref[...],
                                                    preferred_element_type=F32)
        up_sc[lo:lo + part, :] = jnp.dot(xb_sc[lo + HALO:lo + part + HALO, :], wu_ref[...],
                                         preferred_element_type=F32)

    def gated_down(r):
        lo = r * part
        hg = cb_ref[...] + gate_sc[pl.ds(lo + HALO - FFN_CONV + 1, part), :] * cw[0:1]
        for j in range(1, FFN_CONV):
            hg = hg + gate_sc[pl.ds(lo + HALO - FFN_CONV + 1 + j, part), :] * cw[j:j + 1]
        act = (_silu(hg) * up_sc[lo:lo + part, :]).astype(BF16)
        o_ref[lo:lo + part, :] += jnp.dot(act, wd_ref[...], preferred_element_type=F32)

    gate_up(0)
    for r in range(n_parts):
        if r + 1 < n_parts:
            gate_up(r + 1)
        gated_down(r)

    @pl.when(f == pl.num_programs(1) - 1)
    def _finish():
        o_ref[...] = _layer_norm(alpha * h_ref[...] + o_ref[...], g_ref[...], b_ref[...])


def _ffn(h1, h1_meta, w_gate, w_up, conv_w, conv_b, w_down, g, b, *, seq, bm, alpha):
    m, d = h1.shape
    nf, _, tf = w_gate.shape
    dff = nf * tf
    tiles_per_batch = seq // bm
    hb = bm // HALO
    meta_blk = h1_meta.shape[0] // HALO - 1
    return pl.pallas_call(
        functools.partial(_ffn_kernel, bm=bm, n_parts=2 if bm % (2 * HALO) == 0 else 1,
                          tiles_per_batch=tiles_per_batch, alpha=alpha),
        out_shape=jax.ShapeDtypeStruct((m, d), F32),
        grid=(m // bm, dff // tf),
        in_specs=[pl.BlockSpec((bm, d), lambda i, f: (i, 0)),
                  pl.BlockSpec((HALO, d), lambda i, f: (jnp.maximum(i * hb - 1, 0), 0)),
                  pl.BlockSpec((HALO, d), lambda i, f: (meta_blk, 0)),
                  pl.BlockSpec((None, d, tf), lambda i, f: (f, 0, 0)),
                  pl.BlockSpec((None, d, tf), lambda i, f: (f, 0, 0)),
                  pl.BlockSpec((FFN_CONV, tf), lambda i, f: (0, f)),
                  pl.BlockSpec((1, tf), lambda i, f: (0, f)),
                  pl.BlockSpec((tf, d), lambda i, f: (f, 0)),
                  pl.BlockSpec((1, d), lambda i, f: (0, 0)),
                  pl.BlockSpec((1, d), lambda i, f: (0, 0))],
        out_specs=pl.BlockSpec((bm, d), lambda i, f: (i, 0)),
        scratch_shapes=[pltpu.VMEM((bm + HALO, d), BF16),
                        pltpu.VMEM((bm + HALO, tf), F32),
                        pltpu.VMEM((bm, tf), F32)],
        compiler_params=_params("parallel", "arbitrary"),
        name="ffn",
    )(h1, h1, h1_meta, w_gate, w_up, conv_w, conv_b, w_down, g, b)


def _largest_tile(n, cap, mult):
    best = None
    for c in range(mult, min(n, cap) + 1, mult):
        if n % c == 0:
            best = c
    assert best is not None, (n, cap, mult)
    return best


def _split_halves(w_qk_t):
    n_comp, quarter = HEAD_W // ATT_QK_DIM, ATT_QK_DIM // 2
    depth, n, d = w_qk_t.shape
    w = w_qk_t.reshape(depth, -1, n_comp, 2, quarter, d)
    return jnp.swapaxes(w, 2, 3).reshape(depth, n, d)


def _rope_tables(n_pos):
    n_comp = HEAD_W // ATT_QK_DIM
    inv_freq = ROPE_THETA ** (-jnp.arange(0, ATT_QK_DIM, 2, dtype=F32) / ATT_QK_DIM)
    ang = jnp.arange(n_pos, dtype=F32)[:, None] * inv_freq[None, :]
    cos, sin = jnp.cos(ang), jnp.sin(ang)
    cos_t = jnp.tile(cos, (1, 2 * n_comp))
    sin_t = jnp.concatenate([jnp.tile(-sin, (1, n_comp)), jnp.tile(sin, (1, n_comp))], axis=1)
    assert cos_t.shape[1] == HEAD_W
    return cos_t, sin_t


def kernel(x, meta_tokens, w_in, conv_qkv_w, a_log, dt_bias, lambda_q1, lambda_k1, lambda_q2, lambda_k2,
           diff_norm_w, delta_norm_w, w_out, ln1_g, ln1_b, ffn_w_gate, ffn_w_up, ffn_conv_w, ffn_conv_b,
           ffn_w_down, ln2_g, ln2_b):
    batch, seq, d = x.shape
    n_meta = meta_tokens.shape[0]
    depth = w_in.shape[0]
    n_main = N_GROUPS * GROUP_W
    assert w_in.shape[2] == n_main + TAIL_COLS and n_meta % HALO == 0 and n_meta <= CHUNK
    m = batch * seq
    alpha = (2.0 * depth) ** 0.25

    bm_proj = _largest_tile(seq, 1024, HALO)
    bm_row = _largest_tile(seq, 512, HALO)
    tq = _largest_tile(seq, 256, CHUNK)
    rows = _largest_tile(seq, 256, LANES)
    sub = LANES
    hpg = DN_HEADS
    att_hps = ATT_HEADS

    cos_t, sin_t = _rope_tables(n_meta + seq)
    cos_meta, sin_meta = cos_t[:n_meta], sin_t[:n_meta]
    cos_real, sin_real = cos_t[n_meta:], sin_t[n_meta:]

    h = x.reshape(m, d)
    h_meta = meta_tokens.astype(x.dtype)
    n_qk = 2 * GROUP_W
    w_in_t = jnp.swapaxes(w_in, 1, 2)
    w_t = w_in_t.astype(BF16)
    wqk_t = _split_halves(w_t[:, :n_qk])
    pad_rows = sub - n_meta
    for l in range(depth):
        w_tail_t = jnp.pad(w_t[l, n_main:], ((0, LANES - TAIL_COLS), (0, 0)))
        lane_pad = (DN_HEADS, LANES - TAIL_COLS)
        alog = jnp.pad(a_log[l].astype(F32), lane_pad)[None]
        dtb = jnp.pad(dt_bias[l].astype(F32), lane_pad)[None]
        conv_w = conv_qkv_w[l].reshape(DN_CONV, 3 * DN_HEADS, HEAD_W).transpose(1, 0, 2)
        lams = tuple(v[l][None].astype(F32) for v in (lambda_q1, lambda_k1, lambda_q2, lambda_k2))
        lam_init = 0.8 - 0.6 * math.exp(-0.3 * l)

        p, p_meta, ba, ba_meta = _proj(h, h_meta, wqk_t, w_t, l, w_tail_t, cos_real, sin_real,
                                       cos_meta, sin_meta, bm_proj)

        att_nw = diff_norm_w[l].astype(F32)[:, None]
        o_att = _attention(p, p, p_meta, lams, att_nw, batch=batch, seq=seq, tq=tq, hps=att_hps,
                           lam_init=lam_init, meta_only=False)
        q_meta = jnp.pad(p_meta[:ATT_HEADS], ((0, 0), (0, LANES - n_meta), (0, 0)))
        o_att_meta = _attention(q_meta, None, p_meta, lams, att_nw, batch=1, seq=LANES, tq=LANES,
                                hps=att_hps, lam_init=lam_init, meta_only=True)[:n_meta]

        p_meta_pad = jnp.pad(p_meta, ((0, 0), (pad_rows, 0), (0, 0)))
        ba_meta_pad = jnp.pad(ba_meta, ((pad_rows, 0), (0, 0)))
        meta_halo = p_meta[3 * DN_HEADS:6 * DN_HEADS, n_meta - HALO:, :]
        zero_halo = jnp.zeros_like(meta_halo)
        prep_real = _gdn_prep(p, p, meta_halo, conv_w, ba, alog, dtb, batch=batch, seq=seq, rows=rows,
                              sub=sub, n_pad=0)
        prep_meta = _gdn_prep(p_meta_pad, p_meta_pad, zero_halo, conv_w, ba_meta_pad, alog, dtb, batch=1,
                              seq=sub, rows=sub, sub=sub, n_pad=pad_rows)
        o_dn, o_dn_meta = _gdn_scan(prep_real, prep_meta, p, p_meta_pad, delta_norm_w[l][None],
                                    batch=batch, seq=seq, rows=rows, sub=sub, hpg=hpg)
        o_dn_meta = o_dn_meta[0, CHUNK - n_meta:]

        w_out_b = w_out[l].astype(BF16)
        g1, b1 = ln1_g[l][None].astype(F32), ln1_b[l][None].astype(F32)
        h1 = _wout_ln(o_att, o_dn, w_out_b, h, g1, b1, bm=bm_row, alpha=alpha)
        h1_meta = _wout_ln(o_att_meta, o_dn_meta, w_out_b, h_meta, g1, b1, bm=n_meta, alpha=alpha)

        dff = ffn_w_gate.shape[2]
        tf = _largest_tile(dff, 512, LANES)
        tiled = lambda w: w.astype(BF16).reshape(d, dff // tf, tf).transpose(1, 0, 2)
        ffn_args = (tiled(ffn_w_gate[l]), tiled(ffn_w_up[l]), ffn_conv_w[l].astype(F32),
                    ffn_conv_b[l][None].astype(F32), ffn_w_down[l].astype(BF16),
                    ln2_g[l][None].astype(F32), ln2_b[l][None].astype(F32))
        h_next = _ffn(h1, h1_meta, *ffn_args, seq=seq, bm=bm_row, alpha=alpha)
        if l + 1 < depth:
            zero_meta = jnp.zeros_like(h1_meta)
            h_meta = _ffn(h1_meta, zero_meta, *ffn_args, seq=n_meta, bm=n_meta, alpha=alpha)
        h = h_next
    return h.reshape(batch, seq, d)
```

```python
import functools
import math

import jax
import jax.numpy as jnp
from jax import lax
from jax.experimental import pallas as pl
from jax.experimental.pallas import tpu as pltpu

F32 = jnp.float32
BF16 = jnp.bfloat16

CHUNK = 64
ATT_HEADS = 8
ATT_QK_DIM = 64
HEAD_W = 128
DN_HEADS = 8
DN_CONV = 4
FFN_CONV = 3
ROPE_THETA = 10000.0
LN_EPS = 1e-5
RMS_EPS = 1e-6
GROUP_W = ATT_HEADS * HEAD_W
Q_SCALE = ATT_QK_DIM ** -0.5 * math.log2(math.e)
N_GROUPS = 7
TAIL_COLS = 2 * DN_HEADS

LANES = 128
HALO = 8
VMEM_LIMIT = 56 * 1024 * 1024

_NT = (((1,), (1,)), ((), ()))
_TN = (((0,), (0,)), ((), ()))


def _params(*sem):
    return pltpu.CompilerParams(dimension_semantics=sem, vmem_limit_bytes=VMEM_LIMIT)


def _sigmoid(x):
    return 1.0 / (1.0 + jnp.exp(-x))


def _silu(x):
    return x * _sigmoid(x)


def _softplus(x):
    return jnp.maximum(x, 0.0) + jnp.log(1.0 + jnp.exp(-jnp.abs(x)))


def _layer_norm(y, g, b):
    mu = jnp.mean(y, axis=1, keepdims=True)
    d = y - mu
    var = jnp.mean(d * d, axis=1, keepdims=True)
    return d * lax.rsqrt(var + LN_EPS) * g + b


def _proj_kernel(x_ref, xm_ref, wqk_ref, w_ref, wt_ref, cos_ref, sin_ref, cosm_ref, sinm_ref,
                 p_ref, pm_ref, ba_ref, bam_ref, *, heads, n_qk_tiles):
    j = pl.program_id(0)
    i = pl.program_id(1)

    def tail(x_r):
        return lax.dot_general(x_r[...].astype(BF16), wt_ref[...], _NT, preferred_element_type=F32)

    def project(xh, wsel_ref, rope_tables, out_ref):
        acc = lax.dot_general(xh, wsel_ref[...], _NT, preferred_element_type=F32)
        if rope_tables is not None:
            scale = jnp.where(j == 0, Q_SCALE, 1.0).astype(F32)
            cos = rope_tables[0][...] * scale
            sin = rope_tables[1][...] * scale
        for h in range(heads):
            blk = acc[:, h * HEAD_W:(h + 1) * HEAD_W]
            if rope_tables is not None:
                blk = blk * cos + pltpu.roll(blk, HEAD_W // 2, 1) * sin
            out_ref[h] = blk.astype(BF16)

    def tile(wsel_ref, rope):
        @pl.when(i == 0)
        def _meta_rows():
            project(xm_ref[...].astype(BF16), wsel_ref, (cosm_ref, sinm_ref) if rope else None, pm_ref)

        project(x_ref[...].astype(BF16), wsel_ref, (cos_ref, sin_ref) if rope else None, p_ref)

    @pl.when(j < n_qk_tiles)
    def _qk_tile():
        tile(wqk_ref, True)

    @pl.when(j >= n_qk_tiles)
    def _plain_tile():
        tile(w_ref, False)

    @pl.when(j == 0)
    def _tail():
        ba_ref[...] = tail(x_ref)

        @pl.when(i == 0)
        def _meta_tail():
            bam_ref[...] = tail(xm_ref)


def _proj(x2d, x_meta, wqk_t, w_t, layer, w_tail_t, cos, sin, cos_meta, sin_meta, bm):
    m, d = x2d.shape
    n_meta = x_meta.shape[0]
    heads = GROUP_W // HEAD_W
    n_blocks = N_GROUPS * heads
    nm = m // bm
    n_qk_tiles = wqk_t.shape[1] // GROUP_W
    const = lambda shape: pl.BlockSpec(shape, lambda j, i: (0,) * len(shape))
    row = lambda j, i: (i, 0)
    tiles_per_batch = cos.shape[0] // bm
    pos = lambda j, i: (i % tiles_per_batch, 0)
    return pl.pallas_call(
        functools.partial(_proj_kernel, heads=heads, n_qk_tiles=n_qk_tiles),
        out_shape=(jax.ShapeDtypeStruct((n_blocks, m, HEAD_W), BF16),
                   jax.ShapeDtypeStruct((n_blocks, n_meta, HEAD_W), BF16),
                   jax.ShapeDtypeStruct((m, LANES), F32),
                   jax.ShapeDtypeStruct((n_meta, LANES), F32)),
        grid=(N_GROUPS, nm),
        in_specs=[
            pl.BlockSpec((bm, d), row),
            const((n_meta, d)),
            pl.BlockSpec((None, GROUP_W, d), lambda j, i: (layer, jnp.minimum(j, n_qk_tiles - 1), 0)),
            pl.BlockSpec((None, GROUP_W, d), lambda j, i: (layer, jnp.maximum(j, n_qk_tiles), 0)),
            const((LANES, d)),
            pl.BlockSpec((bm, HEAD_W), pos),
            pl.BlockSpec((bm, HEAD_W), pos),
            const((n_meta, HEAD_W)),
            const((n_meta, HEAD_W)),
        ],
        out_specs=(
            pl.BlockSpec((heads, bm, HEAD_W), lambda j, i: (j, i, 0)),
            pl.BlockSpec((heads, n_meta, HEAD_W), lambda j, i: (j, 0, 0)),
            pl.BlockSpec((bm, LANES), lambda j, i: (jnp.where(j == 0, i, nm - 1), 0)),
            const((n_meta, LANES)),
        ),
        compiler_params=_params("arbitrary", "arbitrary"),
        name="proj",
    )(x2d, x_meta, wqk_t, w_t, w_tail_t, cos, sin, cos_meta, sin_meta)


def _attn_kernel(*refs, tq, hps, lam_init, meta_only):
    if meta_only:
        q_ref, km_ref, vm_ref, lq1, lk1, lq2, lk2, nw_ref, o_ref, m_sc, l_sc, acc_sc = refs
    else:
        (q_ref, k_ref, v_ref, km_ref, vm_ref, lq1, lk1, lq2, lk2, nw_ref, o_ref,
         m_sc, l_sc, acc_sc, s_sc, a_sc, p_sc) = refs
    lane = lax.broadcasted_iota(jnp.int32, (1, HEAD_W), 1)
    comp = (lane // (ATT_QK_DIM // 2)) % 2
    qc = []
    for h in range(hps):
        q = q_ref[h]
        zero = jnp.zeros_like(q)
        qc.append((jnp.where(comp == 0, q, zero), jnp.where(comp == 1, q, zero)))

    nc = 2 * hps

    def init_from_meta():
        s_meta = [lax.dot_general(km_ref[i // 2], qc[i // 2][i % 2], _NT, preferred_element_type=F32)
                  for i in range(nc)]
        p_meta = []
        for i in range(nc):
            m0 = jnp.max(s_meta[i], axis=0, keepdims=True)
            p = jnp.exp2(s_meta[i] - m0)
            m_sc[i] = m0
            l_sc[i] = jnp.sum(p, axis=0, keepdims=True)
            p_meta.append(p.astype(BF16))
        for i in range(nc):
            acc_sc[i] = lax.dot_general(vm_ref[i // 2], p_meta[i], _TN, preferred_element_type=F32)

    def scores(t, slot):
        off = pl.multiple_of(t * tq, tq)
        for h in range(hps):
            kt = k_ref[h, pl.ds(off, tq), :]
            for c in range(2):
                s_sc[slot * nc + 2 * h + c] = lax.dot_general(kt, qc[h][c], _NT,
                                                              preferred_element_type=F32)

    def absorb(t, slot, mask):
        off = pl.multiple_of(t * tq, tq)
        for i in range(nc):
            s = s_sc[slot * nc + i]
            if mask is not None:
                s = jnp.where(mask, s, -jnp.inf)
            m_old = m_sc[i]
            m_new = jnp.maximum(m_old, jnp.max(s, axis=0, keepdims=True))
            a = jnp.exp2(m_old - m_new)
            p = jnp.exp2(s - m_new)
            l_sc[i] = a * l_sc[i] + jnp.sum(p, axis=0, keepdims=True)
            m_sc[i] = m_new
            a_sc[i] = a
            p_sc[i] = p.astype(BF16)
        for h in range(hps):
            vt = v_ref[h, pl.ds(off, tq), :]
            for c in range(2):
                i = 2 * h + c
                acc_sc[i] = a_sc[i] * acc_sc[i] + lax.dot_general(vt, p_sc[i], _TN,
                                                                  preferred_element_type=F32)

    if meta_only:
        init_from_meta()
    else:
        qi = pl.program_id(2)
        keys = lax.broadcasted_iota(jnp.int32, (tq, tq), 0)
        queries = lax.broadcasted_iota(jnp.int32, (tq, tq), 1)
        diag = (keys // CHUNK) <= (queries // CHUNK)
        scores(0, 0)
        init_from_meta()

        def body(k, carry):
            scores(2 * k + 1, 1)
            absorb(2 * k, 0, None)
            scores(2 * k + 2, 0)
            absorb(2 * k + 1, 1, None)
            return carry

        lax.fori_loop(0, qi // 2, body, 0)

        @pl.when(qi % 2 == 1)
        def _odd_tail():
            scores(qi, 1)
            absorb(qi - 1, 0, None)
            absorb(qi, 1, diag)

        @pl.when(qi % 2 == 0)
        def _even_tail():
            absorb(qi, 0, diag)

    lam = (jnp.exp(jnp.sum(lq1[...] * lk1[...], axis=1, keepdims=True))
           - jnp.exp(jnp.sum(lq2[...] * lk2[...], axis=1, keepdims=True)) + lam_init)
    for h in range(hps):
        o = acc_sc[2 * h] / l_sc[2 * h] - lam * (acc_sc[2 * h + 1] / l_sc[2 * h + 1])
        ms = jnp.mean(o * o, axis=0, keepdims=True)
        o = o * lax.rsqrt(ms + RMS_EPS) * nw_ref[...] * (1.0 - lam_init)
        o_ref[:, h * HEAD_W:(h + 1) * HEAD_W] = o.T.astype(BF16)


def _attention(p_q, p_kv, p_meta, lams, norm_w, *, batch, seq, tq, hps, lam_init, meta_only):
    n_meta = p_meta.shape[1]
    rows = p_q.shape[1]
    nq = 1 if meta_only else seq // tq
    k_blk, v_blk = ATT_HEADS // hps, 2 * ATT_HEADS // hps
    small = [pl.BlockSpec((1, ATT_QK_DIM), lambda b, g, i: (0, 0))] * 4 + [
        pl.BlockSpec((HEAD_W, 1), lambda b, g, i: (0, 0))]
    meta_specs = [pl.BlockSpec((hps, n_meta, HEAD_W), lambda b, g, i: (k_blk + g, 0, 0)),
                  pl.BlockSpec((hps, n_meta, HEAD_W), lambda b, g, i: (v_blk + g, 0, 0))]
    q_spec = pl.BlockSpec((hps, tq, HEAD_W), lambda b, g, i: (g, b * nq + i, 0))
    if meta_only:
        in_specs = [q_spec] + meta_specs + small
        args = (p_q, p_meta, p_meta) + lams + (norm_w,)
    else:
        kv_specs = [pl.BlockSpec((hps, seq, HEAD_W), lambda b, g, i: (k_blk + g, b, 0)),
                    pl.BlockSpec((hps, seq, HEAD_W), lambda b, g, i: (v_blk + g, b, 0))]
        in_specs = [q_spec] + kv_specs + meta_specs + small
        args = (p_q, p_kv, p_kv, p_meta, p_meta) + lams + (norm_w,)
    scratch = [pltpu.VMEM((2 * hps, 1, tq), F32), pltpu.VMEM((2 * hps, 1, tq), F32),
               pltpu.VMEM((2 * hps, HEAD_W, tq), F32)]
    if not meta_only:
        scratch += [pltpu.VMEM((2 * 2 * hps, tq, tq), F32),
                    pltpu.VMEM((2 * hps, 1, tq), F32),
                    pltpu.VMEM((2 * hps, tq, tq), BF16)]
    return pl.pallas_call(
        functools.partial(_attn_kernel, tq=tq, hps=hps, lam_init=lam_init, meta_only=meta_only),
        out_shape=jax.ShapeDtypeStruct((rows, GROUP_W), BF16),
        grid=(batch, ATT_HEADS // hps, nq),
        in_specs=in_specs,
        out_specs=pl.BlockSpec((tq, hps * HEAD_W), lambda b, g, i: (b * nq + i, g)),
        scratch_shapes=scratch,
        compiler_params=_params("parallel", "parallel", "arbitrary"),
        name="attn_meta" if meta_only else "attn",
    )(*args)


def _gdn_prep_kernel(q_ref, k_ref, v_ref, hq_ref, hk_ref, hv_ref, mh_ref, cw_ref, ba_ref, alog_ref,
                     dtb_ref, u_ref, wq_ref, kd_ref, a_ref, gl_ref, xs_sc, gc_sc, pw_sc, x_sc,
                     *, rows, sub, n_pad):
    first = pl.program_id(1) == 0
    n_sub = rows // sub
    ba = ba_ref[...]
    beta_all = _sigmoid(ba)
    g_all = -jnp.exp(alog_ref[...]) * _softplus(ba + dtb_ref[...])
    row = lax.broadcasted_iota(jnp.int32, (rows, 1), 0)
    if n_pad:
        beta_all = jnp.where(row >= n_pad, beta_all, 0.0)
        g_all = jnp.where(row >= n_pad, g_all, 0.0)

    pos = row % CHUNK
    gc_all = g_all
    shift = 1
    while shift < CHUNK:
        gc_all = gc_all + jnp.where(pos >= shift, pltpu.roll(gc_all, shift, 0), 0.0)
        shift *= 2
    gc_sc[...] = gc_all
    gl_all = jnp.concatenate(
        [jnp.broadcast_to(gc_sc[(c + 1) * CHUNK - 1:(c + 1) * CHUNK, :], (CHUNK, LANES))
         for c in range(rows // CHUNK)], axis=0)
    gl_ref[...] = gl_all
    gct_all = gc_all.T

    ri = lax.broadcasted_iota(jnp.int32, (sub, sub), 0)
    ci = lax.broadcasted_iota(jnp.int32, (sub, sub), 1)
    same = (ri // CHUNK) == (ci // CHUNK)
    tril = jnp.logical_and(same, ci <= ri)
    strict = jnp.logical_and(same, ci < ri)

    def conv_act(main_ref, halo_ref, stream, h):
        halo = jnp.where(first, mh_ref[stream * DN_HEADS + h], halo_ref[h])
        xs_sc[stream, 0:HALO, :] = halo.astype(F32)
        xs_sc[stream, HALO:, :] = main_ref[h].astype(F32)
        w = cw_ref[stream * DN_HEADS + h]
        y = xs_sc[stream, pl.ds(HALO - DN_CONV + 1, rows), :] * w[0:1]
        for j in range(1, DN_CONV):
            y = y + xs_sc[stream, pl.ds(HALO - DN_CONV + 1 + j, rows), :] * w[j:j + 1]
        return _silu(y)

    for h in range(DN_HEADS):
        q = conv_act(q_ref, hq_ref, 0, h)
        k = conv_act(k_ref, hk_ref, 1, h)
        v = conv_act(v_ref, hv_ref, 2, h)
        q = q * lax.rsqrt(jnp.sum(q * q, axis=1, keepdims=True) + RMS_EPS) * (HEAD_W ** -0.5)
        k = k * lax.rsqrt(jnp.sum(k * k, axis=1, keepdims=True) + RMS_EPS)
        beta = beta_all[:, h:h + 1]
        gc = gc_all[:, DN_HEADS + h:DN_HEADS + h + 1]
        gl = gl_all[:, DN_HEADS + h:DN_HEADS + h + 1]
        egc = jnp.exp(gc)
        kb = k * beta
        k16 = k.astype(BF16)
        kb16 = kb.astype(BF16)
        q16 = q.astype(BF16)
        x_all = jnp.concatenate([v * beta, kb * egc], axis=1)
        qd16 = (q * egc).astype(BF16)
        kd_ref[h] = (k * jnp.exp(gl - gc)).astype(BF16)
        for c in range(rows // CHUNK):
            wq_ref[h, (2 * c + 1) * CHUNK:(2 * c + 2) * CHUNK, :] = qd16[c * CHUNK:(c + 1) * CHUNK]
        for s in range(n_sub):
            lo, hi = s * sub, (s + 1) * sub
            gcr = gct_all[DN_HEADS + h:DN_HEADS + h + 1, lo:hi]
            kk = lax.dot_general(kb16[lo:hi], k16[lo:hi], _NT, preferred_element_type=F32)
            qk = lax.dot_general(q16[lo:hi], k16[lo:hi], _NT, preferred_element_type=F32)
            dec = jnp.exp(jnp.where(tril, gc[lo:hi] - gcr, -jnp.inf))
            a_ref[h, lo:hi, :] = (qk * dec).astype(BF16)
            pw_sc[h * n_sub + s] = jnp.where(strict, kk * dec, 0.0).astype(BF16)
            x_sc[h * n_sub + s] = x_all[lo:hi]

    items = range(DN_HEADS * n_sub)
    for it in items:
        x = x_sc[it]
        x_sc[it] = x - jnp.dot(pw_sc[it], x.astype(BF16), preferred_element_type=F32)
    for _ in range(int(math.log2(CHUNK)) - 1):
        for it in items:
            p16 = pw_sc[it]
            pw_sc[it] = jnp.dot(p16, p16, preferred_element_type=F32).astype(BF16)
        for it in items:
            x = x_sc[it]
            x_sc[it] = x + jnp.dot(pw_sc[it], x.astype(BF16), preferred_element_type=F32)
    for it in items:
        h, lo = it // n_sub, (it % n_sub) * sub
        x = x_sc[it]
        u_ref[h, lo:lo + sub, :] = x[:, :HEAD_W].astype(BF16)
        w16 = x[:, HEAD_W:].astype(BF16)
        for c in range(sub // CHUNK):
            r0 = lo + c * CHUNK
            wq_ref[h, 2 * r0:2 * r0 + CHUNK, :] = w16[c * CHUNK:(c + 1) * CHUNK]


def _gdn_prep(p, p_halo_src, meta_halo, conv_w, ba, alog, dtb, *, batch, seq, rows, sub, n_pad):
    m = batch * seq
    nt = seq // rows
    g_q, g_k, g_v = 3, 4, 5
    hb = rows // HALO

    def main_spec(g):
        return pl.BlockSpec((DN_HEADS, rows, HEAD_W), lambda b, t: (g, b * nt + t, 0))

    def halo_spec(g):
        return pl.BlockSpec((DN_HEADS, HALO, HEAD_W),
                            lambda b, t: (g, jnp.maximum((b * nt + t) * hb - 1, 0), 0))

    full = lambda shape: pl.BlockSpec(shape, lambda b, t: (0,) * len(shape))
    out_shape = (jax.ShapeDtypeStruct((DN_HEADS, m, HEAD_W), BF16),
                 jax.ShapeDtypeStruct((DN_HEADS, 2 * m, HEAD_W), BF16),
                 jax.ShapeDtypeStruct((DN_HEADS, m, HEAD_W), BF16),
                 jax.ShapeDtypeStruct((DN_HEADS, m, sub), BF16),
                 jax.ShapeDtypeStruct((m, LANES), F32))
    out_specs = (pl.BlockSpec((DN_HEADS, rows, HEAD_W), lambda b, t: (0, b * nt + t, 0)),
                 pl.BlockSpec((DN_HEADS, 2 * rows, HEAD_W), lambda b, t: (0, b * nt + t, 0)),
                 pl.BlockSpec((DN_HEADS, rows, HEAD_W), lambda b, t: (0, b * nt + t, 0)),
                 pl.BlockSpec((DN_HEADS, rows, sub), lambda b, t: (0, b * nt + t, 0)),
                 pl.BlockSpec((rows, LANES), lambda b, t: (b * nt + t, 0)))
    return pl.pallas_call(
        functools.partial(_gdn_prep_kernel, rows=rows, sub=sub, n_pad=n_pad),
        out_shape=out_shape,
        grid=(batch, nt),
        in_specs=[main_spec(g_q), main_spec(g_k), main_spec(g_v),
                  halo_spec(g_q), halo_spec(g_k), halo_spec(g_v),
                  full(meta_halo.shape), full(conv_w.shape),
                  pl.BlockSpec((rows, LANES), lambda b, t: (b * nt + t, 0)),
                  full(alog.shape), full(dtb.shape)],
        out_specs=out_specs,
        scratch_shapes=[pltpu.VMEM((3, rows + HALO, HEAD_W), F32), pltpu.VMEM((rows, LANES), F32),
                        pltpu.VMEM((DN_HEADS * (rows // sub), sub, sub), BF16),
                        pltpu.VMEM((DN_HEADS * (rows // sub), sub, 2 * HEAD_W), F32)],
        compiler_params=_params("parallel", "parallel"),
        name="gdn_prep",
    )(p, p, p, p_halo_src, p_halo_src, p_halo_src, meta_halo, conv_w, ba, alog, dtb)


def _gdn_scan_kernel(u_ref, wq_ref, kd_ref, a_ref, gl_ref, z_ref,
                     um_ref, wqm_ref, kdm_ref, am_ref, glm_ref, zm_ref, nw_ref,
                     o_ref, om_ref, s_sc, vbuf_sc, sw_sc, *, rows, sub, hpg):
    hg = pl.program_id(1)
    t = pl.program_id(2)
    lane = lax.broadcasted_iota(jnp.int32, (1, LANES), 1)
    nw = nw_ref[...]

    def decay_total(gl_row, h):
        sel = lane == DN_HEADS + hg * hpg + h
        return jnp.sum(jnp.where(sel, jnp.exp(gl_row), 0.0), axis=1, keepdims=True)

    def gate(o, z):
        ms = jnp.mean(o * o, axis=1, keepdims=True)
        return (o * lax.rsqrt(ms + RMS_EPS) * nw * _silu(z.astype(F32))).astype(BF16)

    def run_chunk(r0, refs, out_ref, out_r0):
        u_r, wq_r, kd_r, a_r, gl_r, z_r = refs
        slot = r0 % sub
        gl_row = gl_r[r0:r0 + 1, :]
        for h in range(hpg):
            sw_sc[h] = jnp.dot(wq_r[h, 2 * r0:2 * r0 + 2 * CHUNK, :], s_sc[h].astype(BF16),
                               preferred_element_type=F32)
        for h in range(hpg):
            v_new = u_r[h, r0:r0 + CHUNK, :].astype(F32) - sw_sc[h, 0:CHUNK, :]
            vbuf_sc[h, slot:slot + CHUNK, :] = v_new.astype(BF16)
        for h in range(hpg):
            o = sw_sc[h, CHUNK:, :] + jnp.dot(a_r[h, r0:r0 + CHUNK, :], vbuf_sc[h],
                                              preferred_element_type=F32)
            s_sc[h] = decay_total(gl_row, h) * s_sc[h] + lax.dot_general(
                kd_r[h, r0:r0 + CHUNK, :], vbuf_sc[h, slot:slot + CHUNK, :], _TN,
                preferred_element_type=F32)
            out_ref[out_r0:out_r0 + CHUNK, h * HEAD_W:(h + 1) * HEAD_W] = gate(
                o, z_r[h, r0:r0 + CHUNK, :])

    @pl.when(t == 0)
    def _meta_chunk():
        vbuf_sc[...] = jnp.zeros_like(vbuf_sc)
        s_sc[...] = jnp.zeros_like(s_sc)
        run_chunk(sub - CHUNK, (um_ref, wqm_ref, kdm_ref, am_ref, glm_ref, zm_ref), om_ref, 0)

    for c in range(rows // CHUNK):
        run_chunk(c * CHUNK, (u_ref, wq_ref, kd_ref, a_ref, gl_ref, z_ref), o_ref, c * CHUNK)


def _gdn_scan(real, meta, p, p_meta_pad, norm_w, *, batch, seq, rows, sub, hpg):
    u, wq, kd, a, gl = real
    um, wqm, kdm, am, glm = meta
    m = batch * seq
    nt = seq // rows
    ngrp = DN_HEADS // hpg
    z_blk = (6 * DN_HEADS) // hpg
    mrows = um.shape[1]
    assert mrows == sub

    def real_spec(r, w):
        return pl.BlockSpec((hpg, r, w), lambda b, g, t: (g, b * nt + t, 0))

    def meta_spec(r, w):
        return pl.BlockSpec((hpg, r, w), lambda b, g, t: (g, 0, 0))

    return pl.pallas_call(
        functools.partial(_gdn_scan_kernel, rows=rows, sub=sub, hpg=hpg),
        out_shape=(jax.ShapeDtypeStruct((m, GROUP_W), BF16),
                   jax.ShapeDtypeStruct((batch, CHUNK, GROUP_W), BF16)),
        grid=(batch, ngrp, nt),
        in_specs=[real_spec(rows, HEAD_W), real_spec(2 * rows, HEAD_W), real_spec(rows, HEAD_W),
                  real_spec(rows, sub),
                  pl.BlockSpec((rows, LANES), lambda b, g, t: (b * nt + t, 0)),
                  pl.BlockSpec((hpg, rows, HEAD_W), lambda b, g, t: (z_blk + g, b * nt + t, 0)),
                  meta_spec(mrows, HEAD_W), meta_spec(2 * mrows, HEAD_W), meta_spec(mrows, HEAD_W),
                  meta_spec(mrows, mrows),
                  pl.BlockSpec((mrows, LANES), lambda b, g, t: (0, 0)),
                  pl.BlockSpec((hpg, mrows, HEAD_W), lambda b, g, t: (z_blk + g, 0, 0)),
                  pl.BlockSpec((1, HEAD_W), lambda b, g, t: (0, 0))],
        out_specs=(pl.BlockSpec((rows, hpg * HEAD_W), lambda b, g, t: (b * nt + t, g)),
                   pl.BlockSpec((None, CHUNK, hpg * HEAD_W), lambda b, g, t: (b, 0, g))),
        scratch_shapes=[pltpu.VMEM((hpg, HEAD_W, HEAD_W), F32),
                        pltpu.VMEM((hpg, sub, HEAD_W), BF16),
                        pltpu.VMEM((hpg, 2 * CHUNK, HEAD_W), F32)],
        compiler_params=_params("parallel", "parallel", "arbitrary"),
        name="gdn_scan",
    )(u, wq, kd, a, gl, p, um, wqm, kdm, am, glm, p_meta_pad, norm_w)


def _wout_ln_kernel(oa_ref, od_ref, w_ref, x_ref, g_ref, b_ref, h_ref, *, alpha):
    half = oa_ref.shape[1]
    mix = (jnp.dot(oa_ref[...], w_ref[0:half, :], preferred_element_type=F32)
           + jnp.dot(od_ref[...], w_ref[half:, :], preferred_element_type=F32))
    h_ref[...] = _layer_norm(alpha * x_ref[...] + mix, g_ref[...], b_ref[...])


def _wout_ln(o_att, o_dn, w_out, x2d, g, b, *, bm, alpha):
    m, d = x2d.shape
    half = o_att.shape[1]
    return pl.pallas_call(
        functools.partial(_wout_ln_kernel, alpha=alpha),
        out_shape=jax.ShapeDtypeStruct((m, d), F32),
        grid=(m // bm,),
        in_specs=[pl.BlockSpec((bm, half), lambda i: (i, 0)),
                  pl.BlockSpec((bm, half), lambda i: (i, 0)),
                  pl.BlockSpec((2 * half, d), lambda i: (0, 0)),
                  pl.BlockSpec((bm, d), lambda i: (i, 0)),
                  pl.BlockSpec((1, d), lambda i: (0, 0)),
                  pl.BlockSpec((1, d), lambda i: (0, 0))],
        out_specs=pl.BlockSpec((bm, d), lambda i: (i, 0)),
        compiler_params=_params("parallel"),
        name="wout_ln",
    )(o_att, o_dn, w_out, x2d, g, b)


def _ffn_kernel(h_ref, hprev_ref, hmeta_ref, wg_ref, wu_ref, cw_ref, cb_ref, wd_ref, g_ref, b_ref,
                o_ref, xb_sc, gate_sc, *, bm, tiles_per_batch, alpha):
    i = pl.program_id(0)
    f = pl.program_id(1)

    @pl.when(f == 0)
    def _init():
        halo = jnp.where(i % tiles_per_batch == 0, hmeta_ref[...], hprev_ref[...])
        xb_sc[0:HALO, :] = halo.astype(BF16)
        xb_sc[HALO:, :] = h_ref[...].astype(BF16)
        o_ref[...] = jnp.zeros_like(o_ref)

    gate_sc[...] = jnp.dot(xb_sc[...], wg_ref[...], preferred_element_type=F32)
    up = jnp.dot(xb_sc[HALO:, :], wu_ref[...], preferred_element_type=F32)
    cw = cw_ref[...]
    hg = cb_ref[...] + gate_sc[pl.ds(HALO - FFN_CONV + 1, bm), :] * cw[0:1]
    for j in range(1, FFN_CONV):
        hg = hg + gate_sc[pl.ds(HALO - FFN_CONV + 1 + j, bm), :] * cw[j:j + 1]
    act = (_silu(hg) * up).astype(BF16)
    o_ref[...] += jnp.dot(act, wd_ref[...], preferred_element_type=F32)

    @pl.when(f == pl.num_programs(1) - 1)
    def _finish():
        o_ref[...] = _layer_norm(alpha * h_ref[...] + o_ref[...], g_ref[...], b_ref[...])


def _ffn(h1, h1_meta, w_gate, w_up, conv_w, conv_b, w_down, g, b, *, seq, bm, tf, alpha):
    m, d = h1.shape
    dff = w_gate.shape[1]
    tiles_per_batch = seq // bm
    hb = bm // HALO
    meta_blk = h1_meta.shape[0] // HALO - 1
    return pl.pallas_call(
        functools.partial(_ffn_kernel, bm=bm, tiles_per_batch=tiles_per_batch, alpha=alpha),
        out_shape=jax.ShapeDtypeStruct((m, d), F32),
        grid=(m // bm, dff // tf),
        in_specs=[pl.BlockSpec((bm, d), lambda i, f: (i, 0)),
                  pl.BlockSpec((HALO, d), lambda i, f: (jnp.maximum(i * hb - 1, 0), 0)),
                  pl.BlockSpec((HALO, d), lambda i, f: (meta_blk, 0)),
                  pl.BlockSpec((d, tf), lambda i, f: (0, f)),
                  pl.BlockSpec((d, tf), lambda i, f: (0, f)),
                  pl.BlockSpec((FFN_CONV, tf), lambda i, f: (0, f)),
                  pl.BlockSpec((1, tf), lambda i, f: (0, f)),
                  pl.BlockSpec((tf, d), lambda i, f: (f, 0)),
                  pl.BlockSpec((1, d), lambda i, f: (0, 0)),
                  pl.BlockSpec((1, d), lambda i, f: (0, 0))],
        out_specs=pl.BlockSpec((bm, d), lambda i, f: (i, 0)),
        scratch_shapes=[pltpu.VMEM((bm + HALO, d), BF16),
                        pltpu.VMEM((bm + HALO, tf), F32)],
        compiler_params=_params("parallel", "arbitrary"),
        name="ffn",
    )(h1, h1, h1_meta, w_gate, w_up, conv_w, conv_b, w_down, g, b)


def _largest_tile(n, cap, mult):
    best = None
    for c in range(mult, min(n, cap) + 1, mult):
        if n % c == 0:
            best = c
    assert best is not None, (n, cap, mult)
    return best


def _split_halves(w_qk_t):
    n_comp, quarter = HEAD_W // ATT_QK_DIM, ATT_QK_DIM // 2
    depth, n, d = w_qk_t.shape
    w = w_qk_t.reshape(depth, -1, n_comp, 2, quarter, d)
    return jnp.swapaxes(w, 2, 3).reshape(depth, n, d)


def _rope_tables(n_pos):
    n_comp = HEAD_W // ATT_QK_DIM
    inv_freq = ROPE_THETA ** (-jnp.arange(0, ATT_QK_DIM, 2, dtype=F32) / ATT_QK_DIM)
    ang = jnp.arange(n_pos, dtype=F32)[:, None] * inv_freq[None, :]
    cos, sin = jnp.cos(ang), jnp.sin(ang)
    cos_t = jnp.tile(cos, (1, 2 * n_comp))
    sin_t = jnp.concatenate([jnp.tile(-sin, (1, n_comp)), jnp.tile(sin, (1, n_comp))], axis=1)
    assert cos_t.shape[1] == HEAD_W
    return cos_t, sin_t


def kernel(x, meta_tokens, w_in, conv_qkv_w, a_log, dt_bias, lambda_q1, lambda_k1, lambda_q2, lambda_k2,
           diff_norm_w, delta_norm_w, w_out, ln1_g, ln1_b, ffn_w_gate, ffn_w_up, ffn_conv_w, ffn_conv_b,
           ffn_w_down, ln2_g, ln2_b):
    batch, seq, d = x.shape
    n_meta = meta_tokens.shape[0]
    depth = w_in.shape[0]
    n_main = N_GROUPS * GROUP_W
    assert w_in.shape[2] == n_main + TAIL_COLS and n_meta % HALO == 0 and n_meta <= CHUNK
    m = batch * seq
    alpha = (2.0 * depth) ** 0.25

    bm_proj = _largest_tile(seq, 1024, HALO)
    bm_row = _largest_tile(seq, 512, HALO)
    tq = _largest_tile(seq, 256, CHUNK)
    rows = _largest_tile(seq, 256, LANES)
    sub = LANES
    hpg = DN_HEADS
    att_hps = ATT_HEADS

    cos_t, sin_t = _rope_tables(n_meta + seq)
    cos_meta, sin_meta = cos_t[:n_meta], sin_t[:n_meta]
    cos_real, sin_real = cos_t[n_meta:], sin_t[n_meta:]

    h = x.reshape(m, d)
    h_meta = meta_tokens.astype(x.dtype)
    n_qk = 2 * GROUP_W
    w_in_t = jnp.swapaxes(w_in, 1, 2)
    w_t = w_in_t.astype(BF16)
    wqk_t = _split_halves(w_t[:, :n_qk])
    pad_rows = sub - n_meta
    for l in range(depth):
        w_tail_t = jnp.pad(w_t[l, n_main:], ((0, LANES - TAIL_COLS), (0, 0)))
        lane_pad = (DN_HEADS, LANES - TAIL_COLS)
        alog = jnp.pad(a_log[l].astype(F32), lane_pad)[None]
        dtb = jnp.pad(dt_bias[l].astype(F32), lane_pad)[None]
        conv_w = conv_qkv_w[l].reshape(DN_CONV, 3 * DN_HEADS, HEAD_W).transpose(1, 0, 2)
        lams = tuple(v[l][None].astype(F32) for v in (lambda_q1, lambda_k1, lambda_q2, lambda_k2))
        lam_init = 0.8 - 0.6 * math.exp(-0.3 * l)

        p, p_meta, ba, ba_meta = _proj(h, h_meta, wqk_t, w_t, l, w_tail_t, cos_real, sin_real,
                                       cos_meta, sin_meta, bm_proj)

        att_nw = diff_norm_w[l].astype(F32)[:, None]
        o_att = _attention(p, p, p_meta, lams, att_nw, batch=batch, seq=seq, tq=tq, hps=att_hps,
                           lam_init=lam_init, meta_only=False)
        q_meta = jnp.pad(p_meta[:ATT_HEADS], ((0, 0), (0, LANES - n_meta), (0, 0)))
        o_att_meta = _attention(q_meta, None, p_meta, lams, att_nw, batch=1, seq=LANES, tq=LANES,
                                hps=att_hps, lam_init=lam_init, meta_only=True)[:n_meta]

        p_meta_pad = jnp.pad(p_meta, ((0, 0), (pad_rows, 0), (0, 0)))
        ba_meta_pad = jnp.pad(ba_meta, ((pad_rows, 0), (0, 0)))
        meta_halo = p_meta[3 * DN_HEADS:6 * DN_HEADS, n_meta - HALO:, :]
        zero_halo = jnp.zeros_like(meta_halo)
        prep_real = _gdn_prep(p, p, meta_halo, conv_w, ba, alog, dtb, batch=batch, seq=seq, rows=rows,
                              sub=sub, n_pad=0)
        prep_meta = _gdn_prep(p_meta_pad, p_meta_pad, zero_halo, conv_w, ba_meta_pad, alog, dtb, batch=1,
                              seq=sub, rows=sub, sub=sub, n_pad=pad_rows)
        o_dn, o_dn_meta = _gdn_scan(prep_real, prep_meta, p, p_meta_pad, delta_norm_w[l][None],
                                    batch=batch, seq=seq, rows=rows, sub=sub, hpg=hpg)
        o_dn_meta = o_dn_meta[0, CHUNK - n_meta:]

        w_out_b = w_out[l].astype(BF16)
        g1, b1 = ln1_g[l][None].astype(F32), ln1_b[l][None].astype(F32)
        h1 = _wout_ln(o_att, o_dn, w_out_b, h, g1, b1, bm=bm_row, alpha=alpha)
        h1_meta = _wout_ln(o_att_meta, o_dn_meta, w_out_b, h_meta, g1, b1, bm=n_meta, alpha=alpha)

        dff = ffn_w_gate.shape[2]
        tf = _largest_tile(dff, 512, LANES)
        ffn_args = (ffn_w_gate[l].astype(BF16), ffn_w_up[l].astype(BF16), ffn_conv_w[l].astype(F32),
                    ffn_conv_b[l][None].astype(F32), ffn_w_down[l].astype(BF16),
                    ln2_g[l][None].astype(F32), ln2_b[l][None].astype(F32))
        h_next = _ffn(h1, h1_meta, *ffn_args, seq=seq, bm=bm_row, tf=tf, alpha=alpha)
        if l + 1 < depth:
            zero_meta = jnp.zeros_like(h1_meta)
            h_meta = _ffn(h1_meta, zero_meta, *ffn_args, seq=n_meta, bm=n_meta, tf=tf, alpha=alpha)
        h = h_next
    return h.reshape(batch, seq, d)
```

```python
import functools
import math

import jax
import jax.numpy as jnp
from jax import lax
from jax.experimental import pallas as pl
from jax.experimental.pallas import tpu as pltpu

F32 = jnp.float32
BF16 = jnp.bfloat16

CHUNK = 64
ATT_HEADS = 8
ATT_QK_DIM = 64
HEAD_W = 128
DN_HEADS = 8
DN_CONV = 4
FFN_CONV = 3
ROPE_THETA = 10000.0
LN_EPS = 1e-5
RMS_EPS = 1e-6
GROUP_W = ATT_HEADS * HEAD_W
Q_SCALE = ATT_QK_DIM ** -0.5 * math.log2(math.e)
N_GROUPS = 7
TAIL_COLS = 2 * DN_HEADS

LANES = 128
HALO = 8
VMEM_LIMIT = 56 * 1024 * 1024

_NT = (((1,), (1,)), ((), ()))
_TN = (((0,), (0,)), ((), ()))


def _params(*sem):
    return pltpu.CompilerParams(dimension_semantics=sem, vmem_limit_bytes=VMEM_LIMIT)


def _sigmoid(x):
    return 1.0 / (1.0 + jnp.exp(-x))


def _silu(x):
    return x * _sigmoid(x)


def _softplus(x):
    return jnp.maximum(x, 0.0) + jnp.log(1.0 + jnp.exp(-jnp.abs(x)))


def _layer_norm(y, g, b):
    mu = jnp.mean(y, axis=1, keepdims=True)
    d = y - mu
    var = jnp.mean(d * d, axis=1, keepdims=True)
    return d * lax.rsqrt(var + LN_EPS) * g + b


def _proj_kernel(x_ref, xm_ref, wqk_ref, w_ref, wt_ref, cos_ref, sin_ref, cosm_ref, sinm_ref,
                 p_ref, pm_ref, ba_ref, bam_ref, *, heads, n_qk_tiles):
    j = pl.program_id(0)
    i = pl.program_id(1)

    def tail(x_r):
        return lax.dot_general(x_r[...].astype(BF16), wt_ref[...], _NT, preferred_element_type=F32)

    def project(xh, wsel_ref, rope_tables, out_ref):
        acc = lax.dot_general(xh, wsel_ref[...], _NT, preferred_element_type=F32)
        if rope_tables is not None:
            scale = jnp.where(j == 0, Q_SCALE, 1.0).astype(F32)
            cos = rope_tables[0][...] * scale
            sin = rope_tables[1][...] * scale
        for h in range(heads):
            blk = acc[:, h * HEAD_W:(h + 1) * HEAD_W]
            if rope_tables is not None:
                blk = blk * cos + pltpu.roll(blk, HEAD_W // 2, 1) * sin
            out_ref[h] = blk.astype(BF16)

    def tile(wsel_ref, rope):
        @pl.when(i == 0)
        def _meta_rows():
            project(xm_ref[...].astype(BF16), wsel_ref, (cosm_ref, sinm_ref) if rope else None, pm_ref)

        project(x_ref[...].astype(BF16), wsel_ref, (cos_ref, sin_ref) if rope else None, p_ref)

    @pl.when(j < n_qk_tiles)
    def _qk_tile():
        tile(wqk_ref, True)

    @pl.when(j >= n_qk_tiles)
    def _plain_tile():
        tile(w_ref, False)

    @pl.when(j == 0)
    def _tail():
        ba_ref[...] = tail(x_ref)

        @pl.when(i == 0)
        def _meta_tail():
            bam_ref[...] = tail(xm_ref)


def _proj(x2d, x_meta, wqk_t, w_t, layer, w_tail_t, cos, sin, cos_meta, sin_meta, bm):
    m, d = x2d.shape
    n_meta = x_meta.shape[0]
    heads = GROUP_W // HEAD_W
    n_blocks = N_GROUPS * heads
    nm = m // bm
    n_qk_tiles = wqk_t.shape[1] // GROUP_W
    const = lambda shape: pl.BlockSpec(shape, lambda j, i: (0,) * len(shape))
    row = lambda j, i: (i, 0)
    tiles_per_batch = cos.shape[0] // bm
    pos = lambda j, i: (i % tiles_per_batch, 0)
    return pl.pallas_call(
        functools.partial(_proj_kernel, heads=heads, n_qk_tiles=n_qk_tiles),
        out_shape=(jax.ShapeDtypeStruct((n_blocks, m, HEAD_W), BF16),
                   jax.ShapeDtypeStruct((n_blocks, n_meta, HEAD_W), BF16),
                   jax.ShapeDtypeStruct((m, LANES), F32),
                   jax.ShapeDtypeStruct((n_meta, LANES), F32)),
        grid=(N_GROUPS, nm),
        in_specs=[
            pl.BlockSpec((bm, d), row),
            const((n_meta, d)),
            pl.BlockSpec((None, GROUP_W, d), lambda j, i: (layer, jnp.minimum(j, n_qk_tiles - 1), 0)),
            pl.BlockSpec((None, GROUP_W, d), lambda j, i: (layer, jnp.maximum(j, n_qk_tiles), 0)),
            const((LANES, d)),
            pl.BlockSpec((bm, HEAD_W), pos),
            pl.BlockSpec((bm, HEAD_W), pos),
            const((n_meta, HEAD_W)),
            const((n_meta, HEAD_W)),
        ],
        out_specs=(
            pl.BlockSpec((heads, bm, HEAD_W), lambda j, i: (j, i, 0)),
            pl.BlockSpec((heads, n_meta, HEAD_W), lambda j, i: (j, 0, 0)),
            pl.BlockSpec((bm, LANES), lambda j, i: (jnp.where(j == 0, i, nm - 1), 0)),
            const((n_meta, LANES)),
        ),
        compiler_params=_params("arbitrary", "arbitrary"),
        name="proj",
    )(x2d, x_meta, wqk_t, w_t, w_tail_t, cos, sin, cos_meta, sin_meta)


def _attn_kernel(*refs, tq, hps, lam_init, meta_only):
    if meta_only:
        q_ref, km_ref, vm_ref, lq1, lk1, lq2, lk2, nw_ref, o_ref, m_sc, l_sc, acc_sc = refs
    else:
        (q_ref, k_ref, v_ref, km_ref, vm_ref, lq1, lk1, lq2, lk2, nw_ref, o_ref,
         m_sc, l_sc, acc_sc, s_sc, a_sc, p_sc) = refs
    lane = lax.broadcasted_iota(jnp.int32, (1, HEAD_W), 1)
    comp = (lane // (ATT_QK_DIM // 2)) % 2
    qc = []
    for h in range(hps):
        q = q_ref[h]
        zero = jnp.zeros_like(q)
        qc.append((jnp.where(comp == 0, q, zero), jnp.where(comp == 1, q, zero)))

    nc = 2 * hps

    def init_from_meta():
        s_meta = [lax.dot_general(km_ref[i // 2], qc[i // 2][i % 2], _NT, preferred_element_type=F32)
                  for i in range(nc)]
        p_meta = []
        for i in range(nc):
            m0 = jnp.max(s_meta[i], axis=0, keepdims=True)
            p = jnp.exp2(s_meta[i] - m0)
            m_sc[i] = m0
            l_sc[i] = jnp.sum(p, axis=0, keepdims=True)
            p_meta.append(p.astype(BF16))
        for i in range(nc):
            acc_sc[i] = lax.dot_general(vm_ref[i // 2], p_meta[i], _TN, preferred_element_type=F32)

    def scores(t, slot):
        off = pl.multiple_of(t * tq, tq)
        for h in range(hps):
            kt = k_ref[h, pl.ds(off, tq), :]
            for c in range(2):
                s_sc[slot * nc + 2 * h + c] = lax.dot_general(kt, qc[h][c], _NT,
                                                              preferred_element_type=F32)

    def absorb(t, slot, mask):
        off = pl.multiple_of(t * tq, tq)
        for i in range(nc):
            s = s_sc[slot * nc + i]
            if mask is not None:
                s = jnp.where(mask, s, -jnp.inf)
            m_old = m_sc[i]
            m_new = jnp.maximum(m_old, jnp.max(s, axis=0, keepdims=True))
            a = jnp.exp2(m_old - m_new)
            p = jnp.exp2(s - m_new)
            l_sc[i] = a * l_sc[i] + jnp.sum(p, axis=0, keepdims=True)
            m_sc[i] = m_new
            a_sc[i] = a
            p_sc[i] = p.astype(BF16)
        for h in range(hps):
            vt = v_ref[h, pl.ds(off, tq), :]
            for c in range(2):
                i = 2 * h + c
                acc_sc[i] = a_sc[i] * acc_sc[i] + lax.dot_general(vt, p_sc[i], _TN,
                                                                  preferred_element_type=F32)

    if meta_only:
        init_from_meta()
    else:
        qi = pl.program_id(2)
        keys = lax.broadcasted_iota(jnp.int32, (tq, tq), 0)
        queries = lax.broadcasted_iota(jnp.int32, (tq, tq), 1)
        diag = (keys // CHUNK) <= (queries // CHUNK)
        scores(0, 0)
        init_from_meta()

        def body(k, carry):
            scores(2 * k + 1, 1)
            absorb(2 * k, 0, None)
            scores(2 * k + 2, 0)
            absorb(2 * k + 1, 1, None)
            return carry

        lax.fori_loop(0, qi // 2, body, 0)

        @pl.when(qi % 2 == 1)
        def _odd_tail():
            scores(qi, 1)
            absorb(qi - 1, 0, None)
            absorb(qi, 1, diag)

        @pl.when(qi % 2 == 0)
        def _even_tail():
            absorb(qi, 0, diag)

    lam = (jnp.exp(jnp.sum(lq1[...] * lk1[...], axis=1, keepdims=True))
           - jnp.exp(jnp.sum(lq2[...] * lk2[...], axis=1, keepdims=True)) + lam_init)
    for h in range(hps):
        o = acc_sc[2 * h] / l_sc[2 * h] - lam * (acc_sc[2 * h + 1] / l_sc[2 * h + 1])
        ms = jnp.mean(o * o, axis=0, keepdims=True)
        o = o * lax.rsqrt(ms + RMS_EPS) * nw_ref[...] * (1.0 - lam_init)
        o_ref[:, h * HEAD_W:(h + 1) * HEAD_W] = o.T.astype(BF16)


def _attention(p_q, p_kv, p_meta, lams, norm_w, *, batch, seq, tq, hps, lam_init, meta_only):
    n_meta = p_meta.shape[1]
    rows = p_q.shape[1]
    nq = 1 if meta_only else seq // tq
    k_blk, v_blk = ATT_HEADS // hps, 2 * ATT_HEADS // hps
    small = [pl.BlockSpec((1, ATT_QK_DIM), lambda b, g, i: (0, 0))] * 4 + [
        pl.BlockSpec((HEAD_W, 1), lambda b, g, i: (0, 0))]
    meta_specs = [pl.BlockSpec((hps, n_meta, HEAD_W), lambda b, g, i: (k_blk + g, 0, 0)),
                  pl.BlockSpec((hps, n_meta, HEAD_W), lambda b, g, i: (v_blk + g, 0, 0))]
    q_spec = pl.BlockSpec((hps, tq, HEAD_W), lambda b, g, i: (g, b * nq + i, 0))
    if meta_only:
        in_specs = [q_spec] + meta_specs + small
        args = (p_q, p_meta, p_meta) + lams + (norm_w,)
    else:
        kv_specs = [pl.BlockSpec((hps, seq, HEAD_W), lambda b, g, i: (k_blk + g, b, 0)),
                    pl.BlockSpec((hps, seq, HEAD_W), lambda b, g, i: (v_blk + g, b, 0))]
        in_specs = [q_spec] + kv_specs + meta_specs + small
        args = (p_q, p_kv, p_kv, p_meta, p_meta) + lams + (norm_w,)
    scratch = [pltpu.VMEM((2 * hps, 1, tq), F32), pltpu.VMEM((2 * hps, 1, tq), F32),
               pltpu.VMEM((2 * hps, HEAD_W, tq), F32)]
    if not meta_only:
        scratch += [pltpu.VMEM((2 * 2 * hps, tq, tq), F32),
                    pltpu.VMEM((2 * hps, 1, tq), F32),
                    pltpu.VMEM((2 * hps, tq, tq), BF16)]
    return pl.pallas_call(
        functools.partial(_attn_kernel, tq=tq, hps=hps, lam_init=lam_init, meta_only=meta_only),
        out_shape=jax.ShapeDtypeStruct((rows, GROUP_W), BF16),
        grid=(batch, ATT_HEADS // hps, nq),
        in_specs=in_specs,
        out_specs=pl.BlockSpec((tq, hps * HEAD_W), lambda b, g, i: (b * nq + i, g)),
        scratch_shapes=scratch,
        compiler_params=_params("parallel", "parallel", "arbitrary"),
        name="attn_meta" if meta_only else "attn",
    )(*args)


def _gdn_prep_kernel(*refs, rows, sub, n_pad, n_cast):
    (q_ref, k_ref, v_ref, hq_ref, hk_ref, hv_ref, mh_ref, cw_ref, ba_ref, alog_ref, dtb_ref) = refs[:11]
    cast_src = refs[11:11 + n_cast]
    u_ref, wq_ref, kd_ref, a_ref, gl_ref = refs[11 + n_cast:16 + n_cast]
    cast_dst = refs[16 + n_cast:16 + 2 * n_cast]
    xs_sc, gc_sc, pw_sc, x_sc = refs[16 + 2 * n_cast:]
    for src, dst in zip(cast_src, cast_dst):
        dst[...] = src[...].astype(BF16)

    first = pl.program_id(1) == 0
    n_sub = rows // sub
    ba = ba_ref[...]
    beta_all = _sigmoid(ba)
    g_all = -jnp.exp(alog_ref[...]) * _softplus(ba + dtb_ref[...])
    row = lax.broadcasted_iota(jnp.int32, (rows, 1), 0)
    if n_pad:
        beta_all = jnp.where(row >= n_pad, beta_all, 0.0)
        g_all = jnp.where(row >= n_pad, g_all, 0.0)

    pos = row % CHUNK
    gc_all = g_all
    shift = 1
    while shift < CHUNK:
        gc_all = gc_all + jnp.where(pos >= shift, pltpu.roll(gc_all, shift, 0), 0.0)
        shift *= 2
    gc_sc[...] = gc_all
    gl_all = jnp.concatenate(
        [jnp.broadcast_to(gc_sc[(c + 1) * CHUNK - 1:(c + 1) * CHUNK, :], (CHUNK, LANES))
         for c in range(rows // CHUNK)], axis=0)
    gl_ref[...] = gl_all
    gct_all = gc_all.T

    ri = lax.broadcasted_iota(jnp.int32, (sub, sub), 0)
    ci = lax.broadcasted_iota(jnp.int32, (sub, sub), 1)
    same = (ri // CHUNK) == (ci // CHUNK)
    tril = jnp.logical_and(same, ci <= ri)
    strict = jnp.logical_and(same, ci < ri)

    def conv_act(main_ref, halo_ref, stream, h):
        halo = jnp.where(first, mh_ref[stream * DN_HEADS + h], halo_ref[h])
        xs_sc[stream, 0:HALO, :] = halo.astype(F32)
        xs_sc[stream, HALO:, :] = main_ref[h].astype(F32)
        w = cw_ref[stream * DN_HEADS + h]
        y = xs_sc[stream, pl.ds(HALO - DN_CONV + 1, rows), :] * w[0:1]
        for j in range(1, DN_CONV):
            y = y + xs_sc[stream, pl.ds(HALO - DN_CONV + 1 + j, rows), :] * w[j:j + 1]
        return _silu(y)

    for h in range(DN_HEADS):
        q = conv_act(q_ref, hq_ref, 0, h)
        k = conv_act(k_ref, hk_ref, 1, h)
        v = conv_act(v_ref, hv_ref, 2, h)
        q = q * lax.rsqrt(jnp.sum(q * q, axis=1, keepdims=True) + RMS_EPS) * (HEAD_W ** -0.5)
        k = k * lax.rsqrt(jnp.sum(k * k, axis=1, keepdims=True) + RMS_EPS)
        beta = beta_all[:, h:h + 1]
        gc = gc_all[:, DN_HEADS + h:DN_HEADS + h + 1]
        gl = gl_all[:, DN_HEADS + h:DN_HEADS + h + 1]
        egc = jnp.exp(gc)
        kb = k * beta
        k16 = k.astype(BF16)
        kb16 = kb.astype(BF16)
        q16 = q.astype(BF16)
        x_all = jnp.concatenate([v * beta, kb * egc], axis=1)
        qd16 = (q * egc).astype(BF16)
        kd_ref[h] = (k * jnp.exp(gl - gc)).astype(BF16)
        for c in range(rows // CHUNK):
            wq_ref[h, (2 * c + 1) * CHUNK:(2 * c + 2) * CHUNK, :] = qd16[c * CHUNK:(c + 1) * CHUNK]
        for s in range(n_sub):
            lo, hi = s * sub, (s + 1) * sub
            gcr = gct_all[DN_HEADS + h:DN_HEADS + h + 1, lo:hi]
            kk = lax.dot_general(kb16[lo:hi], k16[lo:hi], _NT, preferred_element_type=F32)
            qk = lax.dot_general(q16[lo:hi], k16[lo:hi], _NT, preferred_element_type=F32)
            dec = jnp.exp(jnp.where(tril, gc[lo:hi] - gcr, -jnp.inf))
            a_ref[h, lo:hi, :] = (qk * dec).astype(BF16)
            pw_sc[h * n_sub + s] = jnp.where(strict, kk * dec, 0.0).astype(BF16)
            x_sc[h * n_sub + s] = x_all[lo:hi]

    items = range(DN_HEADS * n_sub)
    for it in items:
        x = x_sc[it]
        x_sc[it] = x - jnp.dot(pw_sc[it], x.astype(BF16), preferred_element_type=F32)
    for _ in range(int(math.log2(CHUNK)) - 1):
        for it in items:
            p16 = pw_sc[it]
            pw_sc[it] = jnp.dot(p16, p16, preferred_element_type=F32).astype(BF16)
        for it in items:
            x = x_sc[it]
            x_sc[it] = x + jnp.dot(pw_sc[it], x.astype(BF16), preferred_element_type=F32)
    for it in items:
        h, lo = it // n_sub, (it % n_sub) * sub
        x = x_sc[it]
        u_ref[h, lo:lo + sub, :] = x[:, :HEAD_W].astype(BF16)
        w16 = x[:, HEAD_W:].astype(BF16)
        for c in range(sub // CHUNK):
            r0 = lo + c * CHUNK
            wq_ref[h, 2 * r0:2 * r0 + CHUNK, :] = w16[c * CHUNK:(c + 1) * CHUNK]


def _gdn_prep(p, p_halo_src, meta_halo, conv_w, ba, alog, dtb, *, batch, seq, rows, sub, n_pad,
              cast=(), layer=0):
    m = batch * seq
    nt = seq // rows
    steps = batch * nt
    cast_rows = [w.shape[1] // steps for w in cast]
    assert all(w.shape[1] % steps == 0 and r % (2 * HALO) == 0 for w, r in zip(cast, cast_rows))
    cast_in = [pl.BlockSpec((None, r, w.shape[2]), lambda b, t: (layer, b * nt + t, 0))
               for w, r in zip(cast, cast_rows)]
    cast_out = [pl.BlockSpec((r, w.shape[2]), lambda b, t: (b * nt + t, 0)) for w, r in zip(cast, cast_rows)]
    cast_shape = [jax.ShapeDtypeStruct(w.shape[1:], BF16) for w in cast]
    g_q, g_k, g_v = 3, 4, 5
    hb = rows // HALO

    def main_spec(g):
        return pl.BlockSpec((DN_HEADS, rows, HEAD_W), lambda b, t: (g, b * nt + t, 0))

    def halo_spec(g):
        return pl.BlockSpec((DN_HEADS, HALO, HEAD_W),
                            lambda b, t: (g, jnp.maximum((b * nt + t) * hb - 1, 0), 0))

    full = lambda shape: pl.BlockSpec(shape, lambda b, t: (0,) * len(shape))
    out_shape = (jax.ShapeDtypeStruct((DN_HEADS, m, HEAD_W), BF16),
                 jax.ShapeDtypeStruct((DN_HEADS, 2 * m, HEAD_W), BF16),
                 jax.ShapeDtypeStruct((DN_HEADS, m, HEAD_W), BF16),
                 jax.ShapeDtypeStruct((DN_HEADS, m, sub), BF16),
                 jax.ShapeDtypeStruct((m, LANES), F32)) + tuple(cast_shape)
    out_specs = (pl.BlockSpec((DN_HEADS, rows, HEAD_W), lambda b, t: (0, b * nt + t, 0)),
                 pl.BlockSpec((DN_HEADS, 2 * rows, HEAD_W), lambda b, t: (0, b * nt + t, 0)),
                 pl.BlockSpec((DN_HEADS, rows, HEAD_W), lambda b, t: (0, b * nt + t, 0)),
                 pl.BlockSpec((DN_HEADS, rows, sub), lambda b, t: (0, b * nt + t, 0)),
                 pl.BlockSpec((rows, LANES), lambda b, t: (b * nt + t, 0))) + tuple(cast_out)
    outs = pl.pallas_call(
        functools.partial(_gdn_prep_kernel, rows=rows, sub=sub, n_pad=n_pad, n_cast=len(cast)),
        out_shape=out_shape,
        grid=(batch, nt),
        in_specs=[main_spec(g_q), main_spec(g_k), main_spec(g_v),
                  halo_spec(g_q), halo_spec(g_k), halo_spec(g_v),
                  full(meta_halo.shape), full(conv_w.shape),
                  pl.BlockSpec((rows, LANES), lambda b, t: (b * nt + t, 0)),
                  full(alog.shape), full(dtb.shape)] + cast_in,
        out_specs=out_specs,
        scratch_shapes=[pltpu.VMEM((3, rows + HALO, HEAD_W), F32), pltpu.VMEM((rows, LANES), F32),
                        pltpu.VMEM((DN_HEADS * (rows // sub), sub, sub), BF16),
                        pltpu.VMEM((DN_HEADS * (rows // sub), sub, 2 * HEAD_W), F32)],
        compiler_params=_params("parallel", "parallel"),
        name="gdn_prep",
    )(p, p, p, p_halo_src, p_halo_src, p_halo_src, meta_halo, conv_w, ba, alog, dtb, *cast)
    return outs[:5], outs[5:]


def _gdn_scan_kernel(u_ref, wq_ref, kd_ref, a_ref, gl_ref, z_ref,
                     um_ref, wqm_ref, kdm_ref, am_ref, glm_ref, zm_ref, nw_ref,
                     o_ref, om_ref, s_sc, vbuf_sc, sw_sc, *, rows, sub, hpg):
    hg = pl.program_id(1)
    t = pl.program_id(2)
    lane = lax.broadcasted_iota(jnp.int32, (1, LANES), 1)
    nw = nw_ref[...]

    def decay_total(gl_row, h):
        sel = lane == DN_HEADS + hg * hpg + h
        return jnp.sum(jnp.where(sel, jnp.exp(gl_row), 0.0), axis=1, keepdims=True)

    def gate(o, z):
        ms = jnp.mean(o * o, axis=1, keepdims=True)
        return (o * lax.rsqrt(ms + RMS_EPS) * nw * _silu(z.astype(F32))).astype(BF16)

    def run_chunk(r0, refs, out_ref, out_r0):
        u_r, wq_r, kd_r, a_r, gl_r, z_r = refs
        slot = r0 % sub
        gl_row = gl_r[r0:r0 + 1, :]
        for h in range(hpg):
            sw_sc[h] = jnp.dot(wq_r[h, 2 * r0:2 * r0 + 2 * CHUNK, :], s_sc[h].astype(BF16),
                               preferred_element_type=F32)
        for h in range(hpg):
            v_new = u_r[h, r0:r0 + CHUNK, :].astype(F32) - sw_sc[h, 0:CHUNK, :]
            vbuf_sc[h, slot:slot + CHUNK, :] = v_new.astype(BF16)
        for h in range(hpg):
            o = sw_sc[h, CHUNK:, :] + jnp.dot(a_r[h, r0:r0 + CHUNK, :], vbuf_sc[h],
                                              preferred_element_type=F32)
            s_sc[h] = decay_total(gl_row, h) * s_sc[h] + lax.dot_general(
                kd_r[h, r0:r0 + CHUNK, :], vbuf_sc[h, slot:slot + CHUNK, :], _TN,
                preferred_element_type=F32)
            out_ref[out_r0:out_r0 + CHUNK, h * HEAD_W:(h + 1) * HEAD_W] = gate(
                o, z_r[h, r0:r0 + CHUNK, :])

    @pl.when(t == 0)
    def _meta_chunk():
        vbuf_sc[...] = jnp.zeros_like(vbuf_sc)
        s_sc[...] = jnp.zeros_like(s_sc)
        run_chunk(sub - CHUNK, (um_ref, wqm_ref, kdm_ref, am_ref, glm_ref, zm_ref), om_ref, 0)

    for c in range(rows // CHUNK):
        run_chunk(c * CHUNK, (u_ref, wq_ref, kd_ref, a_ref, gl_ref, z_ref), o_ref, c * CHUNK)


def _gdn_scan(real, meta, p, p_meta_pad, norm_w, *, batch, seq, rows, sub, hpg):
    u, wq, kd, a, gl = real
    um, wqm, kdm, am, glm = meta
    m = batch * seq
    nt = seq // rows
    ngrp = DN_HEADS // hpg
    z_blk = (6 * DN_HEADS) // hpg
    mrows = um.shape[1]
    assert mrows == sub

    def real_spec(r, w):
        return pl.BlockSpec((hpg, r, w), lambda b, g, t: (g, b * nt + t, 0))

    def meta_spec(r, w):
        return pl.BlockSpec((hpg, r, w), lambda b, g, t: (g, 0, 0))

    return pl.pallas_call(
        functools.partial(_gdn_scan_kernel, rows=rows, sub=sub, hpg=hpg),
        out_shape=(jax.ShapeDtypeStruct((m, GROUP_W), BF16),
                   jax.ShapeDtypeStruct((batch, CHUNK, GROUP_W), BF16)),
        grid=(batch, ngrp, nt),
        in_specs=[real_spec(rows, HEAD_W), real_spec(2 * rows, HEAD_W), real_spec(rows, HEAD_W),
                  real_spec(rows, sub),
                  pl.BlockSpec((rows, LANES), lambda b, g, t: (b * nt + t, 0)),
                  pl.BlockSpec((hpg, rows, HEAD_W), lambda b, g, t: (z_blk + g, b * nt + t, 0)),
                  meta_spec(mrows, HEAD_W), meta_spec(2 * mrows, HEAD_W), meta_spec(mrows, HEAD_W),
                  meta_spec(mrows, mrows),
                  pl.BlockSpec((mrows, LANES), lambda b, g, t: (0, 0)),
                  pl.BlockSpec((hpg, mrows, HEAD_W), lambda b, g, t: (z_blk + g, 0, 0)),
                  pl.BlockSpec((1, HEAD_W), lambda b, g, t: (0, 0))],
        out_specs=(pl.BlockSpec((rows, hpg * HEAD_W), lambda b, g, t: (b * nt + t, g)),
                   pl.BlockSpec((None, CHUNK, hpg * HEAD_W), lambda b, g, t: (b, 0, g))),
        scratch_shapes=[pltpu.VMEM((hpg, HEAD_W, HEAD_W), F32),
                        pltpu.VMEM((hpg, sub, HEAD_W), BF16),
                        pltpu.VMEM((hpg, 2 * CHUNK, HEAD_W), F32)],
        compiler_params=_params("parallel", "parallel", "arbitrary"),
        name="gdn_scan",
    )(u, wq, kd, a, gl, p, um, wqm, kdm, am, glm, p_meta_pad, norm_w)


def _wout_ln_kernel(oa_ref, od_ref, w_ref, x_ref, g_ref, b_ref, h_ref, *, alpha):
    half = oa_ref.shape[1]
    mix = (jnp.dot(oa_ref[...], w_ref[0:half, :], preferred_element_type=F32)
           + jnp.dot(od_ref[...], w_ref[half:, :], preferred_element_type=F32))
    h_ref[...] = _layer_norm(alpha * x_ref[...] + mix, g_ref[...], b_ref[...])


def _wout_ln(o_att, o_dn, w_out, x2d, g, b, *, bm, alpha):
    m, d = x2d.shape
    half = o_att.shape[1]
    return pl.pallas_call(
        functools.partial(_wout_ln_kernel, alpha=alpha),
        out_shape=jax.ShapeDtypeStruct((m, d), F32),
        grid=(m // bm,),
        in_specs=[pl.BlockSpec((bm, half), lambda i: (i, 0)),
                  pl.BlockSpec((bm, half), lambda i: (i, 0)),
                  pl.BlockSpec((2 * half, d), lambda i: (0, 0)),
                  pl.BlockSpec((bm, d), lambda i: (i, 0)),
                  pl.BlockSpec((1, d), lambda i: (0, 0)),
                  pl.BlockSpec((1, d), lambda i: (0, 0))],
        out_specs=pl.BlockSpec((bm, d), lambda i: (i, 0)),
        compiler_params=_params("parallel"),
        name="wout_ln",
    )(o_att, o_dn, w_out, x2d, g, b)


def _ffn_kernel(h_ref, hprev_ref, hmeta_ref, wg_ref, wu_ref, cw_ref, cb_ref, wd_ref, g_ref, b_ref,
                o_ref, xb_sc, gate_sc, *, bm, tiles_per_batch, alpha):
    i = pl.program_id(0)
    f = pl.program_id(1)

    @pl.when(f == 0)
    def _init():
        halo = jnp.where(i % tiles_per_batch == 0, hmeta_ref[...], hprev_ref[...])
        xb_sc[0:HALO, :] = halo.astype(BF16)
        xb_sc[HALO:, :] = h_ref[...].astype(BF16)
        o_ref[...] = jnp.zeros_like(o_ref)

    gate_sc[...] = jnp.dot(xb_sc[...], wg_ref[...], preferred_element_type=F32)
    up = jnp.dot(xb_sc[HALO:, :], wu_ref[...], preferred_element_type=F32)
    cw = cw_ref[...]
    hg = cb_ref[...] + gate_sc[pl.ds(HALO - FFN_CONV + 1, bm), :] * cw[0:1]
    for j in range(1, FFN_CONV):
        hg = hg + gate_sc[pl.ds(HALO - FFN_CONV + 1 + j, bm), :] * cw[j:j + 1]
    act = (_silu(hg) * up).astype(BF16)
    o_ref[...] += jnp.dot(act, wd_ref[...], preferred_element_type=F32)

    @pl.when(f == pl.num_programs(1) - 1)
    def _finish():
        o_ref[...] = _layer_norm(alpha * h_ref[...] + o_ref[...], g_ref[...], b_ref[...])


def _ffn(h1, h1_meta, w_gate, w_up, conv_w, conv_b, w_down, g, b, *, seq, bm, tf, alpha):
    m, d = h1.shape
    dff = w_gate.shape[1]
    tiles_per_batch = seq // bm
    hb = bm // HALO
    meta_blk = h1_meta.shape[0] // HALO - 1
    return pl.pallas_call(
        functools.partial(_ffn_kernel, bm=bm, tiles_per_batch=tiles_per_batch, alpha=alpha),
        out_shape=jax.ShapeDtypeStruct((m, d), F32),
        grid=(m // bm, dff // tf),
        in_specs=[pl.BlockSpec((bm, d), lambda i, f: (i, 0)),
                  pl.BlockSpec((HALO, d), lambda i, f: (jnp.maximum(i * hb - 1, 0), 0)),
                  pl.BlockSpec((HALO, d), lambda i, f: (meta_blk, 0)),
                  pl.BlockSpec((d, tf), lambda i, f: (0, f)),
                  pl.BlockSpec((d, tf), lambda i, f: (0, f)),
                  pl.BlockSpec((FFN_CONV, tf), lambda i, f: (0, f)),
                  pl.BlockSpec((1, tf), lambda i, f: (0, f)),
                  pl.BlockSpec((tf, d), lambda i, f: (f, 0)),
                  pl.BlockSpec((1, d), lambda i, f: (0, 0)),
                  pl.BlockSpec((1, d), lambda i, f: (0, 0))],
        out_specs=pl.BlockSpec((bm, d), lambda i, f: (i, 0)),
        scratch_shapes=[pltpu.VMEM((bm + HALO, d), BF16),
                        pltpu.VMEM((bm + HALO, tf), F32)],
        compiler_params=_params("parallel", "arbitrary"),
        name="ffn",
    )(h1, h1, h1_meta, w_gate, w_up, conv_w, conv_b, w_down, g, b)


def _largest_tile(n, cap, mult):
    best = None
    for c in range(mult, min(n, cap) + 1, mult):
        if n % c == 0:
            best = c
    assert best is not None, (n, cap, mult)
    return best


def _split_halves(w_qk_t):
    n_comp, quarter = HEAD_W // ATT_QK_DIM, ATT_QK_DIM // 2
    depth, n, d = w_qk_t.shape
    w = w_qk_t.reshape(depth, -1, n_comp, 2, quarter, d)
    return jnp.swapaxes(w, 2, 3).reshape(depth, n, d)


def _rope_tables(n_pos):
    n_comp = HEAD_W // ATT_QK_DIM
    inv_freq = ROPE_THETA ** (-jnp.arange(0, ATT_QK_DIM, 2, dtype=F32) / ATT_QK_DIM)
    ang = jnp.arange(n_pos, dtype=F32)[:, None] * inv_freq[None, :]
    cos, sin = jnp.cos(ang), jnp.sin(ang)
    cos_t = jnp.tile(cos, (1, 2 * n_comp))
    sin_t = jnp.concatenate([jnp.tile(-sin, (1, n_comp)), jnp.tile(sin, (1, n_comp))], axis=1)
    assert cos_t.shape[1] == HEAD_W
    return cos_t, sin_t


def kernel(x, meta_tokens, w_in, conv_qkv_w, a_log, dt_bias, lambda_q1, lambda_k1, lambda_q2, lambda_k2,
           diff_norm_w, delta_norm_w, w_out, ln1_g, ln1_b, ffn_w_gate, ffn_w_up, ffn_conv_w, ffn_conv_b,
           ffn_w_down, ln2_g, ln2_b):
    batch, seq, d = x.shape
    n_meta = meta_tokens.shape[0]
    depth = w_in.shape[0]
    n_main = N_GROUPS * GROUP_W
    assert w_in.shape[2] == n_main + TAIL_COLS and n_meta % HALO == 0 and n_meta <= CHUNK
    m = batch * seq
    alpha = (2.0 * depth) ** 0.25

    bm_proj = _largest_tile(seq, 1024, HALO)
    bm_row = _largest_tile(seq, 512, HALO)
    tq = _largest_tile(seq, 256, CHUNK)
    rows = _largest_tile(seq, 256, LANES)
    sub = LANES
    hpg = DN_HEADS
    att_hps = ATT_HEADS

    cos_t, sin_t = _rope_tables(n_meta + seq)
    cos_meta, sin_meta = cos_t[:n_meta], sin_t[:n_meta]
    cos_real, sin_real = cos_t[n_meta:], sin_t[n_meta:]

    h = x.reshape(m, d)
    h_meta = meta_tokens.astype(x.dtype)
    n_qk = 2 * GROUP_W
    w_in_t = jnp.swapaxes(w_in, 1, 2)
    w_t = w_in_t.astype(BF16)
    wqk_t = _split_halves(w_t[:, :n_qk])
    pad_rows = sub - n_meta
    for l in range(depth):
        w_tail_t = jnp.pad(w_t[l, n_main:], ((0, LANES - TAIL_COLS), (0, 0)))
        lane_pad = (DN_HEADS, LANES - TAIL_COLS)
        alog = jnp.pad(a_log[l].astype(F32), lane_pad)[None]
        dtb = jnp.pad(dt_bias[l].astype(F32), lane_pad)[None]
        conv_w = conv_qkv_w[l].reshape(DN_CONV, 3 * DN_HEADS, HEAD_W).transpose(1, 0, 2)
        lams = tuple(v[l][None].astype(F32) for v in (lambda_q1, lambda_k1, lambda_q2, lambda_k2))
        lam_init = 0.8 - 0.6 * math.exp(-0.3 * l)

        p, p_meta, ba, ba_meta = _proj(h, h_meta, wqk_t, w_t, l, w_tail_t, cos_real, sin_real,
                                       cos_meta, sin_meta, bm_proj)

        att_nw = diff_norm_w[l].astype(F32)[:, None]
        o_att = _attention(p, p, p_meta, lams, att_nw, batch=batch, seq=seq, tq=tq, hps=att_hps,
                           lam_init=lam_init, meta_only=False)
        q_meta = jnp.pad(p_meta[:ATT_HEADS], ((0, 0), (0, LANES - n_meta), (0, 0)))
        o_att_meta = _attention(q_meta, None, p_meta, lams, att_nw, batch=1, seq=LANES, tq=LANES,
                                hps=att_hps, lam_init=lam_init, meta_only=True)[:n_meta]

        p_meta_pad = jnp.pad(p_meta, ((0, 0), (pad_rows, 0), (0, 0)))
        ba_meta_pad = jnp.pad(ba_meta, ((pad_rows, 0), (0, 0)))
        meta_halo = p_meta[3 * DN_HEADS:6 * DN_HEADS, n_meta - HALO:, :]
        zero_halo = jnp.zeros_like(meta_halo)
        prep_real, (w_out_b, w_gate_b, w_up_b, w_down_b) = _gdn_prep(
            p, p, meta_halo, conv_w, ba, alog, dtb, batch=batch, seq=seq, rows=rows, sub=sub, n_pad=0,
            cast=(w_out, ffn_w_gate, ffn_w_up, ffn_w_down), layer=l)
        prep_meta, _ = _gdn_prep(p_meta_pad, p_meta_pad, zero_halo, conv_w, ba_meta_pad, alog, dtb, batch=1,
                                 seq=sub, rows=sub, sub=sub, n_pad=pad_rows)
        o_dn, o_dn_meta = _gdn_scan(prep_real, prep_meta, p, p_meta_pad, delta_norm_w[l][None],
                                    batch=batch, seq=seq, rows=rows, sub=sub, hpg=hpg)
        o_dn_meta = o_dn_meta[0, CHUNK - n_meta:]

        g1, b1 = ln1_g[l][None].astype(F32), ln1_b[l][None].astype(F32)
        h1 = _wout_ln(o_att, o_dn, w_out_b, h, g1, b1, bm=bm_row, alpha=alpha)
        h1_meta = _wout_ln(o_att_meta, o_dn_meta, w_out_b, h_meta, g1, b1, bm=n_meta, alpha=alpha)

        dff = ffn_w_gate.shape[2]
        tf = _largest_tile(dff, 512, LANES)
        ffn_args = (w_gate_b, w_up_b, ffn_conv_w[l].astype(F32),
                    ffn_conv_b[l][None].astype(F32), w_down_b,
                    ln2_g[l][None].astype(F32), ln2_b[l][None].astype(F32))
        h_next = _ffn(h1, h1_meta, *ffn_args, seq=seq, bm=bm_row, tf=tf, alpha=alpha)
        if l + 1 < depth:
            zero_meta = jnp.zeros_like(h1_meta)
            h_meta = _ffn(h1_meta, zero_meta, *ffn_args, seq=n_meta, bm=n_meta, tf=tf, alpha=alpha)
        h = h_next
    return h.reshape(batch, seq, d)
```

```python
import functools
import math

import jax
import jax.numpy as jnp
from jax import lax
from jax.experimental import pallas as pl
from jax.experimental.pallas import tpu as pltpu

F32 = jnp.float32
BF16 = jnp.bfloat16

CHUNK = 64
ATT_HEADS = 8
ATT_QK_DIM = 64
HEAD_W = 128
DN_HEADS = 8
DN_CONV = 4
FFN_CONV = 3
ROPE_THETA = 10000.0
LN_EPS = 1e-5
RMS_EPS = 1e-6
GROUP_W = ATT_HEADS * HEAD_W
Q_SCALE = ATT_QK_DIM ** -0.5 * math.log2(math.e)
N_GROUPS = 7
TAIL_COLS = 2 * DN_HEADS

LANES = 128
HALO = 8
VMEM_LIMIT = 56 * 1024 * 1024

_NT = (((1,), (1,)), ((), ()))
_TN = (((0,), (0,)), ((), ()))


def _params(*sem):
    return pltpu.CompilerParams(dimension_semantics=sem, vmem_limit_bytes=VMEM_LIMIT)


def _sigmoid(x):
    return 1.0 / (1.0 + jnp.exp(-x))


def _silu(x):
    return x * _sigmoid(x)


def _softplus(x):
    return jnp.maximum(x, 0.0) + jnp.log(1.0 + jnp.exp(-jnp.abs(x)))


def _layer_norm(y, g, b):
    mu = jnp.mean(y, axis=1, keepdims=True)
    d = y - mu
    var = jnp.mean(d * d, axis=1, keepdims=True)
    return d * lax.rsqrt(var + LN_EPS) * g + b


def _proj_kernel(x_ref, xm_ref, wqk_ref, w_ref, wt_ref, cos_ref, sin_ref, cosm_ref, sinm_ref,
                 p_ref, pm_ref, ba_ref, bam_ref, *, heads, n_qk_tiles):
    j = pl.program_id(0)
    i = pl.program_id(1)

    def tail(x_r):
        return lax.dot_general(x_r[...].astype(BF16), wt_ref[...], _NT, preferred_element_type=F32)

    def project(xh, wsel_ref, rope_tables, out_ref):
        acc = lax.dot_general(xh, wsel_ref[...], _NT, preferred_element_type=F32)
        if rope_tables is not None:
            scale = jnp.where(j == 0, Q_SCALE, 1.0).astype(F32)
            cos = rope_tables[0][...] * scale
            sin = rope_tables[1][...] * scale
        for h in range(heads):
            blk = acc[:, h * HEAD_W:(h + 1) * HEAD_W]
            if rope_tables is not None:
                blk = blk * cos + pltpu.roll(blk, HEAD_W // 2, 1) * sin
            out_ref[h] = blk.astype(BF16)

    def tile(wsel_ref, rope):
        @pl.when(i == 0)
        def _meta_rows():
            project(xm_ref[...].astype(BF16), wsel_ref, (cosm_ref, sinm_ref) if rope else None, pm_ref)

        project(x_ref[...].astype(BF16), wsel_ref, (cos_ref, sin_ref) if rope else None, p_ref)

    @pl.when(j < n_qk_tiles)
    def _qk_tile():
        tile(wqk_ref, True)

    @pl.when(j >= n_qk_tiles)
    def _plain_tile():
        tile(w_ref, False)

    @pl.when(j == 0)
    def _tail():
        ba_ref[...] = tail(x_ref)

        @pl.when(i == 0)
        def _meta_tail():
            bam_ref[...] = tail(xm_ref)


def _proj(x2d, x_meta, wqk_t, w_t, layer, w_tail_t, cos, sin, cos_meta, sin_meta, bm):
    m, d = x2d.shape
    n_meta = x_meta.shape[0]
    heads = GROUP_W // HEAD_W
    n_blocks = N_GROUPS * heads
    nm = m // bm
    n_qk_tiles = wqk_t.shape[1] // GROUP_W
    const = lambda shape: pl.BlockSpec(shape, lambda j, i: (0,) * len(shape))
    row = lambda j, i: (i, 0)
    tiles_per_batch = cos.shape[0] // bm
    pos = lambda j, i: (i % tiles_per_batch, 0)
    return pl.pallas_call(
        functools.partial(_proj_kernel, heads=heads, n_qk_tiles=n_qk_tiles),
        out_shape=(jax.ShapeDtypeStruct((n_blocks, m, HEAD_W), BF16),
                   jax.ShapeDtypeStruct((n_blocks, n_meta, HEAD_W), BF16),
                   jax.ShapeDtypeStruct((m, LANES), F32),
                   jax.ShapeDtypeStruct((n_meta, LANES), F32)),
        grid=(N_GROUPS, nm),
        in_specs=[
            pl.BlockSpec((bm, d), row),
            const((n_meta, d)),
            pl.BlockSpec((None, GROUP_W, d), lambda j, i: (layer, jnp.minimum(j, n_qk_tiles - 1), 0)),
            pl.BlockSpec((None, GROUP_W, d), lambda j, i: (layer, jnp.maximum(j, n_qk_tiles), 0)),
            const((LANES, d)),
            pl.BlockSpec((bm, HEAD_W), pos),
            pl.BlockSpec((bm, HEAD_W), pos),
            const((n_meta, HEAD_W)),
            const((n_meta, HEAD_W)),
        ],
        out_specs=(
            pl.BlockSpec((heads, bm, HEAD_W), lambda j, i: (j, i, 0)),
            pl.BlockSpec((heads, n_meta, HEAD_W), lambda j, i: (j, 0, 0)),
            pl.BlockSpec((bm, LANES), lambda j, i: (jnp.where(j == 0, i, nm - 1), 0)),
            const((n_meta, LANES)),
        ),
        compiler_params=_params("arbitrary", "arbitrary"),
        name="proj",
    )(x2d, x_meta, wqk_t, w_t, w_tail_t, cos, sin, cos_meta, sin_meta)


def _attn_kernel(*refs, tq, hps, lam_init, meta_only):
    if meta_only:
        q_ref, km_ref, vm_ref, lq1, lk1, lq2, lk2, nw_ref, o_ref, m_sc, l_sc, acc_sc = refs
    else:
        (q_ref, k_ref, v_ref, km_ref, vm_ref, lq1, lk1, lq2, lk2, nw_ref, o_ref,
         m_sc, l_sc, acc_sc, s_sc, a_sc, p_sc) = refs
    lane = lax.broadcasted_iota(jnp.int32, (1, HEAD_W), 1)
    comp = (lane // (ATT_QK_DIM // 2)) % 2
    qc = []
    for h in range(hps):
        q = q_ref[h]
        zero = jnp.zeros_like(q)
        qc.append((jnp.where(comp == 0, q, zero), jnp.where(comp == 1, q, zero)))

    nc = 2 * hps

    def init_from_meta():
        s_meta = [lax.dot_general(km_ref[i // 2], qc[i // 2][i % 2], _NT, preferred_element_type=F32)
                  for i in range(nc)]
        p_meta = []
        for i in range(nc):
            m0 = jnp.max(s_meta[i], axis=0, keepdims=True)
            p = jnp.exp2(s_meta[i] - m0)
            m_sc[i] = m0
            l_sc[i] = jnp.sum(p, axis=0, keepdims=True)
            p_meta.append(p.astype(BF16))
        for i in range(nc):
            acc_sc[i] = lax.dot_general(vm_ref[i // 2], p_meta[i], _TN, preferred_element_type=F32)

    def scores(t, slot):
        off = pl.multiple_of(t * tq, tq)
        for h in range(hps):
            kt = k_ref[h, pl.ds(off, tq), :]
            for c in range(2):
                s_sc[slot * nc + 2 * h + c] = lax.dot_general(kt, qc[h][c], _NT,
                                                              preferred_element_type=F32)

    def absorb(t, slot, mask):
        off = pl.multiple_of(t * tq, tq)
        for i in range(nc):
            s = s_sc[slot * nc + i]
            if mask is not None:
                s = jnp.where(mask, s, -jnp.inf)
            m_old = m_sc[i]
            m_new = jnp.maximum(m_old, jnp.max(s, axis=0, keepdims=True))
            a = jnp.exp2(m_old - m_new)
            p = jnp.exp2(s - m_new)
            l_sc[i] = a * l_sc[i] + jnp.sum(p, axis=0, keepdims=True)
            m_sc[i] = m_new
            a_sc[i] = a
            p_sc[i] = p.astype(BF16)
        for h in range(hps):
            vt = v_ref[h, pl.ds(off, tq), :]
            for c in range(2):
                i = 2 * h + c
                acc_sc[i] = a_sc[i] * acc_sc[i] + lax.dot_general(vt, p_sc[i], _TN,
                                                                  preferred_element_type=F32)

    if meta_only:
        init_from_meta()
    else:
        qi = pl.program_id(2)
        keys = lax.broadcasted_iota(jnp.int32, (tq, tq), 0)
        queries = lax.broadcasted_iota(jnp.int32, (tq, tq), 1)
        diag = (keys // CHUNK) <= (queries // CHUNK)
        scores(0, 0)
        init_from_meta()

        def body(k, carry):
            scores(2 * k + 1, 1)
            absorb(2 * k, 0, None)
            scores(2 * k + 2, 0)
            absorb(2 * k + 1, 1, None)
            return carry

        lax.fori_loop(0, qi // 2, body, 0)

        @pl.when(qi % 2 == 1)
        def _odd_tail():
            scores(qi, 1)
            absorb(qi - 1, 0, None)
            absorb(qi, 1, diag)

        @pl.when(qi % 2 == 0)
        def _even_tail():
            absorb(qi, 0, diag)

    lam = (jnp.exp(jnp.sum(lq1[...] * lk1[...], axis=1, keepdims=True))
           - jnp.exp(jnp.sum(lq2[...] * lk2[...], axis=1, keepdims=True)) + lam_init)
    for h in range(hps):
        o = acc_sc[2 * h] / l_sc[2 * h] - lam * (acc_sc[2 * h + 1] / l_sc[2 * h + 1])
        ms = jnp.mean(o * o, axis=0, keepdims=True)
        o = o * lax.rsqrt(ms + RMS_EPS) * nw_ref[...] * (1.0 - lam_init)
        o_ref[:, h * HEAD_W:(h + 1) * HEAD_W] = o.T.astype(BF16)


def _attention(p_q, p_kv, p_meta, lams, norm_w, *, batch, seq, tq, hps, lam_init, meta_only):
    n_meta = p_meta.shape[1]
    rows = p_q.shape[1]
    nq = 1 if meta_only else seq // tq
    k_blk, v_blk = ATT_HEADS // hps, 2 * ATT_HEADS // hps
    small = [pl.BlockSpec((1, ATT_QK_DIM), lambda b, g, i: (0, 0))] * 4 + [
        pl.BlockSpec((HEAD_W, 1), lambda b, g, i: (0, 0))]
    meta_specs = [pl.BlockSpec((hps, n_meta, HEAD_W), lambda b, g, i: (k_blk + g, 0, 0)),
                  pl.BlockSpec((hps, n_meta, HEAD_W), lambda b, g, i: (v_blk + g, 0, 0))]
    q_spec = pl.BlockSpec((hps, tq, HEAD_W), lambda b, g, i: (g, b * nq + i, 0))
    if meta_only:
        in_specs = [q_spec] + meta_specs + small
        args = (p_q, p_meta, p_meta) + lams + (norm_w,)
    else:
        kv_specs = [pl.BlockSpec((hps, seq, HEAD_W), lambda b, g, i: (k_blk + g, b, 0)),
                    pl.BlockSpec((hps, seq, HEAD_W), lambda b, g, i: (v_blk + g, b, 0))]
        in_specs = [q_spec] + kv_specs + meta_specs + small
        args = (p_q, p_kv, p_kv, p_meta, p_meta) + lams + (norm_w,)
    scratch = [pltpu.VMEM((2 * hps, 1, tq), F32), pltpu.VMEM((2 * hps, 1, tq), F32),
               pltpu.VMEM((2 * hps, HEAD_W, tq), F32)]
    if not meta_only:
        scratch += [pltpu.VMEM((2 * 2 * hps, tq, tq), F32),
                    pltpu.VMEM((2 * hps, 1, tq), F32),
                    pltpu.VMEM((2 * hps, tq, tq), BF16)]
    return pl.pallas_call(
        functools.partial(_attn_kernel, tq=tq, hps=hps, lam_init=lam_init, meta_only=meta_only),
        out_shape=jax.ShapeDtypeStruct((rows, GROUP_W), BF16),
        grid=(batch, ATT_HEADS // hps, nq),
        in_specs=in_specs,
        out_specs=pl.BlockSpec((tq, hps * HEAD_W), lambda b, g, i: (b * nq + i, g)),
        scratch_shapes=scratch,
        compiler_params=_params("parallel", "parallel", "arbitrary"),
        name="attn_meta" if meta_only else "attn",
    )(*args)


def _gdn_prep_kernel(*refs, rows, sub, n_pad, n_cast):
    (q_ref, k_ref, v_ref, hq_ref, hk_ref, hv_ref, mh_ref, cw_ref, ba_ref, alog_ref, dtb_ref) = refs[:11]
    cast_src = refs[11:11 + n_cast]
    u_ref, wq_ref, kd_ref, a_ref, gl_ref = refs[11 + n_cast:16 + n_cast]
    cast_dst = refs[16 + n_cast:16 + 2 * n_cast]
    xs_sc, gc_sc, pw_sc, x_sc, t_sc = refs[16 + 2 * n_cast:]
    for src, dst in zip(cast_src, cast_dst):
        dst[...] = src[...].astype(BF16)

    first = pl.program_id(1) == 0
    n_sub = rows // sub
    ba = ba_ref[...]
    beta_all = _sigmoid(ba)
    g_all = -jnp.exp(alog_ref[...]) * _softplus(ba + dtb_ref[...])
    row = lax.broadcasted_iota(jnp.int32, (rows, 1), 0)
    if n_pad:
        beta_all = jnp.where(row >= n_pad, beta_all, 0.0)
        g_all = jnp.where(row >= n_pad, g_all, 0.0)

    pos = row % CHUNK
    gc_all = g_all
    shift = 1
    while shift < CHUNK:
        gc_all = gc_all + jnp.where(pos >= shift, pltpu.roll(gc_all, shift, 0), 0.0)
        shift *= 2
    gc_sc[...] = gc_all
    gl_all = jnp.concatenate(
        [jnp.broadcast_to(gc_sc[(c + 1) * CHUNK - 1:(c + 1) * CHUNK, :], (CHUNK, LANES))
         for c in range(rows // CHUNK)], axis=0)
    gl_ref[...] = gl_all
    gct_all = gc_all.T

    ri = lax.broadcasted_iota(jnp.int32, (sub, sub), 0)
    ci = lax.broadcasted_iota(jnp.int32, (sub, sub), 1)
    same = (ri // CHUNK) == (ci // CHUNK)
    tril = jnp.logical_and(same, ci <= ri)
    strict = jnp.logical_and(same, ci < ri)
    eye = jnp.where(ri == ci, 1.0, 0.0).astype(F32)

    def conv_act(main_ref, halo_ref, stream, h):
        halo = jnp.where(first, mh_ref[stream * DN_HEADS + h], halo_ref[h])
        xs_sc[stream, 0:HALO, :] = halo.astype(F32)
        xs_sc[stream, HALO:, :] = main_ref[h].astype(F32)
        w = cw_ref[stream * DN_HEADS + h]
        y = xs_sc[stream, pl.ds(HALO - DN_CONV + 1, rows), :] * w[0:1]
        for j in range(1, DN_CONV):
            y = y + xs_sc[stream, pl.ds(HALO - DN_CONV + 1 + j, rows), :] * w[j:j + 1]
        return _silu(y)

    for h in range(DN_HEADS):
        q = conv_act(q_ref, hq_ref, 0, h)
        k = conv_act(k_ref, hk_ref, 1, h)
        v = conv_act(v_ref, hv_ref, 2, h)
        q = q * lax.rsqrt(jnp.sum(q * q, axis=1, keepdims=True) + RMS_EPS) * (HEAD_W ** -0.5)
        k = k * lax.rsqrt(jnp.sum(k * k, axis=1, keepdims=True) + RMS_EPS)
        beta = beta_all[:, h:h + 1]
        gc = gc_all[:, DN_HEADS + h:DN_HEADS + h + 1]
        gl = gl_all[:, DN_HEADS + h:DN_HEADS + h + 1]
        egc = jnp.exp(gc)
        kb = k * beta
        k16 = k.astype(BF16)
        kb16 = kb.astype(BF16)
        q16 = q.astype(BF16)
        x_all = jnp.concatenate([v * beta, kb * egc], axis=1)
        qd16 = (q * egc).astype(BF16)
        kd_ref[h] = (k * jnp.exp(gl - gc)).astype(BF16)
        for c in range(rows // CHUNK):
            wq_ref[h, (2 * c + 1) * CHUNK:(2 * c + 2) * CHUNK, :] = qd16[c * CHUNK:(c + 1) * CHUNK]
        for s in range(n_sub):
            lo, hi = s * sub, (s + 1) * sub
            gcr = gct_all[DN_HEADS + h:DN_HEADS + h + 1, lo:hi]
            kk = lax.dot_general(kb16[lo:hi], k16[lo:hi], _NT, preferred_element_type=F32)
            qk = lax.dot_general(q16[lo:hi], k16[lo:hi], _NT, preferred_element_type=F32)
            dec = jnp.exp(jnp.where(tril, gc[lo:hi] - gcr, -jnp.inf))
            a_ref[h, lo:hi, :] = (qk * dec).astype(BF16)
            m0 = jnp.where(strict, kk * dec, 0.0)
            pw_sc[h * n_sub + s] = m0.astype(BF16)
            t_sc[h * n_sub + s] = eye - m0
            x_sc[h * n_sub + s] = x_all[lo:hi].astype(BF16)

    items = range(DN_HEADS * n_sub)
    for _ in range(int(math.log2(CHUNK)) - 1):
        for it in items:
            p16 = pw_sc[it]
            pw_sc[it] = jnp.dot(p16, p16, preferred_element_type=F32).astype(BF16)
        for it in items:
            t = t_sc[it]
            t_sc[it] = t + jnp.dot(pw_sc[it], t.astype(BF16), preferred_element_type=F32)
    for it in items:
        h, lo = it // n_sub, (it % n_sub) * sub
        x = jnp.dot(t_sc[it].astype(BF16), x_sc[it], preferred_element_type=F32)
        u_ref[h, lo:lo + sub, :] = x[:, :HEAD_W].astype(BF16)
        w16 = x[:, HEAD_W:].astype(BF16)
        for c in range(sub // CHUNK):
            r0 = lo + c * CHUNK
            wq_ref[h, 2 * r0:2 * r0 + CHUNK, :] = w16[c * CHUNK:(c + 1) * CHUNK]


def _gdn_prep(p, p_halo_src, meta_halo, conv_w, ba, alog, dtb, *, batch, seq, rows, sub, n_pad,
              cast=(), layer=0):
    m = batch * seq
    nt = seq // rows
    steps = batch * nt
    cast_rows = [w.shape[1] // steps for w in cast]
    assert all(w.shape[1] % steps == 0 and r % (2 * HALO) == 0 for w, r in zip(cast, cast_rows))
    cast_in = [pl.BlockSpec((None, r, w.shape[2]), lambda b, t: (layer, b * nt + t, 0))
               for w, r in zip(cast, cast_rows)]
    cast_out = [pl.BlockSpec((r, w.shape[2]), lambda b, t: (b * nt + t, 0)) for w, r in zip(cast, cast_rows)]
    cast_shape = [jax.ShapeDtypeStruct(w.shape[1:], BF16) for w in cast]
    g_q, g_k, g_v = 3, 4, 5
    hb = rows // HALO

    def main_spec(g):
        return pl.BlockSpec((DN_HEADS, rows, HEAD_W), lambda b, t: (g, b * nt + t, 0))

    def halo_spec(g):
        return pl.BlockSpec((DN_HEADS, HALO, HEAD_W),
                            lambda b, t: (g, jnp.maximum((b * nt + t) * hb - 1, 0), 0))

    full = lambda shape: pl.BlockSpec(shape, lambda b, t: (0,) * len(shape))
    out_shape = (jax.ShapeDtypeStruct((DN_HEADS, m, HEAD_W), BF16),
                 jax.ShapeDtypeStruct((DN_HEADS, 2 * m, HEAD_W), BF16),
                 jax.ShapeDtypeStruct((DN_HEADS, m, HEAD_W), BF16),
                 jax.ShapeDtypeStruct((DN_HEADS, m, sub), BF16),
                 jax.ShapeDtypeStruct((m, LANES), F32)) + tuple(cast_shape)
    out_specs = (pl.BlockSpec((DN_HEADS, rows, HEAD_W), lambda b, t: (0, b * nt + t, 0)),
                 pl.BlockSpec((DN_HEADS, 2 * rows, HEAD_W), lambda b, t: (0, b * nt + t, 0)),
                 pl.BlockSpec((DN_HEADS, rows, HEAD_W), lambda b, t: (0, b * nt + t, 0)),
                 pl.BlockSpec((DN_HEADS, rows, sub), lambda b, t: (0, b * nt + t, 0)),
                 pl.BlockSpec((rows, LANES), lambda b, t: (b * nt + t, 0))) + tuple(cast_out)
    outs = pl.pallas_call(
        functools.partial(_gdn_prep_kernel, rows=rows, sub=sub, n_pad=n_pad, n_cast=len(cast)),
        out_shape=out_shape,
        grid=(batch, nt),
        in_specs=[main_spec(g_q), main_spec(g_k), main_spec(g_v),
                  halo_spec(g_q), halo_spec(g_k), halo_spec(g_v),
                  full(meta_halo.shape), full(conv_w.shape),
                  pl.BlockSpec((rows, LANES), lambda b, t: (b * nt + t, 0)),
                  full(alog.shape), full(dtb.shape)] + cast_in,
        out_specs=out_specs,
        scratch_shapes=[pltpu.VMEM((3, rows + HALO, HEAD_W), F32), pltpu.VMEM((rows, LANES), F32),
                        pltpu.VMEM((DN_HEADS * (rows // sub), sub, sub), BF16),
                        pltpu.VMEM((DN_HEADS * (rows // sub), sub, 2 * HEAD_W), BF16),
                        pltpu.VMEM((DN_HEADS * (rows // sub), sub, sub), F32)],
        compiler_params=_params("parallel", "parallel"),
        name="gdn_prep",
    )(p, p, p, p_halo_src, p_halo_src, p_halo_src, meta_halo, conv_w, ba, alog, dtb, *cast)
    return outs[:5], outs[5:]


def _gdn_scan_kernel(u_ref, wq_ref, kd_ref, a_ref, gl_ref, z_ref,
                     um_ref, wqm_ref, kdm_ref, am_ref, glm_ref, zm_ref, nw_ref,
                     o_ref, om_ref, s_sc, vbuf_sc, sw_sc, *, rows, sub, hpg):
    hg = pl.program_id(1)
    t = pl.program_id(2)
    lane = lax.broadcasted_iota(jnp.int32, (1, LANES), 1)
    nw = nw_ref[...]

    def decay_total(gl_row, h):
        sel = lane == DN_HEADS + hg * hpg + h
        return jnp.sum(jnp.where(sel, jnp.exp(gl_row), 0.0), axis=1, keepdims=True)

    def gate(o, z):
        ms = jnp.mean(o * o, axis=1, keepdims=True)
        return (o * lax.rsqrt(ms + RMS_EPS) * nw * _silu(z.astype(F32))).astype(BF16)

    def run_chunk(r0, refs, out_ref, out_r0):
        u_r, wq_r, kd_r, a_r, gl_r, z_r = refs
        slot = r0 % sub
        gl_row = gl_r[r0:r0 + 1, :]
        for h in range(hpg):
            sw_sc[h] = jnp.dot(wq_r[h, 2 * r0:2 * r0 + 2 * CHUNK, :], s_sc[h].astype(BF16),
                               preferred_element_type=F32)
        for h in range(hpg):
            v_new = u_r[h, r0:r0 + CHUNK, :].astype(F32) - sw_sc[h, 0:CHUNK, :]
            vbuf_sc[h, slot:slot + CHUNK, :] = v_new.astype(BF16)
        for h in range(hpg):
            o = sw_sc[h, CHUNK:, :] + jnp.dot(a_r[h, r0:r0 + CHUNK, :], vbuf_sc[h],
                                              preferred_element_type=F32)
            s_sc[h] = decay_total(gl_row, h) * s_sc[h] + lax.dot_general(
                kd_r[h, r0:r0 + CHUNK, :], vbuf_sc[h, slot:slot + CHUNK, :], _TN,
                preferred_element_type=F32)
            out_ref[out_r0:out_r0 + CHUNK, h * HEAD_W:(h + 1) * HEAD_W] = gate(
                o, z_r[h, r0:r0 + CHUNK, :])

    @pl.when(t == 0)
    def _meta_chunk():
        vbuf_sc[...] = jnp.zeros_like(vbuf_sc)
        s_sc[...] = jnp.zeros_like(s_sc)
        run_chunk(sub - CHUNK, (um_ref, wqm_ref, kdm_ref, am_ref, glm_ref, zm_ref), om_ref, 0)

    for c in range(rows // CHUNK):
        run_chunk(c * CHUNK, (u_ref, wq_ref, kd_ref, a_ref, gl_ref, z_ref), o_ref, c * CHUNK)


def _gdn_scan(real, meta, p, p_meta_pad, norm_w, *, batch, seq, rows, sub, hpg):
    u, wq, kd, a, gl = real
    um, wqm, kdm, am, glm = meta
    m = batch * seq
    nt = seq // rows
    ngrp = DN_HEADS // hpg
    z_blk = (6 * DN_HEADS) // hpg
    mrows = um.shape[1]
    assert mrows == sub

    def real_spec(r, w):
        return pl.BlockSpec((hpg, r, w), lambda b, g, t: (g, b * nt + t, 0))

    def meta_spec(r, w):
        return pl.BlockSpec((hpg, r, w), lambda b, g, t: (g, 0, 0))

    return pl.pallas_call(
        functools.partial(_gdn_scan_kernel, rows=rows, sub=sub, hpg=hpg),
        out_shape=(jax.ShapeDtypeStruct((m, GROUP_W), BF16),
                   jax.ShapeDtypeStruct((batch, CHUNK, GROUP_W), BF16)),
        grid=(batch, ngrp, nt),
        in_specs=[real_spec(rows, HEAD_W), real_spec(2 * rows, HEAD_W), real_spec(rows, HEAD_W),
                  real_spec(rows, sub),
                  pl.BlockSpec((rows, LANES), lambda b, g, t: (b * nt + t, 0)),
                  pl.BlockSpec((hpg, rows, HEAD_W), lambda b, g, t: (z_blk + g, b * nt + t, 0)),
                  meta_spec(mrows, HEAD_W), meta_spec(2 * mrows, HEAD_W), meta_spec(mrows, HEAD_W),
                  meta_spec(mrows, mrows),
                  pl.BlockSpec((mrows, LANES), lambda b, g, t: (0, 0)),
                  pl.BlockSpec((hpg, mrows, HEAD_W), lambda b, g, t: (z_blk + g, 0, 0)),
                  pl.BlockSpec((1, HEAD_W), lambda b, g, t: (0, 0))],
        out_specs=(pl.BlockSpec((rows, hpg * HEAD_W), lambda b, g, t: (b * nt + t, g)),
                   pl.BlockSpec((None, CHUNK, hpg * HEAD_W), lambda b, g, t: (b, 0, g))),
        scratch_shapes=[pltpu.VMEM((hpg, HEAD_W, HEAD_W), F32),
                        pltpu.VMEM((hpg, sub, HEAD_W), BF16),
                        pltpu.VMEM((hpg, 2 * CHUNK, HEAD_W), F32)],
        compiler_params=_params("parallel", "parallel", "arbitrary"),
        name="gdn_scan",
    )(u, wq, kd, a, gl, p, um, wqm, kdm, am, glm, p_meta_pad, norm_w)


def _wout_ln_kernel(oa_ref, od_ref, w_ref, x_ref, g_ref, b_ref, h_ref, *, alpha):
    half = oa_ref.shape[1]
    mix = (jnp.dot(oa_ref[...], w_ref[0:half, :], preferred_element_type=F32)
           + jnp.dot(od_ref[...], w_ref[half:, :], preferred_element_type=F32))
    h_ref[...] = _layer_norm(alpha * x_ref[...] + mix, g_ref[...], b_ref[...])


def _wout_ln(o_att, o_dn, w_out, x2d, g, b, *, bm, alpha):
    m, d = x2d.shape
    half = o_att.shape[1]
    return pl.pallas_call(
        functools.partial(_wout_ln_kernel, alpha=alpha),
        out_shape=jax.ShapeDtypeStruct((m, d), F32),
        grid=(m // bm,),
        in_specs=[pl.BlockSpec((bm, half), lambda i: (i, 0)),
                  pl.BlockSpec((bm, half), lambda i: (i, 0)),
                  pl.BlockSpec((2 * half, d), lambda i: (0, 0)),
                  pl.BlockSpec((bm, d), lambda i: (i, 0)),
                  pl.BlockSpec((1, d), lambda i: (0, 0)),
                  pl.BlockSpec((1, d), lambda i: (0, 0))],
        out_specs=pl.BlockSpec((bm, d), lambda i: (i, 0)),
        compiler_params=_params("parallel"),
        name="wout_ln",
    )(o_att, o_dn, w_out, x2d, g, b)


def _ffn_kernel(h_ref, hprev_ref, hmeta_ref, wg_ref, wu_ref, cw_ref, cb_ref, wd_ref, g_ref, b_ref,
                o_ref, xb_sc, gate_sc, *, bm, tiles_per_batch, alpha):
    i = pl.program_id(0)
    f = pl.program_id(1)

    @pl.when(f == 0)
    def _init():
        halo = jnp.where(i % tiles_per_batch == 0, hmeta_ref[...], hprev_ref[...])
        xb_sc[0:HALO, :] = halo.astype(BF16)
        xb_sc[HALO:, :] = h_ref[...].astype(BF16)
        o_ref[...] = jnp.zeros_like(o_ref)

    gate_sc[...] = jnp.dot(xb_sc[...], wg_ref[...], preferred_element_type=F32)
    up = jnp.dot(xb_sc[HALO:, :], wu_ref[...], preferred_element_type=F32)
    cw = cw_ref[...]
    hg = cb_ref[...] + gate_sc[pl.ds(HALO - FFN_CONV + 1, bm), :] * cw[0:1]
    for j in range(1, FFN_CONV):
        hg = hg + gate_sc[pl.ds(HALO - FFN_CONV + 1 + j, bm), :] * cw[j:j + 1]
    act = (_silu(hg) * up).astype(BF16)
    o_ref[...] += jnp.dot(act, wd_ref[...], preferred_element_type=F32)

    @pl.when(f == pl.num_programs(1) - 1)
    def _finish():
        o_ref[...] = _layer_norm(alpha * h_ref[...] + o_ref[...], g_ref[...], b_ref[...])


def _ffn(h1, h1_meta, w_gate, w_up, conv_w, conv_b, w_down, g, b, *, seq, bm, tf, alpha):
    m, d = h1.shape
    dff = w_gate.shape[1]
    tiles_per_batch = seq // bm
    hb = bm // HALO
    meta_blk = h1_meta.shape[0] // HALO - 1
    return pl.pallas_call(
        functools.partial(_ffn_kernel, bm=bm, tiles_per_batch=tiles_per_batch, alpha=alpha),
        out_shape=jax.ShapeDtypeStruct((m, d), F32),
        grid=(m // bm, dff // tf),
        in_specs=[pl.BlockSpec((bm, d), lambda i, f: (i, 0)),
                  pl.BlockSpec((HALO, d), lambda i, f: (jnp.maximum(i * hb - 1, 0), 0)),
                  pl.BlockSpec((HALO, d), lambda i, f: (meta_blk, 0)),
                  pl.BlockSpec((d, tf), lambda i, f: (0, f)),
                  pl.BlockSpec((d, tf), lambda i, f: (0, f)),
                  pl.BlockSpec((FFN_CONV, tf), lambda i, f: (0, f)),
                  pl.BlockSpec((1, tf), lambda i, f: (0, f)),
                  pl.BlockSpec((tf, d), lambda i, f: (f, 0)),
                  pl.BlockSpec((1, d), lambda i, f: (0, 0)),
                  pl.BlockSpec((1, d), lambda i, f: (0, 0))],
        out_specs=pl.BlockSpec((bm, d), lambda i, f: (i, 0)),
        scratch_shapes=[pltpu.VMEM((bm + HALO, d), BF16),
                        pltpu.VMEM((bm + HALO, tf), F32)],
        compiler_params=_params("parallel", "arbitrary"),
        name="ffn",
    )(h1, h1, h1_meta, w_gate, w_up, conv_w, conv_b, w_down, g, b)


def _largest_tile(n, cap, mult):
    best = None
    for c in range(mult, min(n, cap) + 1, mult):
        if n % c == 0:
            best = c
    assert best is not None, (n, cap, mult)
    return best


def _split_halves(w_qk_t):
    n_comp, quarter = HEAD_W // ATT_QK_DIM, ATT_QK_DIM // 2
    depth, n, d = w_qk_t.shape
    w = w_qk_t.reshape(depth, -1, n_comp, 2, quarter, d)
    return jnp.swapaxes(w, 2, 3).reshape(depth, n, d)


def _rope_tables(n_pos):
    n_comp = HEAD_W // ATT_QK_DIM
    inv_freq = ROPE_THETA ** (-jnp.arange(0, ATT_QK_DIM, 2, dtype=F32) / ATT_QK_DIM)
    ang = jnp.arange(n_pos, dtype=F32)[:, None] * inv_freq[None, :]
    cos, sin = jnp.cos(ang), jnp.sin(ang)
    cos_t = jnp.tile(cos, (1, 2 * n_comp))
    sin_t = jnp.concatenate([jnp.tile(-sin, (1, n_comp)), jnp.tile(sin, (1, n_comp))], axis=1)
    assert cos_t.shape[1] == HEAD_W
    return cos_t, sin_t


def kernel(x, meta_tokens, w_in, conv_qkv_w, a_log, dt_bias, lambda_q1, lambda_k1, lambda_q2, lambda_k2,
           diff_norm_w, delta_norm_w, w_out, ln1_g, ln1_b, ffn_w_gate, ffn_w_up, ffn_conv_w, ffn_conv_b,
           ffn_w_down, ln2_g, ln2_b):
    batch, seq, d = x.shape
    n_meta = meta_tokens.shape[0]
    depth = w_in.shape[0]
    n_main = N_GROUPS * GROUP_W
    assert w_in.shape[2] == n_main + TAIL_COLS and n_meta % HALO == 0 and n_meta <= CHUNK
    m = batch * seq
    alpha = (2.0 * depth) ** 0.25

    bm_proj = _largest_tile(seq, 1024, HALO)
    bm_row = _largest_tile(seq, 512, HALO)
    tq = _largest_tile(seq, 256, CHUNK)
    rows = _largest_tile(seq, 256, LANES)
    sub = LANES
    hpg = DN_HEADS
    att_hps = ATT_HEADS

    cos_t, sin_t = _rope_tables(n_meta + seq)
    cos_meta, sin_meta = cos_t[:n_meta], sin_t[:n_meta]
    cos_real, sin_real = cos_t[n_meta:], sin_t[n_meta:]

    h = x.reshape(m, d)
    h_meta = meta_tokens.astype(x.dtype)
    n_qk = 2 * GROUP_W
    w_in_t = jnp.swapaxes(w_in, 1, 2)
    w_t = w_in_t.astype(BF16)
    wqk_t = _split_halves(w_t[:, :n_qk])
    pad_rows = sub - n_meta
    for l in range(depth):
        w_tail_t = jnp.pad(w_t[l, n_main:], ((0, LANES - TAIL_COLS), (0, 0)))
        lane_pad = (DN_HEADS, LANES - TAIL_COLS)
        alog = jnp.pad(a_log[l].astype(F32), lane_pad)[None]
        dtb = jnp.pad(dt_bias[l].astype(F32), lane_pad)[None]
        conv_w = conv_qkv_w[l].reshape(DN_CONV, 3 * DN_HEADS, HEAD_W).transpose(1, 0, 2)
        lams = tuple(v[l][None].astype(F32) for v in (lambda_q1, lambda_k1, lambda_q2, lambda_k2))
        lam_init = 0.8 - 0.6 * math.exp(-0.3 * l)

        p, p_meta, ba, ba_meta = _proj(h, h_meta, wqk_t, w_t, l, w_tail_t, cos_real, sin_real,
                                       cos_meta, sin_meta, bm_proj)

        att_nw = diff_norm_w[l].astype(F32)[:, None]
        o_att = _attention(p, p, p_meta, lams, att_nw, batch=batch, seq=seq, tq=tq, hps=att_hps,
                           lam_init=lam_init, meta_only=False)
        q_meta = jnp.pad(p_meta[:ATT_HEADS], ((0, 0), (0, LANES - n_meta), (0, 0)))
        o_att_meta = _attention(q_meta, None, p_meta, lams, att_nw, batch=1, seq=LANES, tq=LANES,
                                hps=att_hps, lam_init=lam_init, meta_only=True)[:n_meta]

        p_meta_pad = jnp.pad(p_meta, ((0, 0), (pad_rows, 0), (0, 0)))
        ba_meta_pad = jnp.pad(ba_meta, ((pad_rows, 0), (0, 0)))
        meta_halo = p_meta[3 * DN_HEADS:6 * DN_HEADS, n_meta - HALO:, :]
        zero_halo = jnp.zeros_like(meta_halo)
        prep_real, (w_out_b, w_gate_b, w_up_b, w_down_b) = _gdn_prep(
            p, p, meta_halo, conv_w, ba, alog, dtb, batch=batch, seq=seq, rows=rows, sub=sub, n_pad=0,
            cast=(w_out, ffn_w_gate, ffn_w_up, ffn_w_down), layer=l)
        prep_meta, _ = _gdn_prep(p_meta_pad, p_meta_pad, zero_halo, conv_w, ba_meta_pad, alog, dtb, batch=1,
                                 seq=sub, rows=sub, sub=sub, n_pad=pad_rows)
        o_dn, o_dn_meta = _gdn_scan(prep_real, prep_meta, p, p_meta_pad, delta_norm_w[l][None],
                                    batch=batch, seq=seq, rows=rows, sub=sub, hpg=hpg)
        o_dn_meta = o_dn_meta[0, CHUNK - n_meta:]

        g1, b1 = ln1_g[l][None].astype(F32), ln1_b[l][None].astype(F32)
        h1 = _wout_ln(o_att, o_dn, w_out_b, h, g1, b1, bm=bm_row, alpha=alpha)
        h1_meta = _wout_ln(o_att_meta, o_dn_meta, w_out_b, h_meta, g1, b1, bm=n_meta, alpha=alpha)

        dff = ffn_w_gate.shape[2]
        tf = _largest_tile(dff, 512, LANES)
        ffn_args = (w_gate_b, w_up_b, ffn_conv_w[l].astype(F32),
                    ffn_conv_b[l][None].astype(F32), w_down_b,
                    ln2_g[l][None].astype(F32), ln2_b[l][None].astype(F32))
        h_next = _ffn(h1, h1_meta, *ffn_args, seq=seq, bm=bm_row, tf=tf, alpha=alpha)
        if l + 1 < depth:
            zero_meta = jnp.zeros_like(h1_meta)
            h_meta = _ffn(h1_meta, zero_meta, *ffn_args, seq=n_meta, bm=n_meta, tf=tf, alpha=alpha)
        h = h_next
    return h.reshape(batch, seq, d)
```

```python
import functools
import math

import jax
import jax.numpy as jnp
from jax import lax
from jax.experimental import pallas as pl
from jax.experimental.pallas import tpu as pltpu

F32 = jnp.float32
BF16 = jnp.bfloat16

CHUNK = 64
ATT_HEADS = 8
ATT_QK_DIM = 64
HEAD_W = 128
DN_HEADS = 8
DN_CONV = 4
FFN_CONV = 3
ROPE_THETA = 10000.0
LN_EPS = 1e-5
RMS_EPS = 1e-6
GROUP_W = ATT_HEADS * HEAD_W
Q_SCALE = ATT_QK_DIM ** -0.5 * math.log2(math.e)
N_GROUPS = 7
TAIL_COLS = 2 * DN_HEADS

LANES = 128
HALO = 8
VMEM_LIMIT = 56 * 1024 * 1024

_NT = (((1,), (1,)), ((), ()))
_TN = (((0,), (0,)), ((), ()))


def _params(*sem):
    return pltpu.CompilerParams(dimension_semantics=sem, vmem_limit_bytes=VMEM_LIMIT)


def _sigmoid(x):
    return 1.0 / (1.0 + jnp.exp(-x))


def _silu(x):
    return x * _sigmoid(x)


def _softplus(x):
    return jnp.maximum(x, 0.0) + jnp.log(1.0 + jnp.exp(-jnp.abs(x)))


def _layer_norm(y, g, b):
    mu = jnp.mean(y, axis=1, keepdims=True)
    d = y - mu
    var = jnp.mean(d * d, axis=1, keepdims=True)
    return d * lax.rsqrt(var + LN_EPS) * g + b


def _proj_kernel(x_ref, xm_ref, wqk_ref, w_ref, wt_ref, cos_ref, sin_ref, cosm_ref, sinm_ref,
                 p_ref, pm_ref, ba_ref, bam_ref, *, heads, n_qk_tiles):
    j = pl.program_id(0)
    i = pl.program_id(1)

    def tail(x_r):
        return lax.dot_general(x_r[...].astype(BF16), wt_ref[...], _NT, preferred_element_type=F32)

    def project(xh, wsel_ref, rope_tables, out_ref):
        acc = lax.dot_general(xh, wsel_ref[...], _NT, preferred_element_type=F32)
        if rope_tables is not None:
            scale = jnp.where(j == 0, Q_SCALE, 1.0).astype(F32)
            cos = rope_tables[0][...] * scale
            sin = rope_tables[1][...] * scale
        for h in range(heads):
            blk = acc[:, h * HEAD_W:(h + 1) * HEAD_W]
            if rope_tables is not None:
                blk = blk * cos + pltpu.roll(blk, HEAD_W // 2, 1) * sin
            out_ref[h] = blk.astype(BF16)

    def tile(wsel_ref, rope):
        @pl.when(i == 0)
        def _meta_rows():
            project(xm_ref[...].astype(BF16), wsel_ref, (cosm_ref, sinm_ref) if rope else None, pm_ref)

        project(x_ref[...].astype(BF16), wsel_ref, (cos_ref, sin_ref) if rope else None, p_ref)

    @pl.when(j < n_qk_tiles)
    def _qk_tile():
        tile(wqk_ref, True)

    @pl.when(j >= n_qk_tiles)
    def _plain_tile():
        tile(w_ref, False)

    @pl.when(j == 0)
    def _tail():
        ba_ref[...] = tail(x_ref)

        @pl.when(i == 0)
        def _meta_tail():
            bam_ref[...] = tail(xm_ref)


def _proj(x2d, x_meta, wqk_t, w_t, layer, w_tail_t, cos, sin, cos_meta, sin_meta, bm):
    m, d = x2d.shape
    n_meta = x_meta.shape[0]
    heads = GROUP_W // HEAD_W
    n_blocks = N_GROUPS * heads
    nm = m // bm
    n_qk_tiles = wqk_t.shape[1] // GROUP_W
    const = lambda shape: pl.BlockSpec(shape, lambda j, i: (0,) * len(shape))
    row = lambda j, i: (i, 0)
    tiles_per_batch = cos.shape[0] // bm
    pos = lambda j, i: (i % tiles_per_batch, 0)
    return pl.pallas_call(
        functools.partial(_proj_kernel, heads=heads, n_qk_tiles=n_qk_tiles),
        out_shape=(jax.ShapeDtypeStruct((n_blocks, m, HEAD_W), BF16),
                   jax.ShapeDtypeStruct((n_blocks, n_meta, HEAD_W), BF16),
                   jax.ShapeDtypeStruct((m, LANES), F32),
                   jax.ShapeDtypeStruct((n_meta, LANES), F32)),
        grid=(N_GROUPS, nm),
        in_specs=[
            pl.BlockSpec((bm, d), row),
            const((n_meta, d)),
            pl.BlockSpec((None, GROUP_W, d), lambda j, i: (layer, jnp.minimum(j, n_qk_tiles - 1), 0)),
            pl.BlockSpec((None, GROUP_W, d), lambda j, i: (layer, jnp.maximum(j, n_qk_tiles), 0)),
            const((LANES, d)),
            pl.BlockSpec((bm, HEAD_W), pos),
            pl.BlockSpec((bm, HEAD_W), pos),
            const((n_meta, HEAD_W)),
            const((n_meta, HEAD_W)),
        ],
        out_specs=(
            pl.BlockSpec((heads, bm, HEAD_W), lambda j, i: (j, i, 0)),
            pl.BlockSpec((heads, n_meta, HEAD_W), lambda j, i: (j, 0, 0)),
            pl.BlockSpec((bm, LANES), lambda j, i: (jnp.where(j == 0, i, nm - 1), 0)),
            const((n_meta, LANES)),
        ),
        compiler_params=_params("arbitrary", "arbitrary"),
        name="proj",
    )(x2d, x_meta, wqk_t, w_t, w_tail_t, cos, sin, cos_meta, sin_meta)


def _attn_kernel(*refs, tq, hps, lam_init, meta_only):
    if meta_only:
        q_ref, km_ref, vm_ref, lq1, lk1, lq2, lk2, nw_ref, o_ref, m_sc, l_sc, acc_sc = refs
    else:
        (q_ref, k_ref, v_ref, km_ref, vm_ref, lq1, lk1, lq2, lk2, nw_ref, o_ref,
         m_sc, l_sc, acc_sc, s_sc, a_sc, p_sc) = refs
    lane = lax.broadcasted_iota(jnp.int32, (1, HEAD_W), 1)
    comp = (lane // (ATT_QK_DIM // 2)) % 2
    qc = []
    for h in range(hps):
        q = q_ref[h]
        zero = jnp.zeros_like(q)
        qc.append((jnp.where(comp == 0, q, zero), jnp.where(comp == 1, q, zero)))

    nc = 2 * hps

    def init_from_meta():
        s_meta = [lax.dot_general(km_ref[i // 2], qc[i // 2][i % 2], _NT, preferred_element_type=F32)
                  for i in range(nc)]
        p_meta = []
        for i in range(nc):
            m0 = jnp.max(s_meta[i], axis=0, keepdims=True)
            p = jnp.exp2(s_meta[i] - m0)
            m_sc[i] = m0
            l_sc[i] = jnp.sum(p, axis=0, keepdims=True)
            p_meta.append(p.astype(BF16))
        for i in range(nc):
            acc_sc[i] = lax.dot_general(vm_ref[i // 2], p_meta[i], _TN, preferred_element_type=F32)

    def scores(t, slot):
        off = pl.multiple_of(t * tq, tq)
        for h in range(hps):
            kt = k_ref[h, pl.ds(off, tq), :]
            for c in range(2):
                s_sc[slot * nc + 2 * h + c] = lax.dot_general(kt, qc[h][c], _NT,
                                                              preferred_element_type=F32)

    def absorb(t, slot, mask):
        off = pl.multiple_of(t * tq, tq)
        for i in range(nc):
            s = s_sc[slot * nc + i]
            if mask is not None:
                s = jnp.where(mask, s, -jnp.inf)
            m_old = m_sc[i]
            m_new = jnp.maximum(m_old, jnp.max(s, axis=0, keepdims=True))
            a = jnp.exp2(m_old - m_new)
            p = jnp.exp2(s - m_new)
            l_sc[i] = a * l_sc[i] + jnp.sum(p, axis=0, keepdims=True)
            m_sc[i] = m_new
            a_sc[i] = a
            p_sc[i] = p.astype(BF16)
        for h in range(hps):
            vt = v_ref[h, pl.ds(off, tq), :]
            for c in range(2):
                i = 2 * h + c
                acc_sc[i] = a_sc[i] * acc_sc[i] + lax.dot_general(vt, p_sc[i], _TN,
                                                                  preferred_element_type=F32)

    if meta_only:
        init_from_meta()
    else:
        qi = pl.program_id(2)
        keys = lax.broadcasted_iota(jnp.int32, (tq, tq), 0)
        queries = lax.broadcasted_iota(jnp.int32, (tq, tq), 1)
        diag = (keys // CHUNK) <= (queries // CHUNK)
        scores(0, 0)
        init_from_meta()

        def body(k, carry):
            scores(2 * k + 1, 1)
            absorb(2 * k, 0, None)
            scores(2 * k + 2, 0)
            absorb(2 * k + 1, 1, None)
            return carry

        lax.fori_loop(0, qi // 2, body, 0)

        @pl.when(qi % 2 == 1)
        def _odd_tail():
            scores(qi, 1)
            absorb(qi - 1, 0, None)
            absorb(qi, 1, diag)

        @pl.when(qi % 2 == 0)
        def _even_tail():
            absorb(qi, 0, diag)

    lam = (jnp.exp(jnp.sum(lq1[...] * lk1[...], axis=1, keepdims=True))
           - jnp.exp(jnp.sum(lq2[...] * lk2[...], axis=1, keepdims=True)) + lam_init)
    for h in range(hps):
        o = acc_sc[2 * h] / l_sc[2 * h] - lam * (acc_sc[2 * h + 1] / l_sc[2 * h + 1])
        ms = jnp.mean(o * o, axis=0, keepdims=True)
        o = o * lax.rsqrt(ms + RMS_EPS) * nw_ref[...] * (1.0 - lam_init)
        o_ref[:, h * HEAD_W:(h + 1) * HEAD_W] = o.T.astype(BF16)


def _attention(p_q, p_kv, p_meta, lams, norm_w, *, batch, seq, tq, hps, lam_init, meta_only):
    n_meta = p_meta.shape[1]
    rows = p_q.shape[1]
    nq = 1 if meta_only else seq // tq
    k_blk, v_blk = ATT_HEADS // hps, 2 * ATT_HEADS // hps
    small = [pl.BlockSpec((1, ATT_QK_DIM), lambda b, g, i: (0, 0))] * 4 + [
        pl.BlockSpec((HEAD_W, 1), lambda b, g, i: (0, 0))]
    meta_specs = [pl.BlockSpec((hps, n_meta, HEAD_W), lambda b, g, i: (k_blk + g, 0, 0)),
                  pl.BlockSpec((hps, n_meta, HEAD_W), lambda b, g, i: (v_blk + g, 0, 0))]
    q_spec = pl.BlockSpec((hps, tq, HEAD_W), lambda b, g, i: (g, b * nq + i, 0))
    if meta_only:
        in_specs = [q_spec] + meta_specs + small
        args = (p_q, p_meta, p_meta) + lams + (norm_w,)
    else:
        kv_specs = [pl.BlockSpec((hps, seq, HEAD_W), lambda b, g, i: (k_blk + g, b, 0)),
                    pl.BlockSpec((hps, seq, HEAD_W), lambda b, g, i: (v_blk + g, b, 0))]
        in_specs = [q_spec] + kv_specs + meta_specs + small
        args = (p_q, p_kv, p_kv, p_meta, p_meta) + lams + (norm_w,)
    scratch = [pltpu.VMEM((2 * hps, 1, tq), F32), pltpu.VMEM((2 * hps, 1, tq), F32),
               pltpu.VMEM((2 * hps, HEAD_W, tq), F32)]
    if not meta_only:
        scratch += [pltpu.VMEM((2 * 2 * hps, tq, tq), F32),
                    pltpu.VMEM((2 * hps, 1, tq), F32),
                    pltpu.VMEM((2 * hps, tq, tq), BF16)]
    return pl.pallas_call(
        functools.partial(_attn_kernel, tq=tq, hps=hps, lam_init=lam_init, meta_only=meta_only),
        out_shape=jax.ShapeDtypeStruct((rows, GROUP_W), BF16),
        grid=(batch, ATT_HEADS // hps, nq),
        in_specs=in_specs,
        out_specs=pl.BlockSpec((tq, hps * HEAD_W), lambda b, g, i: (b * nq + i, g)),
        scratch_shapes=scratch,
        compiler_params=_params("parallel", "parallel", "arbitrary"),
        name="attn_meta" if meta_only else "attn",
    )(*args)


def _gdn_prep_kernel(*refs, rows, sub, n_pad, n_cast):
    (q_ref, k_ref, v_ref, hq_ref, hk_ref, hv_ref, mh_ref, cw_ref, ba_ref, alog_ref, dtb_ref) = refs[:11]
    cast_src = refs[11:11 + n_cast]
    u_ref, wq_ref, kd_ref, a_ref, gl_ref = refs[11 + n_cast:16 + n_cast]
    cast_dst = refs[16 + n_cast:16 + 2 * n_cast]
    xs_sc, gc_sc, pw_sc, x_sc, t_sc = refs[16 + 2 * n_cast:]
    for src, dst in zip(cast_src, cast_dst):
        dst[...] = src[...].astype(BF16)

    first = pl.program_id(1) == 0
    n_sub = rows // sub
    ba = ba_ref[...]
    beta_all = _sigmoid(ba)
    g_all = -jnp.exp(alog_ref[...]) * _softplus(ba + dtb_ref[...])
    row = lax.broadcasted_iota(jnp.int32, (rows, 1), 0)
    if n_pad:
        beta_all = jnp.where(row >= n_pad, beta_all, 0.0)
        g_all = jnp.where(row >= n_pad, g_all, 0.0)

    pos = row % CHUNK
    gc_all = g_all
    shift = 1
    while shift < CHUNK:
        gc_all = gc_all + jnp.where(pos >= shift, pltpu.roll(gc_all, shift, 0), 0.0)
        shift *= 2
    gc_sc[...] = gc_all
    gl_all = jnp.concatenate(
        [jnp.broadcast_to(gc_sc[(c + 1) * CHUNK - 1:(c + 1) * CHUNK, :], (CHUNK, LANES))
         for c in range(rows // CHUNK)], axis=0)
    gl_ref[...] = gl_all
    gct_all = gc_all.T

    ri = lax.broadcasted_iota(jnp.int32, (sub, sub), 0)
    ci = lax.broadcasted_iota(jnp.int32, (sub, sub), 1)
    same = (ri // CHUNK) == (ci // CHUNK)
    tril = jnp.logical_and(same, ci <= ri)
    strict = jnp.logical_and(same, ci < ri)
    eye = jnp.where(ri == ci, 1.0, 0.0).astype(F32)

    def conv_act(main_ref, halo_ref, stream, h):
        halo = jnp.where(first, mh_ref[stream * DN_HEADS + h], halo_ref[h])
        xs_sc[stream, 0:HALO, :] = halo.astype(F32)
        xs_sc[stream, HALO:, :] = main_ref[h].astype(F32)
        w = cw_ref[stream * DN_HEADS + h]
        y = xs_sc[stream, pl.ds(HALO - DN_CONV + 1, rows), :] * w[0:1]
        for j in range(1, DN_CONV):
            y = y + xs_sc[stream, pl.ds(HALO - DN_CONV + 1 + j, rows), :] * w[j:j + 1]
        return _silu(y)

    for h in range(DN_HEADS):
        q = conv_act(q_ref, hq_ref, 0, h)
        k = conv_act(k_ref, hk_ref, 1, h)
        v = conv_act(v_ref, hv_ref, 2, h)
        q = q * lax.rsqrt(jnp.sum(q * q, axis=1, keepdims=True) + RMS_EPS) * (HEAD_W ** -0.5)
        k = k * lax.rsqrt(jnp.sum(k * k, axis=1, keepdims=True) + RMS_EPS)
        beta = beta_all[:, h:h + 1]
        gc = gc_all[:, DN_HEADS + h:DN_HEADS + h + 1]
        gl = gl_all[:, DN_HEADS + h:DN_HEADS + h + 1]
        egc = jnp.exp(gc)
        kb = k * beta
        k16 = k.astype(BF16)
        kb16 = kb.astype(BF16)
        q16 = q.astype(BF16)
        x_all = jnp.concatenate([v * beta, kb * egc], axis=1)
        qd16 = (q * egc).astype(BF16)
        kd_ref[h] = (k * jnp.exp(gl - gc)).astype(BF16)
        for c in range(rows // CHUNK):
            wq_ref[h, (2 * c + 1) * CHUNK:(2 * c + 2) * CHUNK, :] = qd16[c * CHUNK:(c + 1) * CHUNK]
        for s in range(n_sub):
            lo, hi = s * sub, (s + 1) * sub
            gcr = gct_all[DN_HEADS + h:DN_HEADS + h + 1, lo:hi]
            kk = lax.dot_general(kb16[lo:hi], k16[lo:hi], _NT, preferred_element_type=F32)
            qk = lax.dot_general(q16[lo:hi], k16[lo:hi], _NT, preferred_element_type=F32)
            dec = jnp.exp(jnp.where(tril, gc[lo:hi] - gcr, -jnp.inf))
            a_ref[h, lo:hi, :] = (qk * dec).astype(BF16)
            m0 = jnp.where(strict, kk * dec, 0.0)
            pw_sc[h * n_sub + s] = m0.astype(BF16)
            t_sc[h * n_sub + s] = eye - m0
            x_sc[h * n_sub + s] = x_all[lo:hi].astype(BF16)

    items = range(DN_HEADS * n_sub)
    for _ in range(int(math.log2(CHUNK)) - 1):
        for it in items:
            p16 = pw_sc[it]
            pw_sc[it] = jnp.dot(p16, p16, preferred_element_type=F32).astype(BF16)
        for it in items:
            t = t_sc[it]
            t_sc[it] = t + jnp.dot(pw_sc[it], t.astype(BF16), preferred_element_type=F32)
    for it in items:
        h, lo = it // n_sub, (it % n_sub) * sub
        x = jnp.dot(t_sc[it].astype(BF16), x_sc[it], preferred_element_type=F32)
        u_ref[h, lo:lo + sub, :] = x[:, :HEAD_W].astype(BF16)
        w16 = x[:, HEAD_W:].astype(BF16)
        for c in range(sub // CHUNK):
            r0 = lo + c * CHUNK
            wq_ref[h, 2 * r0:2 * r0 + CHUNK, :] = w16[c * CHUNK:(c + 1) * CHUNK]


def _gdn_prep(p, p_halo_src, meta_halo, conv_w, ba, alog, dtb, *, batch, seq, rows, sub, n_pad,
              cast=(), layer=0):
    m = batch * seq
    nt = seq // rows
    steps = batch * nt
    cast_rows = [w.shape[1] // steps for w in cast]
    assert all(w.shape[1] % steps == 0 and r % (2 * HALO) == 0 for w, r in zip(cast, cast_rows))
    cast_in = [pl.BlockSpec((None, r, w.shape[2]), lambda b, t: (layer, b * nt + t, 0))
               for w, r in zip(cast, cast_rows)]
    cast_out = [pl.BlockSpec((r, w.shape[2]), lambda b, t: (b * nt + t, 0)) for w, r in zip(cast, cast_rows)]
    cast_shape = [jax.ShapeDtypeStruct(w.shape[1:], BF16) for w in cast]
    g_q, g_k, g_v = 3, 4, 5
    hb = rows // HALO

    def main_spec(g):
        return pl.BlockSpec((DN_HEADS, rows, HEAD_W), lambda b, t: (g, b * nt + t, 0))

    def halo_spec(g):
        return pl.BlockSpec((DN_HEADS, HALO, HEAD_W),
                            lambda b, t: (g, jnp.maximum((b * nt + t) * hb - 1, 0), 0))

    full = lambda shape: pl.BlockSpec(shape, lambda b, t: (0,) * len(shape))
    out_shape = (jax.ShapeDtypeStruct((DN_HEADS, m, HEAD_W), BF16),
                 jax.ShapeDtypeStruct((DN_HEADS, 2 * m, HEAD_W), BF16),
                 jax.ShapeDtypeStruct((DN_HEADS, m, HEAD_W), BF16),
                 jax.ShapeDtypeStruct((DN_HEADS, m, sub), BF16),
                 jax.ShapeDtypeStruct((m, LANES), F32)) + tuple(cast_shape)
    out_specs = (pl.BlockSpec((DN_HEADS, rows, HEAD_W), lambda b, t: (0, b * nt + t, 0)),
                 pl.BlockSpec((DN_HEADS, 2 * rows, HEAD_W), lambda b, t: (0, b * nt + t, 0)),
                 pl.BlockSpec((DN_HEADS, rows, HEAD_W), lambda b, t: (0, b * nt + t, 0)),
                 pl.BlockSpec((DN_HEADS, rows, sub), lambda b, t: (0, b * nt + t, 0)),
                 pl.BlockSpec((rows, LANES), lambda b, t: (b * nt + t, 0))) + tuple(cast_out)
    outs = pl.pallas_call(
        functools.partial(_gdn_prep_kernel, rows=rows, sub=sub, n_pad=n_pad, n_cast=len(cast)),
        out_shape=out_shape,
        grid=(batch, nt),
        in_specs=[main_spec(g_q), main_spec(g_k), main_spec(g_v),
                  halo_spec(g_q), halo_spec(g_k), halo_spec(g_v),
                  full(meta_halo.shape), full(conv_w.shape),
                  pl.BlockSpec((rows, LANES), lambda b, t: (b * nt + t, 0)),
                  full(alog.shape), full(dtb.shape)] + cast_in,
        out_specs=out_specs,
        scratch_shapes=[pltpu.VMEM((3, rows + HALO, HEAD_W), F32), pltpu.VMEM((rows, LANES), F32),
                        pltpu.VMEM((DN_HEADS * (rows // sub), sub, sub), BF16),
                        pltpu.VMEM((DN_HEADS * (rows // sub), sub, 2 * HEAD_W), BF16),
                        pltpu.VMEM((DN_HEADS * (rows // sub), sub, sub), F32)],
        compiler_params=_params("parallel", "parallel"),
        name="gdn_prep",
    )(p, p, p, p_halo_src, p_halo_src, p_halo_src, meta_halo, conv_w, ba, alog, dtb, *cast)
    return outs[:5], outs[5:]


def _gdn_scan_kernel(u_ref, wq_ref, kd_ref, a_ref, gl_ref, z_ref,
                     um_ref, wqm_ref, kdm_ref, am_ref, glm_ref, zm_ref, nw_ref,
                     o_ref, om_ref, s_sc, vbuf_sc, sw_sc, *, rows, sub, hpg):
    hg = pl.program_id(1)
    t = pl.program_id(2)
    lane = lax.broadcasted_iota(jnp.int32, (1, LANES), 1)
    nw = nw_ref[...]

    def decay_total(gl_row, h):
        sel = lane == DN_HEADS + hg * hpg + h
        return jnp.sum(jnp.where(sel, jnp.exp(gl_row), 0.0), axis=1, keepdims=True)

    def gate(o, z):
        ms = jnp.mean(o * o, axis=1, keepdims=True)
        return (o * lax.rsqrt(ms + RMS_EPS) * nw * _silu(z.astype(F32))).astype(BF16)

    def run_chunk(r0, refs, out_ref, out_r0):
        u_r, wq_r, kd_r, a_r, gl_r, z_r = refs
        slot = r0 % sub
        gl_row = gl_r[r0:r0 + 1, :]
        for h in range(hpg):
            sw_sc[h] = jnp.dot(wq_r[h, 2 * r0:2 * r0 + 2 * CHUNK, :], s_sc[h].astype(BF16),
                               preferred_element_type=F32)
        for h in range(hpg):
            v_new = u_r[h, r0:r0 + CHUNK, :].astype(F32) - sw_sc[h, 0:CHUNK, :]
            vbuf_sc[h, slot:slot + CHUNK, :] = v_new.astype(BF16)
        for h in range(hpg):
            o = sw_sc[h, CHUNK:, :] + jnp.dot(a_r[h, r0:r0 + CHUNK, :], vbuf_sc[h],
                                              preferred_element_type=F32)
            s_sc[h] = decay_total(gl_row, h) * s_sc[h] + lax.dot_general(
                kd_r[h, r0:r0 + CHUNK, :], vbuf_sc[h, slot:slot + CHUNK, :], _TN,
                preferred_element_type=F32)
            out_ref[out_r0:out_r0 + CHUNK, h * HEAD_W:(h + 1) * HEAD_W] = gate(
                o, z_r[h, r0:r0 + CHUNK, :])

    @pl.when(t == 0)
    def _meta_chunk():
        vbuf_sc[...] = jnp.zeros_like(vbuf_sc)
        s_sc[...] = jnp.zeros_like(s_sc)
        run_chunk(sub - CHUNK, (um_ref, wqm_ref, kdm_ref, am_ref, glm_ref, zm_ref), om_ref, 0)

    for c in range(rows // CHUNK):
        run_chunk(c * CHUNK, (u_ref, wq_ref, kd_ref, a_ref, gl_ref, z_ref), o_ref, c * CHUNK)


def _gdn_scan(real, meta, p, p_meta_pad, norm_w, *, batch, seq, rows, sub, hpg):
    u, wq, kd, a, gl = real
    um, wqm, kdm, am, glm = meta
    m = batch * seq
    nt = seq // rows
    ngrp = DN_HEADS // hpg
    z_blk = (6 * DN_HEADS) // hpg
    mrows = um.shape[1]
    assert mrows == sub

    def real_spec(r, w):
        return pl.BlockSpec((hpg, r, w), lambda b, g, t: (g, b * nt + t, 0))

    def meta_spec(r, w):
        return pl.BlockSpec((hpg, r, w), lambda b, g, t: (g, 0, 0))

    return pl.pallas_call(
        functools.partial(_gdn_scan_kernel, rows=rows, sub=sub, hpg=hpg),
        out_shape=(jax.ShapeDtypeStruct((m, GROUP_W), BF16),
                   jax.ShapeDtypeStruct((batch, CHUNK, GROUP_W), BF16)),
        grid=(batch, ngrp, nt),
        in_specs=[real_spec(rows, HEAD_W), real_spec(2 * rows, HEAD_W), real_spec(rows, HEAD_W),
                  real_spec(rows, sub),
                  pl.BlockSpec((rows, LANES), lambda b, g, t: (b * nt + t, 0)),
                  pl.BlockSpec((hpg, rows, HEAD_W), lambda b, g, t: (z_blk + g, b * nt + t, 0)),
                  meta_spec(mrows, HEAD_W), meta_spec(2 * mrows, HEAD_W), meta_spec(mrows, HEAD_W),
                  meta_spec(mrows, mrows),
                  pl.BlockSpec((mrows, LANES), lambda b, g, t: (0, 0)),
                  pl.BlockSpec((hpg, mrows, HEAD_W), lambda b, g, t: (z_blk + g, 0, 0)),
                  pl.BlockSpec((1, HEAD_W), lambda b, g, t: (0, 0))],
        out_specs=(pl.BlockSpec((rows, hpg * HEAD_W), lambda b, g, t: (b * nt + t, g)),
                   pl.BlockSpec((None, CHUNK, hpg * HEAD_W), lambda b, g, t: (b, 0, g))),
        scratch_shapes=[pltpu.VMEM((hpg, HEAD_W, HEAD_W), F32),
                        pltpu.VMEM((hpg, sub, HEAD_W), BF16),
                        pltpu.VMEM((hpg, 2 * CHUNK, HEAD_W), F32)],
        compiler_params=_params("parallel", "parallel", "arbitrary"),
        name="gdn_scan",
    )(u, wq, kd, a, gl, p, um, wqm, kdm, am, glm, p_meta_pad, norm_w)


def _wout_ln_kernel(oa_ref, od_ref, w_ref, x_ref, g_ref, b_ref, h_ref, *, alpha):
    half = oa_ref.shape[1]
    mix = (jnp.dot(oa_ref[...], w_ref[0:half, :], preferred_element_type=F32)
           + jnp.dot(od_ref[...], w_ref[half:, :], preferred_element_type=F32))
    h_ref[...] = _layer_norm(alpha * x_ref[...] + mix, g_ref[...], b_ref[...])


def _wout_ln(o_att, o_dn, w_out, x2d, g, b, *, bm, alpha):
    m, d = x2d.shape
    half = o_att.shape[1]
    return pl.pallas_call(
        functools.partial(_wout_ln_kernel, alpha=alpha),
        out_shape=jax.ShapeDtypeStruct((m, d), F32),
        grid=(m // bm,),
        in_specs=[pl.BlockSpec((bm, half), lambda i: (i, 0)),
                  pl.BlockSpec((bm, half), lambda i: (i, 0)),
                  pl.BlockSpec((2 * half, d), lambda i: (0, 0)),
                  pl.BlockSpec((bm, d), lambda i: (i, 0)),
                  pl.BlockSpec((1, d), lambda i: (0, 0)),
                  pl.BlockSpec((1, d), lambda i: (0, 0))],
        out_specs=pl.BlockSpec((bm, d), lambda i: (i, 0)),
        compiler_params=_params("parallel"),
        name="wout_ln",
    )(o_att, o_dn, w_out, x2d, g, b)


def _ffn_kernel(h_ref, hprev_ref, hmeta_ref, wg_ref, wu_ref, cw_ref, cb_ref, wd_ref, g_ref, b_ref,
                o_ref, xb_sc, gate_sc, *, bm, tiles_per_batch, alpha):
    i = pl.program_id(0)
    f = pl.program_id(1)

    @pl.when(f == 0)
    def _init():
        halo = jnp.where(i % tiles_per_batch == 0, hmeta_ref[...], hprev_ref[...])
        xb_sc[0:HALO, :] = halo.astype(BF16)
        xb_sc[HALO:, :] = h_ref[...].astype(BF16)
        o_ref[...] = jnp.zeros_like(o_ref)

    gate_sc[...] = jnp.dot(xb_sc[...], wg_ref[...], preferred_element_type=F32)
    up = jnp.dot(xb_sc[HALO:, :], wu_ref[...], preferred_element_type=F32)
    cw = cw_ref[...]
    hg = cb_ref[...] + gate_sc[pl.ds(HALO - FFN_CONV + 1, bm), :] * cw[0:1]
    for j in range(1, FFN_CONV):
        hg = hg + gate_sc[pl.ds(HALO - FFN_CONV + 1 + j, bm), :] * cw[j:j + 1]
    act = (_silu(hg) * up).astype(BF16)
    o_ref[...] += jnp.dot(act, wd_ref[...], preferred_element_type=F32)

    @pl.when(f == pl.num_programs(1) - 1)
    def _finish():
        o_ref[...] = _layer_norm(alpha * h_ref[...] + o_ref[...], g_ref[...], b_ref[...])


def _ffn(h1, h1_meta, w_gate, w_up, conv_w, conv_b, w_down, g, b, *, seq, bm, tf, alpha):
    m, d = h1.shape
    dff = w_gate.shape[1]
    tiles_per_batch = seq // bm
    hb = bm // HALO
    meta_blk = h1_meta.shape[0] // HALO - 1
    ahead = pl.Buffered(3)
    pipeline = pltpu.emit_pipeline(
        functools.partial(_ffn_kernel, bm=bm, tiles_per_batch=tiles_per_batch, alpha=alpha),
        grid=(m // bm, dff // tf),
        in_specs=[pl.BlockSpec((bm, d), lambda i, f: (i, 0)),
                  pl.BlockSpec((HALO, d), lambda i, f: (jnp.maximum(i * hb - 1, 0), 0)),
                  pl.BlockSpec((HALO, d), lambda i, f: (meta_blk, 0)),
                  pl.BlockSpec((d, tf), lambda i, f: (0, f), pipeline_mode=ahead),
                  pl.BlockSpec((d, tf), lambda i, f: (0, f), pipeline_mode=ahead),
                  pl.BlockSpec((FFN_CONV, tf), lambda i, f: (0, f)),
                  pl.BlockSpec((1, tf), lambda i, f: (0, f)),
                  pl.BlockSpec((tf, d), lambda i, f: (f, 0), pipeline_mode=ahead),
                  pl.BlockSpec((1, d), lambda i, f: (0, 0)),
                  pl.BlockSpec((1, d), lambda i, f: (0, 0))],
        out_specs=[pl.BlockSpec((bm, d), lambda i, f: (i, 0))],
        dimension_semantics=(pltpu.ARBITRARY, pltpu.ARBITRARY),
    )
    n_in = 10

    def outer(*refs):
        pipeline(*refs[:n_in + 1], scratches=refs[n_in + 1:])

    return pl.pallas_call(
        outer,
        out_shape=jax.ShapeDtypeStruct((m, d), F32),
        in_specs=[pl.BlockSpec(memory_space=pl.ANY)] * n_in,
        out_specs=pl.BlockSpec(memory_space=pl.ANY),
        scratch_shapes=[pltpu.VMEM((bm + HALO, d), BF16),
                        pltpu.VMEM((bm + HALO, tf), F32)],
        compiler_params=pltpu.CompilerParams(vmem_limit_bytes=VMEM_LIMIT),
        name="ffn",
    )(h1, h1, h1_meta, w_gate, w_up, conv_w, conv_b, w_down, g, b)


def _largest_tile(n, cap, mult):
    best = None
    for c in range(mult, min(n, cap) + 1, mult):
        if n % c == 0:
            best = c
    assert best is not None, (n, cap, mult)
    return best


def _split_halves(w_qk_t):
    n_comp, quarter = HEAD_W // ATT_QK_DIM, ATT_QK_DIM // 2
    depth, n, d = w_qk_t.shape
    w = w_qk_t.reshape(depth, -1, n_comp, 2, quarter, d)
    return jnp.swapaxes(w, 2, 3).reshape(depth, n, d)


def _rope_tables(n_pos):
    n_comp = HEAD_W // ATT_QK_DIM
    inv_freq = ROPE_THETA ** (-jnp.arange(0, ATT_QK_DIM, 2, dtype=F32) / ATT_QK_DIM)
    ang = jnp.arange(n_pos, dtype=F32)[:, None] * inv_freq[None, :]
    cos, sin = jnp.cos(ang), jnp.sin(ang)
    cos_t = jnp.tile(cos, (1, 2 * n_comp))
    sin_t = jnp.concatenate([jnp.tile(-sin, (1, n_comp)), jnp.tile(sin, (1, n_comp))], axis=1)
    assert cos_t.shape[1] == HEAD_W
    return cos_t, sin_t


def kernel(x, meta_tokens, w_in, conv_qkv_w, a_log, dt_bias, lambda_q1, lambda_k1, lambda_q2, lambda_k2,
           diff_norm_w, delta_norm_w, w_out, ln1_g, ln1_b, ffn_w_gate, ffn_w_up, ffn_conv_w, ffn_conv_b,
           ffn_w_down, ln2_g, ln2_b):
    batch, seq, d = x.shape
    n_meta = meta_tokens.shape[0]
    depth = w_in.shape[0]
    n_main = N_GROUPS * GROUP_W
    assert w_in.shape[2] == n_main + TAIL_COLS and n_meta % HALO == 0 and n_meta <= CHUNK
    m = batch * seq
    alpha = (2.0 * depth) ** 0.25

    bm_proj = _largest_tile(seq, 1024, HALO)
    bm_row = _largest_tile(seq, 512, HALO)
    tq = _largest_tile(seq, 256, CHUNK)
    rows = _largest_tile(seq, 256, LANES)
    sub = LANES
    hpg = DN_HEADS
    att_hps = ATT_HEADS

    cos_t, sin_t = _rope_tables(n_meta + seq)
    cos_meta, sin_meta = cos_t[:n_meta], sin_t[:n_meta]
    cos_real, sin_real = cos_t[n_meta:], sin_t[n_meta:]

    h = x.reshape(m, d)
    h_meta = meta_tokens.astype(x.dtype)
    n_qk = 2 * GROUP_W
    w_in_t = jnp.swapaxes(w_in, 1, 2)
    w_t = w_in_t.astype(BF16)
    wqk_t = _split_halves(w_t[:, :n_qk])
    pad_rows = sub - n_meta
    for l in range(depth):
        w_tail_t = jnp.pad(w_t[l, n_main:], ((0, LANES - TAIL_COLS), (0, 0)))
        lane_pad = (DN_HEADS, LANES - TAIL_COLS)
        alog = jnp.pad(a_log[l].astype(F32), lane_pad)[None]
        dtb = jnp.pad(dt_bias[l].astype(F32), lane_pad)[None]
        conv_w = conv_qkv_w[l].reshape(DN_CONV, 3 * DN_HEADS, HEAD_W).transpose(1, 0, 2)
        lams = tuple(v[l][None].astype(F32) for v in (lambda_q1, lambda_k1, lambda_q2, lambda_k2))
        lam_init = 0.8 - 0.6 * math.exp(-0.3 * l)

        p, p_meta, ba, ba_meta = _proj(h, h_meta, wqk_t, w_t, l, w_tail_t, cos_real, sin_real,
                                       cos_meta, sin_meta, bm_proj)

        att_nw = diff_norm_w[l].astype(F32)[:, None]
        o_att = _attention(p, p, p_meta, lams, att_nw, batch=batch, seq=seq, tq=tq, hps=att_hps,
                           lam_init=lam_init, meta_only=False)
        q_meta = jnp.pad(p_meta[:ATT_HEADS], ((0, 0), (0, LANES - n_meta), (0, 0)))
        o_att_meta = _attention(q_meta, None, p_meta, lams, att_nw, batch=1, seq=LANES, tq=LANES,
                                hps=att_hps, lam_init=lam_init, meta_only=True)[:n_meta]

        p_meta_pad = jnp.pad(p_meta, ((0, 0), (pad_rows, 0), (0, 0)))
        ba_meta_pad = jnp.pad(ba_meta, ((pad_rows, 0), (0, 0)))
        meta_halo = p_meta[3 * DN_HEADS:6 * DN_HEADS, n_meta - HALO:, :]
        zero_halo = jnp.zeros_like(meta_halo)
        prep_real, (w_out_b, w_gate_b, w_up_b, w_down_b) = _gdn_prep(
            p, p, meta_halo, conv_w, ba, alog, dtb, batch=batch, seq=seq, rows=rows, sub=sub, n_pad=0,
            cast=(w_out, ffn_w_gate, ffn_w_up, ffn_w_down), layer=l)
        prep_meta, _ = _gdn_prep(p_meta_pad, p_meta_pad, zero_halo, conv_w, ba_meta_pad, alog, dtb, batch=1,
                                 seq=sub, rows=sub, sub=sub, n_pad=pad_rows)
        o_dn, o_dn_meta = _gdn_scan(prep_real, prep_meta, p, p_meta_pad, delta_norm_w[l][None],
                                    batch=batch, seq=seq, rows=rows, sub=sub, hpg=hpg)
        o_dn_meta = o_dn_meta[0, CHUNK - n_meta:]

        g1, b1 = ln1_g[l][None].astype(F32), ln1_b[l][None].astype(F32)
        h1 = _wout_ln(o_att, o_dn, w_out_b, h, g1, b1, bm=bm_row, alpha=alpha)
        h1_meta = _wout_ln(o_att_meta, o_dn_meta, w_out_b, h_meta, g1, b1, bm=n_meta, alpha=alpha)

        dff = ffn_w_gate.shape[2]
        tf = _largest_tile(dff, 512, LANES)
        ffn_args = (w_gate_b, w_up_b, ffn_conv_w[l].astype(F32),
                    ffn_conv_b[l][None].astype(F32), w_down_b,
                    ln2_g[l][None].astype(F32), ln2_b[l][None].astype(F32))
        h_next = _ffn(h1, h1_meta, *ffn_args, seq=seq, bm=bm_row, tf=tf, alpha=alpha)
        if l + 1 < depth:
            zero_meta = jnp.zeros_like(h1_meta)
            h_meta = _ffn(h1_meta, zero_meta, *ffn_args, seq=n_meta, bm=n_meta, tf=tf, alpha=alpha)
        h = h_next
    return h.reshape(batch, seq, d)
```

```python
import functools
import math

import jax
import jax.numpy as jnp
from jax import lax
from jax.experimental import pallas as pl
from jax.experimental.pallas import tpu as pltpu

F32 = jnp.float32
BF16 = jnp.bfloat16

CHUNK = 64
ATT_HEADS = 8
ATT_QK_DIM = 64
HEAD_W = 128
DN_HEADS = 8
DN_CONV = 4
FFN_CONV = 3
ROPE_THETA = 10000.0
LN_EPS = 1e-5
RMS_EPS = 1e-6
GROUP_W = ATT_HEADS * HEAD_W
Q_SCALE = ATT_QK_DIM ** -0.5 * math.log2(math.e)
N_GROUPS = 7
TAIL_COLS = 2 * DN_HEADS

LANES = 128
HALO = 8
VMEM_LIMIT = 56 * 1024 * 1024

_NT = (((1,), (1,)), ((), ()))
_TN = (((0,), (0,)), ((), ()))


def _params(*sem):
    return pltpu.CompilerParams(dimension_semantics=sem, vmem_limit_bytes=VMEM_LIMIT)


def _sigmoid(x):
    return 1.0 / (1.0 + jnp.exp(-x))


def _silu(x):
    return x * _sigmoid(x)


def _softplus(x):
    return jnp.maximum(x, 0.0) + jnp.log(1.0 + jnp.exp(-jnp.abs(x)))


def _layer_norm(y, g, b):
    mu = jnp.mean(y, axis=1, keepdims=True)
    d = y - mu
    var = jnp.mean(d * d, axis=1, keepdims=True)
    return d * lax.rsqrt(var + LN_EPS) * g + b


def _proj_kernel(x_ref, xm_ref, wqk_ref, w_ref, wt_ref, cos_ref, sin_ref, cosm_ref, sinm_ref,
                 p_ref, pm_ref, ba_ref, bam_ref, *, heads, n_qk_tiles):
    j = pl.program_id(0)
    i = pl.program_id(1)

    def tail(x_r):
        return lax.dot_general(x_r[...].astype(BF16), wt_ref[...], _NT, preferred_element_type=F32)

    def project(xh, wsel_ref, rope_tables, out_ref):
        acc = lax.dot_general(xh, wsel_ref[...], _NT, preferred_element_type=F32)
        if rope_tables is not None:
            scale = jnp.where(j == 0, Q_SCALE, 1.0).astype(F32)
            cos = rope_tables[0][...] * scale
            sin = rope_tables[1][...] * scale
        for h in range(heads):
            blk = acc[:, h * HEAD_W:(h + 1) * HEAD_W]
            if rope_tables is not None:
                blk = blk * cos + pltpu.roll(blk, HEAD_W // 2, 1) * sin
            out_ref[h] = blk.astype(BF16)

    def tile(wsel_ref, rope):
        @pl.when(i == 0)
        def _meta_rows():
            project(xm_ref[...].astype(BF16), wsel_ref, (cosm_ref, sinm_ref) if rope else None, pm_ref)

        project(x_ref[...].astype(BF16), wsel_ref, (cos_ref, sin_ref) if rope else None, p_ref)

    @pl.when(j < n_qk_tiles)
    def _qk_tile():
        tile(wqk_ref, True)

    @pl.when(j >= n_qk_tiles)
    def _plain_tile():
        tile(w_ref, False)

    @pl.when(j == 0)
    def _tail():
        ba_ref[...] = tail(x_ref)

        @pl.when(i == 0)
        def _meta_tail():
            bam_ref[...] = tail(xm_ref)


def _proj(x2d, x_meta, wqk_t, w_t, layer, w_tail_t, cos, sin, cos_meta, sin_meta, bm):
    m, d = x2d.shape
    n_meta = x_meta.shape[0]
    heads = GROUP_W // HEAD_W
    n_blocks = N_GROUPS * heads
    nm = m // bm
    n_qk_tiles = wqk_t.shape[1] // GROUP_W
    const = lambda shape: pl.BlockSpec(shape, lambda j, i: (0,) * len(shape))
    row = lambda j, i: (i, 0)
    tiles_per_batch = cos.shape[0] // bm
    pos = lambda j, i: (i % tiles_per_batch, 0)
    return pl.pallas_call(
        functools.partial(_proj_kernel, heads=heads, n_qk_tiles=n_qk_tiles),
        out_shape=(jax.ShapeDtypeStruct((n_blocks, m, HEAD_W), BF16),
                   jax.ShapeDtypeStruct((n_blocks, n_meta, HEAD_W), BF16),
                   jax.ShapeDtypeStruct((m, LANES), F32),
                   jax.ShapeDtypeStruct((n_meta, LANES), F32)),
        grid=(N_GROUPS, nm),
        in_specs=[
            pl.BlockSpec((bm, d), row),
            const((n_meta, d)),
            pl.BlockSpec((None, GROUP_W, d), lambda j, i: (layer, jnp.minimum(j, n_qk_tiles - 1), 0)),
            pl.BlockSpec((None, GROUP_W, d), lambda j, i: (layer, jnp.maximum(j, n_qk_tiles), 0)),
            const((LANES, d)),
            pl.BlockSpec((bm, HEAD_W), pos),
            pl.BlockSpec((bm, HEAD_W), pos),
            const((n_meta, HEAD_W)),
            const((n_meta, HEAD_W)),
        ],
        out_specs=(
            pl.BlockSpec((heads, bm, HEAD_W), lambda j, i: (j, i, 0)),
            pl.BlockSpec((heads, n_meta, HEAD_W), lambda j, i: (j, 0, 0)),
            pl.BlockSpec((bm, LANES), lambda j, i: (jnp.where(j == 0, i, nm - 1), 0)),
            const((n_meta, LANES)),
        ),
        compiler_params=_params("arbitrary", "arbitrary"),
        name="proj",
    )(x2d, x_meta, wqk_t, w_t, w_tail_t, cos, sin, cos_meta, sin_meta)


def _attn_kernel(*refs, tq, hps, lam_init, meta_only):
    if meta_only:
        q_ref, km_ref, vm_ref, lq1, lk1, lq2, lk2, nw_ref, o_ref, m_sc, l_sc, acc_sc = refs
    else:
        (q_ref, k_ref, v_ref, km_ref, vm_ref, lq1, lk1, lq2, lk2, nw_ref, o_ref,
         m_sc, l_sc, acc_sc, s_sc, a_sc, p_sc) = refs
    lane = lax.broadcasted_iota(jnp.int32, (1, HEAD_W), 1)
    comp = (lane // (ATT_QK_DIM // 2)) % 2
    qc = []
    for h in range(hps):
        q = q_ref[h]
        zero = jnp.zeros_like(q)
        qc.append((jnp.where(comp == 0, q, zero), jnp.where(comp == 1, q, zero)))

    nc = 2 * hps

    def init_from_meta():
        s_meta = [lax.dot_general(km_ref[i // 2], qc[i // 2][i % 2], _NT, preferred_element_type=F32)
                  for i in range(nc)]
        p_meta = []
        for i in range(nc):
            m0 = jnp.max(s_meta[i], axis=0, keepdims=True)
            p = jnp.exp2(s_meta[i] - m0)
            m_sc[i] = m0
            l_sc[i] = jnp.sum(p, axis=0, keepdims=True)
            p_meta.append(p.astype(BF16))
        for i in range(nc):
            acc_sc[i] = lax.dot_general(vm_ref[i // 2], p_meta[i], _TN, preferred_element_type=F32)

    def scores(t, slot):
        off = pl.multiple_of(t * tq, tq)
        for h in range(hps):
            kt = k_ref[h, pl.ds(off, tq), :]
            for c in range(2):
                s_sc[slot * nc + 2 * h + c] = lax.dot_general(kt, qc[h][c], _NT,
                                                              preferred_element_type=F32)

    def absorb(t, slot, mask):
        off = pl.multiple_of(t * tq, tq)
        for i in range(nc):
            s = s_sc[slot * nc + i]
            if mask is not None:
                s = jnp.where(mask, s, -jnp.inf)
            m_old = m_sc[i]
            m_new = jnp.maximum(m_old, jnp.max(s, axis=0, keepdims=True))
            a = jnp.exp2(m_old - m_new)
            p = jnp.exp2(s - m_new)
            l_sc[i] = a * l_sc[i] + jnp.sum(p, axis=0, keepdims=True)
            m_sc[i] = m_new
            a_sc[i] = a
            p_sc[i] = p.astype(BF16)
        for h in range(hps):
            vt = v_ref[h, pl.ds(off, tq), :]
            for c in range(2):
                i = 2 * h + c
                acc_sc[i] = a_sc[i] * acc_sc[i] + lax.dot_general(vt, p_sc[i], _TN,
                                                                  preferred_element_type=F32)

    if meta_only:
        init_from_meta()
    else:
        qi = pl.program_id(2)
        keys = lax.broadcasted_iota(jnp.int32, (tq, tq), 0)
        queries = lax.broadcasted_iota(jnp.int32, (tq, tq), 1)
        diag = (keys // CHUNK) <= (queries // CHUNK)
        scores(0, 0)
        init_from_meta()

        def body(k, carry):
            scores(2 * k + 1, 1)
            absorb(2 * k, 0, None)
            scores(2 * k + 2, 0)
            absorb(2 * k + 1, 1, None)
            return carry

        lax.fori_loop(0, qi // 2, body, 0)

        @pl.when(qi % 2 == 1)
        def _odd_tail():
            scores(qi, 1)
            absorb(qi - 1, 0, None)
            absorb(qi, 1, diag)

        @pl.when(qi % 2 == 0)
        def _even_tail():
            absorb(qi, 0, diag)

    lam = (jnp.exp(jnp.sum(lq1[...] * lk1[...], axis=1, keepdims=True))
           - jnp.exp(jnp.sum(lq2[...] * lk2[...], axis=1, keepdims=True)) + lam_init)
    for h in range(hps):
        o = acc_sc[2 * h] / l_sc[2 * h] - lam * (acc_sc[2 * h + 1] / l_sc[2 * h + 1])
        ms = jnp.mean(o * o, axis=0, keepdims=True)
        o = o * lax.rsqrt(ms + RMS_EPS) * nw_ref[...] * (1.0 - lam_init)
        o_ref[:, h * HEAD_W:(h + 1) * HEAD_W] = o.T.astype(BF16)


def _attention(p_q, p_kv, p_meta, lams, norm_w, *, batch, seq, tq, hps, lam_init, meta_only):
    n_meta = p_meta.shape[1]
    rows = p_q.shape[1]
    nq = 1 if meta_only else seq // tq
    k_blk, v_blk = ATT_HEADS // hps, 2 * ATT_HEADS // hps
    small = [pl.BlockSpec((1, ATT_QK_DIM), lambda b, g, i: (0, 0))] * 4 + [
        pl.BlockSpec((HEAD_W, 1), lambda b, g, i: (0, 0))]
    meta_specs = [pl.BlockSpec((hps, n_meta, HEAD_W), lambda b, g, i: (k_blk + g, 0, 0)),
                  pl.BlockSpec((hps, n_meta, HEAD_W), lambda b, g, i: (v_blk + g, 0, 0))]
    q_spec = pl.BlockSpec((hps, tq, HEAD_W), lambda b, g, i: (g, b * nq + i, 0))
    if meta_only:
        in_specs = [q_spec] + meta_specs + small
        args = (p_q, p_meta, p_meta) + lams + (norm_w,)
    else:
        kv_specs = [pl.BlockSpec((hps, seq, HEAD_W), lambda b, g, i: (k_blk + g, b, 0)),
                    pl.BlockSpec((hps, seq, HEAD_W), lambda b, g, i: (v_blk + g, b, 0))]
        in_specs = [q_spec] + kv_specs + meta_specs + small
        args = (p_q, p_kv, p_kv, p_meta, p_meta) + lams + (norm_w,)
    scratch = [pltpu.VMEM((2 * hps, 1, tq), F32), pltpu.VMEM((2 * hps, 1, tq), F32),
               pltpu.VMEM((2 * hps, HEAD_W, tq), F32)]
    if not meta_only:
        scratch += [pltpu.VMEM((2 * 2 * hps, tq, tq), F32),
                    pltpu.VMEM((2 * hps, 1, tq), F32),
                    pltpu.VMEM((2 * hps, tq, tq), BF16)]
    return pl.pallas_call(
        functools.partial(_attn_kernel, tq=tq, hps=hps, lam_init=lam_init, meta_only=meta_only),
        out_shape=jax.ShapeDtypeStruct((rows, GROUP_W), BF16),
        grid=(batch, ATT_HEADS // hps, nq),
        in_specs=in_specs,
        out_specs=pl.BlockSpec((tq, hps * HEAD_W), lambda b, g, i: (b * nq + i, g)),
        scratch_shapes=scratch,
        compiler_params=_params("parallel", "parallel", "arbitrary"),
        name="attn_meta" if meta_only else "attn",
    )(*args)


def _gdn_prep_kernel(*refs, rows, sub, n_pad, n_cast):
    (q_ref, k_ref, v_ref, hq_ref, hk_ref, hv_ref, mh_ref, cw_ref, ba_ref, alog_ref, dtb_ref) = refs[:11]
    cast_src = refs[11:11 + n_cast]
    u_ref, wq_ref, kd_ref, a_ref, gl_ref = refs[11 + n_cast:16 + n_cast]
    cast_dst = refs[16 + n_cast:16 + 2 * n_cast]
    xs_sc, gc_sc, pw_sc, x_sc, t_sc = refs[16 + 2 * n_cast:]
    for src, dst in zip(cast_src, cast_dst):
        dst[...] = src[...].astype(BF16)

    first = pl.program_id(1) == 0
    n_sub = rows // sub
    ba = ba_ref[...]
    beta_all = _sigmoid(ba)
    g_all = -jnp.exp(alog_ref[...]) * _softplus(ba + dtb_ref[...])
    row = lax.broadcasted_iota(jnp.int32, (rows, 1), 0)
    if n_pad:
        beta_all = jnp.where(row >= n_pad, beta_all, 0.0)
        g_all = jnp.where(row >= n_pad, g_all, 0.0)

    pos = row % CHUNK
    gc_all = g_all
    shift = 1
    while shift < CHUNK:
        gc_all = gc_all + jnp.where(pos >= shift, pltpu.roll(gc_all, shift, 0), 0.0)
        shift *= 2
    gc_sc[...] = gc_all
    gl_all = jnp.concatenate(
        [jnp.broadcast_to(gc_sc[(c + 1) * CHUNK - 1:(c + 1) * CHUNK, :], (CHUNK, LANES))
         for c in range(rows // CHUNK)], axis=0)
    gl_ref[...] = gl_all
    gct_all = gc_all.T

    ri = lax.broadcasted_iota(jnp.int32, (sub, sub), 0)
    ci = lax.broadcasted_iota(jnp.int32, (sub, sub), 1)
    same = (ri // CHUNK) == (ci // CHUNK)
    tril = jnp.logical_and(same, ci <= ri)
    strict = jnp.logical_and(same, ci < ri)
    eye = jnp.where(ri == ci, 1.0, 0.0).astype(F32)

    def conv_act(main_ref, halo_ref, stream, h):
        halo = jnp.where(first, mh_ref[stream * DN_HEADS + h], halo_ref[h])
        xs_sc[stream, 0:HALO, :] = halo.astype(F32)
        xs_sc[stream, HALO:, :] = main_ref[h].astype(F32)
        w = cw_ref[stream * DN_HEADS + h]
        y = xs_sc[stream, pl.ds(HALO - DN_CONV + 1, rows), :] * w[0:1]
        for j in range(1, DN_CONV):
            y = y + xs_sc[stream, pl.ds(HALO - DN_CONV + 1 + j, rows), :] * w[j:j + 1]
        return _silu(y)

    for h in range(DN_HEADS):
        q = conv_act(q_ref, hq_ref, 0, h)
        k = conv_act(k_ref, hk_ref, 1, h)
        v = conv_act(v_ref, hv_ref, 2, h)
        q = q * lax.rsqrt(jnp.sum(q * q, axis=1, keepdims=True) + RMS_EPS) * (HEAD_W ** -0.5)
        k = k * lax.rsqrt(jnp.sum(k * k, axis=1, keepdims=True) + RMS_EPS)
        beta = beta_all[:, h:h + 1]
        gc = gc_all[:, DN_HEADS + h:DN_HEADS + h + 1]
        gl = gl_all[:, DN_HEADS + h:DN_HEADS + h + 1]
        egc = jnp.exp(gc)
        kb = k * beta
        k16 = k.astype(BF16)
        kb16 = kb.astype(BF16)
        q16 = q.astype(BF16)
        x_all = jnp.concatenate([v * beta, kb * egc], axis=1)
        qd16 = (q * egc).astype(BF16)
        kd_ref[h] = (k * jnp.exp(gl - gc)).astype(BF16)
        for c in range(rows // CHUNK):
            wq_ref[h, (2 * c + 1) * CHUNK:(2 * c + 2) * CHUNK, :] = qd16[c * CHUNK:(c + 1) * CHUNK]
        for s in range(n_sub):
            lo, hi = s * sub, (s + 1) * sub
            gcr = gct_all[DN_HEADS + h:DN_HEADS + h + 1, lo:hi]
            kk = lax.dot_general(kb16[lo:hi], k16[lo:hi], _NT, preferred_element_type=F32)
            qk = lax.dot_general(q16[lo:hi], k16[lo:hi], _NT, preferred_element_type=F32)
            dec = jnp.exp(jnp.where(tril, gc[lo:hi] - gcr, -jnp.inf))
            a_ref[h, lo:hi, :] = (qk * dec).astype(BF16)
            m0 = jnp.where(strict, kk * dec, 0.0)
            pw_sc[h * n_sub + s] = m0.astype(BF16)
            t_sc[h * n_sub + s] = eye - m0
            x_sc[h * n_sub + s] = x_all[lo:hi].astype(BF16)

    items = range(DN_HEADS * n_sub)
    for _ in range(int(math.log2(CHUNK)) - 1):
        for it in items:
            p16 = pw_sc[it]
            pw_sc[it] = jnp.dot(p16, p16, preferred_element_type=F32).astype(BF16)
        for it in items:
            t = t_sc[it]
            t_sc[it] = t + jnp.dot(pw_sc[it], t.astype(BF16), preferred_element_type=F32)
    for it in items:
        h, lo = it // n_sub, (it % n_sub) * sub
        x = jnp.dot(t_sc[it].astype(BF16), x_sc[it], preferred_element_type=F32)
        u_ref[h, lo:lo + sub, :] = x[:, :HEAD_W].astype(BF16)
        w16 = x[:, HEAD_W:].astype(BF16)
        for c in range(sub // CHUNK):
            r0 = lo + c * CHUNK
            wq_ref[h, 2 * r0:2 * r0 + CHUNK, :] = w16[c * CHUNK:(c + 1) * CHUNK]


def _gdn_prep(p, p_halo_src, meta_halo, conv_w, ba, alog, dtb, *, batch, seq, rows, sub, n_pad,
              cast=(), layer=0):
    m = batch * seq
    nt = seq // rows
    steps = batch * nt
    cast_rows = [w.shape[1] // steps for w in cast]
    assert all(w.shape[1] % steps == 0 and r % (2 * HALO) == 0 for w, r in zip(cast, cast_rows))
    cast_in = [pl.BlockSpec((None, r, w.shape[2]), lambda b, t: (layer, b * nt + t, 0))
               for w, r in zip(cast, cast_rows)]
    cast_out = [pl.BlockSpec((r, w.shape[2]), lambda b, t: (b * nt + t, 0)) for w, r in zip(cast, cast_rows)]
    cast_shape = [jax.ShapeDtypeStruct(w.shape[1:], BF16) for w in cast]
    g_q, g_k, g_v = 3, 4, 5
    hb = rows // HALO

    def main_spec(g):
        return pl.BlockSpec((DN_HEADS, rows, HEAD_W), lambda b, t: (g, b * nt + t, 0))

    def halo_spec(g):
        return pl.BlockSpec((DN_HEADS, HALO, HEAD_W),
                            lambda b, t: (g, jnp.maximum((b * nt + t) * hb - 1, 0), 0))

    full = lambda shape: pl.BlockSpec(shape, lambda b, t: (0,) * len(shape))
    out_shape = (jax.ShapeDtypeStruct((DN_HEADS, m, HEAD_W), BF16),
                 jax.ShapeDtypeStruct((DN_HEADS, 2 * m, HEAD_W), BF16),
                 jax.ShapeDtypeStruct((DN_HEADS, m, HEAD_W), BF16),
                 jax.ShapeDtypeStruct((DN_HEADS, m, sub), BF16),
                 jax.ShapeDtypeStruct((m, LANES), F32)) + tuple(cast_shape)
    out_specs = (pl.BlockSpec((DN_HEADS, rows, HEAD_W), lambda b, t: (0, b * nt + t, 0)),
                 pl.BlockSpec((DN_HEADS, 2 * rows, HEAD_W), lambda b, t: (0, b * nt + t, 0)),
                 pl.BlockSpec((DN_HEADS, rows, HEAD_W), lambda b, t: (0, b * nt + t, 0)),
                 pl.BlockSpec((DN_HEADS, rows, sub), lambda b, t: (0, b * nt + t, 0)),
                 pl.BlockSpec((rows, LANES), lambda b, t: (b * nt + t, 0))) + tuple(cast_out)
    ahead = pl.Buffered(3)
    streamed = lambda spec: pl.BlockSpec(spec.block_shape, spec.index_map, pipeline_mode=ahead)
    pipeline = pltpu.emit_pipeline(
        functools.partial(_gdn_prep_kernel, rows=rows, sub=sub, n_pad=n_pad, n_cast=len(cast)),
        grid=(batch, nt),
        in_specs=[streamed(main_spec(g_q)), streamed(main_spec(g_k)), streamed(main_spec(g_v)),
                  halo_spec(g_q), halo_spec(g_k), halo_spec(g_v),
                  full(meta_halo.shape), full(conv_w.shape),
                  pl.BlockSpec((rows, LANES), lambda b, t: (b * nt + t, 0)),
                  full(alog.shape), full(dtb.shape)] + [streamed(c) for c in cast_in],
        out_specs=list(out_specs),
        dimension_semantics=(pltpu.ARBITRARY, pltpu.ARBITRARY),
    )
    n_io = 11 + len(cast) + len(out_specs)

    def outer(*refs):
        pipeline(*refs[:n_io], scratches=refs[n_io:])

    outs = pl.pallas_call(
        outer,
        out_shape=out_shape,
        in_specs=[pl.BlockSpec(memory_space=pl.ANY)] * (11 + len(cast)),
        out_specs=tuple(pl.BlockSpec(memory_space=pl.ANY) for _ in out_specs),
        scratch_shapes=[pltpu.VMEM((3, rows + HALO, HEAD_W), F32), pltpu.VMEM((rows, LANES), F32),
                        pltpu.VMEM((DN_HEADS * (rows // sub), sub, sub), BF16),
                        pltpu.VMEM((DN_HEADS * (rows // sub), sub, 2 * HEAD_W), BF16),
                        pltpu.VMEM((DN_HEADS * (rows // sub), sub, sub), F32)],
        compiler_params=pltpu.CompilerParams(vmem_limit_bytes=VMEM_LIMIT),
        name="gdn_prep",
    )(p, p, p, p_halo_src, p_halo_src, p_halo_src, meta_halo, conv_w, ba, alog, dtb, *cast)
    return outs[:5], outs[5:]


def _gdn_scan_kernel(u_ref, wq_ref, kd_ref, a_ref, gl_ref, z_ref,
                     um_ref, wqm_ref, kdm_ref, am_ref, glm_ref, zm_ref, nw_ref,
                     o_ref, om_ref, s_sc, vbuf_sc, sw_sc, *, rows, sub, hpg):
    hg = pl.program_id(1)
    t = pl.program_id(2)
    lane = lax.broadcasted_iota(jnp.int32, (1, LANES), 1)
    nw = nw_ref[...]

    def decay_total(gl_row, h):
        sel = lane == DN_HEADS + hg * hpg + h
        return jnp.sum(jnp.where(sel, jnp.exp(gl_row), 0.0), axis=1, keepdims=True)

    def gate(o, z):
        ms = jnp.mean(o * o, axis=1, keepdims=True)
        return (o * lax.rsqrt(ms + RMS_EPS) * nw * _silu(z.astype(F32))).astype(BF16)

    def run_chunk(r0, refs, out_ref, out_r0):
        u_r, wq_r, kd_r, a_r, gl_r, z_r = refs
        slot = r0 % sub
        gl_row = gl_r[r0:r0 + 1, :]
        for h in range(hpg):
            sw_sc[h] = jnp.dot(wq_r[h, 2 * r0:2 * r0 + 2 * CHUNK, :], s_sc[h].astype(BF16),
                               preferred_element_type=F32)
        for h in range(hpg):
            v_new = u_r[h, r0:r0 + CHUNK, :].astype(F32) - sw_sc[h, 0:CHUNK, :]
            vbuf_sc[h, slot:slot + CHUNK, :] = v_new.astype(BF16)
        for h in range(hpg):
            o = sw_sc[h, CHUNK:, :] + jnp.dot(a_r[h, r0:r0 + CHUNK, :], vbuf_sc[h],
                                              preferred_element_type=F32)
            s_sc[h] = decay_total(gl_row, h) * s_sc[h] + lax.dot_general(
                kd_r[h, r0:r0 + CHUNK, :], vbuf_sc[h, slot:slot + CHUNK, :], _TN,
                preferred_element_type=F32)
            out_ref[out_r0:out_r0 + CHUNK, h * HEAD_W:(h + 1) * HEAD_W] = gate(
                o, z_r[h, r0:r0 + CHUNK, :])

    @pl.when(t == 0)
    def _meta_chunk():
        vbuf_sc[...] = jnp.zeros_like(vbuf_sc)
        s_sc[...] = jnp.zeros_like(s_sc)
        run_chunk(sub - CHUNK, (um_ref, wqm_ref, kdm_ref, am_ref, glm_ref, zm_ref), om_ref, 0)

    for c in range(rows // CHUNK):
        run_chunk(c * CHUNK, (u_ref, wq_ref, kd_ref, a_ref, gl_ref, z_ref), o_ref, c * CHUNK)


def _gdn_scan(real, meta, p, p_meta_pad, norm_w, *, batch, seq, rows, sub, hpg):
    u, wq, kd, a, gl = real
    um, wqm, kdm, am, glm = meta
    m = batch * seq
    nt = seq // rows
    ngrp = DN_HEADS // hpg
    z_blk = (6 * DN_HEADS) // hpg
    mrows = um.shape[1]
    assert mrows == sub

    def real_spec(r, w):
        return pl.BlockSpec((hpg, r, w), lambda b, g, t: (g, b * nt + t, 0))

    def meta_spec(r, w):
        return pl.BlockSpec((hpg, r, w), lambda b, g, t: (g, 0, 0))

    return pl.pallas_call(
        functools.partial(_gdn_scan_kernel, rows=rows, sub=sub, hpg=hpg),
        out_shape=(jax.ShapeDtypeStruct((m, GROUP_W), BF16),
                   jax.ShapeDtypeStruct((batch, CHUNK, GROUP_W), BF16)),
        grid=(batch, ngrp, nt),
        in_specs=[real_spec(rows, HEAD_W), real_spec(2 * rows, HEAD_W), real_spec(rows, HEAD_W),
                  real_spec(rows, sub),
                  pl.BlockSpec((rows, LANES), lambda b, g, t: (b * nt + t, 0)),
                  pl.BlockSpec((hpg, rows, HEAD_W), lambda b, g, t: (z_blk + g, b * nt + t, 0)),
                  meta_spec(mrows, HEAD_W), meta_spec(2 * mrows, HEAD_W), meta_spec(mrows, HEAD_W),
                  meta_spec(mrows, mrows),
                  pl.BlockSpec((mrows, LANES), lambda b, g, t: (0, 0)),
                  pl.BlockSpec((hpg, mrows, HEAD_W), lambda b, g, t: (z_blk + g, 0, 0)),
                  pl.BlockSpec((1, HEAD_W), lambda b, g, t: (0, 0))],
        out_specs=(pl.BlockSpec((rows, hpg * HEAD_W), lambda b, g, t: (b * nt + t, g)),
                   pl.BlockSpec((None, CHUNK, hpg * HEAD_W), lambda b, g, t: (b, 0, g))),
        scratch_shapes=[pltpu.VMEM((hpg, HEAD_W, HEAD_W), F32),
                        pltpu.VMEM((hpg, sub, HEAD_W), BF16),
                        pltpu.VMEM((hpg, 2 * CHUNK, HEAD_W), F32)],
        compiler_params=_params("parallel", "parallel", "arbitrary"),
        name="gdn_scan",
    )(u, wq, kd, a, gl, p, um, wqm, kdm, am, glm, p_meta_pad, norm_w)


def _wout_ln_kernel(oa_ref, od_ref, w_ref, x_ref, g_ref, b_ref, h_ref, *, alpha):
    half = oa_ref.shape[1]
    mix = (jnp.dot(oa_ref[...], w_ref[0:half, :], preferred_element_type=F32)
           + jnp.dot(od_ref[...], w_ref[half:, :], preferred_element_type=F32))
    h_ref[...] = _layer_norm(alpha * x_ref[...] + mix, g_ref[...], b_ref[...])


def _wout_ln(o_att, o_dn, w_out, x2d, g, b, *, bm, alpha):
    m, d = x2d.shape
    half = o_att.shape[1]
    return pl.pallas_call(
        functools.partial(_wout_ln_kernel, alpha=alpha),
        out_shape=jax.ShapeDtypeStruct((m, d), F32),
        grid=(m // bm,),
        in_specs=[pl.BlockSpec((bm, half), lambda i: (i, 0)),
                  pl.BlockSpec((bm, half), lambda i: (i, 0)),
                  pl.BlockSpec((2 * half, d), lambda i: (0, 0)),
                  pl.BlockSpec((bm, d), lambda i: (i, 0)),
                  pl.BlockSpec((1, d), lambda i: (0, 0)),
                  pl.BlockSpec((1, d), lambda i: (0, 0))],
        out_specs=pl.BlockSpec((bm, d), lambda i: (i, 0)),
        compiler_params=_params("parallel"),
        name="wout_ln",
    )(o_att, o_dn, w_out, x2d, g, b)


def _ffn_kernel(h_ref, hprev_ref, hmeta_ref, wg_ref, wu_ref, cw_ref, cb_ref, wd_ref, g_ref, b_ref,
                o_ref, xb_sc, gate_sc, *, bm, tiles_per_batch, alpha):
    i = pl.program_id(0)
    f = pl.program_id(1)

    @pl.when(f == 0)
    def _init():
        halo = jnp.where(i % tiles_per_batch == 0, hmeta_ref[...], hprev_ref[...])
        xb_sc[0:HALO, :] = halo.astype(BF16)
        xb_sc[HALO:, :] = h_ref[...].astype(BF16)
        o_ref[...] = jnp.zeros_like(o_ref)

    gate_sc[...] = jnp.dot(xb_sc[...], wg_ref[...], preferred_element_type=F32)
    up = jnp.dot(xb_sc[HALO:, :], wu_ref[...], preferred_element_type=F32)
    cw = cw_ref[...]
    hg = cb_ref[...] + gate_sc[pl.ds(HALO - FFN_CONV + 1, bm), :] * cw[0:1]
    for j in range(1, FFN_CONV):
        hg = hg + gate_sc[pl.ds(HALO - FFN_CONV + 1 + j, bm), :] * cw[j:j + 1]
    act = (_silu(hg) * up).astype(BF16)
    o_ref[...] += jnp.dot(act, wd_ref[...], preferred_element_type=F32)

    @pl.when(f == pl.num_programs(1) - 1)
    def _finish():
        o_ref[...] = _layer_norm(alpha * h_ref[...] + o_ref[...], g_ref[...], b_ref[...])


def _ffn(h1, h1_meta, w_gate, w_up, conv_w, conv_b, w_down, g, b, *, seq, bm, tf, alpha):
    m, d = h1.shape
    dff = w_gate.shape[1]
    tiles_per_batch = seq // bm
    hb = bm // HALO
    meta_blk = h1_meta.shape[0] // HALO - 1
    return pl.pallas_call(
        functools.partial(_ffn_kernel, bm=bm, tiles_per_batch=tiles_per_batch, alpha=alpha),
        out_shape=jax.ShapeDtypeStruct((m, d), F32),
        grid=(m // bm, dff // tf),
        in_specs=[pl.BlockSpec((bm, d), lambda i, f: (i, 0)),
                  pl.BlockSpec((HALO, d), lambda i, f: (jnp.maximum(i * hb - 1, 0), 0)),
                  pl.BlockSpec((HALO, d), lambda i, f: (meta_blk, 0)),
                  pl.BlockSpec((d, tf), lambda i, f: (0, f)),
                  pl.BlockSpec((d, tf), lambda i, f: (0, f)),
                  pl.BlockSpec((FFN_CONV, tf), lambda i, f: (0, f)),
                  pl.BlockSpec((1, tf), lambda i, f: (0, f)),
                  pl.BlockSpec((tf, d), lambda i, f: (f, 0)),
                  pl.BlockSpec((1, d), lambda i, f: (0, 0)),
                  pl.BlockSpec((1, d), lambda i, f: (0, 0))],
        out_specs=pl.BlockSpec((bm, d), lambda i, f: (i, 0)),
        scratch_shapes=[pltpu.VMEM((bm + HALO, d), BF16),
                        pltpu.VMEM((bm + HALO, tf), F32)],
        compiler_params=_params("parallel", "arbitrary"),
        name="ffn",
    )(h1, h1, h1_meta, w_gate, w_up, conv_w, conv_b, w_down, g, b)


def _largest_tile(n, cap, mult):
    best = None
    for c in range(mult, min(n, cap) + 1, mult):
        if n % c == 0:
            best = c
    assert best is not None, (n, cap, mult)
    return best


def _split_halves(w_qk_t):
    n_comp, quarter = HEAD_W // ATT_QK_DIM, ATT_QK_DIM // 2
    depth, n, d = w_qk_t.shape
    w = w_qk_t.reshape(depth, -1, n_comp, 2, quarter, d)
    return jnp.swapaxes(w, 2, 3).reshape(depth, n, d)


def _rope_tables(n_pos):
    n_comp = HEAD_W // ATT_QK_DIM
    inv_freq = ROPE_THETA ** (-jnp.arange(0, ATT_QK_DIM, 2, dtype=F32) / ATT_QK_DIM)
    ang = jnp.arange(n_pos, dtype=F32)[:, None] * inv_freq[None, :]
    cos, sin = jnp.cos(ang), jnp.sin(ang)
    cos_t = jnp.tile(cos, (1, 2 * n_comp))
    sin_t = jnp.concatenate([jnp.tile(-sin, (1, n_comp)), jnp.tile(sin, (1, n_comp))], axis=1)
    assert cos_t.shape[1] == HEAD_W
    return cos_t, sin_t


def kernel(x, meta_tokens, w_in, conv_qkv_w, a_log, dt_bias, lambda_q1, lambda_k1, lambda_q2, lambda_k2,
           diff_norm_w, delta_norm_w, w_out, ln1_g, ln1_b, ffn_w_gate, ffn_w_up, ffn_conv_w, ffn_conv_b,
           ffn_w_down, ln2_g, ln2_b):
    batch, seq, d = x.shape
    n_meta = meta_tokens.shape[0]
    depth = w_in.shape[0]
    n_main = N_GROUPS * GROUP_W
    assert w_in.shape[2] == n_main + TAIL_COLS and n_meta % HALO == 0 and n_meta <= CHUNK
    m = batch * seq
    alpha = (2.0 * depth) ** 0.25

    bm_proj = _largest_tile(seq, 1024, HALO)
    bm_row = _largest_tile(seq, 512, HALO)
    tq = _largest_tile(seq, 256, CHUNK)
    rows = _largest_tile(seq, 256, LANES)
    sub = LANES
    hpg = DN_HEADS
    att_hps = ATT_HEADS

    cos_t, sin_t = _rope_tables(n_meta + seq)
    cos_meta, sin_meta = cos_t[:n_meta], sin_t[:n_meta]
    cos_real, sin_real = cos_t[n_meta:], sin_t[n_meta:]

    h = x.reshape(m, d)
    h_meta = meta_tokens.astype(x.dtype)
    n_qk = 2 * GROUP_W
    w_in_t = jnp.swapaxes(w_in, 1, 2)
    w_t = w_in_t.astype(BF16)
    wqk_t = _split_halves(w_t[:, :n_qk])
    pad_rows = sub - n_meta
    for l in range(depth):
        w_tail_t = jnp.pad(w_t[l, n_main:], ((0, LANES - TAIL_COLS), (0, 0)))
        lane_pad = (DN_HEADS, LANES - TAIL_COLS)
        alog = jnp.pad(a_log[l].astype(F32), lane_pad)[None]
        dtb = jnp.pad(dt_bias[l].astype(F32), lane_pad)[None]
        conv_w = conv_qkv_w[l].reshape(DN_CONV, 3 * DN_HEADS, HEAD_W).transpose(1, 0, 2)
        lams = tuple(v[l][None].astype(F32) for v in (lambda_q1, lambda_k1, lambda_q2, lambda_k2))
        lam_init = 0.8 - 0.6 * math.exp(-0.3 * l)

        p, p_meta, ba, ba_meta = _proj(h, h_meta, wqk_t, w_t, l, w_tail_t, cos_real, sin_real,
                                       cos_meta, sin_meta, bm_proj)

        att_nw = diff_norm_w[l].astype(F32)[:, None]
        o_att = _attention(p, p, p_meta, lams, att_nw, batch=batch, seq=seq, tq=tq, hps=att_hps,
                           lam_init=lam_init, meta_only=False)
        q_meta = jnp.pad(p_meta[:ATT_HEADS], ((0, 0), (0, LANES - n_meta), (0, 0)))
        o_att_meta = _attention(q_meta, None, p_meta, lams, att_nw, batch=1, seq=LANES, tq=LANES,
                                hps=att_hps, lam_init=lam_init, meta_only=True)[:n_meta]

        p_meta_pad = jnp.pad(p_meta, ((0, 0), (pad_rows, 0), (0, 0)))
        ba_meta_pad = jnp.pad(ba_meta, ((pad_rows, 0), (0, 0)))
        meta_halo = p_meta[3 * DN_HEADS:6 * DN_HEADS, n_meta - HALO:, :]
        zero_halo = jnp.zeros_like(meta_halo)
        prep_real, (w_out_b, w_gate_b, w_up_b, w_down_b) = _gdn_prep(
            p, p, meta_halo, conv_w, ba, alog, dtb, batch=batch, seq=seq, rows=rows, sub=sub, n_pad=0,
            cast=(w_out, ffn_w_gate, ffn_w_up, ffn_w_down), layer=l)
        prep_meta, _ = _gdn_prep(p_meta_pad, p_meta_pad, zero_halo, conv_w, ba_meta_pad, alog, dtb, batch=1,
                                 seq=sub, rows=sub, sub=sub, n_pad=pad_rows)
        o_dn, o_dn_meta = _gdn_scan(prep_real, prep_meta, p, p_meta_pad, delta_norm_w[l][None],
                                    batch=batch, seq=seq, rows=rows, sub=sub, hpg=hpg)
        o_dn_meta = o_dn_meta[0, CHUNK - n_meta:]

        g1, b1 = ln1_g[l][None].astype(F32), ln1_b[l][None].astype(F32)
        h1 = _wout_ln(o_att, o_dn, w_out_b, h, g1, b1, bm=bm_row, alpha=alpha)
        h1_meta = _wout_ln(o_att_meta, o_dn_meta, w_out_b, h_meta, g1, b1, bm=n_meta, alpha=alpha)

        dff = ffn_w_gate.shape[2]
        tf = _largest_tile(dff, 512, LANES)
        ffn_args = (w_gate_b, w_up_b, ffn_conv_w[l].astype(F32),
                    ffn_conv_b[l][None].astype(F32), w_down_b,
                    ln2_g[l][None].astype(F32), ln2_b[l][None].astype(F32))
        h_next = _ffn(h1, h1_meta, *ffn_args, seq=seq, bm=bm_row, tf=tf, alpha=alpha)
        if l + 1 < depth:
            zero_meta = jnp.zeros_like(h1_meta)
            h_meta = _ffn(h1_meta, zero_meta, *ffn_args, seq=n_meta, bm=n_meta, tf=tf, alpha=alpha)
        h = h_next
    return h.reshape(batch, seq, d)
```

```python
import functools
import math

import jax
import jax.numpy as jnp
from jax import lax
from jax.experimental import pallas as pl
from jax.experimental.pallas import tpu as pltpu

F32 = jnp.float32
BF16 = jnp.bfloat16

CHUNK = 64
ATT_HEADS = 8
ATT_QK_DIM = 64
HEAD_W = 128
DN_HEADS = 8
DN_CONV = 4
FFN_CONV = 3
ROPE_THETA = 10000.0
LN_EPS = 1e-5
RMS_EPS = 1e-6
GROUP_W = ATT_HEADS * HEAD_W
Q_SCALE = ATT_QK_DIM ** -0.5 * math.log2(math.e)
N_GROUPS = 7
TAIL_COLS = 2 * DN_HEADS

LANES = 128
HALO = 8
VMEM_LIMIT = 56 * 1024 * 1024

_NT = (((1,), (1,)), ((), ()))
_TN = (((0,), (0,)), ((), ()))


def _params(*sem):
    return pltpu.CompilerParams(dimension_semantics=sem, vmem_limit_bytes=VMEM_LIMIT)


def _sigmoid(x):
    return 1.0 / (1.0 + jnp.exp(-x))


def _silu(x):
    return x * _sigmoid(x)


def _softplus(x):
    return jnp.maximum(x, 0.0) + jnp.log(1.0 + jnp.exp(-jnp.abs(x)))


def _layer_norm(y, g, b):
    mu = jnp.mean(y, axis=1, keepdims=True)
    d = y - mu
    var = jnp.mean(d * d, axis=1, keepdims=True)
    return d * lax.rsqrt(var + LN_EPS) * g + b


def _proj_kernel(x_ref, xm_ref, wqk_ref, w_ref, wt_ref, cos_ref, sin_ref, cosm_ref, sinm_ref,
                 p_ref, pm_ref, ba_ref, bam_ref, *, heads, n_qk_tiles):
    j = pl.program_id(0)
    i = pl.program_id(1)

    def tail(x_r):
        return lax.dot_general(x_r[...].astype(BF16), wt_ref[...], _NT, preferred_element_type=F32)

    def project(xh, wsel_ref, rope_tables, out_ref):
        acc = lax.dot_general(xh, wsel_ref[...], _NT, preferred_element_type=F32)
        if rope_tables is not None:
            scale = jnp.where(j == 0, Q_SCALE, 1.0).astype(F32)
            cos = rope_tables[0][...] * scale
            sin = rope_tables[1][...] * scale
        for h in range(heads):
            blk = acc[:, h * HEAD_W:(h + 1) * HEAD_W]
            if rope_tables is not None:
                blk = blk * cos + pltpu.roll(blk, HEAD_W // 2, 1) * sin
            out_ref[h] = blk.astype(BF16)

    def tile(wsel_ref, rope):
        @pl.when(i == 0)
        def _meta_rows():
            project(xm_ref[...].astype(BF16), wsel_ref, (cosm_ref, sinm_ref) if rope else None, pm_ref)

        project(x_ref[...].astype(BF16), wsel_ref, (cos_ref, sin_ref) if rope else None, p_ref)

    @pl.when(j < n_qk_tiles)
    def _qk_tile():
        tile(wqk_ref, True)

    @pl.when(j >= n_qk_tiles)
    def _plain_tile():
        tile(w_ref, False)

    @pl.when(j == 0)
    def _tail():
        ba_ref[...] = tail(x_ref)

        @pl.when(i == 0)
        def _meta_tail():
            bam_ref[...] = tail(xm_ref)


def _proj(x2d, x_meta, wqk_t, w_t, layer, w_tail_t, cos, sin, cos_meta, sin_meta, bm):
    m, d = x2d.shape
    n_meta = x_meta.shape[0]
    heads = GROUP_W // HEAD_W
    n_blocks = N_GROUPS * heads
    nm = m // bm
    n_qk_tiles = wqk_t.shape[1] // GROUP_W
    const = lambda shape: pl.BlockSpec(shape, lambda j, i: (0,) * len(shape))
    row = lambda j, i: (i, 0)
    tiles_per_batch = cos.shape[0] // bm
    pos = lambda j, i: (i % tiles_per_batch, 0)
    return pl.pallas_call(
        functools.partial(_proj_kernel, heads=heads, n_qk_tiles=n_qk_tiles),
        out_shape=(jax.ShapeDtypeStruct((n_blocks, m, HEAD_W), BF16),
                   jax.ShapeDtypeStruct((n_blocks, n_meta, HEAD_W), BF16),
                   jax.ShapeDtypeStruct((m, LANES), F32),
                   jax.ShapeDtypeStruct((n_meta, LANES), F32)),
        grid=(N_GROUPS, nm),
        in_specs=[
            pl.BlockSpec((bm, d), row),
            const((n_meta, d)),
            pl.BlockSpec((None, GROUP_W, d), lambda j, i: (layer, jnp.minimum(j, n_qk_tiles - 1), 0)),
            pl.BlockSpec((None, GROUP_W, d), lambda j, i: (layer, jnp.maximum(j, n_qk_tiles), 0)),
            const((LANES, d)),
            pl.BlockSpec((bm, HEAD_W), pos),
            pl.BlockSpec((bm, HEAD_W), pos),
            const((n_meta, HEAD_W)),
            const((n_meta, HEAD_W)),
        ],
        out_specs=(
            pl.BlockSpec((heads, bm, HEAD_W), lambda j, i: (j, i, 0)),
            pl.BlockSpec((heads, n_meta, HEAD_W), lambda j, i: (j, 0, 0)),
            pl.BlockSpec((bm, LANES), lambda j, i: (jnp.where(j == 0, i, nm - 1), 0)),
            const((n_meta, LANES)),
        ),
        compiler_params=_params("arbitrary", "arbitrary"),
        name="proj",
    )(x2d, x_meta, wqk_t, w_t, w_tail_t, cos, sin, cos_meta, sin_meta)


def _attn_kernel(*refs, tq, hps, lam_init, meta_only):
    if meta_only:
        q_ref, km_ref, vm_ref, lq1, lk1, lq2, lk2, nw_ref, o_ref, m_sc, l_sc, acc_sc = refs
    else:
        (q_ref, k_ref, v_ref, km_ref, vm_ref, lq1, lk1, lq2, lk2, nw_ref, o_ref,
         m_sc, l_sc, acc_sc, s_sc, a_sc, p_sc) = refs
    lane = lax.broadcasted_iota(jnp.int32, (1, HEAD_W), 1)
    comp = (lane // (ATT_QK_DIM // 2)) % 2
    qc = []
    for h in range(hps):
        q = q_ref[h]
        zero = jnp.zeros_like(q)
        qc.append((jnp.where(comp == 0, q, zero), jnp.where(comp == 1, q, zero)))

    nc = 2 * hps

    def init_from_meta():
        s_meta = [lax.dot_general(km_ref[i // 2], qc[i // 2][i % 2], _NT, preferred_element_type=F32)
                  for i in range(nc)]
        p_meta = []
        for i in range(nc):
            m0 = jnp.max(s_meta[i], axis=0, keepdims=True)
            p = jnp.exp2(s_meta[i] - m0)
            m_sc[i] = m0
            l_sc[i] = jnp.sum(p, axis=0, keepdims=True)
            p_meta.append(p.astype(BF16))
        for i in range(nc):
            acc_sc[i] = lax.dot_general(vm_ref[i // 2], p_meta[i], _TN, preferred_element_type=F32)

    def scores(t, slot):
        off = pl.multiple_of(t * tq, tq)
        for h in range(hps):
            kt = k_ref[h, pl.ds(off, tq), :]
            for c in range(2):
                s_sc[slot * nc + 2 * h + c] = lax.dot_general(kt, qc[h][c], _NT,
                                                              preferred_element_type=F32)

    def absorb(t, slot, mask):
        off = pl.multiple_of(t * tq, tq)
        for i in range(nc):
            s = s_sc[slot * nc + i]
            if mask is not None:
                s = jnp.where(mask, s, -jnp.inf)
            m_old = m_sc[i]
            m_new = jnp.maximum(m_old, jnp.max(s, axis=0, keepdims=True))
            a = jnp.exp2(m_old - m_new)
            p = jnp.exp2(s - m_new)
            l_sc[i] = a * l_sc[i] + jnp.sum(p, axis=0, keepdims=True)
            m_sc[i] = m_new
            a_sc[i] = a
            p_sc[i] = p.astype(BF16)
        for h in range(hps):
            vt = v_ref[h, pl.ds(off, tq), :]
            for c in range(2):
                i = 2 * h + c
                acc_sc[i] = a_sc[i] * acc_sc[i] + lax.dot_general(vt, p_sc[i], _TN,
                                                                  preferred_element_type=F32)

    if meta_only:
        init_from_meta()
    else:
        qi = pl.program_id(2)
        keys = lax.broadcasted_iota(jnp.int32, (tq, tq), 0)
        queries = lax.broadcasted_iota(jnp.int32, (tq, tq), 1)
        diag = (keys // CHUNK) <= (queries // CHUNK)
        scores(0, 0)
        init_from_meta()

        def body(k, carry):
            scores(2 * k + 1, 1)
            absorb(2 * k, 0, None)
            scores(2 * k + 2, 0)
            absorb(2 * k + 1, 1, None)
            return carry

        lax.fori_loop(0, qi // 2, body, 0)

        @pl.when(qi % 2 == 1)
        def _odd_tail():
            scores(qi, 1)
            absorb(qi - 1, 0, None)
            absorb(qi, 1, diag)

        @pl.when(qi % 2 == 0)
        def _even_tail():
            absorb(qi, 0, diag)

    lam = (jnp.exp(jnp.sum(lq1[...] * lk1[...], axis=1, keepdims=True))
           - jnp.exp(jnp.sum(lq2[...] * lk2[...], axis=1, keepdims=True)) + lam_init)
    for h in range(hps):
        o = acc_sc[2 * h] / l_sc[2 * h] - lam * (acc_sc[2 * h + 1] / l_sc[2 * h + 1])
        ms = jnp.mean(o * o, axis=0, keepdims=True)
        o = o * lax.rsqrt(ms + RMS_EPS) * nw_ref[...] * (1.0 - lam_init)
        o_ref[:, h * HEAD_W:(h + 1) * HEAD_W] = o.T.astype(BF16)


def _attention(p_q, p_kv, p_meta, lams, norm_w, *, batch, seq, tq, hps, lam_init, meta_only):
    n_meta = p_meta.shape[1]
    rows = p_q.shape[1]
    nq = 1 if meta_only else seq // tq
    k_blk, v_blk = ATT_HEADS // hps, 2 * ATT_HEADS // hps
    small = [pl.BlockSpec((1, ATT_QK_DIM), lambda b, g, i: (0, 0))] * 4 + [
        pl.BlockSpec((HEAD_W, 1), lambda b, g, i: (0, 0))]
    meta_specs = [pl.BlockSpec((hps, n_meta, HEAD_W), lambda b, g, i: (k_blk + g, 0, 0)),
                  pl.BlockSpec((hps, n_meta, HEAD_W), lambda b, g, i: (v_blk + g, 0, 0))]
    q_spec = pl.BlockSpec((hps, tq, HEAD_W), lambda b, g, i: (g, b * nq + i, 0))
    if meta_only:
        in_specs = [q_spec] + meta_specs + small
        args = (p_q, p_meta, p_meta) + lams + (norm_w,)
    else:
        kv_specs = [pl.BlockSpec((hps, seq, HEAD_W), lambda b, g, i: (k_blk + g, b, 0)),
                    pl.BlockSpec((hps, seq, HEAD_W), lambda b, g, i: (v_blk + g, b, 0))]
        in_specs = [q_spec] + kv_specs + meta_specs + small
        args = (p_q, p_kv, p_kv, p_meta, p_meta) + lams + (norm_w,)
    scratch = [pltpu.VMEM((2 * hps, 1, tq), F32), pltpu.VMEM((2 * hps, 1, tq), F32),
               pltpu.VMEM((2 * hps, HEAD_W, tq), F32)]
    if not meta_only:
        scratch += [pltpu.VMEM((2 * 2 * hps, tq, tq), F32),
                    pltpu.VMEM((2 * hps, 1, tq), F32),
                    pltpu.VMEM((2 * hps, tq, tq), BF16)]
    return pl.pallas_call(
        functools.partial(_attn_kernel, tq=tq, hps=hps, lam_init=lam_init, meta_only=meta_only),
        out_shape=jax.ShapeDtypeStruct((rows, GROUP_W), BF16),
        grid=(batch, ATT_HEADS // hps, nq),
        in_specs=in_specs,
        out_specs=pl.BlockSpec((tq, hps * HEAD_W), lambda b, g, i: (b * nq + i, g)),
        scratch_shapes=scratch,
        compiler_params=_params("parallel", "parallel", "arbitrary"),
        name="attn_meta" if meta_only else "attn",
    )(*args)


def _gdn_prep_kernel(*refs, rows, sub, n_pad, n_cast):
    (q_ref, k_ref, v_ref, hq_ref, hk_ref, hv_ref, mh_ref, cw_ref, ba_ref, alog_ref, dtb_ref) = refs[:11]
    cast_src = refs[11:11 + n_cast]
    u_ref, wq_ref, kd_ref, a_ref, gl_ref = refs[11 + n_cast:16 + n_cast]
    cast_dst = refs[16 + n_cast:16 + 2 * n_cast]
    xs_sc, gc_sc, pw_sc, x_sc, t_sc = refs[16 + 2 * n_cast:]
    for src, dst in zip(cast_src, cast_dst):
        dst[...] = src[...].astype(BF16)

    first = pl.program_id(1) == 0
    n_sub = rows // sub
    ba = ba_ref[...]
    beta_all = _sigmoid(ba)
    g_all = -jnp.exp(alog_ref[...]) * _softplus(ba + dtb_ref[...])
    row = lax.broadcasted_iota(jnp.int32, (rows, 1), 0)
    if n_pad:
        beta_all = jnp.where(row >= n_pad, beta_all, 0.0)
        g_all = jnp.where(row >= n_pad, g_all, 0.0)

    pos = row % CHUNK
    gc_all = g_all
    shift = 1
    while shift < CHUNK:
        gc_all = gc_all + jnp.where(pos >= shift, pltpu.roll(gc_all, shift, 0), 0.0)
        shift *= 2
    gc_sc[...] = gc_all
    gl_all = jnp.concatenate(
        [jnp.broadcast_to(gc_sc[(c + 1) * CHUNK - 1:(c + 1) * CHUNK, :], (CHUNK, LANES))
         for c in range(rows // CHUNK)], axis=0)
    gl_ref[...] = gl_all
    gct_all = gc_all.T

    ri = lax.broadcasted_iota(jnp.int32, (sub, sub), 0)
    ci = lax.broadcasted_iota(jnp.int32, (sub, sub), 1)
    same = (ri // CHUNK) == (ci // CHUNK)
    tril = jnp.logical_and(same, ci <= ri)
    strict = jnp.logical_and(same, ci < ri)
    eye = jnp.where(ri == ci, 1.0, 0.0).astype(F32)

    def conv_act(main_ref, halo_ref, stream, h):
        halo = jnp.where(first, mh_ref[stream * DN_HEADS + h], halo_ref[h])
        xs_sc[stream, 0:HALO, :] = halo.astype(F32)
        xs_sc[stream, HALO:, :] = main_ref[h].astype(F32)
        w = cw_ref[stream * DN_HEADS + h]
        y = xs_sc[stream, pl.ds(HALO - DN_CONV + 1, rows), :] * w[0:1]
        for j in range(1, DN_CONV):
            y = y + xs_sc[stream, pl.ds(HALO - DN_CONV + 1 + j, rows), :] * w[j:j + 1]
        return _silu(y)

    for h in range(DN_HEADS):
        q = conv_act(q_ref, hq_ref, 0, h)
        k = conv_act(k_ref, hk_ref, 1, h)
        v = conv_act(v_ref, hv_ref, 2, h)
        q = q * lax.rsqrt(jnp.sum(q * q, axis=1, keepdims=True) + RMS_EPS) * (HEAD_W ** -0.5)
        k = k * lax.rsqrt(jnp.sum(k * k, axis=1, keepdims=True) + RMS_EPS)
        beta = beta_all[:, h:h + 1]
        gc = gc_all[:, DN_HEADS + h:DN_HEADS + h + 1]
        gl = gl_all[:, DN_HEADS + h:DN_HEADS + h + 1]
        egc = jnp.exp(gc)
        kb = k * beta
        k16 = k.astype(BF16)
        kb16 = kb.astype(BF16)
        q16 = q.astype(BF16)
        x_all = jnp.concatenate([v * beta, kb * egc], axis=1)
        qd16 = (q * egc).astype(BF16)
        kd_ref[h] = (k * jnp.exp(gl - gc)).astype(BF16)
        for c in range(rows // CHUNK):
            wq_ref[h, (2 * c + 1) * CHUNK:(2 * c + 2) * CHUNK, :] = qd16[c * CHUNK:(c + 1) * CHUNK]
        for s in range(n_sub):
            lo, hi = s * sub, (s + 1) * sub
            gcr = gct_all[DN_HEADS + h:DN_HEADS + h + 1, lo:hi]
            kk = lax.dot_general(kb16[lo:hi], k16[lo:hi], _NT, preferred_element_type=F32)
            qk = lax.dot_general(q16[lo:hi], k16[lo:hi], _NT, preferred_element_type=F32)
            dec = jnp.exp(jnp.where(tril, gc[lo:hi] - gcr, -jnp.inf))
            a_ref[h, lo:hi, :] = (qk * dec).astype(BF16)
            m0 = jnp.where(strict, kk * dec, 0.0)
            pw_sc[h * n_sub + s] = m0.astype(BF16)
            t_sc[h * n_sub + s] = eye - m0
            x_sc[h * n_sub + s] = x_all[lo:hi].astype(BF16)

    items = range(DN_HEADS * n_sub)
    for _ in range(int(math.log2(CHUNK)) - 1):
        for it in items:
            p16 = pw_sc[it]
            pw_sc[it] = jnp.dot(p16, p16, preferred_element_type=F32).astype(BF16)
        for it in items:
            t = t_sc[it]
            t_sc[it] = t + jnp.dot(pw_sc[it], t.astype(BF16), preferred_element_type=F32)
    for it in items:
        h, lo = it // n_sub, (it % n_sub) * sub
        x = jnp.dot(t_sc[it].astype(BF16), x_sc[it], preferred_element_type=F32)
        u_ref[h, lo:lo + sub, :] = x[:, :HEAD_W].astype(BF16)
        w16 = x[:, HEAD_W:].astype(BF16)
        for c in range(sub // CHUNK):
            r0 = lo + c * CHUNK
            wq_ref[h, 2 * r0:2 * r0 + CHUNK, :] = w16[c * CHUNK:(c + 1) * CHUNK]


def _gdn_prep(p, p_halo_src, meta_halo, conv_w, ba, alog, dtb, *, batch, seq, rows, sub, n_pad,
              cast=(), layer=0):
    m = batch * seq
    nt = seq // rows
    steps = batch * nt
    cast_rows = [w.shape[1] // steps for w in cast]
    assert all(w.shape[1] % steps == 0 and r % (2 * HALO) == 0 for w, r in zip(cast, cast_rows))
    cast_in = [pl.BlockSpec((None, r, w.shape[2]), lambda b, t: (layer, b * nt + t, 0))
               for w, r in zip(cast, cast_rows)]
    cast_out = [pl.BlockSpec((r, w.shape[2]), lambda b, t: (b * nt + t, 0)) for w, r in zip(cast, cast_rows)]
    cast_shape = [jax.ShapeDtypeStruct(w.shape[1:], BF16) for w in cast]
    g_q, g_k, g_v = 3, 4, 5
    hb = rows // HALO

    def main_spec(g):
        return pl.BlockSpec((DN_HEADS, rows, HEAD_W), lambda b, t: (g, b * nt + t, 0))

    def halo_spec(g):
        return pl.BlockSpec((DN_HEADS, HALO, HEAD_W),
                            lambda b, t: (g, jnp.maximum((b * nt + t) * hb - 1, 0), 0))

    full = lambda shape: pl.BlockSpec(shape, lambda b, t: (0,) * len(shape))
    out_shape = (jax.ShapeDtypeStruct((DN_HEADS, m, HEAD_W), BF16),
                 jax.ShapeDtypeStruct((DN_HEADS, 2 * m, HEAD_W), BF16),
                 jax.ShapeDtypeStruct((DN_HEADS, m, HEAD_W), BF16),
                 jax.ShapeDtypeStruct((DN_HEADS, m, sub), BF16),
                 jax.ShapeDtypeStruct((m, LANES), F32)) + tuple(cast_shape)
    out_specs = (pl.BlockSpec((DN_HEADS, rows, HEAD_W), lambda b, t: (0, b * nt + t, 0)),
                 pl.BlockSpec((DN_HEADS, 2 * rows, HEAD_W), lambda b, t: (0, b * nt + t, 0)),
                 pl.BlockSpec((DN_HEADS, rows, HEAD_W), lambda b, t: (0, b * nt + t, 0)),
                 pl.BlockSpec((DN_HEADS, rows, sub), lambda b, t: (0, b * nt + t, 0)),
                 pl.BlockSpec((rows, LANES), lambda b, t: (b * nt + t, 0))) + tuple(cast_out)
    outs = pl.pallas_call(
        functools.partial(_gdn_prep_kernel, rows=rows, sub=sub, n_pad=n_pad, n_cast=len(cast)),
        out_shape=out_shape,
        grid=(batch, nt),
        in_specs=[main_spec(g_q), main_spec(g_k), main_spec(g_v),
                  halo_spec(g_q), halo_spec(g_k), halo_spec(g_v),
                  full(meta_halo.shape), full(conv_w.shape),
                  pl.BlockSpec((rows, LANES), lambda b, t: (b * nt + t, 0)),
                  full(alog.shape), full(dtb.shape)] + cast_in,
        out_specs=out_specs,
        scratch_shapes=[pltpu.VMEM((3, rows + HALO, HEAD_W), F32), pltpu.VMEM((rows, LANES), F32),
                        pltpu.VMEM((DN_HEADS * (rows // sub), sub, sub), BF16),
                        pltpu.VMEM((DN_HEADS * (rows // sub), sub, 2 * HEAD_W), BF16),
                        pltpu.VMEM((DN_HEADS * (rows // sub), sub, sub), F32)],
        compiler_params=_params("parallel", "parallel"),
        name="gdn_prep",
    )(p, p, p, p_halo_src, p_halo_src, p_halo_src, meta_halo, conv_w, ba, alog, dtb, *cast)
    return outs[:5], outs[5:]


def _gdn_scan_kernel(u_ref, wq_ref, kd_ref, a_ref, gl_ref, z_ref,
                     um_ref, wqm_ref, kdm_ref, am_ref, glm_ref, zm_ref, nw_ref,
                     o_ref, om_ref, s_sc, vbuf_sc, sw_sc, *, rows, sub, hpg):
    hg = pl.program_id(1)
    t = pl.program_id(2)
    lane = lax.broadcasted_iota(jnp.int32, (1, LANES), 1)
    nw = nw_ref[...]

    def decay_total(gl_row, h):
        sel = lane == DN_HEADS + hg * hpg + h
        return jnp.sum(jnp.where(sel, jnp.exp(gl_row), 0.0), axis=1, keepdims=True)

    def gate(o, z):
        ms = jnp.mean(o * o, axis=1, keepdims=True)
        return (o * lax.rsqrt(ms + RMS_EPS) * nw * _silu(z.astype(F32))).astype(BF16)

    def run_chunk(r0, refs, out_ref, out_r0):
        u_r, wq_r, kd_r, a_r, gl_r, z_r = refs
        slot = r0 % sub
        gl_row = gl_r[r0:r0 + 1, :]
        for h in range(hpg):
            sw_sc[h] = jnp.dot(wq_r[h, 2 * r0:2 * r0 + 2 * CHUNK, :], s_sc[h].astype(BF16),
                               preferred_element_type=F32)
        for h in range(hpg):
            v_new = u_r[h, r0:r0 + CHUNK, :].astype(F32) - sw_sc[h, 0:CHUNK, :]
            vbuf_sc[h, slot:slot + CHUNK, :] = v_new.astype(BF16)
        for h in range(hpg):
            o = sw_sc[h, CHUNK:, :] + jnp.dot(a_r[h, r0:r0 + CHUNK, :], vbuf_sc[h],
                                              preferred_element_type=F32)
            s_sc[h] = decay_total(gl_row, h) * s_sc[h] + lax.dot_general(
                kd_r[h, r0:r0 + CHUNK, :], vbuf_sc[h, slot:slot + CHUNK, :], _TN,
                preferred_element_type=F32)
            out_ref[out_r0:out_r0 + CHUNK, h * HEAD_W:(h + 1) * HEAD_W] = gate(
                o, z_r[h, r0:r0 + CHUNK, :])

    @pl.when(t == 0)
    def _meta_chunk():
        vbuf_sc[...] = jnp.zeros_like(vbuf_sc)
        s_sc[...] = jnp.zeros_like(s_sc)
        run_chunk(sub - CHUNK, (um_ref, wqm_ref, kdm_ref, am_ref, glm_ref, zm_ref), om_ref, 0)

    for c in range(rows // CHUNK):
        run_chunk(c * CHUNK, (u_ref, wq_ref, kd_ref, a_ref, gl_ref, z_ref), o_ref, c * CHUNK)


def _gdn_scan(real, meta, p, p_meta_pad, norm_w, *, batch, seq, rows, sub, hpg):
    u, wq, kd, a, gl = real
    um, wqm, kdm, am, glm = meta
    m = batch * seq
    nt = seq // rows
    ngrp = DN_HEADS // hpg
    z_blk = (6 * DN_HEADS) // hpg
    mrows = um.shape[1]
    assert mrows == sub

    def real_spec(r, w):
        return pl.BlockSpec((hpg, r, w), lambda b, g, t: (g, b * nt + t, 0))

    def meta_spec(r, w):
        return pl.BlockSpec((hpg, r, w), lambda b, g, t: (g, 0, 0))

    return pl.pallas_call(
        functools.partial(_gdn_scan_kernel, rows=rows, sub=sub, hpg=hpg),
        out_shape=(jax.ShapeDtypeStruct((m, GROUP_W), BF16),
                   jax.ShapeDtypeStruct((batch, CHUNK, GROUP_W), BF16)),
        grid=(batch, ngrp, nt),
        in_specs=[real_spec(rows, HEAD_W), real_spec(2 * rows, HEAD_W), real_spec(rows, HEAD_W),
                  real_spec(rows, sub),
                  pl.BlockSpec((rows, LANES), lambda b, g, t: (b * nt + t, 0)),
                  pl.BlockSpec((hpg, rows, HEAD_W), lambda b, g, t: (z_blk + g, b * nt + t, 0)),
                  meta_spec(mrows, HEAD_W), meta_spec(2 * mrows, HEAD_W), meta_spec(mrows, HEAD_W),
                  meta_spec(mrows, mrows),
                  pl.BlockSpec((mrows, LANES), lambda b, g, t: (0, 0)),
                  pl.BlockSpec((hpg, mrows, HEAD_W), lambda b, g, t: (z_blk + g, 0, 0)),
                  pl.BlockSpec((1, HEAD_W), lambda b, g, t: (0, 0))],
        out_specs=(pl.BlockSpec((rows, hpg * HEAD_W), lambda b, g, t: (b * nt + t, g)),
                   pl.BlockSpec((None, CHUNK, hpg * HEAD_W), lambda b, g, t: (b, 0, g))),
        scratch_shapes=[pltpu.VMEM((hpg, HEAD_W, HEAD_W), F32),
                        pltpu.VMEM((hpg, sub, HEAD_W), BF16),
                        pltpu.VMEM((hpg, 2 * CHUNK, HEAD_W), F32)],
        compiler_params=_params("parallel", "parallel", "arbitrary"),
        name="gdn_scan",
    )(u, wq, kd, a, gl, p, um, wqm, kdm, am, glm, p_meta_pad, norm_w)


def _wout_ln_kernel(oa_ref, od_ref, w_ref, x_ref, g_ref, b_ref, h_ref, *, alpha, n_parts):
    half = oa_ref.shape[1]
    part = oa_ref.shape[0] // n_parts
    for r in range(n_parts):
        rows = slice(r * part, (r + 1) * part)
        mix = (jnp.dot(oa_ref[rows, :], w_ref[0:half, :], preferred_element_type=F32)
               + jnp.dot(od_ref[rows, :], w_ref[half:, :], preferred_element_type=F32))
        h_ref[rows, :] = _layer_norm(alpha * x_ref[rows, :] + mix, g_ref[...], b_ref[...])


def _wout_ln(o_att, o_dn, w_out, x2d, g, b, *, bm, alpha):
    m, d = x2d.shape
    half = o_att.shape[1]
    return pl.pallas_call(
        functools.partial(_wout_ln_kernel, alpha=alpha, n_parts=4 if bm % (4 * 2 * HALO) == 0 else 1),
        out_shape=jax.ShapeDtypeStruct((m, d), F32),
        grid=(m // bm,),
        in_specs=[pl.BlockSpec((bm, half), lambda i: (i, 0)),
                  pl.BlockSpec((bm, half), lambda i: (i, 0)),
                  pl.BlockSpec((2 * half, d), lambda i: (0, 0)),
                  pl.BlockSpec((bm, d), lambda i: (i, 0)),
                  pl.BlockSpec((1, d), lambda i: (0, 0)),
                  pl.BlockSpec((1, d), lambda i: (0, 0))],
        out_specs=pl.BlockSpec((bm, d), lambda i: (i, 0)),
        compiler_params=_params("parallel"),
        name="wout_ln",
    )(o_att, o_dn, w_out, x2d, g, b)


def _ffn_kernel(h_ref, hprev_ref, hmeta_ref, wg_ref, wu_ref, cw_ref, cb_ref, wd_ref, g_ref, b_ref,
                o_ref, xb_sc, gate_sc, *, bm, tiles_per_batch, alpha):
    i = pl.program_id(0)
    f = pl.program_id(1)

    @pl.when(f == 0)
    def _init():
        halo = jnp.where(i % tiles_per_batch == 0, hmeta_ref[...], hprev_ref[...])
        xb_sc[0:HALO, :] = halo.astype(BF16)
        xb_sc[HALO:, :] = h_ref[...].astype(BF16)
        o_ref[...] = jnp.zeros_like(o_ref)

    gate_sc[...] = jnp.dot(xb_sc[...], wg_ref[...], preferred_element_type=F32)
    up = jnp.dot(xb_sc[HALO:, :], wu_ref[...], preferred_element_type=F32)
    cw = cw_ref[...]
    hg = cb_ref[...] + gate_sc[pl.ds(HALO - FFN_CONV + 1, bm), :] * cw[0:1]
    for j in range(1, FFN_CONV):
        hg = hg + gate_sc[pl.ds(HALO - FFN_CONV + 1 + j, bm), :] * cw[j:j + 1]
    act = (_silu(hg) * up).astype(BF16)
    o_ref[...] += jnp.dot(act, wd_ref[...], preferred_element_type=F32)

    @pl.when(f == pl.num_programs(1) - 1)
    def _finish():
        o_ref[...] = _layer_norm(alpha * h_ref[...] + o_ref[...], g_ref[...], b_ref[...])


def _ffn(h1, h1_meta, w_gate, w_up, conv_w, conv_b, w_down, g, b, *, seq, bm, tf, alpha):
    m, d = h1.shape
    dff = w_gate.shape[1]
    tiles_per_batch = seq // bm
    hb = bm // HALO
    meta_blk = h1_meta.shape[0] // HALO - 1
    return pl.pallas_call(
        functools.partial(_ffn_kernel, bm=bm, tiles_per_batch=tiles_per_batch, alpha=alpha),
        out_shape=jax.ShapeDtypeStruct((m, d), F32),
        grid=(m // bm, dff // tf),
        in_specs=[pl.BlockSpec((bm, d), lambda i, f: (i, 0)),
                  pl.BlockSpec((HALO, d), lambda i, f: (jnp.maximum(i * hb - 1, 0), 0)),
                  pl.BlockSpec((HALO, d), lambda i, f: (meta_blk, 0)),
                  pl.BlockSpec((d, tf), lambda i, f: (0, f)),
                  pl.BlockSpec((d, tf), lambda i, f: (0, f)),
                  pl.BlockSpec((FFN_CONV, tf), lambda i, f: (0, f)),
                  pl.BlockSpec((1, tf), lambda i, f: (0, f)),
                  pl.BlockSpec((tf, d), lambda i, f: (f, 0)),
                  pl.BlockSpec((1, d), lambda i, f: (0, 0)),
                  pl.BlockSpec((1, d), lambda i, f: (0, 0))],
        out_specs=pl.BlockSpec((bm, d), lambda i, f: (i, 0)),
        scratch_shapes=[pltpu.VMEM((bm + HALO, d), BF16),
                        pltpu.VMEM((bm + HALO, tf), F32)],
        compiler_params=_params("parallel", "arbitrary"),
        name="ffn",
    )(h1, h1, h1_meta, w_gate, w_up, conv_w, conv_b, w_down, g, b)


def _largest_tile(n, cap, mult):
    best = None
    for c in range(mult, min(n, cap) + 1, mult):
        if n % c == 0:
            best = c
    assert best is not None, (n, cap, mult)
    return best


def _split_halves(w_qk_t):
    n_comp, quarter = HEAD_W // ATT_QK_DIM, ATT_QK_DIM // 2
    depth, n, d = w_qk_t.shape
    w = w_qk_t.reshape(depth, -1, n_comp, 2, quarter, d)
    return jnp.swapaxes(w, 2, 3).reshape(depth, n, d)


def _rope_tables(n_pos):
    n_comp = HEAD_W // ATT_QK_DIM
    inv_freq = ROPE_THETA ** (-jnp.arange(0, ATT_QK_DIM, 2, dtype=F32) / ATT_QK_DIM)
    ang = jnp.arange(n_pos, dtype=F32)[:, None] * inv_freq[None, :]
    cos, sin = jnp.cos(ang), jnp.sin(ang)
    cos_t = jnp.tile(cos, (1, 2 * n_comp))
    sin_t = jnp.concatenate([jnp.tile(-sin, (1, n_comp)), jnp.tile(sin, (1, n_comp))], axis=1)
    assert cos_t.shape[1] == HEAD_W
    return cos_t, sin_t


def kernel(x, meta_tokens, w_in, conv_qkv_w, a_log, dt_bias, lambda_q1, lambda_k1, lambda_q2, lambda_k2,
           diff_norm_w, delta_norm_w, w_out, ln1_g, ln1_b, ffn_w_gate, ffn_w_up, ffn_conv_w, ffn_conv_b,
           ffn_w_down, ln2_g, ln2_b):
    batch, seq, d = x.shape
    n_meta = meta_tokens.shape[0]
    depth = w_in.shape[0]
    n_main = N_GROUPS * GROUP_W
    assert w_in.shape[2] == n_main + TAIL_COLS and n_meta % HALO == 0 and n_meta <= CHUNK
    m = batch * seq
    alpha = (2.0 * depth) ** 0.25

    bm_proj = _largest_tile(seq, 1024, HALO)
    bm_row = _largest_tile(seq, 512, HALO)
    tq = _largest_tile(seq, 256, CHUNK)
    rows = _largest_tile(seq, 256, LANES)
    sub = LANES
    hpg = DN_HEADS
    att_hps = ATT_HEADS

    cos_t, sin_t = _rope_tables(n_meta + seq)
    cos_meta, sin_meta = cos_t[:n_meta], sin_t[:n_meta]
    cos_real, sin_real = cos_t[n_meta:], sin_t[n_meta:]

    h = x.reshape(m, d)
    h_meta = meta_tokens.astype(x.dtype)
    n_qk = 2 * GROUP_W
    w_in_t = jnp.swapaxes(w_in, 1, 2)
    w_t = w_in_t.astype(BF16)
    wqk_t = _split_halves(w_t[:, :n_qk])
    pad_rows = sub - n_meta
    for l in range(depth):
        w_tail_t = jnp.pad(w_t[l, n_main:], ((0, LANES - TAIL_COLS), (0, 0)))
        lane_pad = (DN_HEADS, LANES - TAIL_COLS)
        alog = jnp.pad(a_log[l].astype(F32), lane_pad)[None]
        dtb = jnp.pad(dt_bias[l].astype(F32), lane_pad)[None]
        conv_w = conv_qkv_w[l].reshape(DN_CONV, 3 * DN_HEADS, HEAD_W).transpose(1, 0, 2)
        lams = tuple(v[l][None].astype(F32) for v in (lambda_q1, lambda_k1, lambda_q2, lambda_k2))
        lam_init = 0.8 - 0.6 * math.exp(-0.3 * l)

        p, p_meta, ba, ba_meta = _proj(h, h_meta, wqk_t, w_t, l, w_tail_t, cos_real, sin_real,
                                       cos_meta, sin_meta, bm_proj)

        att_nw = diff_norm_w[l].astype(F32)[:, None]
        o_att = _attention(p, p, p_meta, lams, att_nw, batch=batch, seq=seq, tq=tq, hps=att_hps,
                           lam_init=lam_init, meta_only=False)
        q_meta = jnp.pad(p_meta[:ATT_HEADS], ((0, 0), (0, LANES - n_meta), (0, 0)))
        o_att_meta = _attention(q_meta, None, p_meta, lams, att_nw, batch=1, seq=LANES, tq=LANES,
                                hps=att_hps, lam_init=lam_init, meta_only=True)[:n_meta]

        p_meta_pad = jnp.pad(p_meta, ((0, 0), (pad_rows, 0), (0, 0)))
        ba_meta_pad = jnp.pad(ba_meta, ((pad_rows, 0), (0, 0)))
        meta_halo = p_meta[3 * DN_HEADS:6 * DN_HEADS, n_meta - HALO:, :]
        zero_halo = jnp.zeros_like(meta_halo)
        prep_real, (w_out_b, w_gate_b, w_up_b, w_down_b) = _gdn_prep(
            p, p, meta_halo, conv_w, ba, alog, dtb, batch=batch, seq=seq, rows=rows, sub=sub, n_pad=0,
            cast=(w_out, ffn_w_gate, ffn_w_up, ffn_w_down), layer=l)
        prep_meta, _ = _gdn_prep(p_meta_pad, p_meta_pad, zero_halo, conv_w, ba_meta_pad, alog, dtb, batch=1,
                                 seq=sub, rows=sub, sub=sub, n_pad=pad_rows)
        o_dn, o_dn_meta = _gdn_scan(prep_real, prep_meta, p, p_meta_pad, delta_norm_w[l][None],
                                    batch=batch, seq=seq, rows=rows, sub=sub, hpg=hpg)
        o_dn_meta = o_dn_meta[0, CHUNK - n_meta:]

        g1, b1 = ln1_g[l][None].astype(F32), ln1_b[l][None].astype(F32)
        h1 = _wout_ln(o_att, o_dn, w_out_b, h, g1, b1, bm=bm_row, alpha=alpha)
        h1_meta = _wout_ln(o_att_meta, o_dn_meta, w_out_b, h_meta, g1, b1, bm=n_meta, alpha=alpha)

        dff = ffn_w_gate.shape[2]
        tf = _largest_tile(dff, 512, LANES)
        ffn_args = (w_gate_b, w_up_b, ffn_conv_w[l].astype(F32),
                    ffn_conv_b[l][None].astype(F32), w_down_b,
                    ln2_g[l][None].astype(F32), ln2_b[l][None].astype(F32))
        h_next = _ffn(h1, h1_meta, *ffn_args, seq=seq, bm=bm_row, tf=tf, alpha=alpha)
        if l + 1 < depth:
            zero_meta = jnp.zeros_like(h1_meta)
            h_meta = _ffn(h1_meta, zero_meta, *ffn_args, seq=n_meta, bm=n_meta, tf=tf, alpha=alpha)
        h = h_next
    return h.reshape(batch, seq, d)
```

```python
import functools
import math

import jax
import jax.numpy as jnp
from jax import lax
from jax.experimental import pallas as pl
from jax.experimental.pallas import tpu as pltpu

F32 = jnp.float32
BF16 = jnp.bfloat16

CHUNK = 64
ATT_HEADS = 8
ATT_QK_DIM = 64
HEAD_W = 128
DN_HEADS = 8
DN_CONV = 4
FFN_CONV = 3
ROPE_THETA = 10000.0
LN_EPS = 1e-5
RMS_EPS = 1e-6
GROUP_W = ATT_HEADS * HEAD_W
Q_SCALE = ATT_QK_DIM ** -0.5 * math.log2(math.e)
N_GROUPS = 7
TAIL_COLS = 2 * DN_HEADS

LANES = 128
HALO = 8
ROW_PARTS = 4
VMEM_LIMIT = 56 * 1024 * 1024

_NT = (((1,), (1,)), ((), ()))
_TN = (((0,), (0,)), ((), ()))


def _params(*sem):
    return pltpu.CompilerParams(dimension_semantics=sem, vmem_limit_bytes=VMEM_LIMIT)


def _sigmoid(x):
    return 1.0 / (1.0 + jnp.exp(-x))


def _silu(x):
    return x * _sigmoid(x)


def _softplus(x):
    return jnp.maximum(x, 0.0) + jnp.log(1.0 + jnp.exp(-jnp.abs(x)))


def _layer_norm(y, g, b):
    mu = jnp.mean(y, axis=1, keepdims=True)
    d = y - mu
    var = jnp.mean(d * d, axis=1, keepdims=True)
    return d * lax.rsqrt(var + LN_EPS) * g + b


def _proj_kernel(x_ref, xm_ref, wqk_ref, w_ref, wt_ref, cos_ref, sin_ref, cosm_ref, sinm_ref,
                 p_ref, pm_ref, ba_ref, bam_ref, *, heads, n_qk_tiles):
    j = pl.program_id(0)
    i = pl.program_id(1)

    def tail(x_r):
        return lax.dot_general(x_r[...].astype(BF16), wt_ref[...], _NT, preferred_element_type=F32)

    def project(xh, wsel_ref, rope_tables, out_ref):
        n_rows = xh.shape[0]
        n_parts = ROW_PARTS if n_rows % (ROW_PARTS * 2 * HALO) == 0 else 1
        part = n_rows // n_parts
        if rope_tables is not None:
            scale = jnp.where(j == 0, Q_SCALE, 1.0).astype(F32)
        for r in range(n_parts):
            lo = r * part
            acc = lax.dot_general(xh[lo:lo + part], wsel_ref[...], _NT, preferred_element_type=F32)
            if rope_tables is not None:
                cos = rope_tables[0][lo:lo + part, :] * scale
                sin = rope_tables[1][lo:lo + part, :] * scale
            for h in range(heads):
                blk = acc[:, h * HEAD_W:(h + 1) * HEAD_W]
                if rope_tables is not None:
                    blk = blk * cos + pltpu.roll(blk, HEAD_W // 2, 1) * sin
                out_ref[h, lo:lo + part, :] = blk.astype(BF16)

    def tile(wsel_ref, rope):
        @pl.when(i == 0)
        def _meta_rows():
            project(xm_ref[...].astype(BF16), wsel_ref, (cosm_ref, sinm_ref) if rope else None, pm_ref)

        project(x_ref[...].astype(BF16), wsel_ref, (cos_ref, sin_ref) if rope else None, p_ref)

    @pl.when(j < n_qk_tiles)
    def _qk_tile():
        tile(wqk_ref, True)

    @pl.when(j >= n_qk_tiles)
    def _plain_tile():
        tile(w_ref, False)

    @pl.when(j == 0)
    def _tail():
        ba_ref[...] = tail(x_ref)

        @pl.when(i == 0)
        def _meta_tail():
            bam_ref[...] = tail(xm_ref)


def _proj(x2d, x_meta, wqk_t, w_t, layer, w_tail_t, cos, sin, cos_meta, sin_meta, bm):
    m, d = x2d.shape
    n_meta = x_meta.shape[0]
    heads = GROUP_W // HEAD_W
    n_blocks = N_GROUPS * heads
    nm = m // bm
    n_qk_tiles = wqk_t.shape[1] // GROUP_W
    const = lambda shape: pl.BlockSpec(shape, lambda j, i: (0,) * len(shape))
    row = lambda j, i: (i, 0)
    tiles_per_batch = cos.shape[0] // bm
    pos = lambda j, i: (i % tiles_per_batch, 0)
    return pl.pallas_call(
        functools.partial(_proj_kernel, heads=heads, n_qk_tiles=n_qk_tiles),
        out_shape=(jax.ShapeDtypeStruct((n_blocks, m, HEAD_W), BF16),
                   jax.ShapeDtypeStruct((n_blocks, n_meta, HEAD_W), BF16),
                   jax.ShapeDtypeStruct((m, LANES), F32),
                   jax.ShapeDtypeStruct((n_meta, LANES), F32)),
        grid=(N_GROUPS, nm),
        in_specs=[
            pl.BlockSpec((bm, d), row),
            const((n_meta, d)),
            pl.BlockSpec((None, GROUP_W, d), lambda j, i: (layer, jnp.minimum(j, n_qk_tiles - 1), 0)),
            pl.BlockSpec((None, GROUP_W, d), lambda j, i: (layer, jnp.maximum(j, n_qk_tiles), 0)),
            const((LANES, d)),
            pl.BlockSpec((bm, HEAD_W), pos),
            pl.BlockSpec((bm, HEAD_W), pos),
            const((n_meta, HEAD_W)),
            const((n_meta, HEAD_W)),
        ],
        out_specs=(
            pl.BlockSpec((heads, bm, HEAD_W), lambda j, i: (j, i, 0)),
            pl.BlockSpec((heads, n_meta, HEAD_W), lambda j, i: (j, 0, 0)),
            pl.BlockSpec((bm, LANES), lambda j, i: (jnp.where(j == 0, i, nm - 1), 0)),
            const((n_meta, LANES)),
        ),
        compiler_params=_params("arbitrary", "arbitrary"),
        name="proj",
    )(x2d, x_meta, wqk_t, w_t, w_tail_t, cos, sin, cos_meta, sin_meta)


def _attn_kernel(*refs, tq, hps, lam_init, meta_only):
    if meta_only:
        q_ref, km_ref, vm_ref, lq1, lk1, lq2, lk2, nw_ref, o_ref, m_sc, l_sc, acc_sc = refs
    else:
        (q_ref, k_ref, v_ref, km_ref, vm_ref, lq1, lk1, lq2, lk2, nw_ref, o_ref,
         m_sc, l_sc, acc_sc, s_sc, a_sc, p_sc) = refs
    lane = lax.broadcasted_iota(jnp.int32, (1, HEAD_W), 1)
    comp = (lane // (ATT_QK_DIM // 2)) % 2
    qc = []
    for h in range(hps):
        q = q_ref[h]
        zero = jnp.zeros_like(q)
        qc.append((jnp.where(comp == 0, q, zero), jnp.where(comp == 1, q, zero)))

    nc = 2 * hps

    def init_from_meta():
        s_meta = [lax.dot_general(km_ref[i // 2], qc[i // 2][i % 2], _NT, preferred_element_type=F32)
                  for i in range(nc)]
        p_meta = []
        for i in range(nc):
            m0 = jnp.max(s_meta[i], axis=0, keepdims=True)
            p = jnp.exp2(s_meta[i] - m0)
            m_sc[i] = m0
            l_sc[i] = jnp.sum(p, axis=0, keepdims=True)
            p_meta.append(p.astype(BF16))
        for i in range(nc):
            acc_sc[i] = lax.dot_general(vm_ref[i // 2], p_meta[i], _TN, preferred_element_type=F32)

    def scores(t, slot):
        off = pl.multiple_of(t * tq, tq)
        for h in range(hps):
            kt = k_ref[h, pl.ds(off, tq), :]
            for c in range(2):
                s_sc[slot * nc + 2 * h + c] = lax.dot_general(kt, qc[h][c], _NT,
                                                              preferred_element_type=F32)

    def absorb(t, slot, mask):
        off = pl.multiple_of(t * tq, tq)
        for i in range(nc):
            s = s_sc[slot * nc + i]
            if mask is not None:
                s = jnp.where(mask, s, -jnp.inf)
            m_old = m_sc[i]
            m_new = jnp.maximum(m_old, jnp.max(s, axis=0, keepdims=True))
            a = jnp.exp2(m_old - m_new)
            p = jnp.exp2(s - m_new)
            l_sc[i] = a * l_sc[i] + jnp.sum(p, axis=0, keepdims=True)
            m_sc[i] = m_new
            a_sc[i] = a
            p_sc[i] = p.astype(BF16)
        for h in range(hps):
            vt = v_ref[h, pl.ds(off, tq), :]
            for c in range(2):
                i = 2 * h + c
                acc_sc[i] = a_sc[i] * acc_sc[i] + lax.dot_general(vt, p_sc[i], _TN,
                                                                  preferred_element_type=F32)

    if meta_only:
        init_from_meta()
    else:
        qi = pl.program_id(2)
        keys = lax.broadcasted_iota(jnp.int32, (tq, tq), 0)
        queries = lax.broadcasted_iota(jnp.int32, (tq, tq), 1)
        diag = (keys // CHUNK) <= (queries // CHUNK)
        scores(0, 0)
        init_from_meta()

        def body(k, carry):
            scores(2 * k + 1, 1)
            absorb(2 * k, 0, None)
            scores(2 * k + 2, 0)
            absorb(2 * k + 1, 1, None)
            return carry

        lax.fori_loop(0, qi // 2, body, 0)

        @pl.when(qi % 2 == 1)
        def _odd_tail():
            scores(qi, 1)
            absorb(qi - 1, 0, None)
            absorb(qi, 1, diag)

        @pl.when(qi % 2 == 0)
        def _even_tail():
            absorb(qi, 0, diag)

    lam = (jnp.exp(jnp.sum(lq1[...] * lk1[...], axis=1, keepdims=True))
           - jnp.exp(jnp.sum(lq2[...] * lk2[...], axis=1, keepdims=True)) + lam_init)
    for h in range(hps):
        o = acc_sc[2 * h] / l_sc[2 * h] - lam * (acc_sc[2 * h + 1] / l_sc[2 * h + 1])
        ms = jnp.mean(o * o, axis=0, keepdims=True)
        o = o * lax.rsqrt(ms + RMS_EPS) * nw_ref[...] * (1.0 - lam_init)
        o_ref[:, h * HEAD_W:(h + 1) * HEAD_W] = o.T.astype(BF16)


def _attention(p_q, p_kv, p_meta, lams, norm_w, *, batch, seq, tq, hps, lam_init, meta_only):
    n_meta = p_meta.shape[1]
    rows = p_q.shape[1]
    nq = 1 if meta_only else seq // tq
    k_blk, v_blk = ATT_HEADS // hps, 2 * ATT_HEADS // hps
    small = [pl.BlockSpec((1, ATT_QK_DIM), lambda b, g, i: (0, 0))] * 4 + [
        pl.BlockSpec((HEAD_W, 1), lambda b, g, i: (0, 0))]
    meta_specs = [pl.BlockSpec((hps, n_meta, HEAD_W), lambda b, g, i: (k_blk + g, 0, 0)),
                  pl.BlockSpec((hps, n_meta, HEAD_W), lambda b, g, i: (v_blk + g, 0, 0))]
    q_spec = pl.BlockSpec((hps, tq, HEAD_W), lambda b, g, i: (g, b * nq + i, 0))
    if meta_only:
        in_specs = [q_spec] + meta_specs + small
        args = (p_q, p_meta, p_meta) + lams + (norm_w,)
    else:
        kv_specs = [pl.BlockSpec((hps, seq, HEAD_W), lambda b, g, i: (k_blk + g, b, 0)),
                    pl.BlockSpec((hps, seq, HEAD_W), lambda b, g, i: (v_blk + g, b, 0))]
        in_specs = [q_spec] + kv_specs + meta_specs + small
        args = (p_q, p_kv, p_kv, p_meta, p_meta) + lams + (norm_w,)
    scratch = [pltpu.VMEM((2 * hps, 1, tq), F32), pltpu.VMEM((2 * hps, 1, tq), F32),
               pltpu.VMEM((2 * hps, HEAD_W, tq), F32)]
    if not meta_only:
        scratch += [pltpu.VMEM((2 * 2 * hps, tq, tq), F32),
                    pltpu.VMEM((2 * hps, 1, tq), F32),
                    pltpu.VMEM((2 * hps, tq, tq), BF16)]
    return pl.pallas_call(
        functools.partial(_attn_kernel, tq=tq, hps=hps, lam_init=lam_init, meta_only=meta_only),
        out_shape=jax.ShapeDtypeStruct((rows, GROUP_W), BF16),
        grid=(batch, ATT_HEADS // hps, nq),
        in_specs=in_specs,
        out_specs=pl.BlockSpec((tq, hps * HEAD_W), lambda b, g, i: (b * nq + i, g)),
        scratch_shapes=scratch,
        compiler_params=_params("parallel", "parallel", "arbitrary"),
        name="attn_meta" if meta_only else "attn",
    )(*args)


def _gdn_prep_kernel(*refs, rows, sub, n_pad, n_cast):
    (q_ref, k_ref, v_ref, hq_ref, hk_ref, hv_ref, mh_ref, cw_ref, ba_ref, alog_ref, dtb_ref) = refs[:11]
    cast_src = refs[11:11 + n_cast]
    u_ref, wq_ref, kd_ref, a_ref, gl_ref = refs[11 + n_cast:16 + n_cast]
    cast_dst = refs[16 + n_cast:16 + 2 * n_cast]
    xs_sc, gc_sc, pw_sc, x_sc, t_sc = refs[16 + 2 * n_cast:]
    for src, dst in zip(cast_src, cast_dst):
        dst[...] = src[...].astype(BF16)

    first = pl.program_id(1) == 0
    n_sub = rows // sub
    ba = ba_ref[...]
    beta_all = _sigmoid(ba)
    g_all = -jnp.exp(alog_ref[...]) * _softplus(ba + dtb_ref[...])
    row = lax.broadcasted_iota(jnp.int32, (rows, 1), 0)
    if n_pad:
        beta_all = jnp.where(row >= n_pad, beta_all, 0.0)
        g_all = jnp.where(row >= n_pad, g_all, 0.0)

    pos = row % CHUNK
    gc_all = g_all
    shift = 1
    while shift < CHUNK:
        gc_all = gc_all + jnp.where(pos >= shift, pltpu.roll(gc_all, shift, 0), 0.0)
        shift *= 2
    gc_sc[...] = gc_all
    gl_all = jnp.concatenate(
        [jnp.broadcast_to(gc_sc[(c + 1) * CHUNK - 1:(c + 1) * CHUNK, :], (CHUNK, LANES))
         for c in range(rows // CHUNK)], axis=0)
    gl_ref[...] = gl_all
    gct_all = gc_all.T

    ri = lax.broadcasted_iota(jnp.int32, (sub, sub), 0)
    ci = lax.broadcasted_iota(jnp.int32, (sub, sub), 1)
    same = (ri // CHUNK) == (ci // CHUNK)
    tril = jnp.logical_and(same, ci <= ri)
    strict = jnp.logical_and(same, ci < ri)
    eye = jnp.where(ri == ci, 1.0, 0.0).astype(F32)

    def conv_act(main_ref, halo_ref, stream, h):
        halo = jnp.where(first, mh_ref[stream * DN_HEADS + h], halo_ref[h])
        xs_sc[stream, 0:HALO, :] = halo.astype(F32)
        xs_sc[stream, HALO:, :] = main_ref[h].astype(F32)
        w = cw_ref[stream * DN_HEADS + h]
        y = xs_sc[stream, pl.ds(HALO - DN_CONV + 1, rows), :] * w[0:1]
        for j in range(1, DN_CONV):
            y = y + xs_sc[stream, pl.ds(HALO - DN_CONV + 1 + j, rows), :] * w[j:j + 1]
        return _silu(y)

    for h in range(DN_HEADS):
        q = conv_act(q_ref, hq_ref, 0, h)
        k = conv_act(k_ref, hk_ref, 1, h)
        v = conv_act(v_ref, hv_ref, 2, h)
        q = q * lax.rsqrt(jnp.sum(q * q, axis=1, keepdims=True) + RMS_EPS) * (HEAD_W ** -0.5)
        k = k * lax.rsqrt(jnp.sum(k * k, axis=1, keepdims=True) + RMS_EPS)
        beta = beta_all[:, h:h + 1]
        gc = gc_all[:, DN_HEADS + h:DN_HEADS + h + 1]
        gl = gl_all[:, DN_HEADS + h:DN_HEADS + h + 1]
        egc = jnp.exp(gc)
        kb = k * beta
        k16 = k.astype(BF16)
        kb16 = kb.astype(BF16)
        q16 = q.astype(BF16)
        x_all = jnp.concatenate([v * beta, kb * egc], axis=1)
        qd16 = (q * egc).astype(BF16)
        kd_ref[h] = (k * jnp.exp(gl - gc)).astype(BF16)
        for c in range(rows // CHUNK):
            wq_ref[h, (2 * c + 1) * CHUNK:(2 * c + 2) * CHUNK, :] = qd16[c * CHUNK:(c + 1) * CHUNK]
        for s in range(n_sub):
            lo, hi = s * sub, (s + 1) * sub
            gcr = gct_all[DN_HEADS + h:DN_HEADS + h + 1, lo:hi]
            kk = lax.dot_general(kb16[lo:hi], k16[lo:hi], _NT, preferred_element_type=F32)
            qk = lax.dot_general(q16[lo:hi], k16[lo:hi], _NT, preferred_element_type=F32)
            dec = jnp.exp(jnp.where(tril, gc[lo:hi] - gcr, -jnp.inf))
            a_ref[h, lo:hi, :] = (qk * dec).astype(BF16)
            m0 = jnp.where(strict, kk * dec, 0.0)
            pw_sc[h * n_sub + s] = m0.astype(BF16)
            t_sc[h * n_sub + s] = eye - m0
            x_sc[h * n_sub + s] = x_all[lo:hi].astype(BF16)

    items = range(DN_HEADS * n_sub)
    for _ in range(int(math.log2(CHUNK)) - 1):
        for it in items:
            p16 = pw_sc[it]
            pw_sc[it] = jnp.dot(p16, p16, preferred_element_type=F32).astype(BF16)
        for it in items:
            t = t_sc[it]
            t_sc[it] = t + jnp.dot(pw_sc[it], t.astype(BF16), preferred_element_type=F32)
    for it in items:
        h, lo = it // n_sub, (it % n_sub) * sub
        x = jnp.dot(t_sc[it].astype(BF16), x_sc[it], preferred_element_type=F32)
        u_ref[h, lo:lo + sub, :] = x[:, :HEAD_W].astype(BF16)
        w16 = x[:, HEAD_W:].astype(BF16)
        for c in range(sub // CHUNK):
            r0 = lo + c * CHUNK
            wq_ref[h, 2 * r0:2 * r0 + CHUNK, :] = w16[c * CHUNK:(c + 1) * CHUNK]


def _gdn_prep(p, p_halo_src, meta_halo, conv_w, ba, alog, dtb, *, batch, seq, rows, sub, n_pad,
              cast=(), layer=0):
    m = batch * seq
    nt = seq // rows
    steps = batch * nt
    cast_rows = [w.shape[1] // steps for w in cast]
    assert all(w.shape[1] % steps == 0 and r % (2 * HALO) == 0 for w, r in zip(cast, cast_rows))
    cast_in = [pl.BlockSpec((None, r, w.shape[2]), lambda b, t: (layer, b * nt + t, 0))
               for w, r in zip(cast, cast_rows)]
    cast_out = [pl.BlockSpec((r, w.shape[2]), lambda b, t: (b * nt + t, 0)) for w, r in zip(cast, cast_rows)]
    cast_shape = [jax.ShapeDtypeStruct(w.shape[1:], BF16) for w in cast]
    g_q, g_k, g_v = 3, 4, 5
    hb = rows // HALO

    def main_spec(g):
        return pl.BlockSpec((DN_HEADS, rows, HEAD_W), lambda b, t: (g, b * nt + t, 0))

    def halo_spec(g):
        return pl.BlockSpec((DN_HEADS, HALO, HEAD_W),
                            lambda b, t: (g, jnp.maximum((b * nt + t) * hb - 1, 0), 0))

    full = lambda shape: pl.BlockSpec(shape, lambda b, t: (0,) * len(shape))
    out_shape = (jax.ShapeDtypeStruct((DN_HEADS, m, HEAD_W), BF16),
                 jax.ShapeDtypeStruct((DN_HEADS, 2 * m, HEAD_W), BF16),
                 jax.ShapeDtypeStruct((DN_HEADS, m, HEAD_W), BF16),
                 jax.ShapeDtypeStruct((DN_HEADS, m, sub), BF16),
                 jax.ShapeDtypeStruct((m, LANES), F32)) + tuple(cast_shape)
    out_specs = (pl.BlockSpec((DN_HEADS, rows, HEAD_W), lambda b, t: (0, b * nt + t, 0)),
                 pl.BlockSpec((DN_HEADS, 2 * rows, HEAD_W), lambda b, t: (0, b * nt + t, 0)),
                 pl.BlockSpec((DN_HEADS, rows, HEAD_W), lambda b, t: (0, b * nt + t, 0)),
                 pl.BlockSpec((DN_HEADS, rows, sub), lambda b, t: (0, b * nt + t, 0)),
                 pl.BlockSpec((rows, LANES), lambda b, t: (b * nt + t, 0))) + tuple(cast_out)
    outs = pl.pallas_call(
        functools.partial(_gdn_prep_kernel, rows=rows, sub=sub, n_pad=n_pad, n_cast=len(cast)),
        out_shape=out_shape,
        grid=(batch, nt),
        in_specs=[main_spec(g_q), main_spec(g_k), main_spec(g_v),
                  halo_spec(g_q), halo_spec(g_k), halo_spec(g_v),
                  full(meta_halo.shape), full(conv_w.shape),
                  pl.BlockSpec((rows, LANES), lambda b, t: (b * nt + t, 0)),
                  full(alog.shape), full(dtb.shape)] + cast_in,
        out_specs=out_specs,
        scratch_shapes=[pltpu.VMEM((3, rows + HALO, HEAD_W), F32), pltpu.VMEM((rows, LANES), F32),
                        pltpu.VMEM((DN_HEADS * (rows // sub), sub, sub), BF16),
                        pltpu.VMEM((DN_HEADS * (rows // sub), sub, 2 * HEAD_W), BF16),
                        pltpu.VMEM((DN_HEADS * (rows // sub), sub, sub), F32)],
        compiler_params=_params("parallel", "parallel"),
        name="gdn_prep",
    )(p, p, p, p_halo_src, p_halo_src, p_halo_src, meta_halo, conv_w, ba, alog, dtb, *cast)
    return outs[:5], outs[5:]


def _gdn_scan_kernel(u_ref, wq_ref, kd_ref, a_ref, gl_ref, z_ref,
                     um_ref, wqm_ref, kdm_ref, am_ref, glm_ref, zm_ref, nw_ref,
                     o_ref, om_ref, s_sc, vbuf_sc, sw_sc, *, rows, sub, hpg):
    hg = pl.program_id(1)
    t = pl.program_id(2)
    lane = lax.broadcasted_iota(jnp.int32, (1, LANES), 1)
    nw = nw_ref[...]

    def decay_total(gl_row, h):
        sel = lane == DN_HEADS + hg * hpg + h
        return jnp.sum(jnp.where(sel, jnp.exp(gl_row), 0.0), axis=1, keepdims=True)

    def gate(o, z):
        ms = jnp.mean(o * o, axis=1, keepdims=True)
        return (o * lax.rsqrt(ms + RMS_EPS) * nw * _silu(z.astype(F32))).astype(BF16)

    def run_chunk(r0, refs, out_ref, out_r0):
        u_r, wq_r, kd_r, a_r, gl_r, z_r = refs
        slot = r0 % sub
        gl_row = gl_r[r0:r0 + 1, :]
        for h in range(hpg):
            sw_sc[h] = jnp.dot(wq_r[h, 2 * r0:2 * r0 + 2 * CHUNK, :], s_sc[h].astype(BF16),
                               preferred_element_type=F32)
        for h in range(hpg):
            v_new = u_r[h, r0:r0 + CHUNK, :].astype(F32) - sw_sc[h, 0:CHUNK, :]
            vbuf_sc[h, slot:slot + CHUNK, :] = v_new.astype(BF16)
        for h in range(hpg):
            o = sw_sc[h, CHUNK:, :] + jnp.dot(a_r[h, r0:r0 + CHUNK, :], vbuf_sc[h],
                                              preferred_element_type=F32)
            s_sc[h] = decay_total(gl_row, h) * s_sc[h] + lax.dot_general(
                kd_r[h, r0:r0 + CHUNK, :], vbuf_sc[h, slot:slot + CHUNK, :], _TN,
                preferred_element_type=F32)
            out_ref[out_r0:out_r0 + CHUNK, h * HEAD_W:(h + 1) * HEAD_W] = gate(
                o, z_r[h, r0:r0 + CHUNK, :])

    @pl.when(t == 0)
    def _meta_chunk():
        vbuf_sc[...] = jnp.zeros_like(vbuf_sc)
        s_sc[...] = jnp.zeros_like(s_sc)
        run_chunk(sub - CHUNK, (um_ref, wqm_ref, kdm_ref, am_ref, glm_ref, zm_ref), om_ref, 0)

    for c in range(rows // CHUNK):
        run_chunk(c * CHUNK, (u_ref, wq_ref, kd_ref, a_ref, gl_ref, z_ref), o_ref, c * CHUNK)


def _gdn_scan(real, meta, p, p_meta_pad, norm_w, *, batch, seq, rows, sub, hpg):
    u, wq, kd, a, gl = real
    um, wqm, kdm, am, glm = meta
    m = batch * seq
    nt = seq // rows
    ngrp = DN_HEADS // hpg
    z_blk = (6 * DN_HEADS) // hpg
    mrows = um.shape[1]
    assert mrows == sub

    def real_spec(r, w):
        return pl.BlockSpec((hpg, r, w), lambda b, g, t: (g, b * nt + t, 0))

    def meta_spec(r, w):
        return pl.BlockSpec((hpg, r, w), lambda b, g, t: (g, 0, 0))

    return pl.pallas_call(
        functools.partial(_gdn_scan_kernel, rows=rows, sub=sub, hpg=hpg),
        out_shape=(jax.ShapeDtypeStruct((m, GROUP_W), BF16),
                   jax.ShapeDtypeStruct((batch, CHUNK, GROUP_W), BF16)),
        grid=(batch, ngrp, nt),
        in_specs=[real_spec(rows, HEAD_W), real_spec(2 * rows, HEAD_W), real_spec(rows, HEAD_W),
                  real_spec(rows, sub),
                  pl.BlockSpec((rows, LANES), lambda b, g, t: (b * nt + t, 0)),
                  pl.BlockSpec((hpg, rows, HEAD_W), lambda b, g, t: (z_blk + g, b * nt + t, 0)),
                  meta_spec(mrows, HEAD_W), meta_spec(2 * mrows, HEAD_W), meta_spec(mrows, HEAD_W),
                  meta_spec(mrows, mrows),
                  pl.BlockSpec((mrows, LANES), lambda b, g, t: (0, 0)),
                  pl.BlockSpec((hpg, mrows, HEAD_W), lambda b, g, t: (z_blk + g, 0, 0)),
                  pl.BlockSpec((1, HEAD_W), lambda b, g, t: (0, 0))],
        out_specs=(pl.BlockSpec((rows, hpg * HEAD_W), lambda b, g, t: (b * nt + t, g)),
                   pl.BlockSpec((None, CHUNK, hpg * HEAD_W), lambda b, g, t: (b, 0, g))),
        scratch_shapes=[pltpu.VMEM((hpg, HEAD_W, HEAD_W), F32),
                        pltpu.VMEM((hpg, sub, HEAD_W), BF16),
                        pltpu.VMEM((hpg, 2 * CHUNK, HEAD_W), F32)],
        compiler_params=_params("parallel", "parallel", "arbitrary"),
        name="gdn_scan",
    )(u, wq, kd, a, gl, p, um, wqm, kdm, am, glm, p_meta_pad, norm_w)


def _wout_ln_kernel(oa_ref, od_ref, w_ref, x_ref, g_ref, b_ref, h_ref, *, alpha, n_parts):
    half = oa_ref.shape[1]
    part = oa_ref.shape[0] // n_parts
    for r in range(n_parts):
        rows = slice(r * part, (r + 1) * part)
        mix = (jnp.dot(oa_ref[rows, :], w_ref[0:half, :], preferred_element_type=F32)
               + jnp.dot(od_ref[rows, :], w_ref[half:, :], preferred_element_type=F32))
        h_ref[rows, :] = _layer_norm(alpha * x_ref[rows, :] + mix, g_ref[...], b_ref[...])


def _wout_ln(o_att, o_dn, w_out, x2d, g, b, *, bm, alpha):
    m, d = x2d.shape
    half = o_att.shape[1]
    return pl.pallas_call(
        functools.partial(_wout_ln_kernel, alpha=alpha, n_parts=4 if bm % (4 * 2 * HALO) == 0 else 1),
        out_shape=jax.ShapeDtypeStruct((m, d), F32),
        grid=(m // bm,),
        in_specs=[pl.BlockSpec((bm, half), lambda i: (i, 0)),
                  pl.BlockSpec((bm, half), lambda i: (i, 0)),
                  pl.BlockSpec((2 * half, d), lambda i: (0, 0)),
                  pl.BlockSpec((bm, d), lambda i: (i, 0)),
                  pl.BlockSpec((1, d), lambda i: (0, 0)),
                  pl.BlockSpec((1, d), lambda i: (0, 0))],
        out_specs=pl.BlockSpec((bm, d), lambda i: (i, 0)),
        compiler_params=_params("parallel"),
        name="wout_ln",
    )(o_att, o_dn, w_out, x2d, g, b)


def _ffn_kernel(h_ref, hprev_ref, hmeta_ref, wg_ref, wu_ref, cw_ref, cb_ref, wd_ref, g_ref, b_ref,
                o_ref, xb_sc, gate_sc, *, bm, tiles_per_batch, alpha):
    i = pl.program_id(0)
    f = pl.program_id(1)

    @pl.when(f == 0)
    def _init():
        halo = jnp.where(i % tiles_per_batch == 0, hmeta_ref[...], hprev_ref[...])
        xb_sc[0:HALO, :] = halo.astype(BF16)
        xb_sc[HALO:, :] = h_ref[...].astype(BF16)
        o_ref[...] = jnp.zeros_like(o_ref)

    gate_sc[...] = jnp.dot(xb_sc[...], wg_ref[...], preferred_element_type=F32)
    up = jnp.dot(xb_sc[HALO:, :], wu_ref[...], preferred_element_type=F32)
    cw = cw_ref[...]
    hg = cb_ref[...] + gate_sc[pl.ds(HALO - FFN_CONV + 1, bm), :] * cw[0:1]
    for j in range(1, FFN_CONV):
        hg = hg + gate_sc[pl.ds(HALO - FFN_CONV + 1 + j, bm), :] * cw[j:j + 1]
    act = (_silu(hg) * up).astype(BF16)
    o_ref[...] += jnp.dot(act, wd_ref[...], preferred_element_type=F32)

    @pl.when(f == pl.num_programs(1) - 1)
    def _finish():
        o_ref[...] = _layer_norm(alpha * h_ref[...] + o_ref[...], g_ref[...], b_ref[...])


def _ffn(h1, h1_meta, w_gate, w_up, conv_w, conv_b, w_down, g, b, *, seq, bm, tf, alpha):
    m, d = h1.shape
    dff = w_gate.shape[1]
    tiles_per_batch = seq // bm
    hb = bm // HALO
    meta_blk = h1_meta.shape[0] // HALO - 1
    return pl.pallas_call(
        functools.partial(_ffn_kernel, bm=bm, tiles_per_batch=tiles_per_batch, alpha=alpha),
        out_shape=jax.ShapeDtypeStruct((m, d), F32),
        grid=(m // bm, dff // tf),
        in_specs=[pl.BlockSpec((bm, d), lambda i, f: (i, 0)),
                  pl.BlockSpec((HALO, d), lambda i, f: (jnp.maximum(i * hb - 1, 0), 0)),
                  pl.BlockSpec((HALO, d), lambda i, f: (meta_blk, 0)),
                  pl.BlockSpec((d, tf), lambda i, f: (0, f)),
                  pl.BlockSpec((d, tf), lambda i, f: (0, f)),
                  pl.BlockSpec((FFN_CONV, tf), lambda i, f: (0, f)),
                  pl.BlockSpec((1, tf), lambda i, f: (0, f)),
                  pl.BlockSpec((tf, d), lambda i, f: (f, 0)),
                  pl.BlockSpec((1, d), lambda i, f: (0, 0)),
                  pl.BlockSpec((1, d), lambda i, f: (0, 0))],
        out_specs=pl.BlockSpec((bm, d), lambda i, f: (i, 0)),
        scratch_shapes=[pltpu.VMEM((bm + HALO, d), BF16),
                        pltpu.VMEM((bm + HALO, tf), F32)],
        compiler_params=_params("parallel", "arbitrary"),
        name="ffn",
    )(h1, h1, h1_meta, w_gate, w_up, conv_w, conv_b, w_down, g, b)


def _largest_tile(n, cap, mult):
    best = None
    for c in range(mult, min(n, cap) + 1, mult):
        if n % c == 0:
            best = c
    assert best is not None, (n, cap, mult)
    return best


def _split_halves(w_qk_t):
    n_comp, quarter = HEAD_W // ATT_QK_DIM, ATT_QK_DIM // 2
    depth, n, d = w_qk_t.shape
    w = w_qk_t.reshape(depth, -1, n_comp, 2, quarter, d)
    return jnp.swapaxes(w, 2, 3).reshape(depth, n, d)


def _rope_tables(n_pos):
    n_comp = HEAD_W // ATT_QK_DIM
    inv_freq = ROPE_THETA ** (-jnp.arange(0, ATT_QK_DIM, 2, dtype=F32) / ATT_QK_DIM)
    ang = jnp.arange(n_pos, dtype=F32)[:, None] * inv_freq[None, :]
    cos, sin = jnp.cos(ang), jnp.sin(ang)
    cos_t = jnp.tile(cos, (1, 2 * n_comp))
    sin_t = jnp.concatenate([jnp.tile(-sin, (1, n_comp)), jnp.tile(sin, (1, n_comp))], axis=1)
    assert cos_t.shape[1] == HEAD_W
    return cos_t, sin_t


def kernel(x, meta_tokens, w_in, conv_qkv_w, a_log, dt_bias, lambda_q1, lambda_k1, lambda_q2, lambda_k2,
           diff_norm_w, delta_norm_w, w_out, ln1_g, ln1_b, ffn_w_gate, ffn_w_up, ffn_conv_w, ffn_conv_b,
           ffn_w_down, ln2_g, ln2_b):
    batch, seq, d = x.shape
    n_meta = meta_tokens.shape[0]
    depth = w_in.shape[0]
    n_main = N_GROUPS * GROUP_W
    assert w_in.shape[2] == n_main + TAIL_COLS and n_meta % HALO == 0 and n_meta <= CHUNK
    m = batch * seq
    alpha = (2.0 * depth) ** 0.25

    bm_proj = _largest_tile(seq, 1024, HALO)
    bm_row = _largest_tile(seq, 512, HALO)
    tq = _largest_tile(seq, 256, CHUNK)
    rows = _largest_tile(seq, 256, LANES)
    sub = LANES
    hpg = DN_HEADS
    att_hps = ATT_HEADS

    cos_t, sin_t = _rope_tables(n_meta + seq)
    cos_meta, sin_meta = cos_t[:n_meta], sin_t[:n_meta]
    cos_real, sin_real = cos_t[n_meta:], sin_t[n_meta:]

    h = x.reshape(m, d)
    h_meta = meta_tokens.astype(x.dtype)
    n_qk = 2 * GROUP_W
    w_in_t = jnp.swapaxes(w_in, 1, 2)
    w_t = w_in_t.astype(BF16)
    wqk_t = _split_halves(w_t[:, :n_qk])
    pad_rows = sub - n_meta
    for l in range(depth):
        w_tail_t = jnp.pad(w_t[l, n_main:], ((0, LANES - TAIL_COLS), (0, 0)))
        lane_pad = (DN_HEADS, LANES - TAIL_COLS)
        alog = jnp.pad(a_log[l].astype(F32), lane_pad)[None]
        dtb = jnp.pad(dt_bias[l].astype(F32), lane_pad)[None]
        conv_w = conv_qkv_w[l].reshape(DN_CONV, 3 * DN_HEADS, HEAD_W).transpose(1, 0, 2)
        lams = tuple(v[l][None].astype(F32) for v in (lambda_q1, lambda_k1, lambda_q2, lambda_k2))
        lam_init = 0.8 - 0.6 * math.exp(-0.3 * l)

        p, p_meta, ba, ba_meta = _proj(h, h_meta, wqk_t, w_t, l, w_tail_t, cos_real, sin_real,
                                       cos_meta, sin_meta, bm_proj)

        att_nw = diff_norm_w[l].astype(F32)[:, None]
        o_att = _attention(p, p, p_meta, lams, att_nw, batch=batch, seq=seq, tq=tq, hps=att_hps,
                           lam_init=lam_init, meta_only=False)
        q_meta = jnp.pad(p_meta[:ATT_HEADS], ((0, 0), (0, LANES - n_meta), (0, 0)))
        o_att_meta = _attention(q_meta, None, p_meta, lams, att_nw, batch=1, seq=LANES, tq=LANES,
                                hps=att_hps, lam_init=lam_init, meta_only=True)[:n_meta]

        p_meta_pad = jnp.pad(p_meta, ((0, 0), (pad_rows, 0), (0, 0)))
        ba_meta_pad = jnp.pad(ba_meta, ((pad_rows, 0), (0, 0)))
        meta_halo = p_meta[3 * DN_HEADS:6 * DN_HEADS, n_meta - HALO:, :]
        zero_halo = jnp.zeros_like(meta_halo)
        prep_real, (w_out_b, w_gate_b, w_up_b, w_down_b) = _gdn_prep(
            p, p, meta_halo, conv_w, ba, alog, dtb, batch=batch, seq=seq, rows=rows, sub=sub, n_pad=0,
            cast=(w_out, ffn_w_gate, ffn_w_up, ffn_w_down), layer=l)
        prep_meta, _ = _gdn_prep(p_meta_pad, p_meta_pad, zero_halo, conv_w, ba_meta_pad, alog, dtb, batch=1,
                                 seq=sub, rows=sub, sub=sub, n_pad=pad_rows)
        o_dn, o_dn_meta = _gdn_scan(prep_real, prep_meta, p, p_meta_pad, delta_norm_w[l][None],
                                    batch=batch, seq=seq, rows=rows, sub=sub, hpg=hpg)
        o_dn_meta = o_dn_meta[0, CHUNK - n_meta:]

        g1, b1 = ln1_g[l][None].astype(F32), ln1_b[l][None].astype(F32)
        h1 = _wout_ln(o_att, o_dn, w_out_b, h, g1, b1, bm=bm_row, alpha=alpha)
        h1_meta = _wout_ln(o_att_meta, o_dn_meta, w_out_b, h_meta, g1, b1, bm=n_meta, alpha=alpha)

        dff = ffn_w_gate.shape[2]
        tf = _largest_tile(dff, 512, LANES)
        ffn_args = (w_gate_b, w_up_b, ffn_conv_w[l].astype(F32),
                    ffn_conv_b[l][None].astype(F32), w_down_b,
                    ln2_g[l][None].astype(F32), ln2_b[l][None].astype(F32))
        h_next = _ffn(h1, h1_meta, *ffn_args, seq=seq, bm=bm_row, tf=tf, alpha=alpha)
        if l + 1 < depth:
            zero_meta = jnp.zeros_like(h1_meta)
            h_meta = _ffn(h1_meta, zero_meta, *ffn_args, seq=n_meta, bm=n_meta, tf=tf, alpha=alpha)
        h = h_next
    return h.reshape(batch, seq, d)
```
